```python
import jax, jax.numpy as jnp
from jax import lax
import numpy as np

D_MODEL = 2048
BATCH = 1
SEQ = 8192
DEPTH = 2
DEC_BATCH = 128
DEC_SEQ = 4
PAST_LEN = 8192
PAGE_SIZE = 128

MIX_W = D_MODEL // 2
N_BRANCH = 4
A_DK = 128
A_HEADS = MIX_W // A_DK
A_DV = MIX_W // A_HEADS
HGRN_CHUNK = 32
POOL_WINDOWS = (2, 4, 8, 16)
POOL_GROUPS = len(POOL_WINDOWS)
POOL_GC = MIX_W // POOL_GROUPS
POOL_BUF = max(POOL_WINDOWS) - 1
SWA_HEAD_DIM = 64
SWA_Q_HEADS = MIX_W // SWA_HEAD_DIM
SWA_KV_HEADS = SWA_Q_HEADS // 4
SWA_GROUP = SWA_Q_HEADS // SWA_KV_HEADS
WINDOW = 128
ROT_DIM = SWA_HEAD_DIM // 4
ROPE_THETA = 500000.0
N_MEM = 256
MEM_HEADS = 4
MEM_HEAD_DIM = MIX_W // MEM_HEADS
D_FF = 11 * D_MODEL // 4
CONV_W = 3
EPS = 1e-6
IN_SPLIT_SIZES = (MIX_W, MIX_W, MIX_W, MIX_W, MIX_W, MIX_W,
                  SWA_KV_HEADS * SWA_HEAD_DIM, SWA_KV_HEADS * SWA_HEAD_DIM, MIX_W, N_BRANCH * D_MODEL)
IN_COLS = sum(IN_SPLIT_SIZES)

kernel_name = "hybrid_hgrn2_pool_swa_mem_decoder_step"


def rms_norm(x, g):
    xf = x.astype(jnp.float32)
    y = xf * lax.rsqrt(jnp.mean(xf * xf, axis=-1, keepdims=True) + EPS)
    return (y * g.astype(jnp.float32)).astype(x.dtype)


def split_cols(p):
    idx = []
    acc = 0
    for sz in IN_SPLIT_SIZES[:-1]:
        acc += sz
        idx.append(acc)
    return jnp.split(p, idx, axis=-1)


def partial_rope(x, pos):
    half = ROT_DIM // 2
    inv = jnp.power(ROPE_THETA, -jnp.arange(0, ROT_DIM, 2, dtype=jnp.float32) / ROT_DIM)
    ang = pos.astype(jnp.float32)[:, None] * inv[None, :]
    cos = jnp.cos(ang)[None, :, None, :]
    sin = jnp.sin(ang)[None, :, None, :]
    xr = x[..., :ROT_DIM].astype(jnp.float32)
    x1, x2 = xr[..., :half], xr[..., half:]
    rot = jnp.concatenate([x1 * cos - x2 * sin, x2 * cos + x1 * sin], axis=-1)
    return jnp.concatenate([rot.astype(x.dtype), x[..., ROT_DIM:]], axis=-1)


def hgrn2_chunked(q, k, v, log_f, s0):
    B, L, H, DK = q.shape
    DV = v.shape[-1]
    C = min(HGRN_CHUNK, L)
    pad = (-L) % C
    q, k, v, log_f = (t.astype(jnp.float32) for t in (q, k, v, log_f))
    if pad:
        cfg = ((0, 0), (0, pad), (0, 0), (0, 0))
        q, k, v, log_f = (jnp.pad(t, cfg) for t in (q, k, v, log_f))
    n = (L + pad) // C

    def to_chunks(t):
        return t.reshape(B, n, C, H, t.shape[-1]).transpose(1, 0, 3, 2, 4)

    causal = jnp.tril(jnp.ones((C, C), dtype=bool))[:, :, None]

    def step(S, inp):
        qc, kc, vc, gc = inp
        b = jnp.cumsum(gc, axis=2)
        o_inter = jnp.einsum('bhtd,bhde->bhte', qc * jnp.exp(b), S)
        decay = jnp.exp(jnp.where(causal, b[:, :, :, None, :] - b[:, :, None, :, :], -jnp.inf))
        att = jnp.einsum('bhtd,bhsd,bhtsd->bhts', qc, kc, decay)
        o = o_inter + jnp.einsum('bhts,bhse->bhte', att, vc)
        b_last = b[:, :, -1:, :]
        S = jnp.exp(b_last[:, :, 0, :, None]) * S + jnp.einsum('bhsd,bhse->bhde', kc * jnp.exp(b_last - b), vc)
        return S, o

    S, o = lax.scan(step, s0.astype(jnp.float32),
                    (to_chunks(q), to_chunks(k), to_chunks(v), to_chunks(log_f)))
    o = o.transpose(1, 0, 3, 2, 4).reshape(B, n * C, H, DV)[:, :L]
    return S, o


def pool_mixer(u, buf, pos, pool_w, pool_scale):
    B, L, _ = u.shape
    P = buf.shape[1]
    ext = jnp.concatenate([buf, u], axis=1).astype(jnp.float32)
    c = jnp.concatenate([jnp.zeros((B, 1, MIX_W), jnp.float32), jnp.cumsum(ext, axis=1)], axis=1)
    hi = c[:, P + 1:]
    uf = u.astype(jnp.float32)
    outs = []
    for g, win in enumerate(POOL_WINDOWS):
        sl = slice(g * POOL_GC, (g + 1) * POOL_GC)
        lo = c[:, P + 1 - win:P + 1 - win + L, sl]
        cnt = jnp.minimum(pos + 1, win).astype(jnp.float32)[None, :, None]
        outs.append((hi[..., sl] - lo) / cnt - uf[..., sl])
    d = jnp.stack(outs, axis=2)
    y = jnp.einsum('blgc,gce->blge', d, pool_w.astype(jnp.float32)).reshape(B, L, MIX_W)
    y = (y * pool_scale.astype(jnp.float32)).astype(u.dtype)
    return y, jnp.concatenate([buf, u], axis=1)[:, -POOL_BUF:]


def sink_softmax(s, valid, sink):
    s = jnp.where(valid, s.astype(jnp.float32), -jnp.inf)
    sk = sink.astype(jnp.float32)[..., None]
    m = jnp.maximum(jnp.max(s, axis=-1, keepdims=True), sk)
    e = jnp.exp(s - m)
    return e / (jnp.sum(e, axis=-1, keepdims=True) + jnp.exp(sk - m))


def swa_attention(q, k, v, sinks, k_buf, v_buf):
    B, L = q.shape[:2]
    scale = SWA_HEAD_DIM ** -0.5
    sink = sinks.reshape(SWA_KV_HEADS, SWA_GROUP)
    if k_buf is None:
        nb = L // WINDOW
        qb = q.reshape(B, nb, WINDOW, SWA_KV_HEADS, SWA_GROUP, SWA_HEAD_DIM)

        def band_keys(t):
            tb = t.reshape(B, nb, WINDOW, SWA_KV_HEADS, SWA_HEAD_DIM)
            prev = jnp.concatenate([jnp.zeros_like(tb[:, :1]), tb[:, :-1]], axis=1)
            return jnp.concatenate([prev, tb], axis=2)

        kk, vv = band_keys(k), band_keys(v)
        qi = jnp.arange(WINDOW)[:, None]
        ci = jnp.arange(2 * WINDOW)[None, :]
        band = (ci > qi) & (ci <= qi + WINDOW)
        not_first = jnp.arange(nb)[:, None, None] > 0
        valid = band[None] & (not_first | (ci >= WINDOW)[None])
        s = jnp.einsum('bnqkgd,bnckd->bnkgqc', qb, kk) * scale
        p = sink_softmax(s, valid[None, :, None, None], sink[None, None, :, :, None])
        o = jnp.einsum('bnkgqc,bnckd->bnqkgd', p.astype(v.dtype), vv).reshape(B, L, MIX_W)
        return o, k[:, -WINDOW:], v[:, -WINDOW:]
    kk = jnp.concatenate([k_buf.astype(k.dtype), k], axis=1)
    vv = jnp.concatenate([v_buf.astype(v.dtype), v], axis=1)
    qi = jnp.arange(L)[:, None]
    ci = jnp.arange(WINDOW + L)[None, :]
    valid = (ci > qi) & (ci <= qi + WINDOW)
    qg = q.reshape(B, L, SWA_KV_HEADS, SWA_GROUP, SWA_HEAD_DIM)
    s = jnp.einsum('bqkgd,bckd->bkgqc', qg, kk) * scale
    p = sink_softmax(s, valid, sink[None, :, :, None])
    o = jnp.einsum('bkgqc,bckd->bqkgd', p.astype(v.dtype), vv).reshape(B, L, MIX_W)
    return o, kk[:, -WINDOW:], vv[:, -WINDOW:]


def memory_kv(mem, mem_norm_g, w_mem_kv, mem_knorm_g):
    B, M, _ = mem.shape
    kv = rms_norm(mem, mem_norm_g) @ w_mem_kv
    k, v = jnp.split(kv, 2, axis=-1)
    k = rms_norm(k.reshape(B, M, MEM_HEADS, MEM_HEAD_DIM), mem_knorm_g)
    return k, v.reshape(B, M, MEM_HEADS, MEM_HEAD_DIM)


def mem_attention(q, mem_k, mem_v):
    B, L = q.shape[:2]
    s = jnp.einsum('blhd,bmhd->bhlm', q, mem_k.astype(q.dtype)) * MEM_HEAD_DIM ** -0.5
    p = jax.nn.softmax(s.astype(jnp.float32), axis=-1)
    o = jnp.einsum('bhlm,bmhd->blhd', p.astype(mem_v.dtype), mem_v)
    return o.reshape(B, L, MIX_W).astype(q.dtype)


def conv_ffn(m, conv_buf, w_up, conv_w, conv_b, w_down):
    L = m.shape[1]
    a, v = jnp.split(m @ w_up, 2, axis=-1)
    ext = jnp.concatenate([conv_buf.astype(a.dtype), a], axis=1)
    c = conv_b
    for j in range(CONV_W):
        c = c + conv_w[j] * ext[:, j:j + L]
    y = (jax.nn.gelu(c, approximate=False) * v) @ w_down
    return y, ext[:, -(CONV_W - 1):]


def layer_forward(x, pos, mem_k, mem_v, hgrn_s0, pool_buf, conv_buf, k_buf, v_buf, lb,
                  norm1_g, w_in, hgrn_norm_g, pool_w, pool_scale, swa_qnorm_g, swa_knorm_g,
                  swa_sinks, mem_qnorm_g, w_branch, w_o, norm2_g, w_up, conv_w, conv_b, w_down):
    B, L, _ = x.shape
    n = rms_norm(x, norm1_g)
    qa, fa, ia, ga, ub, qc, kc, vc, qm, gz = split_cols(n @ w_in)
    z = fa.reshape(B, L, A_HEADS, A_DK).astype(jnp.float32)
    lbh = lb.reshape(A_HEADS, A_DK)
    log_f = jnp.logaddexp(jnp.log(lbh), jnp.log1p(-lbh) + jax.nn.log_sigmoid(z))
    k_a = (1.0 - lbh) * jax.nn.sigmoid(-z)
    s_a, o_a = hgrn2_chunked(jax.nn.silu(qa.reshape(B, L, A_HEADS, A_DK)), k_a,
                             ia.reshape(B, L, A_HEADS, A_DV), log_f, hgrn_s0)
    y_a = rms_norm(o_a, hgrn_norm_g.reshape(A_HEADS, A_DV)) * \
        jax.nn.silu(ga.reshape(B, L, A_HEADS, A_DV).astype(jnp.float32))
    y_a = y_a.astype(x.dtype).reshape(B, L, MIX_W)
    y_b, pool_new = pool_mixer(ub, pool_buf, pos, pool_w, pool_scale)
    q_c = partial_rope(rms_norm(qc.reshape(B, L, SWA_Q_HEADS, SWA_HEAD_DIM), swa_qnorm_g), pos)
    k_c = partial_rope(rms_norm(kc.reshape(B, L, SWA_KV_HEADS, SWA_HEAD_DIM), swa_knorm_g), pos)
    v_c = vc.reshape(B, L, SWA_KV_HEADS, SWA_HEAD_DIM)
    y_c, k_new, v_new = swa_attention(q_c, k_c, v_c, swa_sinks, k_buf, v_buf)
    y_m = mem_attention(rms_norm(qm.reshape(B, L, MEM_HEADS, MEM_HEAD_DIM), mem_qnorm_g), mem_k, mem_v)
    branches = jnp.stack([y_a, y_b, y_c, y_m], axis=2)
    z_br = jnp.einsum('blnc,ncd->blnd', branches, w_branch)
    gates = jax.nn.sigmoid(gz.reshape(B, L, N_BRANCH, D_MODEL).astype(jnp.float32))
    merged = jnp.sum(gates * z_br.astype(jnp.float32), axis=2).astype(x.dtype)
    h = x + merged @ w_o
    y_f, conv_new = conv_ffn(rms_norm(h, norm2_g), conv_buf, w_up, conv_w, conv_b, w_down)
    return h + y_f, (s_a, pool_new, k_new, v_new, conv_new)


def setup_inputs(seed: int = 0) -> dict:
    key = jax.random.key(seed)
    ks = jax.random.split(key, 40)
    f32 = jnp.float32

    def nrm(i, shape, s=1.0):
        return jax.random.normal(ks[i], shape, f32) * s

    def gain(i, shape):
        return 1.0 + 0.1 * jax.random.normal(ks[i], shape, f32)

    return {
        "x_prompt": nrm(0, (BATCH, SEQ, D_MODEL)),
        "x_sample": nrm(1, (DEC_BATCH, DEC_SEQ, D_MODEL)),
        "mem_prompt": nrm(2, (BATCH, N_MEM, D_MODEL)),
        "state_hgrn": nrm(3, (DEPTH, DEC_BATCH, A_HEADS, A_DK, A_DV), 0.5),
        "cache_pool": nrm(4, (DEPTH, DEC_BATCH, POOL_BUF, MIX_W)),
        "cache_swa_k": nrm(5, (DEPTH, DEC_BATCH, WINDOW, SWA_KV_HEADS, SWA_HEAD_DIM)),
        "cache_swa_v": nrm(6, (DEPTH, DEC_BATCH, WINDOW, SWA_KV_HEADS, SWA_HEAD_DIM)),
        "state_conv": nrm(7, (DEPTH, DEC_BATCH, CONV_W - 1, D_FF)),
        "cache_mem_k": nrm(8, (DEPTH, DEC_BATCH, N_MEM, MEM_HEADS, MEM_HEAD_DIM)),
        "cache_mem_v": nrm(9, (DEPTH, DEC_BATCH, N_MEM, MEM_HEADS, MEM_HEAD_DIM)),
        "norm1_g": gain(10, (DEPTH, D_MODEL)),
        "w_in": nrm(11, (DEPTH, D_MODEL, IN_COLS), D_MODEL ** -0.5),
        "hgrn_lb": nrm(12, (DEPTH, MIX_W)),
        "hgrn_norm_g": gain(13, (DEPTH, MIX_W)),
        "pool_w": nrm(14, (DEPTH, POOL_GROUPS, POOL_GC, POOL_GC), POOL_GC ** -0.5),
        "pool_scale": gain(15, (DEPTH, MIX_W)),
        "swa_qnorm_g": gain(16, (DEPTH, SWA_HEAD_DIM)),
        "swa_knorm_g": gain(17, (DEPTH, SWA_HEAD_DIM)),
        "swa_sinks": nrm(18, (DEPTH, SWA_Q_HEADS)),
        "mem_norm_g": gain(19, (DEPTH, D_MODEL)),
        "w_mem_kv": nrm(20, (DEPTH, D_MODEL, 2 * MIX_W), D_MODEL ** -0.5),
        "mem_qnorm_g": gain(21, (DEPTH, MEM_HEAD_DIM)),
        "mem_knorm_g": gain(22, (DEPTH, MEM_HEAD_DIM)),
        "w_branch": nrm(23, (DEPTH, N_BRANCH, MIX_W, D_MODEL), MIX_W ** -0.5),
        "w_o": nrm(24, (DEPTH, D_MODEL, D_MODEL), D_MODEL ** -0.5),
        "norm2_g": gain(25, (DEPTH, D_MODEL)),
        "w_up": nrm(26, (DEPTH, D_MODEL, 2 * D_FF), D_MODEL ** -0.5),
        "conv_w": nrm(27, (DEPTH, CONV_W, D_FF), 0.5),
        "conv_b": nrm(28, (DEPTH, D_FF), 0.02),
        "w_down": nrm(29, (DEPTH, D_FF, D_MODEL), D_FF ** -0.5),
    }


def reference(x_prompt, x_sample, mem_prompt, state_hgrn, cache_pool, cache_swa_k, cache_swa_v,
              state_conv, cache_mem_k, cache_mem_v, norm1_g, w_in, hgrn_lb, hgrn_norm_g, pool_w,
              pool_scale, swa_qnorm_g, swa_knorm_g, swa_sinks, mem_norm_g, w_mem_kv, mem_qnorm_g,
              mem_knorm_g, w_branch, w_o, norm2_g, w_up, conv_w, conv_b, w_down):
    lb_all = jnp.cumsum(jax.nn.softmax(hgrn_lb.astype(jnp.float32), axis=0), axis=0)
    lb_all = lb_all - lb_all[:1]
    pos_p = jnp.arange(x_prompt.shape[1], dtype=jnp.int32)
    pos_s = PAST_LEN + jnp.arange(x_sample.shape[1], dtype=jnp.int32)
    bp = x_prompt.shape[0]
    xp, xs = x_prompt, x_sample
    st_p, st_s, mks, mvs = [], [], [], []
    for l in range(DEPTH):
        lw = (norm1_g[l], w_in[l], hgrn_norm_g[l], pool_w[l], pool_scale[l], swa_qnorm_g[l],
              swa_knorm_g[l], swa_sinks[l], mem_qnorm_g[l], w_branch[l], w_o[l], norm2_g[l],
              w_up[l], conv_w[l], conv_b[l], w_down[l])
        mk, mv = memory_kv(mem_prompt, mem_norm_g[l], w_mem_kv[l], mem_knorm_g[l])
        mks.append(mk)
        mvs.append(mv)
        xp, sp = layer_forward(
            xp, pos_p, mk, mv,
            jnp.zeros((bp, A_HEADS, A_DK, A_DV), jnp.float32),
            jnp.zeros((bp, POOL_BUF, MIX_W), xp.dtype),
            jnp.zeros((bp, CONV_W - 1, D_FF), xp.dtype),
            None, None, lb_all[l], *lw)
        xs, ss = layer_forward(
            xs, pos_s, cache_mem_k[l], cache_mem_v[l], state_hgrn[l], cache_pool[l], state_conv[l],
            cache_swa_k[l], cache_swa_v[l], lb_all[l], *lw)
        st_p.append(sp)
        st_s.append(ss)

    def stk(sts, i):
        return jnp.stack([s[i] for s in sts], axis=0)

    return (xp, xs,
            stk(st_p, 0), stk(st_s, 0),
            stk(st_p, 1), stk(st_s, 1),
            stk(st_p, 2), stk(st_s, 2),
            stk(st_p, 3), stk(st_s, 3),
            stk(st_p, 4), stk(st_s, 4),
            jnp.stack(mks, axis=0), jnp.stack(mvs, axis=0))
```

```python
import functools

import numpy as np
import jax
import jax.numpy as jnp
from jax import lax
from jax.experimental import pallas as pl
from jax.experimental.pallas import tpu as pltpu

F32 = jnp.float32
BF16 = jnp.bfloat16

D_MODEL = 2048
MIX_W = D_MODEL // 2
N_BRANCH = 4
A_DK = 128
A_HEADS = MIX_W // A_DK
POOL_WINDOWS = (2, 4, 8, 16)
POOL_GC = MIX_W // len(POOL_WINDOWS)
POOL_BUF = max(POOL_WINDOWS) - 1
SWA_HEAD_DIM = 64
SWA_Q_HEADS = MIX_W // SWA_HEAD_DIM
SWA_KV_HEADS = SWA_Q_HEADS // 4
WINDOW = 128
ROT_DIM = SWA_HEAD_DIM // 4
ROPE_THETA = 500000.0
N_MEM = 256
MEM_HEADS = 4
MEM_HEAD_DIM = MIX_W // MEM_HEADS
D_FF = 11 * D_MODEL // 4
EPS = 1e-6
PAST_LEN = 8192

COL_HQ, COL_HF, COL_HI, COL_HG = 0, MIX_W, 2 * MIX_W, 3 * MIX_W
COL_POOL = 4 * MIX_W
COL_SQ = 5 * MIX_W
COL_SK = 6 * MIX_W
COL_SV = COL_SK + SWA_KV_HEADS * SWA_HEAD_DIM
COL_MQ = COL_SV + SWA_KV_HEADS * SWA_HEAD_DIM
COL_GATE = COL_MQ + MIX_W
IN_COLS = COL_GATE + N_BRANCH * D_MODEL

LANES = 128
HGRN_CHUNK = 128
VMEM_LIMIT = 56 * 1024 * 1024
NEG_BIG = -1e30


def _params(*sem):
    return pltpu.CompilerParams(dimension_semantics=sem, vmem_limit_bytes=VMEM_LIMIT)


def _sigmoid(x):
    return 1.0 / (1.0 + jnp.exp(-x))


def _dot(a, b):
    return jnp.dot(a, b, preferred_element_type=F32)


def _dot_nt(a, b):
    return lax.dot_general(a, b, (((1,), (1,)), ((), ())), preferred_element_type=F32)


def _norm_matmul_kernel(x_ref, g_ref, w_ref, o_ref, xn_ref):
    @pl.when(pl.program_id(1) == 0)
    def _():
        x = x_ref[...]
        ms = jnp.mean(x * x, axis=-1, keepdims=True)
        xn_ref[...] = (x * lax.rsqrt(ms + EPS) * g_ref[...]).astype(BF16)

    o_ref[...] = _dot(xn_ref[...], w_ref[...].astype(BF16))


def norm_matmul(x, g, w, layer, tm, tn):
    T, K = x.shape
    N = w.shape[2]
    return pl.pallas_call(
        _norm_matmul_kernel,
        grid=(T // tm, N // tn),
        in_specs=[
            pl.BlockSpec((tm, K), lambda i, j: (i, 0)),
            pl.BlockSpec((None, 1, K), lambda i, j: (layer, 0, 0)),
            pl.BlockSpec((None, K, tn), lambda i, j: (layer, 0, j)),
        ],
        out_specs=pl.BlockSpec((tm, tn), lambda i, j: (i, j)),
        out_shape=jax.ShapeDtypeStruct((T, N), F32),
        scratch_shapes=[pltpu.VMEM((tm, K), BF16)],
        compiler_params=_params("parallel", "arbitrary"),
        name="norm_matmul",
    )(x, g, w)


def _matmul_res_kernel(a_ref, w_ref, r_ref, o_ref):
    o_ref[...] = r_ref[...] + _dot(a_ref[...], w_ref[...].astype(BF16))


def matmul_res(a, w, layer, res, tm, tn):
    T, K = a.shape
    N = w.shape[2]
    return pl.pallas_call(
        _matmul_res_kernel,
        grid=(T // tm, N // tn),
        in_specs=[
            pl.BlockSpec((tm, K), lambda i, j: (i, 0)),
            pl.BlockSpec((None, K, tn), lambda i, j: (layer, 0, j)),
            pl.BlockSpec((tm, tn), lambda i, j: (i, j)),
        ],
        out_specs=pl.BlockSpec((tm, tn), lambda i, j: (i, j)),
        out_shape=jax.ShapeDtypeStruct((T, N), F32),
        compiler_params=_params("parallel", "arbitrary"),
        name="matmul_res",
    )(a, w, res)


def _merge_kernel(ya_ref, yb_ref, yc_ref, ym_ref, g0_ref, g1_ref, g2_ref, g3_ref, w_ref, o_ref):
    acc = None
    for n, (y_ref, g_ref) in enumerate(((ya_ref, g0_ref), (yb_ref, g1_ref), (yc_ref, g2_ref), (ym_ref, g3_ref))):
        z = _dot(y_ref[...], w_ref[n].astype(BF16))
        t = _sigmoid(g_ref[...]) * z
        acc = t if acc is None else acc + t
    o_ref[...] = acc.astype(BF16)


def merge_branches(ys, proj, w_branch, layer, tm, tn):
    T = proj.shape[0]
    y_spec = pl.BlockSpec((tm, MIX_W), lambda i, j: (i, 0))

    def gate_spec(n):
        off = (COL_GATE + n * D_MODEL) // tn
        return pl.BlockSpec((tm, tn), lambda i, j: (i, off + j))

    return pl.pallas_call(
        _merge_kernel,
        grid=(T // tm, D_MODEL // tn),
        in_specs=[y_spec] * 4 + [gate_spec(n) for n in range(4)] + [
            pl.BlockSpec((None, N_BRANCH, MIX_W, tn), lambda i, j: (layer, 0, 0, j)),
        ],
        out_specs=pl.BlockSpec((tm, tn), lambda i, j: (i, j)),
        out_shape=jax.ShapeDtypeStruct((T, D_MODEL), BF16),
        compiler_params=_params("parallel", "arbitrary"),
        name="merge_branches",
    )(*ys, proj, proj, proj, proj, w_branch)


def _hgrn_gates(q_in, z, lb):
    q = q_in * _sigmoid(q_in)
    log_sig = jnp.minimum(z, 0.0) - jnp.log1p(jnp.exp(-jnp.abs(z)))
    a1 = jnp.log(lb)
    a2 = jnp.log1p(-lb) + log_sig
    log_f = jnp.maximum(a1, a2) + jnp.log1p(jnp.exp(-jnp.abs(a1 - a2)))
    k = (1.0 - lb) * _sigmoid(-z)
    return q, log_f, k


def _hgrn_out(o, gate, gn):
    ms = jnp.mean(o * o, axis=-1, keepdims=True)
    return o * lax.rsqrt(ms + EPS) * gn * (gate * _sigmoid(gate))


def _cumsum_rows(x, tril):
    hi = x.astype(BF16)
    r1 = x - hi.astype(F32)
    mid = r1.astype(BF16)
    lo = (r1 - mid.astype(F32)).astype(BF16)
    return _dot(tril, hi) + _dot(tril, mid) + _dot(tril, lo)


def _block_row(x, blk, r):
    C = x.shape[0]
    x3 = x.reshape(C // blk, blk, LANES)
    return jnp.broadcast_to(x3[:, r:r + 1, :], (C // blk, blk, LANES)).reshape(C, LANES)


def _hgrn_chunk(q, log_f, k, v, S):
    C = HGRN_CHUNK
    row = lax.broadcasted_iota(jnp.int32, (C, C), 0)
    col = lax.broadcasted_iota(jnp.int32, (C, C), 1)
    rowl = lax.broadcasted_iota(jnp.int32, (C, LANES), 0)
    tril = jnp.where(col <= row, 1.0, 0.0).astype(BF16)
    b = _cumsum_rows(log_f, tril)

    att = jnp.zeros((C, C), F32)
    roff = rowl & 7
    blk0 = row & ~7
    for s in range(8):
        bs = _block_row(b, 8, s)
        ks = _block_row(k, 8, s)
        e = jnp.exp(jnp.where(roff >= s, b - bs, NEG_BIG))
        r = jnp.sum(q * ks * e, axis=-1, keepdims=True)
        att = jnp.where(col == blk0 + s, r, att)
    m = 8
    while m < C:
        gam = _block_row(b, 2 * m, m - 1)
        isq = (rowl & (2 * m - 1)) >= m
        x = (jnp.where(isq, q, k) * jnp.exp(jnp.where(isq, b - gam, gam - b))).astype(BF16)
        a = _dot_nt(x, x)
        mask = ((row & ~(2 * m - 1)) == (col & ~(2 * m - 1))) & ((row & (2 * m - 1)) >= m) & ((col & (2 * m - 1)) < m)
        att = jnp.where(mask, a, att)
        m *= 2

    vb = v.astype(BF16)
    o = _dot((q * jnp.exp(b)).astype(BF16), S.astype(BF16)) + _dot(att.astype(BF16), vb)
    bl = b[C - 1:C, :]
    kk = k * jnp.exp(bl - b)
    ecol = jnp.transpose(jnp.broadcast_to(jnp.exp(bl), (LANES, LANES)))
    s_new = ecol * S + _dot(jnp.transpose(kk).astype(BF16), vb)
    return o, s_new


def _hgrn_prompt_kernel(q_ref, f_ref, i_ref, g_ref, lb_ref, gn_ref, y_ref, so_ref, s_ref, *, rows):
    @pl.when(pl.program_id(1) == 0)
    def _():
        s_ref[...] = jnp.zeros_like(s_ref)

    lb = lb_ref[...]
    gn = gn_ref[...]
    for c in range(rows // HGRN_CHUNK):
        sl = pl.ds(c * HGRN_CHUNK, HGRN_CHUNK)
        q, log_f, k = _hgrn_gates(q_ref[sl, :], f_ref[sl, :], lb)
        o, s_new = _hgrn_chunk(q, log_f, k, i_ref[sl, :], s_ref[...])
        s_ref[...] = s_new
        y_ref[sl, :] = _hgrn_out(o, g_ref[sl, :], gn).astype(BF16)

    @pl.when(pl.program_id(1) == pl.num_programs(1) - 1)
    def _():
        so_ref[...] = s_ref[...]


def hgrn_prompt(proj, lb, gn, rows):
    T = proj.shape[0]

    def col(off):
        base = off // LANES
        return pl.BlockSpec((rows, LANES), lambda h, c: (c, base + h))

    vec = pl.BlockSpec((1, LANES), lambda h, c: (0, h))
    return pl.pallas_call(
        functools.partial(_hgrn_prompt_kernel, rows=rows),
        grid=(A_HEADS, T // rows),
        in_specs=[col(COL_HQ), col(COL_HF), col(COL_HI), col(COL_HG), vec, vec],
        out_specs=[
            pl.BlockSpec((rows, LANES), lambda h, c: (c, h)),
            pl.BlockSpec((None, A_DK, LANES), lambda h, c: (h, 0, 0)),
        ],
        out_shape=[
            jax.ShapeDtypeStruct((T, MIX_W), BF16),
            jax.ShapeDtypeStruct((A_HEADS, A_DK, LANES), F32),
        ],
        scratch_shapes=[pltpu.VMEM((A_DK, LANES), F32)],
        compiler_params=_params("parallel", "arbitrary"),
        name="hgrn_prompt",
    )(proj, proj, proj, proj, lb, gn)


def _hgrn_sample_kernel(q_ref, f_ref, i_ref, g_ref, lb_ref, gn_ref, s_ref, y_ref, so_ref, *, steps, bb):
    lb = lb_ref[...]
    gn = gn_ref[...]
    qs, ks, vs, bs = [], [], [], []
    b = None
    for t in range(steps):
        q, log_f, k = _hgrn_gates(q_ref[t], f_ref[t], lb)
        b = log_f if b is None else b + log_f
        qs.append(q)
        ks.append(k)
        vs.append(i_ref[t])
        bs.append(b)
    intra = []
    for t in range(steps):
        acc = None
        for s in range(t + 1):
            w = jnp.sum(qs[t] * ks[s] * jnp.exp(bs[t] - bs[s]), axis=-1, keepdims=True)
            acc = w * vs[s] if acc is None else acc + w * vs[s]
        intra.append(acc)
    R = steps * bb
    q_stack = jnp.concatenate([qs[t] * jnp.exp(bs[t]) for t in range(steps)], axis=0).astype(BF16)
    k_stack = jnp.concatenate([ks[t] * jnp.exp(bs[-1] - bs[t]) for t in range(steps)], axis=0)
    v_stack = jnp.concatenate(vs, axis=0).astype(BF16)
    k_t = jnp.transpose(k_stack)
    f_pad = jnp.concatenate([jnp.exp(bs[-1])] + [jnp.zeros((R - bb, LANES), F32)], axis=0)
    f_t = jnp.transpose(f_pad)
    rowi = lax.broadcasted_iota(jnp.int32, (R, LANES), 0) % bb
    lanei = lax.broadcasted_iota(jnp.int32, (LANES, R), 1)

    def body(bi, o_acc):
        s_b = s_ref[bi]
        o_acc = jnp.where(rowi == bi, _dot(q_stack, s_b.astype(BF16)), o_acc)
        f_col = jnp.sum(jnp.where(lanei == bi, f_t, 0.0), axis=-1, keepdims=True)
        k_b = jnp.where(lanei % bb == bi, k_t, 0.0).astype(BF16)
        so_ref[bi] = f_col * s_b + _dot(k_b, v_stack)
        return o_acc

    o_inter = lax.fori_loop(0, bb, body, jnp.zeros((R, LANES), F32))
    for t in range(steps):
        o = o_inter[t * bb:(t + 1) * bb] + intra[t]
        y_ref[t] = _hgrn_out(o, g_ref[t], gn).astype(BF16)


def hgrn_sample(proj3, lb, gn, state, layer, bb):
    steps, B, _ = proj3.shape
    assert steps * bb == LANES

    def col(off):
        base = off // LANES
        return pl.BlockSpec((steps, bb, LANES), lambda g, h: (0, g, base + h))

    vec = pl.BlockSpec((1, LANES), lambda g, h: (0, h))
    return pl.pallas_call(
        functools.partial(_hgrn_sample_kernel, steps=steps, bb=bb),
        grid=(B // bb, A_HEADS),
        in_specs=[col(COL_HQ), col(COL_HF), col(COL_HI), col(COL_HG), vec, vec,
                  pl.BlockSpec((None, bb, None, A_DK, LANES), lambda g, h: (layer, g, h, 0, 0))],
        out_specs=[
            pl.BlockSpec((steps, bb, LANES), lambda g, h: (0, g, h)),
            pl.BlockSpec((bb, None, A_DK, LANES), lambda g, h: (g, h, 0, 0)),
        ],
        out_shape=[
            jax.ShapeDtypeStruct((steps, B, MIX_W), BF16),
            jax.ShapeDtypeStruct((B, A_HEADS, A_DK, LANES), F32),
        ],
        compiler_params=_params("parallel", "arbitrary"),
        name="hgrn_sample",
    )(proj3, proj3, proj3, proj3, lb, gn, state)


POOL_HIST = 16


def _pool_project(d, w_ref, sc_ref, g):
    sl = slice(g * POOL_GC, (g + 1) * POOL_GC)
    return _dot(d.astype(BF16), w_ref[g].astype(BF16)) * sc_ref[:, sl]


def _pool_prompt_kernel(u_ref, prev_ref, w_ref, sc_ref, y_ref, ext_ref, *, tm):
    i = pl.program_id(0)
    ext_ref[0:POOL_HIST, :] = jnp.where(i == 0, 0.0, prev_ref[...])
    ext_ref[POOL_HIST:, :] = u_ref[...]
    pos = i * tm + lax.broadcasted_iota(jnp.int32, (tm, 1), 0)
    for g, win in enumerate(POOL_WINDOWS):
        sl = slice(g * POOL_GC, (g + 1) * POOL_GC)
        acc = ext_ref[pl.ds(POOL_HIST, tm), sl]
        for j in range(1, win):
            acc = acc + ext_ref[pl.ds(POOL_HIST - j, tm), sl]
        cnt = jnp.minimum(pos + 1, win).astype(F32)
        d = acc / cnt - u_ref[:, sl]
        y_ref[:, sl] = _pool_project(d, w_ref, sc_ref, g).astype(BF16)


def pool_prompt(proj, pool_w, pool_scale, layer, tm):
    T = proj.shape[0]
    cb = COL_POOL // MIX_W
    return pl.pallas_call(
        functools.partial(_pool_prompt_kernel, tm=tm),
        grid=(T // tm,),
        in_specs=[
            pl.BlockSpec((tm, MIX_W), lambda i: (i, cb)),
            pl.BlockSpec((POOL_HIST, MIX_W), lambda i: (jnp.maximum(i * (tm // POOL_HIST) - 1, 0), cb)),
            pl.BlockSpec((None, len(POOL_WINDOWS), POOL_GC, POOL_GC), lambda i: (layer, 0, 0, 0)),
            pl.BlockSpec((None, 1, MIX_W), lambda i: (layer, 0, 0)),
        ],
        out_specs=pl.BlockSpec((tm, MIX_W), lambda i: (i, 0)),
        out_shape=jax.ShapeDtypeStruct((T, MIX_W), BF16),
        scratch_shapes=[pltpu.VMEM((POOL_HIST + tm, MIX_W), F32)],
        compiler_params=_params("arbitrary"),
        name="pool_prompt",
    )(proj, proj, pool_w, pool_scale)


def _pool_sample_kernel(u_ref, c_ref, w_ref, sc_ref, y_ref, *, steps):
    for t in range(steps):
        for g, win in enumerate(POOL_WINDOWS):
            sl = slice(g * POOL_GC, (g + 1) * POOL_GC)
            acc = u_ref[t, :, sl]
            for j in range(1, win):
                if j <= t:
                    acc = acc + u_ref[t - j, :, sl]
                else:
                    acc = acc + c_ref[:, POOL_BUF + t - j, sl]
            d = acc / float(win) - u_ref[t, :, sl]
            y_ref[t, :, sl] = _pool_project(d, w_ref, sc_ref, g).astype(BF16)


def pool_sample(proj3, cache_pool, pool_w, pool_scale, layer, bb):
    steps, B, _ = proj3.shape
    cb = COL_POOL // MIX_W
    return pl.pallas_call(
        functools.partial(_pool_sample_kernel, steps=steps),
        grid=(B // bb,),
        in_specs=[
            pl.BlockSpec((steps, bb, MIX_W), lambda g: (0, g, cb)),
            pl.BlockSpec((None, bb, POOL_BUF, MIX_W), lambda g: (layer, g, 0, 0)),
            pl.BlockSpec((None, len(POOL_WINDOWS), POOL_GC, POOL_GC), lambda g: (layer, 0, 0, 0)),
            pl.BlockSpec((None, 1, MIX_W), lambda g: (layer, 0, 0)),
        ],
        out_specs=pl.BlockSpec((steps, bb, MIX_W), lambda g: (0, g, 0)),
        out_shape=jax.ShapeDtypeStruct((steps, B, MIX_W), BF16),
        compiler_params=_params("arbitrary"),
        name="pool_sample",
    )(proj3, cache_pool, pool_w, pool_scale)


def _rope_tables(positions):
    half = ROT_DIM // 2
    inv = np.power(ROPE_THETA, -np.arange(0, ROT_DIM, 2, dtype=np.float64) / ROT_DIM)
    ang = np.asarray(positions, np.float64)[:, None] * inv[None, :]
    cos, sin = np.cos(ang), np.sin(ang)
    n = len(positions)
    ct = np.ones((n, LANES))
    sn = np.zeros((n, LANES))
    sp = np.zeros((n, LANES))
    for base in (0, SWA_HEAD_DIM):
        ct[:, base:base + half] = cos
        ct[:, base + half:base + ROT_DIM] = cos
        sn[:, base:base + half] = -sin
        sp[:, base + half:base + ROT_DIM] = sin
    return tuple(jnp.asarray(t, F32) for t in (ct, sn, sp))


def _head_norm_rope(x, g, ct, sn, sp):
    lane = lax.broadcasted_iota(jnp.int32, x.shape, 1)
    lo = lane < SWA_HEAD_DIM
    x2 = x * x
    ms_lo = jnp.sum(jnp.where(lo, x2, 0.0), axis=-1, keepdims=True) / SWA_HEAD_DIM
    ms_hi = jnp.sum(jnp.where(lo, 0.0, x2), axis=-1, keepdims=True) / SWA_HEAD_DIM
    xn = x * jnp.where(lo, lax.rsqrt(ms_lo + EPS), lax.rsqrt(ms_hi + EPS)) * g
    half = ROT_DIM // 2
    return xn * ct + pltpu.roll(xn, LANES - half, 1) * sn + pltpu.roll(xn, half, 1) * sp


def _kprep_kernel(k_ref, g_ref, ct_ref, sn_ref, sp_ref, o_ref):
    ct, sn, sp = ct_ref[...], sn_ref[...], sp_ref[...]
    for j in range(2):
        sl = slice(j * LANES, (j + 1) * LANES)
        o_ref[:, sl] = _head_norm_rope(k_ref[:, sl], g_ref[...], ct, sn, sp)


def swa_kprep(proj, g2, tables, tm):
    T = proj.shape[0]
    kw = SWA_KV_HEADS * SWA_HEAD_DIM
    tab = pl.BlockSpec((tm, LANES), lambda i: (i, 0))
    return pl.pallas_call(
        _kprep_kernel,
        grid=(T // tm,),
        in_specs=[pl.BlockSpec((tm, kw), lambda i: (i, COL_SK // kw)),
                  pl.BlockSpec((1, LANES), lambda i: (0, 0)), tab, tab, tab],
        out_specs=pl.BlockSpec((tm, kw), lambda i: (i, 0)),
        out_shape=jax.ShapeDtypeStruct((T, kw), F32),
        compiler_params=_params("arbitrary"),
        name="swa_kprep",
    )(proj, g2, *tables)


def _dup_head(x, kvh):
    lane = lax.broadcasted_iota(jnp.int32, x.shape, 1)
    keep = (lane < SWA_HEAD_DIM) if kvh % 2 == 0 else (lane >= SWA_HEAD_DIM)
    return jnp.where(keep, x, pltpu.roll(x, SWA_HEAD_DIM, 1))


def _stack_heads(q):
    lane = lax.broadcasted_iota(jnp.int32, q.shape, 1)
    lo = lane < SWA_HEAD_DIM
    return jnp.concatenate([jnp.where(lo, q, 0.0), jnp.where(lo, 0.0, q)], axis=0)


def _unstack_heads(o2):
    R = o2.shape[0] // 2
    lane = lax.broadcasted_iota(jnp.int32, (R, LANES), 1)
    return jnp.where(lane < SWA_HEAD_DIM, o2[:R], o2[R:])


def _swa_prompt_kernel(sink_ref, q_ref, kc_ref, kp_ref, vc_ref, vp_ref, g_ref, ct_ref, sn_ref, sp_ref, y_ref,
                       *, layer):
    i = pl.program_id(0)
    W = WINDOW
    ct, sn, sp = ct_ref[...], sn_ref[...], sp_ref[...]
    qi = lax.broadcasted_iota(jnp.int32, (2 * W, 2 * W), 0) % W
    ci = lax.broadcasted_iota(jnp.int32, (2 * W, 2 * W), 1)
    valid = (ci > qi) & (ci <= qi + W) & ((i > 0) | (ci >= W))
    top = lax.broadcasted_iota(jnp.int32, (2 * W, 1), 0) < W
    scale = SWA_HEAD_DIM ** -0.5
    for kvh in range(SWA_KV_HEADS):
        ksl = slice((kvh // 2) * LANES, (kvh // 2 + 1) * LANES)
        k_dup = _dup_head(jnp.concatenate([kp_ref[:, ksl], kc_ref[:, ksl]], axis=0), kvh).astype(BF16)
        v_dup = _dup_head(jnp.concatenate([vp_ref[:, ksl], vc_ref[:, ksl]], axis=0), kvh).astype(BF16)
        for jj in range(2):
            j = 2 * kvh + jj
            qsl = slice(j * LANES, (j + 1) * LANES)
            q = _head_norm_rope(q_ref[:, qsl], g_ref[...], ct, sn, sp) * scale
            s = _dot_nt(_stack_heads(q).astype(BF16), k_dup)
            sink = jnp.where(top, sink_ref[layer, 2 * j], sink_ref[layer, 2 * j + 1])
            s = jnp.where(valid, s, NEG_BIG)
            m = jnp.maximum(jnp.max(s, axis=-1, keepdims=True), sink)
            e = jnp.where(valid, jnp.exp(s - m), 0.0)
            p = e / (jnp.sum(e, axis=-1, keepdims=True) + jnp.exp(sink - m))
            y_ref[:, qsl] = _unstack_heads(_dot(p.astype(BF16), v_dup)).astype(BF16)


def swa_prompt(proj, khat, sinks, g2, tables, layer):
    T = proj.shape[0]
    W = WINDOW
    kw = SWA_KV_HEADS * SWA_HEAD_DIM
    tab = pl.BlockSpec((W, LANES), lambda i: (i, 0))
    prev = lambda i: jnp.maximum(i - 1, 0)
    return pl.pallas_call(
        functools.partial(_swa_prompt_kernel, layer=layer),
        grid=(T // W,),
        in_specs=[
            pl.BlockSpec(memory_space=pltpu.SMEM),
            pl.BlockSpec((W, MIX_W), lambda i: (i, COL_SQ // MIX_W)),
            pl.BlockSpec((W, kw), lambda i: (i, 0)),
            pl.BlockSpec((W, kw), lambda i: (prev(i), 0)),
            pl.BlockSpec((W, kw), lambda i: (i, COL_SV // kw)),
            pl.BlockSpec((W, kw), lambda i: (prev(i), COL_SV // kw)),
            pl.BlockSpec((1, LANES), lambda i: (0, 0)), tab, tab, tab,
        ],
        out_specs=pl.BlockSpec((W, MIX_W), lambda i: (i, 0)),
        out_shape=jax.ShapeDtypeStruct((T, MIX_W), BF16),
        compiler_params=_params("arbitrary"),
        name="swa_prompt",
    )(sinks, proj, khat, khat, proj, proj, g2, *tables)


def _swa_sample_kernel(sink_ref, q_ref, kn_ref, vn_ref, kc_ref, vc_ref, g_ref, ct_ref, sn_ref, sp_ref, y_ref,
                       *, layer, steps, bb):
    W = WINDOW
    R = steps * bb
    ct, sn, sp = ct_ref[...], sn_ref[...], sp_ref[...]
    scale = SWA_HEAD_DIM ** -0.5
    r2 = lax.broadcasted_iota(jnp.int32, (2 * R, 1), 0)
    top = r2 < R
    tq = (r2 % R) // bb
    bq = r2 % bb
    c_new = lax.broadcasted_iota(jnp.int32, (1, R), 1)
    valid_new = (c_new % bb == bq) & (c_new // bb <= tq)
    c_old = lax.broadcasted_iota(jnp.int32, (1, W), 1)
    valid_old = c_old > tq
    for kvh in range(SWA_KV_HEADS):
        ksl = slice((kvh // 2) * LANES, (kvh // 2 + 1) * LANES)
        kn = _dup_head(jnp.concatenate([kn_ref[t, :, ksl] for t in range(steps)], axis=0), kvh).astype(BF16)
        vn = _dup_head(jnp.concatenate([vn_ref[t, :, ksl] for t in range(steps)], axis=0), kvh).astype(BF16)
        for jj in range(2):
            j = 2 * kvh + jj
            qsl = slice(j * LANES, (j + 1) * LANES)
            q = jnp.concatenate([q_ref[t, :, qsl] for t in range(steps)], axis=0)
            q = _head_norm_rope(q, g_ref[...], ct, sn, sp) * scale
            qs = _stack_heads(q).astype(BF16)
            s_new = jnp.where(valid_new, _dot_nt(qs, kn), NEG_BIG)

            def old_scores(bi, acc):
                k_b = _dup_head(kc_ref[bi, :, ksl], kvh).astype(BF16)
                return jnp.where(bq == bi, _dot_nt(qs, k_b), acc)

            s_old = lax.fori_loop(0, bb, old_scores, jnp.zeros((2 * R, W), F32))
            s_old = jnp.where(valid_old, s_old, NEG_BIG)
            sink = jnp.where(top, sink_ref[layer, 2 * j], sink_ref[layer, 2 * j + 1])
            m = jnp.maximum(jnp.maximum(jnp.max(s_new, axis=-1, keepdims=True),
                                        jnp.max(s_old, axis=-1, keepdims=True)), sink)
            e_new = jnp.where(valid_new, jnp.exp(s_new - m), 0.0)
            e_old = jnp.where(valid_old, jnp.exp(s_old - m), 0.0)
            den = (jnp.sum(e_new, axis=-1, keepdims=True) + jnp.sum(e_old, axis=-1, keepdims=True)
                   + jnp.exp(sink - m))
            p_new = (e_new / den).astype(BF16)
            p_old = e_old / den

            def old_values(bi, acc):
                v_b = _dup_head(vc_ref[bi, :, ksl], kvh).astype(BF16)
                return acc + _dot(jnp.where(bq == bi, p_old, 0.0).astype(BF16), v_b)

            o2 = lax.fori_loop(0, bb, old_values, _dot(p_new, vn))
            o = _unstack_heads(o2)
            for t in range(steps):
                y_ref[t, :, qsl] = o[t * bb:(t + 1) * bb]


def swa_sample(proj3, khat3, cache_k, cache_v, sinks, g2, tables, layer, bb):
    steps, B, _ = proj3.shape
    kw = SWA_KV_HEADS * SWA_HEAD_DIM
    R = steps * bb
    tab = pl.BlockSpec((R, LANES), lambda g: (0, 0))
    cache = pl.BlockSpec((None, bb, WINDOW, kw), lambda g: (layer, g, 0, 0))
    return pl.pallas_call(
        functools.partial(_swa_sample_kernel, layer=layer, steps=steps, bb=bb),
        grid=(B // bb,),
        in_specs=[
            pl.BlockSpec(memory_space=pltpu.SMEM),
            pl.BlockSpec((steps, bb, MIX_W), lambda g: (0, g, COL_SQ // MIX_W)),
            pl.BlockSpec((steps, bb, kw), lambda g: (0, g, 0)),
            pl.BlockSpec((steps, bb, kw), lambda g: (0, g, COL_SV // kw)),
            cache, cache,
            pl.BlockSpec((1, LANES), lambda g: (0, 0)), tab, tab, tab,
        ],
        out_specs=pl.BlockSpec((steps, bb, MIX_W), lambda g: (0, g, 0)),
        out_shape=jax.ShapeDtypeStruct((steps, B, MIX_W), F32),
        compiler_params=_params("arbitrary"),
        name="swa_sample",
    )(sinks, proj3, khat3, proj3, cache_k, cache_v, g2, *tables)


def _mem_kv_kernel(x_ref, g_ref, w_ref, kg_ref, o_ref, xn_ref):
    j = pl.program_id(0)

    @pl.when(j == 0)
    def _():
        x = x_ref[...]
        ms = jnp.mean(x * x, axis=-1, keepdims=True)
        xn_ref[...] = (x * lax.rsqrt(ms + EPS) * g_ref[...]).astype(BF16)

    y = _dot(xn_ref[...], w_ref[...].astype(BF16))

    @pl.when(j < MEM_HEADS)
    def _():
        ms = jnp.mean(y * y, axis=-1, keepdims=True)
        o_ref[...] = y * lax.rsqrt(ms + EPS) * kg_ref[...]

    @pl.when(j >= MEM_HEADS)
    def _():
        o_ref[...] = y


def mem_kv(mem, mem_norm_g, w_mem_kv, mem_knorm_g, layer):
    M, K = mem.shape
    hd = MEM_HEAD_DIM
    return pl.pallas_call(
        _mem_kv_kernel,
        grid=(2 * MEM_HEADS,),
        in_specs=[
            pl.BlockSpec((M, K), lambda j: (0, 0)),
            pl.BlockSpec((None, 1, K), lambda j: (layer, 0, 0)),
            pl.BlockSpec((None, K, hd), lambda j: (layer, 0, j)),
            pl.BlockSpec((None, 1, hd), lambda j: (layer, 0, 0)),
        ],
        out_specs=pl.BlockSpec((M, hd), lambda j: (0, j)),
        out_shape=jax.ShapeDtypeStruct((M, 2 * MIX_W), F32),
        scratch_shapes=[pltpu.VMEM((M, K), BF16)],
        compiler_params=_params("arbitrary"),
        name="mem_kv",
    )(mem, mem_norm_g, w_mem_kv, mem_knorm_g)


def _mem_qnorm(q, g):
    ms = jnp.mean(q * q, axis=-1, keepdims=True)
    return q * lax.rsqrt(ms + EPS) * g * (MEM_HEAD_DIM ** -0.5)


def _softmax_rows(s):
    m = jnp.max(s, axis=-1, keepdims=True)
    e = jnp.exp(s - m)
    return e / jnp.sum(e, axis=-1, keepdims=True)


def _mem_prompt_kernel(q_ref, k_ref, v_ref, g_ref, y_ref):
    q = _mem_qnorm(q_ref[...], g_ref[...]).astype(BF16)
    p = _softmax_rows(_dot_nt(q, k_ref[...].astype(BF16)))
    y_ref[...] = _dot(p.astype(BF16), v_ref[...].astype(BF16)).astype(BF16)


def mem_attn_prompt(proj, kv, mem_qnorm_g, layer, tq):
    T = proj.shape[0]
    hd = MEM_HEAD_DIM
    return pl.pallas_call(
        _mem_prompt_kernel,
        grid=(T // tq, MEM_HEADS),
        in_specs=[
            pl.BlockSpec((tq, hd), lambda i, h: (i, COL_MQ // hd + h)),
            pl.BlockSpec((N_MEM, hd), lambda i, h: (0, h)),
            pl.BlockSpec((N_MEM, hd), lambda i, h: (0, MEM_HEADS + h)),
            pl.BlockSpec((None, 1, hd), lambda i, h: (layer, 0, 0)),
        ],
        out_specs=pl.BlockSpec((tq, hd), lambda i, h: (i, h)),
        out_shape=jax.ShapeDtypeStruct((T, MIX_W), BF16),
        compiler_params=_params("parallel", "arbitrary"),
        name="mem_prompt",
    )(proj, kv, kv, mem_qnorm_g)


def _mem_sample_kernel(q0_ref, q1_ref, q2_ref, q3_ref, k_ref, v_ref, g_ref, y_ref, *, steps, bb):
    R = steps * bb
    hd = MEM_HEAD_DIM
    bq = lax.broadcasted_iota(jnp.int32, (R, 1), 0) % bb
    for h, q_ref in enumerate((q0_ref, q1_ref, q2_ref, q3_ref)):
        sl = slice(h * hd, (h + 1) * hd)
        q = jnp.concatenate([q_ref[t] for t in range(steps)], axis=0)
        q = _mem_qnorm(q, g_ref[...]).astype(BF16)

        def scores(bi, acc):
            return jnp.where(bq == bi, _dot_nt(q, k_ref[bi, :, sl].astype(BF16)), acc)

        p = _softmax_rows(lax.fori_loop(0, bb, scores, jnp.zeros((R, N_MEM), F32)))

        def values(bi, acc):
            return acc + _dot(jnp.where(bq == bi, p, 0.0).astype(BF16), v_ref[bi, :, sl].astype(BF16))

        o = lax.fori_loop(0, bb, values, jnp.zeros((R, hd), F32))
        for t in range(steps):
            y_ref[t, :, sl] = o[t * bb:(t + 1) * bb]


def mem_attn_sample(proj3, cache_k, cache_v, mem_qnorm_g, layer, bb):
    steps, B, _ = proj3.shape
    hd = MEM_HEAD_DIM
    cache = pl.BlockSpec((None, bb, N_MEM, MIX_W), lambda g: (layer, g, 0, 0))

    def q_spec(h):
        cb = COL_MQ // hd + h
        return pl.BlockSpec((steps, bb, hd), lambda g: (0, g, cb))

    return pl.pallas_call(
        functools.partial(_mem_sample_kernel, steps=steps, bb=bb),
        grid=(B // bb,),
        in_specs=[q_spec(h) for h in range(MEM_HEADS)] + [
            cache, cache,
            pl.BlockSpec((None, 1, hd), lambda g: (layer, 0, 0)),
        ],
        out_specs=pl.BlockSpec((steps, bb, MIX_W), lambda g: (0, g, 0)),
        out_shape=jax.ShapeDtypeStruct((steps, B, MIX_W), F32),
        compiler_params=_params("arbitrary"),
        name="mem_sample",
    )(proj3, proj3, proj3, proj3, cache_k, cache_v, mem_qnorm_g)


def _gelu(x):
    return 0.5 * x * (1.0 + lax.erf(x * (2.0 ** -0.5)))


CONV_HIST = 8


def _conv_prompt_kernel(a_ref, prev_ref, v_ref, cw_ref, cb_ref, o_ref, ext_ref, *, tm):
    i = pl.program_id(0)
    ext_ref[0:CONV_HIST, :] = jnp.where(i == 0, 0.0, prev_ref[...])
    ext_ref[CONV_HIST:, :] = a_ref[...]
    c = (cb_ref[...] + cw_ref[0:1, :] * ext_ref[pl.ds(CONV_HIST - 2, tm), :]
         + cw_ref[1:2, :] * ext_ref[pl.ds(CONV_HIST - 1, tm), :] + cw_ref[2:3, :] * a_ref[...])
    o_ref[...] = (_gelu(c) * v_ref[...]).astype(BF16)


def conv_gate_prompt(up, conv_w, conv_b, layer, tm, tc):
    T = up.shape[0]
    nc = D_FF // tc
    return pl.pallas_call(
        functools.partial(_conv_prompt_kernel, tm=tm),
        grid=(T // tm, nc),
        in_specs=[
            pl.BlockSpec((tm, tc), lambda i, j: (i, j)),
            pl.BlockSpec((CONV_HIST, tc), lambda i, j: (jnp.maximum(i * (tm // CONV_HIST) - 1, 0), j)),
            pl.BlockSpec((tm, tc), lambda i, j: (i, nc + j)),
            pl.BlockSpec((None, 3, tc), lambda i, j: (layer, 0, j)),
            pl.BlockSpec((None, 1, tc), lambda i, j: (layer, 0, j)),
        ],
        out_specs=pl.BlockSpec((tm, tc), lambda i, j: (i, j)),
        out_shape=jax.ShapeDtypeStruct((T, D_FF), BF16),
        scratch_shapes=[pltpu.VMEM((CONV_HIST + tm, tc), F32)],
        compiler_params=_params("parallel", "arbitrary"),
        name="conv_gate_prompt",
    )(up, up, up, conv_w, conv_b)


def _conv_sample_kernel(a_ref, v_ref, st_ref, cw_ref, cb_ref, o_ref, *, steps):
    hist = [st_ref[:, 0, :], st_ref[:, 1, :]] + [a_ref[t] for t in range(steps)]
    for t in range(steps):
        c = cb_ref[...] + cw_ref[0:1, :] * hist[t] + cw_ref[1:2, :] * hist[t + 1] + cw_ref[2:3, :] * hist[t + 2]
        o_ref[t] = (_gelu(c) * v_ref[t]).astype(BF16)


def conv_gate_sample(up3, state_conv, conv_w, conv_b, layer, tc):
    steps, B, _ = up3.shape
    nc = D_FF // tc
    return pl.pallas_call(
        functools.partial(_conv_sample_kernel, steps=steps),
        grid=(nc,),
        in_specs=[
            pl.BlockSpec((steps, B, tc), lambda j: (0, 0, j)),
            pl.BlockSpec((steps, B, tc), lambda j: (0, 0, nc + j)),
            pl.BlockSpec((None, B, 2, tc), lambda j: (layer, 0, 0, j)),
            pl.BlockSpec((None, 3, tc), lambda j: (layer, 0, j)),
            pl.BlockSpec((None, 1, tc), lambda j: (layer, 0, j)),
        ],
        out_specs=pl.BlockSpec((steps, B, tc), lambda j: (0, 0, j)),
        out_shape=jax.ShapeDtypeStruct((steps, B, D_FF), BF16),
        compiler_params=_params("arbitrary"),
        name="conv_gate_sample",
    )(up3, up3, state_conv, conv_w, conv_b)


def _tile_sizes(T):
    return 1024 if T % 1024 == 0 else 512 if T % 512 == 0 else 256


def _token_tail(x, proj, ys, layer, w_branch, w_o, norm2_g, w_up, tm):
    merged = merge_branches(ys, proj, w_branch, layer, tm, 256)
    h = matmul_res(merged, w_o, layer, x, tm, 512)
    up = norm_matmul(h, norm2_g, w_up, layer, tm, 512)
    return h, up


def kernel(x_prompt, x_sample, mem_prompt, state_hgrn, cache_pool, cache_swa_k, cache_swa_v, state_conv, cache_mem_k, cache_mem_v, norm1_g, w_in, hgrn_lb, hgrn_norm_g, pool_w, pool_scale, swa_qnorm_g, swa_knorm_g, swa_sinks, mem_norm_g, w_mem_kv, mem_qnorm_g, mem_knorm_g, w_branch, w_o, norm2_g, w_up, conv_w, conv_b, w_down):
    depth = w_in.shape[0]
    bp, L, _ = x_prompt.shape
    B, steps, _ = x_sample.shape
    assert bp == 1
    kw = SWA_KV_HEADS * SWA_HEAD_DIM

    lb_all = jnp.cumsum(jax.nn.softmax(hgrn_lb.astype(F32), axis=0), axis=0)
    lb_all = lb_all - lb_all[:1]

    tab_p = _rope_tables(np.arange(L))
    tab_s = _rope_tables(np.repeat(PAST_LEN + np.arange(steps), B))
    swa_bb = 8
    tab_sb = _rope_tables(np.repeat(PAST_LEN + np.arange(steps), swa_bb))

    xp = x_prompt.reshape(L, D_MODEL)
    xs = jnp.transpose(x_sample, (1, 0, 2)).reshape(steps * B, D_MODEL)
    mem = mem_prompt.reshape(N_MEM, D_MODEL)
    ck_all = cache_swa_k.reshape(depth, B, WINDOW, kw)
    cv_all = cache_swa_v.reshape(depth, B, WINDOW, kw)
    mk_all = cache_mem_k.reshape(depth, B, N_MEM, MIX_W)
    mv_all = cache_mem_v.reshape(depth, B, N_MEM, MIX_W)
    tm_p = _tile_sizes(L)
    tm_s = _tile_sizes(steps * B)
    row3 = lambda a: a.reshape(depth, 1, a.shape[-1])
    norm1_g, norm2_g, pool_scale, conv_b = row3(norm1_g), row3(norm2_g), row3(pool_scale), row3(conv_b)
    mem_norm_g, mem_qnorm_g, mem_knorm_g = row3(mem_norm_g), row3(mem_qnorm_g), row3(mem_knorm_g)

    outs = {k: [] for k in ("sp", "ss", "pp", "ps", "kp", "ks", "vp", "vs", "cp", "cs", "mk", "mv")}
    for l in range(depth):
        lb = lb_all[l].reshape(1, MIX_W)
        gn = hgrn_norm_g[l].reshape(1, MIX_W)
        gq2 = jnp.tile(swa_qnorm_g[l], 2).reshape(1, LANES)
        gk2 = jnp.tile(swa_knorm_g[l], 2).reshape(1, LANES)

        kv = mem_kv(mem, mem_norm_g, w_mem_kv, mem_knorm_g, l)

        proj = norm_matmul(xp, norm1_g, w_in, l, tm_p, 512)
        ya, s_p = hgrn_prompt(proj, lb, gn, 512)
        yb = pool_prompt(proj, pool_w, pool_scale, l, 512)
        khat = swa_kprep(proj, gk2, tab_p, 512)
        yc = swa_prompt(proj, khat, swa_sinks, gq2, tab_p, l)
        ym = mem_attn_prompt(proj, kv, mem_qnorm_g, l, 512)
        h, up = _token_tail(xp, proj, (ya, yb, yc, ym), l, w_branch, w_o, norm2_g, w_up, tm_p)
        gact = conv_gate_prompt(up, conv_w, conv_b, l, 512, 512)
        xp_new = matmul_res(gact, w_down, l, h, tm_p, 256)

        outs["sp"].append(s_p[None])
        outs["pp"].append(proj[None, L - POOL_BUF:, COL_POOL:COL_POOL + MIX_W])
        outs["kp"].append(khat[None, L - WINDOW:].reshape(1, WINDOW, SWA_KV_HEADS, SWA_HEAD_DIM))
        outs["vp"].append(proj[None, L - WINDOW:, COL_SV:COL_SV + kw].reshape(1, WINDOW, SWA_KV_HEADS, SWA_HEAD_DIM))
        outs["cp"].append(up[None, L - 2:, :D_FF])
        outs["mk"].append(kv[None, :, :MIX_W].reshape(1, N_MEM, MEM_HEADS, MEM_HEAD_DIM))
        outs["mv"].append(kv[None, :, MIX_W:].reshape(1, N_MEM, MEM_HEADS, MEM_HEAD_DIM))
        xp = xp_new

        proj_s = norm_matmul(xs, norm1_g, w_in, l, tm_s, 512)
        proj3 = proj_s.reshape(steps, B, IN_COLS)
        ya, s_s = hgrn_sample(proj3, lb, gn, state_hgrn, l, LANES // steps)
        yb = pool_sample(proj3, cache_pool, pool_w, pool_scale, l, 64)
        khat_s = swa_kprep(proj_s, gk2, tab_s, tm_s)
        khat3 = khat_s.reshape(steps, B, kw)
        yc = swa_sample(proj3, khat3, ck_all, cv_all, swa_sinks, gq2, tab_sb, l, swa_bb)
        ym = mem_attn_sample(proj3, mk_all, mv_all, mem_qnorm_g, l, 8)
        ys = tuple(y.reshape(steps * B, MIX_W).astype(BF16) for y in (ya, yb, yc, ym))
        h, up = _token_tail(xs, proj_s, ys, l, w_branch, w_o, norm2_g, w_up, tm_s)
        up3 = up.reshape(steps, B, 2 * D_FF)
        gact = conv_gate_sample(up3, state_conv, conv_w, conv_b, l, 512)
        xs_new = matmul_res(gact.reshape(steps * B, D_FF), w_down, l, h, tm_s, 256)

        u_new = jnp.transpose(proj3[:, :, COL_POOL:COL_POOL + MIX_W], (1, 0, 2))
        outs["ss"].append(s_s)
        outs["ps"].append(jnp.concatenate([cache_pool[l], u_new], axis=1)[:, -POOL_BUF:])
        k_new = jnp.transpose(khat3, (1, 0, 2)).reshape(B, steps, SWA_KV_HEADS, SWA_HEAD_DIM)
        v_new = jnp.transpose(proj3[:, :, COL_SV:COL_SV + kw], (1, 0, 2)).reshape(B, steps, SWA_KV_HEADS, SWA_HEAD_DIM)
        outs["ks"].append(jnp.concatenate([cache_swa_k[l], k_new], axis=1)[:, -WINDOW:])
        outs["vs"].append(jnp.concatenate([cache_swa_v[l], v_new], axis=1)[:, -WINDOW:])
        a_new = jnp.transpose(up3[:, :, :D_FF], (1, 0, 2))
        outs["cs"].append(jnp.concatenate([state_conv[l], a_new], axis=1)[:, -2:])
        xs = xs_new

    stk = lambda k: jnp.stack(outs[k], axis=0)
    y_prompt = xp.reshape(1, L, D_MODEL)
    y_sample = jnp.transpose(xs.reshape(steps, B, D_MODEL), (1, 0, 2))
    return (y_prompt, y_sample,
            stk("sp"), stk("ss"), stk("pp"), stk("ps"), stk("kp"), stk("ks"), stk("vp"), stk("vs"),
            stk("cp"), stk("cs"), jnp.concatenate(outs["mk"], axis=0)[:, None], jnp.concatenate(outs["mv"], axis=0)[:, None])
```

```python
import functools

import numpy as np
import jax
import jax.numpy as jnp
from jax import lax
from jax.experimental import pallas as pl
from jax.experimental.pallas import tpu as pltpu

F32 = jnp.float32
BF16 = jnp.bfloat16

D_MODEL = 2048
MIX_W = D_MODEL // 2
N_BRANCH = 4
A_DK = 128
A_HEADS = MIX_W // A_DK
POOL_WINDOWS = (2, 4, 8, 16)
POOL_GC = MIX_W // len(POOL_WINDOWS)
POOL_BUF = max(POOL_WINDOWS) - 1
SWA_HEAD_DIM = 64
SWA_Q_HEADS = MIX_W // SWA_HEAD_DIM
SWA_KV_HEADS = SWA_Q_HEADS // 4
WINDOW = 128
ROT_DIM = SWA_HEAD_DIM // 4
ROPE_THETA = 500000.0
N_MEM = 256
MEM_HEADS = 4
MEM_HEAD_DIM = MIX_W // MEM_HEADS
D_FF = 11 * D_MODEL // 4
EPS = 1e-6
PAST_LEN = 8192

COL_HQ, COL_HF, COL_HI, COL_HG = 0, MIX_W, 2 * MIX_W, 3 * MIX_W
COL_POOL = 4 * MIX_W
COL_SQ = 5 * MIX_W
COL_SK = 6 * MIX_W
COL_SV = COL_SK + SWA_KV_HEADS * SWA_HEAD_DIM
COL_MQ = COL_SV + SWA_KV_HEADS * SWA_HEAD_DIM
COL_GATE = COL_MQ + MIX_W
IN_COLS = COL_GATE + N_BRANCH * D_MODEL

LANES = 128
HGRN_CHUNK = 128
VMEM_LIMIT = 56 * 1024 * 1024
NEG_BIG = -1e30


def _params(*sem):
    return pltpu.CompilerParams(dimension_semantics=sem, vmem_limit_bytes=VMEM_LIMIT)


def _sigmoid(x):
    return 1.0 / (1.0 + jnp.exp(-x))


def _dot(a, b):
    return jnp.dot(a, b, preferred_element_type=F32)


def _dot_nt(a, b):
    return lax.dot_general(a, b, (((1,), (1,)), ((), ())), preferred_element_type=F32)


def _skip_ref(kernel_fn, idx):
    def wrapped(*refs):
        return kernel_fn(*refs[:idx], *refs[idx + 1:])
    return wrapped


def _layer_slab_call(kernel_fn, in_specs, args, slab_out, **kw):
    if slab_out is not None:
        idx = len(args)
        in_specs = list(in_specs) + [pl.BlockSpec(memory_space=pl.ANY)]
        args = list(args) + [slab_out]
        kernel_fn = _skip_ref(kernel_fn, idx)
        kw["input_output_aliases"] = {idx: len(kw["out_shape"]) - 1}
    return pl.pallas_call(kernel_fn, in_specs=in_specs, **kw)(*args)


def _rms_rows(x, g):
    ms = jnp.mean(x * x, axis=-1, keepdims=True)
    return x * lax.rsqrt(ms + EPS) * g


def _prenorm_kernel(x_ref, g_ref, o_ref):
    o_ref[...] = _rms_rows(x_ref[...], g_ref[...]).astype(BF16)


def prenorm(x, g, layer, tm):
    T, K = x.shape
    return pl.pallas_call(
        _prenorm_kernel,
        grid=(T // tm,),
        in_specs=[pl.BlockSpec((tm, K), lambda i: (i, 0)),
                  pl.BlockSpec((None, 1, K), lambda i: (layer, 0, 0))],
        out_specs=pl.BlockSpec((tm, K), lambda i: (i, 0)),
        out_shape=jax.ShapeDtypeStruct((T, K), BF16),
        compiler_params=_params("arbitrary"),
        name="prenorm",
    )(x, g)


def _matmul_kernel(a_ref, w_ref, o_ref):
    o_ref[...] = _dot(a_ref[...], w_ref[...].astype(BF16))


def matmul_cols(a, w, layer, n_cols, tm, tn):
    T, K = a.shape
    return pl.pallas_call(
        _matmul_kernel,
        grid=(T // tm, n_cols // tn),
        in_specs=[
            pl.BlockSpec((tm, K), lambda i, j: (i, 0)),
            pl.BlockSpec((None, K, tn), lambda i, j: (layer, 0, j)),
        ],
        out_specs=pl.BlockSpec((tm, tn), lambda i, j: (i, j)),
        out_shape=jax.ShapeDtypeStruct((T, n_cols), F32),
        compiler_params=_params("parallel", "arbitrary"),
        name="matmul_cols",
    )(a, w)


def _matmul_res_kernel(a_ref, w_ref, r_ref, o_ref):
    o_ref[...] = r_ref[...] + _dot(a_ref[...], w_ref[...].astype(BF16))


def matmul_res(a, w, layer, res, tm, tn):
    T, K = a.shape
    N = w.shape[2]
    return pl.pallas_call(
        _matmul_res_kernel,
        grid=(T // tm, N // tn),
        in_specs=[
            pl.BlockSpec((tm, K), lambda i, j: (i, 0)),
            pl.BlockSpec((None, K, tn), lambda i, j: (layer, 0, j)),
            pl.BlockSpec((tm, tn), lambda i, j: (i, j)),
        ],
        out_specs=pl.BlockSpec((tm, tn), lambda i, j: (i, j)),
        out_shape=jax.ShapeDtypeStruct((T, N), F32),
        compiler_params=_params("parallel", "arbitrary"),
        name="matmul_res",
    )(a, w, res)


def _merge_kernel(xn_ref, ya_ref, yb_ref, yc_ref, ym_ref, wg0_ref, wg1_ref, wg2_ref, wg3_ref, wb_ref, o_ref,
                  wg_s, wb_s):
    @pl.when(pl.program_id(1) == 0)
    def _():
        for n, wg_ref in enumerate((wg0_ref, wg1_ref, wg2_ref, wg3_ref)):
            wg_s[n] = wg_ref[...].astype(BF16)
            wb_s[n] = wb_ref[n].astype(BF16)

    xn = xn_ref[...]
    acc = None
    for n, y_ref in enumerate((ya_ref, yb_ref, yc_ref, ym_ref)):
        t = _sigmoid(_dot(xn, wg_s[n])) * _dot(y_ref[...], wb_s[n])
        acc = t if acc is None else acc + t
    o_ref[...] = acc.astype(BF16)


def merge_branches(xn, ys, w_in, w_branch, layer, tm, tn):
    T = xn.shape[0]
    once = pl.Buffered(1)
    y_spec = pl.BlockSpec((tm, MIX_W), lambda j, i: (i, 0))

    def gate_spec(n):
        off = (COL_GATE + n * D_MODEL) // tn
        return pl.BlockSpec((None, D_MODEL, tn), lambda j, i: (layer, 0, off + j), pipeline_mode=once)

    return pl.pallas_call(
        _merge_kernel,
        grid=(D_MODEL // tn, T // tm),
        in_specs=[pl.BlockSpec((tm, D_MODEL), lambda j, i: (i, 0))] + [y_spec] * 4
        + [gate_spec(n) for n in range(N_BRANCH)]
        + [pl.BlockSpec((None, N_BRANCH, MIX_W, tn), lambda j, i: (layer, 0, 0, j), pipeline_mode=once)],
        out_specs=pl.BlockSpec((tm, tn), lambda j, i: (i, j)),
        out_shape=jax.ShapeDtypeStruct((T, D_MODEL), BF16),
        scratch_shapes=[pltpu.VMEM((N_BRANCH, D_MODEL, tn), BF16), pltpu.VMEM((N_BRANCH, MIX_W, tn), BF16)],
        compiler_params=_params("arbitrary", "arbitrary"),
        name="merge_branches",
    )(xn, *ys, w_in, w_in, w_in, w_in, w_branch)


CONV_HIST = 8


def _gelu(x):
    return 0.5 * x * (1.0 + lax.erf(x * (2.0 ** -0.5)))


def _up_conv_prompt_kernel(xn_ref, wa_ref, wv_ref, cw_ref, cb_ref, g_ref, tail_ref, carry_ref, *, tm):
    i, j = pl.program_id(0), pl.program_id(1)
    xn = xn_ref[...]
    a = _dot(xn, wa_ref[...].astype(BF16))
    v = _dot(xn, wv_ref[...].astype(BF16))
    prev = jnp.where(i == 0, 0.0, carry_ref[j])
    row = lax.broadcasted_iota(jnp.int32, a.shape, 0)
    a1 = jnp.where(row == 0, prev[CONV_HIST - 1:CONV_HIST], pltpu.roll(a, 1, 0))
    a2 = jnp.where(row == 0, prev[CONV_HIST - 2:CONV_HIST - 1],
                   jnp.where(row == 1, prev[CONV_HIST - 1:CONV_HIST], pltpu.roll(a, 2, 0)))
    c = cb_ref[...] + cw_ref[0:1, :] * a2 + cw_ref[1:2, :] * a1 + cw_ref[2:3, :] * a
    g_ref[...] = (_gelu(c) * v).astype(BF16)
    last = a[tm - CONV_HIST:, :]
    carry_ref[j] = last
    tail_ref[...] = last


def up_conv_prompt(xn, w_up, conv_w, conv_b, layer, tm, tn):
    T, K = xn.shape
    nc = D_FF // tn
    return pl.pallas_call(
        functools.partial(_up_conv_prompt_kernel, tm=tm),
        grid=(T // tm, nc),
        in_specs=[
            pl.BlockSpec((tm, K), lambda i, j: (i, 0)),
            pl.BlockSpec((None, K, tn), lambda i, j: (layer, 0, j)),
            pl.BlockSpec((None, K, tn), lambda i, j: (layer, 0, nc + j)),
            pl.BlockSpec((None, 3, tn), lambda i, j: (layer, 0, j)),
            pl.BlockSpec((None, 1, tn), lambda i, j: (layer, 0, j)),
        ],
        out_specs=[
            pl.BlockSpec((tm, tn), lambda i, j: (i, j)),
            pl.BlockSpec((CONV_HIST, tn), lambda i, j: (0, j)),
        ],
        out_shape=[
            jax.ShapeDtypeStruct((T, D_FF), BF16),
            jax.ShapeDtypeStruct((CONV_HIST, D_FF), F32),
        ],
        scratch_shapes=[pltpu.VMEM((nc, CONV_HIST, tn), F32)],
        compiler_params=_params("arbitrary", "arbitrary"),
        name="up_conv_prompt",
    )(xn, w_up, w_up, conv_w, conv_b)


def _up_conv_sample_kernel(xn_ref, wa_ref, wv_ref, st_ref, cw_ref, cb_ref, g_ref, so_ref, *, steps, B):
    xn = xn_ref[...]
    a = _dot(xn, wa_ref[...].astype(BF16))
    v = _dot(xn, wv_ref[...].astype(BF16))
    hist = [st_ref[:, 0, :], st_ref[:, 1, :]] + [a[t * B:(t + 1) * B] for t in range(steps)]
    for t in range(steps):
        c = cb_ref[...] + cw_ref[0:1, :] * hist[t] + cw_ref[1:2, :] * hist[t + 1] + cw_ref[2:3, :] * hist[t + 2]
        g_ref[t * B:(t + 1) * B, :] = (_gelu(c) * v[t * B:(t + 1) * B]).astype(BF16)
    so_ref[:, 0, :] = hist[steps]
    so_ref[:, 1, :] = hist[steps + 1]


def up_conv_sample(xn, w_up, state_conv, conv_w, conv_b, layer, steps, tn, state_out):
    T, K = xn.shape
    B = T // steps
    nc = D_FF // tn
    in_specs = [
        pl.BlockSpec((T, K), lambda j: (0, 0)),
        pl.BlockSpec((None, K, tn), lambda j: (layer, 0, j)),
        pl.BlockSpec((None, K, tn), lambda j: (layer, 0, nc + j)),
        pl.BlockSpec((None, B, 2, tn), lambda j: (layer, 0, 0, j)),
        pl.BlockSpec((None, 3, tn), lambda j: (layer, 0, j)),
        pl.BlockSpec((None, 1, tn), lambda j: (layer, 0, j)),
    ]
    return _layer_slab_call(
        functools.partial(_up_conv_sample_kernel, steps=steps, B=B),
        in_specs, [xn, w_up, w_up, state_conv, conv_w, conv_b], state_out,
        grid=(nc,),
        out_specs=[
            pl.BlockSpec((T, tn), lambda j: (0, j)),
            pl.BlockSpec((None, B, 2, tn), lambda j: (layer, 0, 0, j)),
        ],
        out_shape=[
            jax.ShapeDtypeStruct((T, D_FF), BF16),
            jax.ShapeDtypeStruct(state_conv.shape, F32),
        ],
        compiler_params=_params("arbitrary"),
        name="up_conv_sample",
    )


def _hgrn_gates(q_in, z, lb):
    q = q_in * _sigmoid(q_in)
    log_sig = jnp.minimum(z, 0.0) - jnp.log1p(jnp.exp(-jnp.abs(z)))
    a1 = jnp.log(lb)
    a2 = jnp.log1p(-lb) + log_sig
    log_f = jnp.maximum(a1, a2) + jnp.log1p(jnp.exp(-jnp.abs(a1 - a2)))
    k = (1.0 - lb) * _sigmoid(-z)
    return q, log_f, k


def _hgrn_out(o, gate, gn):
    ms = jnp.mean(o * o, axis=-1, keepdims=True)
    return o * lax.rsqrt(ms + EPS) * gn * (gate * _sigmoid(gate))


def _cumsum_rows(x, tril):
    hi = x.astype(BF16)
    r1 = x - hi.astype(F32)
    mid = r1.astype(BF16)
    lo = (r1 - mid.astype(F32)).astype(BF16)
    return _dot(tril, hi) + _dot(tril, mid) + _dot(tril, lo)


def _block_row(x, blk, r):
    C = x.shape[0]
    x3 = x.reshape(C // blk, blk, LANES)
    return jnp.broadcast_to(x3[:, r:r + 1, :], (C // blk, blk, LANES)).reshape(C, LANES)


def _hgrn_chunk(q, log_f, k, v, S):
    C = HGRN_CHUNK
    row = lax.broadcasted_iota(jnp.int32, (C, C), 0)
    col = lax.broadcasted_iota(jnp.int32, (C, C), 1)
    rowl = lax.broadcasted_iota(jnp.int32, (C, LANES), 0)
    tril = jnp.where(col <= row, 1.0, 0.0).astype(BF16)
    b = _cumsum_rows(log_f, tril)

    att = jnp.zeros((C, C), F32)
    roff = rowl & 7
    blk0 = row & ~7
    for s in range(8):
        bs = _block_row(b, 8, s)
        ks = _block_row(k, 8, s)
        e = jnp.exp(jnp.where(roff >= s, b - bs, NEG_BIG))
        r = jnp.sum(q * ks * e, axis=-1, keepdims=True)
        att = jnp.where(col == blk0 + s, r, att)
    m = 8
    while m < C:
        gam = _block_row(b, 2 * m, m - 1)
        isq = (rowl & (2 * m - 1)) >= m
        x = (jnp.where(isq, q, k) * jnp.exp(jnp.where(isq, b - gam, gam - b))).astype(BF16)
        a = _dot_nt(x, x)
        mask = ((row & ~(2 * m - 1)) == (col & ~(2 * m - 1))) & ((row & (2 * m - 1)) >= m) & ((col & (2 * m - 1)) < m)
        att = jnp.where(mask, a, att)
        m *= 2

    vb = v.astype(BF16)
    o = _dot((q * jnp.exp(b)).astype(BF16), S.astype(BF16)) + _dot(att.astype(BF16), vb)
    bl = b[C - 1:C, :]
    kk = k * jnp.exp(bl - b)
    ecol = jnp.transpose(jnp.broadcast_to(jnp.exp(bl), (LANES, LANES)))
    s_new = ecol * S + _dot(jnp.transpose(kk).astype(BF16), vb)
    return o, s_new


def _hgrn_prompt_kernel(q_ref, f_ref, i_ref, g_ref, lb_ref, gn_ref, y_ref, so_ref, s_ref, *, rows):
    @pl.when(pl.program_id(1) == 0)
    def _():
        s_ref[...] = jnp.zeros_like(s_ref)

    lb = lb_ref[...]
    gn = gn_ref[...]
    for c in range(rows // HGRN_CHUNK):
        sl = pl.ds(c * HGRN_CHUNK, HGRN_CHUNK)
        q, log_f, k = _hgrn_gates(q_ref[sl, :], f_ref[sl, :], lb)
        o, s_new = _hgrn_chunk(q, log_f, k, i_ref[sl, :], s_ref[...])
        s_ref[...] = s_new
        y_ref[sl, :] = _hgrn_out(o, g_ref[sl, :], gn).astype(BF16)

    @pl.when(pl.program_id(1) == pl.num_programs(1) - 1)
    def _():
        so_ref[...] = s_ref[...]


def hgrn_prompt(proj, lb, gn, rows):
    T = proj.shape[0]

    def col(off):
        base = off // LANES
        return pl.BlockSpec((rows, LANES), lambda h, c: (c, base + h))

    vec = pl.BlockSpec((1, LANES), lambda h, c: (0, h))
    return pl.pallas_call(
        functools.partial(_hgrn_prompt_kernel, rows=rows),
        grid=(A_HEADS, T // rows),
        in_specs=[col(COL_HQ), col(COL_HF), col(COL_HI), col(COL_HG), vec, vec],
        out_specs=[
            pl.BlockSpec((rows, LANES), lambda h, c: (c, h)),
            pl.BlockSpec((None, A_DK, LANES), lambda h, c: (h, 0, 0)),
        ],
        out_shape=[
            jax.ShapeDtypeStruct((T, MIX_W), BF16),
            jax.ShapeDtypeStruct((A_HEADS, A_DK, LANES), F32),
        ],
        scratch_shapes=[pltpu.VMEM((A_DK, LANES), F32)],
        compiler_params=_params("parallel", "arbitrary"),
        name="hgrn_prompt",
    )(proj, proj, proj, proj, lb, gn)


def _hgrn_sample_kernel(q_ref, f_ref, i_ref, g_ref, lb_ref, gn_ref, s_ref, y_ref, so_ref, *, steps, bb):
    lb = lb_ref[...]
    gn = gn_ref[...]
    qs, ks, vs, bs = [], [], [], []
    b = None
    for t in range(steps):
        q, log_f, k = _hgrn_gates(q_ref[t], f_ref[t], lb)
        b = log_f if b is None else b + log_f
        qs.append(q)
        ks.append(k)
        vs.append(i_ref[t])
        bs.append(b)
    intra = []
    for t in range(steps):
        acc = None
        for s in range(t + 1):
            w = jnp.sum(qs[t] * ks[s] * jnp.exp(bs[t] - bs[s]), axis=-1, keepdims=True)
            acc = w * vs[s] if acc is None else acc + w * vs[s]
        intra.append(acc)
    R = steps * bb
    q_stack = jnp.concatenate([qs[t] * jnp.exp(bs[t]) for t in range(steps)], axis=0).astype(BF16)
    k_stack = jnp.concatenate([ks[t] * jnp.exp(bs[-1] - bs[t]) for t in range(steps)], axis=0)
    v_stack = jnp.concatenate(vs, axis=0).astype(BF16)
    k_t = jnp.transpose(k_stack)
    f_pad = jnp.concatenate([jnp.exp(bs[-1])] + [jnp.zeros((R - bb, LANES), F32)], axis=0)
    f_t = jnp.transpose(f_pad)
    rowi = lax.broadcasted_iota(jnp.int32, (R, LANES), 0) % bb
    lanei = lax.broadcasted_iota(jnp.int32, (LANES, R), 1)

    def body(bi, o_acc):
        s_b = s_ref[bi]
        o_acc = jnp.where(rowi == bi, _dot(q_stack, s_b.astype(BF16)), o_acc)
        f_col = jnp.sum(jnp.where(lanei == bi, f_t, 0.0), axis=-1, keepdims=True)
        k_b = jnp.where(lanei % bb == bi, k_t, 0.0).astype(BF16)
        so_ref[bi] = f_col * s_b + _dot(k_b, v_stack)
        return o_acc

    o_inter = lax.fori_loop(0, bb, body, jnp.zeros((R, LANES), F32), unroll=4)
    for t in range(steps):
        o = o_inter[t * bb:(t + 1) * bb] + intra[t]
        y_ref[t] = _hgrn_out(o, g_ref[t], gn).astype(BF16)


def hgrn_sample(proj3, lb, gn, state, layer, bb, state_out):
    steps, B, _ = proj3.shape
    assert steps * bb == LANES

    def col(off):
        base = off // LANES
        return pl.BlockSpec((steps, bb, LANES), lambda g, h: (0, g, base + h))

    vec = pl.BlockSpec((1, LANES), lambda g, h: (0, h))
    slab = pl.BlockSpec((None, bb, None, A_DK, LANES), lambda g, h: (layer, g, h, 0, 0))
    return _layer_slab_call(
        functools.partial(_hgrn_sample_kernel, steps=steps, bb=bb),
        [col(COL_HQ), col(COL_HF), col(COL_HI), col(COL_HG), vec, vec, slab],
        [proj3, proj3, proj3, proj3, lb, gn, state], state_out,
        grid=(B // bb, A_HEADS),
        out_specs=[pl.BlockSpec((steps, bb, LANES), lambda g, h: (0, g, h)), slab],
        out_shape=[
            jax.ShapeDtypeStruct((steps, B, MIX_W), BF16),
            jax.ShapeDtypeStruct(state.shape, F32),
        ],
        compiler_params=_params("parallel", "arbitrary"),
        name="hgrn_sample",
    )


POOL_HIST = 16


def _pool_project(d, w_ref, sc_ref, g):
    sl = slice(g * POOL_GC, (g + 1) * POOL_GC)
    return _dot(d.astype(BF16), w_ref[g].astype(BF16)) * sc_ref[:, sl]


def _pool_prompt_kernel(u_ref, prev_ref, w_ref, sc_ref, y_ref, ext_ref, *, tm):
    i = pl.program_id(0)
    ext_ref[0:POOL_HIST, :] = jnp.where(i == 0, 0.0, prev_ref[...])
    ext_ref[POOL_HIST:, :] = u_ref[...]
    pos = i * tm + lax.broadcasted_iota(jnp.int32, (tm, 1), 0)
    for g, win in enumerate(POOL_WINDOWS):
        sl = slice(g * POOL_GC, (g + 1) * POOL_GC)
        acc = ext_ref[pl.ds(POOL_HIST, tm), sl]
        for j in range(1, win):
            acc = acc + ext_ref[pl.ds(POOL_HIST - j, tm), sl]
        cnt = jnp.minimum(pos + 1, win).astype(F32)
        d = acc / cnt - u_ref[:, sl]
        y_ref[:, sl] = _pool_project(d, w_ref, sc_ref, g).astype(BF16)


def pool_prompt(proj, pool_w, pool_scale, layer, tm):
    T = proj.shape[0]
    cb = COL_POOL // MIX_W
    return pl.pallas_call(
        functools.partial(_pool_prompt_kernel, tm=tm),
        grid=(T // tm,),
        in_specs=[
            pl.BlockSpec((tm, MIX_W), lambda i: (i, cb)),
            pl.BlockSpec((POOL_HIST, MIX_W), lambda i: (jnp.maximum(i * (tm // POOL_HIST) - 1, 0), cb)),
            pl.BlockSpec((None, len(POOL_WINDOWS), POOL_GC, POOL_GC), lambda i: (layer, 0, 0, 0)),
            pl.BlockSpec((None, 1, MIX_W), lambda i: (layer, 0, 0)),
        ],
        out_specs=pl.BlockSpec((tm, MIX_W), lambda i: (i, 0)),
        out_shape=jax.ShapeDtypeStruct((T, MIX_W), BF16),
        scratch_shapes=[pltpu.VMEM((POOL_HIST + tm, MIX_W), F32)],
        compiler_params=_params("arbitrary"),
        name="pool_prompt",
    )(proj, proj, pool_w, pool_scale)


def _pool_sample_kernel(u_ref, c_ref, w_ref, sc_ref, y_ref, *, steps):
    for t in range(steps):
        for g, win in enumerate(POOL_WINDOWS):
            sl = slice(g * POOL_GC, (g + 1) * POOL_GC)
            acc = u_ref[t, :, sl]
            for j in range(1, win):
                if j <= t:
                    acc = acc + u_ref[t - j, :, sl]
                else:
                    acc = acc + c_ref[:, POOL_BUF + t - j, sl]
            d = acc / float(win) - u_ref[t, :, sl]
            y_ref[t, :, sl] = _pool_project(d, w_ref, sc_ref, g).astype(BF16)


def pool_sample(proj3, cache_pool, pool_w, pool_scale, layer, bb):
    steps, B, _ = proj3.shape
    cb = COL_POOL // MIX_W
    return pl.pallas_call(
        functools.partial(_pool_sample_kernel, steps=steps),
        grid=(B // bb,),
        in_specs=[
            pl.BlockSpec((steps, bb, MIX_W), lambda g: (0, g, cb)),
            pl.BlockSpec((None, bb, POOL_BUF, MIX_W), lambda g: (layer, g, 0, 0)),
            pl.BlockSpec((None, len(POOL_WINDOWS), POOL_GC, POOL_GC), lambda g: (layer, 0, 0, 0)),
            pl.BlockSpec((None, 1, MIX_W), lambda g: (layer, 0, 0)),
        ],
        out_specs=pl.BlockSpec((steps, bb, MIX_W), lambda g: (0, g, 0)),
        out_shape=jax.ShapeDtypeStruct((steps, B, MIX_W), BF16),
        compiler_params=_params("arbitrary"),
        name="pool_sample",
    )(proj3, cache_pool, pool_w, pool_scale)


def _rope_tables(positions):
    half = ROT_DIM // 2
    inv = np.power(ROPE_THETA, -np.arange(0, ROT_DIM, 2, dtype=np.float64) / ROT_DIM)
    ang = np.asarray(positions, np.float64)[:, None] * inv[None, :]
    cos, sin = np.cos(ang), np.sin(ang)
    n = len(positions)
    ct = np.ones((n, LANES))
    sn = np.zeros((n, LANES))
    sp = np.zeros((n, LANES))
    for base in (0, SWA_HEAD_DIM):
        ct[:, base:base + half] = cos
        ct[:, base + half:base + ROT_DIM] = cos
        sn[:, base:base + half] = -sin
        sp[:, base + half:base + ROT_DIM] = sin
    return tuple(jnp.asarray(t, F32) for t in (ct, sn, sp))


def _head_norm_rope(x, g, ct, sn, sp):
    lane = lax.broadcasted_iota(jnp.int32, x.shape, 1)
    lo = lane < SWA_HEAD_DIM
    x2 = x * x
    ms_lo = jnp.sum(jnp.where(lo, x2, 0.0), axis=-1, keepdims=True) / SWA_HEAD_DIM
    ms_hi = jnp.sum(jnp.where(lo, 0.0, x2), axis=-1, keepdims=True) / SWA_HEAD_DIM
    xn = x * jnp.where(lo, lax.rsqrt(ms_lo + EPS), lax.rsqrt(ms_hi + EPS)) * g
    half = ROT_DIM // 2
    return xn * ct + pltpu.roll(xn, LANES - half, 1) * sn + pltpu.roll(xn, half, 1) * sp


def _kprep_kernel(k_ref, g_ref, ct_ref, sn_ref, sp_ref, o_ref):
    ct, sn, sp = ct_ref[...], sn_ref[...], sp_ref[...]
    for j in range(2):
        sl = slice(j * LANES, (j + 1) * LANES)
        o_ref[:, sl] = _head_norm_rope(k_ref[:, sl], g_ref[...], ct, sn, sp)


def swa_kprep(proj, g2, tables, tm):
    T = proj.shape[0]
    kw = SWA_KV_HEADS * SWA_HEAD_DIM
    tab = pl.BlockSpec((tm, LANES), lambda i: (i, 0))
    return pl.pallas_call(
        _kprep_kernel,
        grid=(T // tm,),
        in_specs=[pl.BlockSpec((tm, kw), lambda i: (i, COL_SK // kw)),
                  pl.BlockSpec((1, LANES), lambda i: (0, 0)), tab, tab, tab],
        out_specs=pl.BlockSpec((tm, kw), lambda i: (i, 0)),
        out_shape=jax.ShapeDtypeStruct((T, kw), F32),
        compiler_params=_params("arbitrary"),
        name="swa_kprep",
    )(proj, g2, *tables)


def _dup_head(x, parity):
    lane = lax.broadcasted_iota(jnp.int32, x.shape, 1)
    return jnp.where(lane // SWA_HEAD_DIM == parity, x, pltpu.roll(x, SWA_HEAD_DIM, 1))


def _stack_heads(q):
    lane = lax.broadcasted_iota(jnp.int32, q.shape, 1)
    lo = lane < SWA_HEAD_DIM
    return jnp.concatenate([jnp.where(lo, q, 0.0), jnp.where(lo, 0.0, q)], axis=0)


def _unstack_heads(o2):
    R = o2.shape[0] // 2
    lane = lax.broadcasted_iota(jnp.int32, (R, LANES), 1)
    return jnp.where(lane < SWA_HEAD_DIM, o2[:R], o2[R:])


def _swa_prompt_kernel(sink_ref, q_ref, kc_ref, kp_ref, vc_ref, vp_ref, g_ref, ct_ref, sn_ref, sp_ref, y_ref,
                       *, layer):
    i = pl.program_id(0)
    W = WINDOW
    ct, sn, sp = ct_ref[...], sn_ref[...], sp_ref[...]
    qi = lax.broadcasted_iota(jnp.int32, (2 * W, 2 * W), 0) % W
    ci = lax.broadcasted_iota(jnp.int32, (2 * W, 2 * W), 1)
    valid = (ci > qi) & (ci <= qi + W) & ((i > 0) | (ci >= W))
    top = lax.broadcasted_iota(jnp.int32, (2 * W, 1), 0) < W
    scale = SWA_HEAD_DIM ** -0.5
    for kvh in range(SWA_KV_HEADS):
        ksl = slice((kvh // 2) * LANES, (kvh // 2 + 1) * LANES)
        k_dup = _dup_head(jnp.concatenate([kp_ref[:, ksl], kc_ref[:, ksl]], axis=0), kvh % 2).astype(BF16)
        v_dup = _dup_head(jnp.concatenate([vp_ref[:, ksl], vc_ref[:, ksl]], axis=0), kvh % 2).astype(BF16)
        for jj in range(2):
            j = 2 * kvh + jj
            qsl = slice(j * LANES, (j + 1) * LANES)
            q = _head_norm_rope(q_ref[:, qsl], g_ref[...], ct, sn, sp) * scale
            s = _dot_nt(_stack_heads(q).astype(BF16), k_dup)
            sink = jnp.where(top, sink_ref[layer, 2 * j], sink_ref[layer, 2 * j + 1])
            s = jnp.where(valid, s, NEG_BIG)
            m = jnp.maximum(jnp.max(s, axis=-1, keepdims=True), sink)
            e = jnp.where(valid, jnp.exp(s - m), 0.0)
            p = e / (jnp.sum(e, axis=-1, keepdims=True) + jnp.exp(sink - m))
            y_ref[:, qsl] = _unstack_heads(_dot(p.astype(BF16), v_dup)).astype(BF16)


def swa_prompt(proj, khat, sinks, g2, tables, layer):
    T = proj.shape[0]
    W = WINDOW
    kw = SWA_KV_HEADS * SWA_HEAD_DIM
    tab = pl.BlockSpec((W, LANES), lambda i: (i, 0))
    prev = lambda i: jnp.maximum(i - 1, 0)
    return pl.pallas_call(
        functools.partial(_swa_prompt_kernel, layer=layer),
        grid=(T // W,),
        in_specs=[
            pl.BlockSpec(memory_space=pltpu.SMEM),
            pl.BlockSpec((W, MIX_W), lambda i: (i, COL_SQ // MIX_W)),
            pl.BlockSpec((W, kw), lambda i: (i, 0)),
            pl.BlockSpec((W, kw), lambda i: (prev(i), 0)),
            pl.BlockSpec((W, kw), lambda i: (i, COL_SV // kw)),
            pl.BlockSpec((W, kw), lambda i: (prev(i), COL_SV // kw)),
            pl.BlockSpec((1, LANES), lambda i: (0, 0)), tab, tab, tab,
        ],
        out_specs=pl.BlockSpec((W, MIX_W), lambda i: (i, 0)),
        out_shape=jax.ShapeDtypeStruct((T, MIX_W), BF16),
        compiler_params=_params("arbitrary"),
        name="swa_prompt",
    )(sinks, proj, khat, khat, proj, proj, g2, *tables)


def _swa_sample_kernel(sink_ref, q_ref, kn_ref, vn_ref, kc_ref, vc_ref, g_ref, ct_ref, sn_ref, sp_ref, y_ref,
                       *, layer, steps, bb):
    kvh = pl.program_id(1)
    parity = kvh % 2
    W = WINDOW
    R = steps * bb
    G = SWA_Q_HEADS // SWA_KV_HEADS
    ct, sn, sp = ct_ref[...], sn_ref[...], sp_ref[...]
    scale = SWA_HEAD_DIM ** -0.5
    r4 = lax.broadcasted_iota(jnp.int32, (G * R, 1), 0)
    hh = r4 // R
    tq = (r4 % R) // bb
    bq = r4 % bb
    c_new = lax.broadcasted_iota(jnp.int32, (1, R), 1)
    valid_new = (c_new % bb == bq) & (c_new // bb <= tq)
    c_old = lax.broadcasted_iota(jnp.int32, (1, W), 1)
    valid_old = c_old > tq
    kn = _dup_head(jnp.concatenate([kn_ref[t] for t in range(steps)], axis=0), parity).astype(BF16)
    vn = _dup_head(jnp.concatenate([vn_ref[t] for t in range(steps)], axis=0), parity).astype(BF16)
    qs = []
    for jj in range(G // 2):
        qsl = slice(jj * LANES, (jj + 1) * LANES)
        q = jnp.concatenate([q_ref[t, :, qsl] for t in range(steps)], axis=0)
        qs.append(_stack_heads(_head_norm_rope(q, g_ref[...], ct, sn, sp) * scale))
    q4 = jnp.concatenate(qs, axis=0)
    s_new = jnp.where(valid_new, _dot_nt(q4.astype(BF16), kn), NEG_BIG)
    s_old = None
    for b in range(bb):
        k_t = kc_ref[b].astype(BF16)
        d = _dot(jnp.where(bq == b, q4, 0.0).astype(BF16), jnp.concatenate([k_t, k_t], axis=0))
        s_old = d if s_old is None else s_old + d
    s_old = jnp.where(valid_old, s_old, NEG_BIG)
    sink = sink_ref[layer, G * kvh + G - 1]
    for i in range(G - 2, -1, -1):
        sink = jnp.where(hh == i, sink_ref[layer, G * kvh + i], sink)
    m = jnp.maximum(jnp.maximum(jnp.max(s_new, axis=-1, keepdims=True),
                                jnp.max(s_old, axis=-1, keepdims=True)), sink)
    e_new = jnp.where(valid_new, jnp.exp(s_new - m), 0.0)
    e_old = jnp.where(valid_old, jnp.exp(s_old - m), 0.0)
    den = jnp.sum(e_new, axis=-1, keepdims=True) + jnp.sum(e_old, axis=-1, keepdims=True) + jnp.exp(sink - m)
    p_old = e_old / den
    o = _dot((e_new / den).astype(BF16), vn)
    for b in range(bb):
        v_t = vc_ref[b].astype(BF16)
        o = o + _dot_nt(jnp.where(bq == b, p_old, 0.0).astype(BF16), jnp.concatenate([v_t, v_t], axis=0))
    for jj in range(G // 2):
        o_j = _unstack_heads(o[2 * jj * R:(2 * jj + 2) * R])
        for t in range(steps):
            y_ref[t, :, jj * LANES:(jj + 1) * LANES] = o_j[t * bb:(t + 1) * bb]


def swa_sample(proj3, khat3, cache_kt, cache_vt, sinks, g2, tables, layer, bb):
    steps, B, _ = proj3.shape
    R = steps * bb
    qw = MIX_W // SWA_KV_HEADS
    tab = pl.BlockSpec((R, LANES), lambda g, h: (0, 0))
    cache = pl.BlockSpec((None, bb, None, SWA_HEAD_DIM, WINDOW), lambda g, h: (layer, g, h, 0, 0))
    return pl.pallas_call(
        functools.partial(_swa_sample_kernel, layer=layer, steps=steps, bb=bb),
        grid=(B // bb, SWA_KV_HEADS),
        in_specs=[
            pl.BlockSpec(memory_space=pltpu.SMEM),
            pl.BlockSpec((steps, bb, qw), lambda g, h: (0, g, COL_SQ // qw + h)),
            pl.BlockSpec((steps, bb, LANES), lambda g, h: (0, g, h // 2)),
            pl.BlockSpec((steps, bb, LANES), lambda g, h: (0, g, COL_SV // LANES + h // 2)),
            cache, cache,
            pl.BlockSpec((1, LANES), lambda g, h: (0, 0)), tab, tab, tab,
        ],
        out_specs=pl.BlockSpec((steps, bb, qw), lambda g, h: (0, g, h)),
        out_shape=jax.ShapeDtypeStruct((steps, B, MIX_W), F32),
        compiler_params=_params("parallel", "arbitrary"),
        name="swa_sample",
    )(sinks, proj3, khat3, proj3, cache_kt, cache_vt, g2, *tables)


def _mem_kv_kernel(x_ref, g_ref, w_ref, kg_ref, o_ref, xn_ref):
    j = pl.program_id(0)

    @pl.when(j == 0)
    def _():
        x = x_ref[...]
        ms = jnp.mean(x * x, axis=-1, keepdims=True)
        xn_ref[...] = (x * lax.rsqrt(ms + EPS) * g_ref[...]).astype(BF16)

    y = _dot(xn_ref[...], w_ref[...].astype(BF16))

    @pl.when(j < MEM_HEADS)
    def _():
        ms = jnp.mean(y * y, axis=-1, keepdims=True)
        o_ref[...] = y * lax.rsqrt(ms + EPS) * kg_ref[...]

    @pl.when(j >= MEM_HEADS)
    def _():
        o_ref[...] = y


def mem_kv(mem, mem_norm_g, w_mem_kv, mem_knorm_g, layer):
    M, K = mem.shape
    hd = MEM_HEAD_DIM
    return pl.pallas_call(
        _mem_kv_kernel,
        grid=(2 * MEM_HEADS,),
        in_specs=[
            pl.BlockSpec((M, K), lambda j: (0, 0)),
            pl.BlockSpec((None, 1, K), lambda j: (layer, 0, 0)),
            pl.BlockSpec((None, K, hd), lambda j: (layer, 0, j)),
            pl.BlockSpec((None, 1, hd), lambda j: (layer, 0, 0)),
        ],
        out_specs=pl.BlockSpec((M, hd), lambda j: (0, j)),
        out_shape=jax.ShapeDtypeStruct((M, 2 * MIX_W), F32),
        scratch_shapes=[pltpu.VMEM((M, K), BF16)],
        compiler_params=_params("arbitrary"),
        name="mem_kv",
    )(mem, mem_norm_g, w_mem_kv, mem_knorm_g)


def _mem_qnorm(q, g):
    ms = jnp.mean(q * q, axis=-1, keepdims=True)
    return q * lax.rsqrt(ms + EPS) * g * (MEM_HEAD_DIM ** -0.5)


def _softmax_rows(s):
    m = jnp.max(s, axis=-1, keepdims=True)
    e = jnp.exp(s - m)
    return e / jnp.sum(e, axis=-1, keepdims=True)


def _mem_prompt_kernel(q_ref, k_ref, v_ref, g_ref, y_ref):
    q = _mem_qnorm(q_ref[...], g_ref[...]).astype(BF16)
    p = _softmax_rows(_dot_nt(q, k_ref[...].astype(BF16)))
    y_ref[...] = _dot(p.astype(BF16), v_ref[...].astype(BF16)).astype(BF16)


def mem_attn_prompt(proj, kv, mem_qnorm_g, layer, tq):
    T = proj.shape[0]
    hd = MEM_HEAD_DIM
    return pl.pallas_call(
        _mem_prompt_kernel,
        grid=(T // tq, MEM_HEADS),
        in_specs=[
            pl.BlockSpec((tq, hd), lambda i, h: (i, COL_MQ // hd + h)),
            pl.BlockSpec((N_MEM, hd), lambda i, h: (0, h)),
            pl.BlockSpec((N_MEM, hd), lambda i, h: (0, MEM_HEADS + h)),
            pl.BlockSpec((None, 1, hd), lambda i, h: (layer, 0, 0)),
        ],
        out_specs=pl.BlockSpec((tq, hd), lambda i, h: (i, h)),
        out_shape=jax.ShapeDtypeStruct((T, MIX_W), BF16),
        compiler_params=_params("parallel", "arbitrary"),
        name="mem_prompt",
    )(proj, kv, kv, mem_qnorm_g)


def _mem_sample_kernel(q0_ref, q1_ref, q2_ref, q3_ref, k_ref, v_ref, g_ref, y_ref, *, steps, bb):
    R = steps * bb
    hd = MEM_HEAD_DIM
    bq = lax.broadcasted_iota(jnp.int32, (R, 1), 0) % bb
    for h, q_ref in enumerate((q0_ref, q1_ref, q2_ref, q3_ref)):
        q = _mem_qnorm(jnp.concatenate([q_ref[t] for t in range(steps)], axis=0), g_ref[...])
        s = None
        for b in range(bb):
            d = _dot_nt(jnp.where(bq == b, q, 0.0).astype(BF16), k_ref[b, :, h, :].astype(BF16))
            s = d if s is None else s + d
        p = _softmax_rows(s)
        o = None
        for b in range(bb):
            d = _dot(jnp.where(bq == b, p, 0.0).astype(BF16), v_ref[b, :, h, :].astype(BF16))
            o = d if o is None else o + d
        for t in range(steps):
            y_ref[t, :, h * hd:(h + 1) * hd] = o[t * bb:(t + 1) * bb]


def mem_attn_sample(proj3, cache_k, cache_v, mem_qnorm_g, layer, bb):
    steps, B, _ = proj3.shape
    hd = MEM_HEAD_DIM
    cache = pl.BlockSpec((None, bb, N_MEM, MEM_HEADS, hd), lambda g: (layer, g, 0, 0, 0))

    def q_spec(h):
        cb = COL_MQ // hd + h
        return pl.BlockSpec((steps, bb, hd), lambda g: (0, g, cb))

    return pl.pallas_call(
        functools.partial(_mem_sample_kernel, steps=steps, bb=bb),
        grid=(B // bb,),
        in_specs=[q_spec(h) for h in range(MEM_HEADS)] + [
            cache, cache,
            pl.BlockSpec((None, 1, hd), lambda g: (layer, 0, 0)),
        ],
        out_specs=pl.BlockSpec((steps, bb, MIX_W), lambda g: (0, g, 0)),
        out_shape=jax.ShapeDtypeStruct((steps, B, MIX_W), F32),
        compiler_params=_params("arbitrary"),
        name="mem_sample",
    )(proj3, proj3, proj3, proj3, cache_k, cache_v, mem_qnorm_g)


def _row_tile(T, cap):
    t = cap
    while T % t:
        t //= 2
    return t


TM_STREAM = 2048
TM_DOWN = 1024
TM_LOCAL = 512


def _token_tail(x, xn, ys, layer, w_in, w_branch, w_o, norm2_g):
    T = x.shape[0]
    merged = merge_branches(xn, ys, w_in, w_branch, layer, _row_tile(T, 256), 512)
    h = matmul_res(merged, w_o, layer, x, _row_tile(T, TM_DOWN), 512)
    return h, prenorm(h, norm2_g, layer, _row_tile(T, TM_LOCAL))


def kernel(x_prompt, x_sample, mem_prompt, state_hgrn, cache_pool, cache_swa_k, cache_swa_v, state_conv, cache_mem_k, cache_mem_v, norm1_g, w_in, hgrn_lb, hgrn_norm_g, pool_w, pool_scale, swa_qnorm_g, swa_knorm_g, swa_sinks, mem_norm_g, w_mem_kv, mem_qnorm_g, mem_knorm_g, w_branch, w_o, norm2_g, w_up, conv_w, conv_b, w_down):
    depth = w_in.shape[0]
    bp, L, _ = x_prompt.shape
    B, steps, _ = x_sample.shape
    assert bp == 1
    kw = SWA_KV_HEADS * SWA_HEAD_DIM
    Ts = steps * B

    lb_all = jnp.cumsum(jax.nn.softmax(hgrn_lb.astype(F32), axis=0), axis=0)
    lb_all = lb_all - lb_all[:1]

    swa_bb = 8
    tab_p = _rope_tables(np.arange(L))
    tab_s = _rope_tables(np.repeat(PAST_LEN + np.arange(steps), B))
    tab_sb = _rope_tables(np.repeat(PAST_LEN + np.arange(steps), swa_bb))

    xp = x_prompt.reshape(L, D_MODEL)
    xs = jnp.transpose(x_sample, (1, 0, 2)).reshape(Ts, D_MODEL)
    mem = mem_prompt.reshape(N_MEM, D_MODEL)
    ckt_all = jnp.transpose(cache_swa_k, (0, 1, 3, 4, 2))
    cvt_all = jnp.transpose(cache_swa_v, (0, 1, 3, 4, 2))
    row3 = lambda a: a.reshape(depth, 1, a.shape[-1])
    norm1_g, norm2_g, pool_scale, conv_b = row3(norm1_g), row3(norm2_g), row3(pool_scale), row3(conv_b)
    mem_norm_g, mem_qnorm_g, mem_knorm_g = row3(mem_norm_g), row3(mem_qnorm_g), row3(mem_knorm_g)
    tm_p = _row_tile(L, TM_STREAM)
    tl_p = _row_tile(L, TM_LOCAL)

    outs = {k: [] for k in ("sp", "pp", "ps", "kp", "ks", "vp", "vs", "cp", "mk", "mv")}
    hgrn_states = None
    conv_states = None
    for l in range(depth):
        lb = lb_all[l].reshape(1, MIX_W)
        gn = hgrn_norm_g[l].reshape(1, MIX_W)
        gq2 = jnp.tile(swa_qnorm_g[l], 2).reshape(1, LANES)
        gk2 = jnp.tile(swa_knorm_g[l], 2).reshape(1, LANES)

        kv = mem_kv(mem, mem_norm_g, w_mem_kv, mem_knorm_g, l)

        xn = prenorm(xp, norm1_g, l, tl_p)
        proj = matmul_cols(xn, w_in, l, COL_GATE, tm_p, 512)
        ya, s_p = hgrn_prompt(proj, lb, gn, tl_p)
        yb = pool_prompt(proj, pool_w, pool_scale, l, tl_p)
        khat = swa_kprep(proj, gk2, tab_p, tl_p)
        yc = swa_prompt(proj, khat, swa_sinks, gq2, tab_p, l)
        ym = mem_attn_prompt(proj, kv, mem_qnorm_g, l, tl_p)
        h, hn = _token_tail(xp, xn, (ya, yb, yc, ym), l, w_in, w_branch, w_o, norm2_g)
        gact, a_tail = up_conv_prompt(hn, w_up, conv_w, conv_b, l, tm_p, 256)
        xp = matmul_res(gact, w_down, l, h, _row_tile(L, TM_DOWN), 256)

        outs["sp"].append(s_p[None])
        outs["pp"].append(proj[None, L - POOL_BUF:, COL_POOL:COL_POOL + MIX_W])
        outs["kp"].append(khat[None, L - WINDOW:].reshape(1, WINDOW, SWA_KV_HEADS, SWA_HEAD_DIM))
        outs["vp"].append(proj[None, L - WINDOW:, COL_SV:COL_SV + kw].reshape(1, WINDOW, SWA_KV_HEADS, SWA_HEAD_DIM))
        outs["cp"].append(a_tail[None, CONV_HIST - 2:])
        outs["mk"].append(kv[None, :, :MIX_W].reshape(1, N_MEM, MEM_HEADS, MEM_HEAD_DIM))
        outs["mv"].append(kv[None, :, MIX_W:].reshape(1, N_MEM, MEM_HEADS, MEM_HEAD_DIM))

        xn = prenorm(xs, norm1_g, l, Ts)
        proj_s = matmul_cols(xn, w_in, l, COL_GATE, Ts, 512)
        proj3 = proj_s.reshape(steps, B, COL_GATE)
        ya, hgrn_states = hgrn_sample(proj3, lb, gn, state_hgrn, l, LANES // steps, hgrn_states)
        yb = pool_sample(proj3, cache_pool, pool_w, pool_scale, l, 64)
        khat_s = swa_kprep(proj_s, gk2, tab_s, Ts)
        khat3 = khat_s.reshape(steps, B, kw)
        yc = swa_sample(proj3, khat3, ckt_all, cvt_all, swa_sinks, gq2, tab_sb, l, swa_bb)
        ym = mem_attn_sample(proj3, cache_mem_k, cache_mem_v, mem_qnorm_g, l, 8)
        ys = tuple(y.reshape(Ts, MIX_W).astype(BF16) for y in (ya, yb, yc, ym))
        h, hn = _token_tail(xs, xn, ys, l, w_in, w_branch, w_o, norm2_g)
        gact, conv_states = up_conv_sample(hn, w_up, state_conv, conv_w, conv_b, l, steps, 256, conv_states)
        xs = matmul_res(gact, w_down, l, h, Ts, 256)

        u_new = jnp.transpose(proj3[:, :, COL_POOL:COL_POOL + MIX_W], (1, 0, 2))
        outs["ps"].append(jnp.concatenate([cache_pool[l], u_new], axis=1)[:, -POOL_BUF:])
        k_new = jnp.transpose(khat3, (1, 0, 2)).reshape(B, steps, SWA_KV_HEADS, SWA_HEAD_DIM)
        v_new = jnp.transpose(proj3[:, :, COL_SV:COL_SV + kw], (1, 0, 2)).reshape(B, steps, SWA_KV_HEADS, SWA_HEAD_DIM)
        outs["ks"].append(jnp.concatenate([cache_swa_k[l], k_new], axis=1)[:, -WINDOW:])
        outs["vs"].append(jnp.concatenate([cache_swa_v[l], v_new], axis=1)[:, -WINDOW:])

    stk = lambda k: jnp.stack(outs[k], axis=0)
    y_prompt = xp.reshape(1, L, D_MODEL)
    y_sample = jnp.transpose(xs.reshape(steps, B, D_MODEL), (1, 0, 2))
    return (y_prompt, y_sample,
            stk("sp"), hgrn_states, stk("pp"), stk("ps"), stk("kp"), stk("ks"), stk("vp"), stk("vs"),
            stk("cp"), conv_states, jnp.concatenate(outs["mk"], axis=0)[:, None], jnp.concatenate(outs["mv"], axis=0)[:, None])
```

```python
import functools

import numpy as np
import jax
import jax.numpy as jnp
from jax import lax
from jax.experimental import pallas as pl
from jax.experimental.pallas import tpu as pltpu

F32 = jnp.float32
BF16 = jnp.bfloat16

D_MODEL = 2048
MIX_W = D_MODEL // 2
N_BRANCH = 4
A_DK = 128
A_HEADS = MIX_W // A_DK
POOL_WINDOWS = (2, 4, 8, 16)
POOL_GC = MIX_W // len(POOL_WINDOWS)
POOL_BUF = max(POOL_WINDOWS) - 1
SWA_HEAD_DIM = 64
SWA_Q_HEADS = MIX_W // SWA_HEAD_DIM
SWA_KV_HEADS = SWA_Q_HEADS // 4
WINDOW = 128
ROT_DIM = SWA_HEAD_DIM // 4
ROPE_THETA = 500000.0
N_MEM = 256
MEM_HEADS = 4
MEM_HEAD_DIM = MIX_W // MEM_HEADS
D_FF = 11 * D_MODEL // 4
EPS = 1e-6
PAST_LEN = 8192

COL_HQ, COL_HF, COL_HI, COL_HG = 0, MIX_W, 2 * MIX_W, 3 * MIX_W
COL_POOL = 4 * MIX_W
COL_SQ = 5 * MIX_W
COL_SK = 6 * MIX_W
COL_SV = COL_SK + SWA_KV_HEADS * SWA_HEAD_DIM
COL_MQ = COL_SV + SWA_KV_HEADS * SWA_HEAD_DIM
COL_GATE = COL_MQ + MIX_W
IN_COLS = COL_GATE + N_BRANCH * D_MODEL

LANES = 128
HGRN_CHUNK = 128
VMEM_LIMIT = 56 * 1024 * 1024
NEG_BIG = -1e30


def _params(*sem):
    return pltpu.CompilerParams(dimension_semantics=sem, vmem_limit_bytes=VMEM_LIMIT)


def _sigmoid(x):
    return 1.0 / (1.0 + jnp.exp(-x))


def _dot(a, b):
    return jnp.dot(a, b, preferred_element_type=F32)


def _dot_nt(a, b):
    return lax.dot_general(a, b, (((1,), (1,)), ((), ())), preferred_element_type=F32)


def _skip_ref(kernel_fn, idx):
    def wrapped(*refs):
        return kernel_fn(*refs[:idx], *refs[idx + 1:])
    return wrapped


def _layer_slab_call(kernel_fn, in_specs, args, slab_out, **kw):
    if slab_out is not None:
        idx = len(args)
        in_specs = list(in_specs) + [pl.BlockSpec(memory_space=pl.ANY)]
        args = list(args) + [slab_out]
        kernel_fn = _skip_ref(kernel_fn, idx)
        kw["input_output_aliases"] = {idx: len(kw["out_shape"]) - 1}
    return pl.pallas_call(kernel_fn, in_specs=in_specs, **kw)(*args)


def _rms_rows(x, g):
    ms = jnp.mean(x * x, axis=-1, keepdims=True)
    return x * lax.rsqrt(ms + EPS) * g


def _prenorm_kernel(x_ref, g_ref, o_ref):
    o_ref[...] = _rms_rows(x_ref[...], g_ref[...]).astype(BF16)


def prenorm(x, g, layer, tm):
    T, K = x.shape
    return pl.pallas_call(
        _prenorm_kernel,
        grid=(T // tm,),
        in_specs=[pl.BlockSpec((tm, K), lambda i: (i, 0)),
                  pl.BlockSpec((None, 1, K), lambda i: (layer, 0, 0))],
        out_specs=pl.BlockSpec((tm, K), lambda i: (i, 0)),
        out_shape=jax.ShapeDtypeStruct((T, K), BF16),
        compiler_params=_params("arbitrary"),
        name="prenorm",
    )(x, g)


def _matmul_kernel(a_ref, w_ref, o_ref):
    o_ref[...] = _dot(a_ref[...], w_ref[...].astype(BF16))


def matmul_cols(a, w, layer, n_cols, tm, tn):
    T, K = a.shape
    return pl.pallas_call(
        _matmul_kernel,
        grid=(T // tm, n_cols // tn),
        in_specs=[
            pl.BlockSpec((tm, K), lambda i, j: (i, 0)),
            pl.BlockSpec((None, K, tn), lambda i, j: (layer, 0, j)),
        ],
        out_specs=pl.BlockSpec((tm, tn), lambda i, j: (i, j)),
        out_shape=jax.ShapeDtypeStruct((T, n_cols), F32),
        compiler_params=_params("parallel", "arbitrary"),
        name="matmul_cols",
    )(a, w)


def _matmul_res_kernel(a_ref, w_ref, r_ref, o_ref):
    o_ref[...] = r_ref[...] + _dot(a_ref[...], w_ref[...].astype(BF16))


def matmul_res(a, w, layer, res, tm, tn):
    T, K = a.shape
    N = w.shape[2]
    return pl.pallas_call(
        _matmul_res_kernel,
        grid=(T // tm, N // tn),
        in_specs=[
            pl.BlockSpec((tm, K), lambda i, j: (i, 0)),
            pl.BlockSpec((None, K, tn), lambda i, j: (layer, 0, j)),
            pl.BlockSpec((tm, tn), lambda i, j: (i, j)),
        ],
        out_specs=pl.BlockSpec((tm, tn), lambda i, j: (i, j)),
        out_shape=jax.ShapeDtypeStruct((T, N), F32),
        compiler_params=_params("parallel", "arbitrary"),
        name="matmul_res",
    )(a, w, res)


def _merge_kernel(xn_ref, ya_ref, yb_ref, yc_ref, ym_ref, wg0_ref, wg1_ref, wg2_ref, wg3_ref, wb_ref, o_ref,
                  wg_s, wb_s):
    @pl.when(pl.program_id(1) == 0)
    def _():
        for n, wg_ref in enumerate((wg0_ref, wg1_ref, wg2_ref, wg3_ref)):
            wg_s[n] = wg_ref[...].astype(BF16)
            wb_s[n] = wb_ref[n].astype(BF16)

    xn = xn_ref[...]
    acc = None
    for n, y_ref in enumerate((ya_ref, yb_ref, yc_ref, ym_ref)):
        t = _sigmoid(_dot(xn, wg_s[n])) * _dot(y_ref[...], wb_s[n])
        acc = t if acc is None else acc + t
    o_ref[...] = acc.astype(BF16)


def merge_branches(xn, ys, w_in, w_branch, layer, tm, tn):
    T = xn.shape[0]
    once = pl.Buffered(1)
    y_spec = pl.BlockSpec((tm, MIX_W), lambda j, i: (i, 0))

    def gate_spec(n):
        off = (COL_GATE + n * D_MODEL) // tn
        return pl.BlockSpec((None, D_MODEL, tn), lambda j, i: (layer, 0, off + j), pipeline_mode=once)

    return pl.pallas_call(
        _merge_kernel,
        grid=(D_MODEL // tn, T // tm),
        in_specs=[pl.BlockSpec((tm, D_MODEL), lambda j, i: (i, 0))] + [y_spec] * 4
        + [gate_spec(n) for n in range(N_BRANCH)]
        + [pl.BlockSpec((None, N_BRANCH, MIX_W, tn), lambda j, i: (layer, 0, 0, j), pipeline_mode=once)],
        out_specs=pl.BlockSpec((tm, tn), lambda j, i: (i, j)),
        out_shape=jax.ShapeDtypeStruct((T, D_MODEL), BF16),
        scratch_shapes=[pltpu.VMEM((N_BRANCH, D_MODEL, tn), BF16), pltpu.VMEM((N_BRANCH, MIX_W, tn), BF16)],
        compiler_params=_params("arbitrary", "arbitrary"),
        name="merge_branches",
    )(xn, *ys, w_in, w_in, w_in, w_in, w_branch)


CONV_HIST = 8


def _gelu(x):
    return 0.5 * x * (1.0 + lax.erf(x * (2.0 ** -0.5)))


def _up_conv_prompt_kernel(xn_ref, wa_ref, wv_ref, cw_ref, cb_ref, g_ref, tail_ref, carry_ref, *, tm, rc):
    i, j = pl.program_id(0), pl.program_id(1)
    wa = wa_ref[...].astype(BF16)
    wv = wv_ref[...].astype(BF16)
    prev = jnp.where(i == 0, 0.0, carry_ref[j])
    row = lax.broadcasted_iota(jnp.int32, (rc, wa.shape[1]), 0)
    for c in range(tm // rc):
        sl = pl.ds(c * rc, rc)
        xn = xn_ref[sl, :]
        a = _dot(xn, wa)
        v = _dot(xn, wv)
        a1 = jnp.where(row == 0, prev[CONV_HIST - 1:CONV_HIST], pltpu.roll(a, 1, 0))
        a2 = jnp.where(row == 0, prev[CONV_HIST - 2:CONV_HIST - 1],
                       jnp.where(row == 1, prev[CONV_HIST - 1:CONV_HIST], pltpu.roll(a, 2, 0)))
        cc = cb_ref[...] + cw_ref[0:1, :] * a2 + cw_ref[1:2, :] * a1 + cw_ref[2:3, :] * a
        g_ref[sl, :] = (_gelu(cc) * v).astype(BF16)
        prev = a[rc - CONV_HIST:, :]
    carry_ref[j] = prev
    tail_ref[...] = prev


def up_conv_prompt(xn, w_up, conv_w, conv_b, layer, tm, tn):
    T, K = xn.shape
    nc = D_FF // tn
    return pl.pallas_call(
        functools.partial(_up_conv_prompt_kernel, tm=tm, rc=min(tm, TM_LOCAL)),
        grid=(T // tm, nc),
        in_specs=[
            pl.BlockSpec((tm, K), lambda i, j: (i, 0)),
            pl.BlockSpec((None, K, tn), lambda i, j: (layer, 0, j)),
            pl.BlockSpec((None, K, tn), lambda i, j: (layer, 0, nc + j)),
            pl.BlockSpec((None, 3, tn), lambda i, j: (layer, 0, j)),
            pl.BlockSpec((None, 1, tn), lambda i, j: (layer, 0, j)),
        ],
        out_specs=[
            pl.BlockSpec((tm, tn), lambda i, j: (i, j)),
            pl.BlockSpec((None, CONV_HIST, tn), lambda i, j: (i, 0, j)),
        ],
        out_shape=[
            jax.ShapeDtypeStruct((T, D_FF), BF16),
            jax.ShapeDtypeStruct((T // tm, CONV_HIST, D_FF), F32),
        ],
        scratch_shapes=[pltpu.VMEM((nc, CONV_HIST, tn), F32)],
        compiler_params=_params("arbitrary", "arbitrary"),
        name="up_conv_prompt",
    )(xn, w_up, w_up, conv_w, conv_b)


def _up_conv_sample_kernel(xn_ref, wa_ref, wv_ref, st_ref, cw_ref, cb_ref, g_ref, so_ref, *, steps, B):
    xn = xn_ref[...]
    a = _dot(xn, wa_ref[...].astype(BF16))
    v = _dot(xn, wv_ref[...].astype(BF16))
    hist = [st_ref[:, 0, :], st_ref[:, 1, :]] + [a[t * B:(t + 1) * B] for t in range(steps)]
    for t in range(steps):
        c = cb_ref[...] + cw_ref[0:1, :] * hist[t] + cw_ref[1:2, :] * hist[t + 1] + cw_ref[2:3, :] * hist[t + 2]
        g_ref[t * B:(t + 1) * B, :] = (_gelu(c) * v[t * B:(t + 1) * B]).astype(BF16)
    so_ref[:, 0, :] = hist[steps]
    so_ref[:, 1, :] = hist[steps + 1]


def up_conv_sample(xn, w_up, state_conv, conv_w, conv_b, layer, steps, tn, state_out):
    T, K = xn.shape
    B = T // steps
    nc = D_FF // tn
    in_specs = [
        pl.BlockSpec((T, K), lambda j: (0, 0)),
        pl.BlockSpec((None, K, tn), lambda j: (layer, 0, j)),
        pl.BlockSpec((None, K, tn), lambda j: (layer, 0, nc + j)),
        pl.BlockSpec((None, B, 2, tn), lambda j: (layer, 0, 0, j)),
        pl.BlockSpec((None, 3, tn), lambda j: (layer, 0, j)),
        pl.BlockSpec((None, 1, tn), lambda j: (layer, 0, j)),
    ]
    return _layer_slab_call(
        functools.partial(_up_conv_sample_kernel, steps=steps, B=B),
        in_specs, [xn, w_up, w_up, state_conv, conv_w, conv_b], state_out,
        grid=(nc,),
        out_specs=[
            pl.BlockSpec((T, tn), lambda j: (0, j)),
            pl.BlockSpec((None, B, 2, tn), lambda j: (layer, 0, 0, j)),
        ],
        out_shape=[
            jax.ShapeDtypeStruct((T, D_FF), BF16),
            jax.ShapeDtypeStruct(state_conv.shape, F32),
        ],
        compiler_params=_params("arbitrary"),
        name="up_conv_sample",
    )


def _hgrn_gates(q_in, z, lb):
    q = q_in * _sigmoid(q_in)
    log_sig = jnp.minimum(z, 0.0) - jnp.log(1.0 + jnp.exp(-jnp.abs(z)))
    a1 = jnp.log(lb)
    a2 = jnp.log1p(-lb) + log_sig
    log_f = jnp.maximum(a1, a2) + jnp.log(1.0 + jnp.exp(-jnp.abs(a1 - a2)))
    k = (1.0 - lb) * _sigmoid(-z)
    return q, log_f, k


def _hgrn_out(o, gate, gn):
    ms = jnp.mean(o * o, axis=-1, keepdims=True)
    return o * lax.rsqrt(ms + EPS) * gn * (gate * _sigmoid(gate))


def _cumsum_rows(x, tril):
    hi = x.astype(BF16)
    r1 = x - hi.astype(F32)
    mid = r1.astype(BF16)
    lo = (r1 - mid.astype(F32)).astype(BF16)
    return _dot(tril, hi) + _dot(tril, mid) + _dot(tril, lo)


def _block_row(x, blk, r):
    C = x.shape[0]
    x3 = x.reshape(C // blk, blk, LANES)
    return jnp.broadcast_to(x3[:, r:r + 1, :], (C // blk, blk, LANES)).reshape(C, LANES)


def _hgrn_chunk(q, log_f, k, v, S):
    C = HGRN_CHUNK
    row = lax.broadcasted_iota(jnp.int32, (C, C), 0)
    col = lax.broadcasted_iota(jnp.int32, (C, C), 1)
    rowl = lax.broadcasted_iota(jnp.int32, (C, LANES), 0)
    tril = jnp.where(col <= row, 1.0, 0.0).astype(BF16)
    b = _cumsum_rows(log_f, tril)

    att = jnp.zeros((C, C), F32)
    roff = rowl & 7
    blk0 = row & ~7
    for s in range(8):
        bs = _block_row(b, 8, s)
        ks = _block_row(k, 8, s)
        e = jnp.exp(jnp.where(roff >= s, b - bs, NEG_BIG))
        r = jnp.sum(q * ks * e, axis=-1, keepdims=True)
        att = jnp.where(col == blk0 + s, r, att)
    m = 8
    while m < C:
        gam = _block_row(b, 2 * m, m - 1)
        isq = (rowl & (2 * m - 1)) >= m
        x = (jnp.where(isq, q, k) * jnp.exp(jnp.where(isq, b - gam, gam - b))).astype(BF16)
        a = _dot_nt(x, x)
        mask = ((row & ~(2 * m - 1)) == (col & ~(2 * m - 1))) & ((row & (2 * m - 1)) >= m) & ((col & (2 * m - 1)) < m)
        att = jnp.where(mask, a, att)
        m *= 2

    vb = v.astype(BF16)
    o = _dot((q * jnp.exp(b)).astype(BF16), S.astype(BF16)) + _dot(att.astype(BF16), vb)
    bl = b[C - 1:C, :]
    kk = k * jnp.exp(bl - b)
    ecol = jnp.transpose(jnp.broadcast_to(jnp.exp(bl), (LANES, LANES)))
    s_new = ecol * S + _dot(jnp.transpose(kk).astype(BF16), vb)
    return o, s_new


def _hgrn_prompt_kernel(q_ref, f_ref, i_ref, g_ref, lb_ref, gn_ref, y_ref, so_ref, s_ref, *, rows):
    @pl.when(pl.program_id(1) == 0)
    def _():
        s_ref[...] = jnp.zeros_like(s_ref)

    lb = lb_ref[...]
    gn = gn_ref[...]
    for c in range(rows // HGRN_CHUNK):
        sl = pl.ds(c * HGRN_CHUNK, HGRN_CHUNK)
        q, log_f, k = _hgrn_gates(q_ref[sl, :], f_ref[sl, :], lb)
        o, s_new = _hgrn_chunk(q, log_f, k, i_ref[sl, :], s_ref[...])
        s_ref[...] = s_new
        y_ref[sl, :] = _hgrn_out(o, g_ref[sl, :], gn).astype(BF16)

    @pl.when(pl.program_id(1) == pl.num_programs(1) - 1)
    def _():
        so_ref[...] = s_ref[...]


def hgrn_prompt(proj, lb, gn, rows):
    T = proj.shape[0]

    def col(off):
        base = off // LANES
        return pl.BlockSpec((rows, LANES), lambda h, c: (c, base + h))

    vec = pl.BlockSpec((1, LANES), lambda h, c: (0, h))
    return pl.pallas_call(
        functools.partial(_hgrn_prompt_kernel, rows=rows),
        grid=(A_HEADS, T // rows),
        in_specs=[col(COL_HQ), col(COL_HF), col(COL_HI), col(COL_HG), vec, vec],
        out_specs=[
            pl.BlockSpec((rows, LANES), lambda h, c: (c, h)),
            pl.BlockSpec((None, A_DK, LANES), lambda h, c: (h, 0, 0)),
        ],
        out_shape=[
            jax.ShapeDtypeStruct((T, MIX_W), BF16),
            jax.ShapeDtypeStruct((A_HEADS, A_DK, LANES), F32),
        ],
        scratch_shapes=[pltpu.VMEM((A_DK, LANES), F32)],
        compiler_params=_params("parallel", "arbitrary"),
        name="hgrn_prompt",
    )(proj, proj, proj, proj, lb, gn)


def _hgrn_sample_kernel(q_ref, f_ref, i_ref, g_ref, lb_ref, gn_ref, s_ref, y_ref, so_ref, *, steps, bb):
    lb = lb_ref[...]
    gn = gn_ref[...]
    qs, ks, vs, bs = [], [], [], []
    b = None
    for t in range(steps):
        q, log_f, k = _hgrn_gates(q_ref[t], f_ref[t], lb)
        b = log_f if b is None else b + log_f
        qs.append(q)
        ks.append(k)
        vs.append(i_ref[t])
        bs.append(b)
    intra = []
    for t in range(steps):
        acc = None
        for s in range(t + 1):
            w = jnp.sum(qs[t] * ks[s] * jnp.exp(bs[t] - bs[s]), axis=-1, keepdims=True)
            acc = w * vs[s] if acc is None else acc + w * vs[s]
        intra.append(acc)
    R = steps * bb
    q_stack = jnp.concatenate([qs[t] * jnp.exp(bs[t]) for t in range(steps)], axis=0).astype(BF16)
    k_stack = jnp.concatenate([ks[t] * jnp.exp(bs[-1] - bs[t]) for t in range(steps)], axis=0)
    v_stack = jnp.concatenate(vs, axis=0).astype(BF16)
    k_t = jnp.transpose(k_stack)
    f_pad = jnp.concatenate([jnp.exp(bs[-1])] + [jnp.zeros((R - bb, LANES), F32)], axis=0)
    f_t = jnp.transpose(f_pad)
    rowi = lax.broadcasted_iota(jnp.int32, (R, LANES), 0) % bb
    lanei = lax.broadcasted_iota(jnp.int32, (LANES, R), 1)

    def body(bi, o_acc):
        s_b = s_ref[bi]
        o_acc = jnp.where(rowi == bi, _dot(q_stack, s_b.astype(BF16)), o_acc)
        f_col = jnp.sum(jnp.where(lanei == bi, f_t, 0.0), axis=-1, keepdims=True)
        k_b = jnp.where(lanei % bb == bi, k_t, 0.0).astype(BF16)
        so_ref[bi] = f_col * s_b + _dot(k_b, v_stack)
        return o_acc

    o_inter = lax.fori_loop(0, bb, body, jnp.zeros((R, LANES), F32), unroll=4)
    for t in range(steps):
        o = o_inter[t * bb:(t + 1) * bb] + intra[t]
        y_ref[t] = _hgrn_out(o, g_ref[t], gn).astype(BF16)


def hgrn_sample(proj3, lb, gn, state, layer, bb, state_out):
    steps, B, _ = proj3.shape
    assert steps * bb == LANES

    def col(off):
        base = off // LANES
        return pl.BlockSpec((steps, bb, LANES), lambda g, h: (0, g, base + h))

    vec = pl.BlockSpec((1, LANES), lambda g, h: (0, h))
    slab = pl.BlockSpec((None, bb, None, A_DK, LANES), lambda g, h: (layer, g, h, 0, 0))
    return _layer_slab_call(
        functools.partial(_hgrn_sample_kernel, steps=steps, bb=bb),
        [col(COL_HQ), col(COL_HF), col(COL_HI), col(COL_HG), vec, vec, slab],
        [proj3, proj3, proj3, proj3, lb, gn, state], state_out,
        grid=(B // bb, A_HEADS),
        out_specs=[pl.BlockSpec((steps, bb, LANES), lambda g, h: (0, g, h)), slab],
        out_shape=[
            jax.ShapeDtypeStruct((steps, B, MIX_W), BF16),
            jax.ShapeDtypeStruct(state.shape, F32),
        ],
        compiler_params=_params("parallel", "arbitrary"),
        name="hgrn_sample",
    )


POOL_HIST = 16


def _pool_project(d, w_ref, sc_ref, g):
    sl = slice(g * POOL_GC, (g + 1) * POOL_GC)
    return _dot(d.astype(BF16), w_ref[g].astype(BF16)) * sc_ref[:, sl]


def _pool_prompt_kernel(u_ref, prev_ref, w_ref, sc_ref, y_ref, ext_ref, *, tm):
    i = pl.program_id(0)
    ext_ref[0:POOL_HIST, :] = jnp.where(i == 0, 0.0, prev_ref[...])
    ext_ref[POOL_HIST:, :] = u_ref[...]
    pos = i * tm + lax.broadcasted_iota(jnp.int32, (tm, 1), 0)
    for g, win in enumerate(POOL_WINDOWS):
        sl = slice(g * POOL_GC, (g + 1) * POOL_GC)
        acc = ext_ref[pl.ds(POOL_HIST, tm), sl]
        for j in range(1, win):
            acc = acc + ext_ref[pl.ds(POOL_HIST - j, tm), sl]
        cnt = jnp.minimum(pos + 1, win).astype(F32)
        d = acc / cnt - u_ref[:, sl]
        y_ref[:, sl] = _pool_project(d, w_ref, sc_ref, g).astype(BF16)


def pool_prompt(proj, pool_w, pool_scale, layer, tm):
    T = proj.shape[0]
    cb = COL_POOL // MIX_W
    return pl.pallas_call(
        functools.partial(_pool_prompt_kernel, tm=tm),
        grid=(T // tm,),
        in_specs=[
            pl.BlockSpec((tm, MIX_W), lambda i: (i, cb)),
            pl.BlockSpec((POOL_HIST, MIX_W), lambda i: (jnp.maximum(i * (tm // POOL_HIST) - 1, 0), cb)),
            pl.BlockSpec((None, len(POOL_WINDOWS), POOL_GC, POOL_GC), lambda i: (layer, 0, 0, 0)),
            pl.BlockSpec((None, 1, MIX_W), lambda i: (layer, 0, 0)),
        ],
        out_specs=pl.BlockSpec((tm, MIX_W), lambda i: (i, 0)),
        out_shape=jax.ShapeDtypeStruct((T, MIX_W), BF16),
        scratch_shapes=[pltpu.VMEM((POOL_HIST + tm, MIX_W), F32)],
        compiler_params=_params("arbitrary"),
        name="pool_prompt",
    )(proj, proj, pool_w, pool_scale)


def _pool_sample_kernel(u_ref, c_ref, w_ref, sc_ref, y_ref, *, steps):
    for t in range(steps):
        for g, win in enumerate(POOL_WINDOWS):
            sl = slice(g * POOL_GC, (g + 1) * POOL_GC)
            acc = u_ref[t, :, sl]
            for j in range(1, win):
                if j <= t:
                    acc = acc + u_ref[t - j, :, sl]
                else:
                    acc = acc + c_ref[:, POOL_BUF + t - j, sl]
            d = acc / float(win) - u_ref[t, :, sl]
            y_ref[t, :, sl] = _pool_project(d, w_ref, sc_ref, g).astype(BF16)


def pool_sample(proj3, cache_pool, pool_w, pool_scale, layer, bb):
    steps, B, _ = proj3.shape
    cb = COL_POOL // MIX_W
    return pl.pallas_call(
        functools.partial(_pool_sample_kernel, steps=steps),
        grid=(B // bb,),
        in_specs=[
            pl.BlockSpec((steps, bb, MIX_W), lambda g: (0, g, cb)),
            pl.BlockSpec((None, bb, POOL_BUF, MIX_W), lambda g: (layer, g, 0, 0)),
            pl.BlockSpec((None, len(POOL_WINDOWS), POOL_GC, POOL_GC), lambda g: (layer, 0, 0, 0)),
            pl.BlockSpec((None, 1, MIX_W), lambda g: (layer, 0, 0)),
        ],
        out_specs=pl.BlockSpec((steps, bb, MIX_W), lambda g: (0, g, 0)),
        out_shape=jax.ShapeDtypeStruct((steps, B, MIX_W), BF16),
        compiler_params=_params("arbitrary"),
        name="pool_sample",
    )(proj3, cache_pool, pool_w, pool_scale)


def _rope_tables(positions):
    half = ROT_DIM // 2
    inv = np.power(ROPE_THETA, -np.arange(0, ROT_DIM, 2, dtype=np.float64) / ROT_DIM)
    ang = np.asarray(positions, np.float64)[:, None] * inv[None, :]
    cos, sin = np.cos(ang), np.sin(ang)
    n = len(positions)
    ct = np.ones((n, LANES))
    sn = np.zeros((n, LANES))
    sp = np.zeros((n, LANES))
    for base in (0, SWA_HEAD_DIM):
        ct[:, base:base + half] = cos
        ct[:, base + half:base + ROT_DIM] = cos
        sn[:, base:base + half] = -sin
        sp[:, base + half:base + ROT_DIM] = sin
    return tuple(jnp.asarray(t, F32) for t in (ct, sn, sp))


def _head_norm_rope(x, g, ct, sn, sp):
    lane = lax.broadcasted_iota(jnp.int32, x.shape, 1)
    lo = lane < SWA_HEAD_DIM
    x2 = x * x
    ms_lo = jnp.sum(jnp.where(lo, x2, 0.0), axis=-1, keepdims=True) / SWA_HEAD_DIM
    ms_hi = jnp.sum(jnp.where(lo, 0.0, x2), axis=-1, keepdims=True) / SWA_HEAD_DIM
    xn = x * jnp.where(lo, lax.rsqrt(ms_lo + EPS), lax.rsqrt(ms_hi + EPS)) * g
    half = ROT_DIM // 2
    return xn * ct + pltpu.roll(xn, LANES - half, 1) * sn + pltpu.roll(xn, half, 1) * sp


def _kprep_kernel(k_ref, g_ref, ct_ref, sn_ref, sp_ref, o_ref):
    ct, sn, sp = ct_ref[...], sn_ref[...], sp_ref[...]
    for j in range(2):
        sl = slice(j * LANES, (j + 1) * LANES)
        o_ref[:, sl] = _head_norm_rope(k_ref[:, sl], g_ref[...], ct, sn, sp)


def swa_kprep(proj, g2, tables, tm):
    T = proj.shape[0]
    kw = SWA_KV_HEADS * SWA_HEAD_DIM
    tab = pl.BlockSpec((tm, LANES), lambda i: (i, 0))
    return pl.pallas_call(
        _kprep_kernel,
        grid=(T // tm,),
        in_specs=[pl.BlockSpec((tm, kw), lambda i: (i, COL_SK // kw)),
                  pl.BlockSpec((1, LANES), lambda i: (0, 0)), tab, tab, tab],
        out_specs=pl.BlockSpec((tm, kw), lambda i: (i, 0)),
        out_shape=jax.ShapeDtypeStruct((T, kw), F32),
        compiler_params=_params("arbitrary"),
        name="swa_kprep",
    )(proj, g2, *tables)


def _dup_head(x, parity):
    lane = lax.broadcasted_iota(jnp.int32, x.shape, 1)
    return jnp.where(lane // SWA_HEAD_DIM == parity, x, pltpu.roll(x, SWA_HEAD_DIM, 1))


def _stack_heads(q):
    lane = lax.broadcasted_iota(jnp.int32, q.shape, 1)
    lo = lane < SWA_HEAD_DIM
    return jnp.concatenate([jnp.where(lo, q, 0.0), jnp.where(lo, 0.0, q)], axis=0)


def _unstack_heads(o2):
    R = o2.shape[0] // 2
    lane = lax.broadcasted_iota(jnp.int32, (R, LANES), 1)
    return jnp.where(lane < SWA_HEAD_DIM, o2[:R], o2[R:])


def _swa_prompt_kernel(sink_ref, q_ref, kc_ref, kp_ref, vc_ref, vp_ref, g_ref, ct_ref, sn_ref, sp_ref, y_ref,
                       *, layer):
    i = pl.program_id(0)
    W = WINDOW
    ct, sn, sp = ct_ref[...], sn_ref[...], sp_ref[...]
    qi = lax.broadcasted_iota(jnp.int32, (2 * W, 2 * W), 0) % W
    ci = lax.broadcasted_iota(jnp.int32, (2 * W, 2 * W), 1)
    valid = (ci > qi) & (ci <= qi + W) & ((i > 0) | (ci >= W))
    top = lax.broadcasted_iota(jnp.int32, (2 * W, 1), 0) < W
    scale = SWA_HEAD_DIM ** -0.5
    for kvh in range(SWA_KV_HEADS):
        ksl = slice((kvh // 2) * LANES, (kvh // 2 + 1) * LANES)
        k_dup = _dup_head(jnp.concatenate([kp_ref[:, ksl], kc_ref[:, ksl]], axis=0), kvh % 2).astype(BF16)
        v_dup = _dup_head(jnp.concatenate([vp_ref[:, ksl], vc_ref[:, ksl]], axis=0), kvh % 2).astype(BF16)
        for jj in range(2):
            j = 2 * kvh + jj
            qsl = slice(j * LANES, (j + 1) * LANES)
            q = _head_norm_rope(q_ref[:, qsl], g_ref[...], ct, sn, sp) * scale
            s = _dot_nt(_stack_heads(q).astype(BF16), k_dup)
            sink = jnp.where(top, sink_ref[layer, 2 * j], sink_ref[layer, 2 * j + 1])
            s = jnp.where(valid, s, NEG_BIG)
            m = jnp.maximum(jnp.max(s, axis=-1, keepdims=True), sink)
            e = jnp.where(valid, jnp.exp(s - m), 0.0)
            p = e / (jnp.sum(e, axis=-1, keepdims=True) + jnp.exp(sink - m))
            y_ref[:, qsl] = _unstack_heads(_dot(p.astype(BF16), v_dup)).astype(BF16)


def swa_prompt(proj, khat, sinks, g2, tables, layer):
    T = proj.shape[0]
    W = WINDOW
    kw = SWA_KV_HEADS * SWA_HEAD_DIM
    tab = pl.BlockSpec((W, LANES), lambda i: (i, 0))
    prev = lambda i: jnp.maximum(i - 1, 0)
    return pl.pallas_call(
        functools.partial(_swa_prompt_kernel, layer=layer),
        grid=(T // W,),
        in_specs=[
            pl.BlockSpec(memory_space=pltpu.SMEM),
            pl.BlockSpec((W, MIX_W), lambda i: (i, COL_SQ // MIX_W)),
            pl.BlockSpec((W, kw), lambda i: (i, 0)),
            pl.BlockSpec((W, kw), lambda i: (prev(i), 0)),
            pl.BlockSpec((W, kw), lambda i: (i, COL_SV // kw)),
            pl.BlockSpec((W, kw), lambda i: (prev(i), COL_SV // kw)),
            pl.BlockSpec((1, LANES), lambda i: (0, 0)), tab, tab, tab,
        ],
        out_specs=pl.BlockSpec((W, MIX_W), lambda i: (i, 0)),
        out_shape=jax.ShapeDtypeStruct((T, MIX_W), BF16),
        compiler_params=_params("arbitrary"),
        name="swa_prompt",
    )(sinks, proj, khat, khat, proj, proj, g2, *tables)


def _swa_sample_kernel(sink_ref, q_ref, kn_ref, vn_ref, kc_ref, vc_ref, g_ref, ct_ref, sn_ref, sp_ref, y_ref,
                       *, layer, steps, bb):
    kvh = pl.program_id(1)
    parity = kvh % 2
    W = WINDOW
    R = steps * bb
    G = SWA_Q_HEADS // SWA_KV_HEADS
    ct, sn, sp = ct_ref[...], sn_ref[...], sp_ref[...]
    scale = SWA_HEAD_DIM ** -0.5
    r4 = lax.broadcasted_iota(jnp.int32, (G * R, 1), 0)
    hh = r4 // R
    tq = (r4 % R) // bb
    bq = r4 % bb
    c_new = lax.broadcasted_iota(jnp.int32, (1, R), 1)
    valid_new = (c_new % bb == bq) & (c_new // bb <= tq)
    c_old = lax.broadcasted_iota(jnp.int32, (1, W), 1)
    valid_old = c_old > tq
    kn = _dup_head(jnp.concatenate([kn_ref[t] for t in range(steps)], axis=0), parity).astype(BF16)
    vn = _dup_head(jnp.concatenate([vn_ref[t] for t in range(steps)], axis=0), parity).astype(BF16)
    qs = []
    for jj in range(G // 2):
        qsl = slice(jj * LANES, (jj + 1) * LANES)
        q = jnp.concatenate([q_ref[t, :, qsl] for t in range(steps)], axis=0)
        qs.append(_stack_heads(_head_norm_rope(q, g_ref[...], ct, sn, sp) * scale))
    q4 = jnp.concatenate(qs, axis=0)
    s_new = jnp.where(valid_new, _dot_nt(q4.astype(BF16), kn), NEG_BIG)
    s_old = None
    for b in range(bb):
        k_t = kc_ref[b].astype(BF16)
        d = _dot(jnp.where(bq == b, q4, 0.0).astype(BF16), jnp.concatenate([k_t, k_t], axis=0))
        s_old = d if s_old is None else s_old + d
    s_old = jnp.where(valid_old, s_old, NEG_BIG)
    sink = sink_ref[layer, G * kvh + G - 1]
    for i in range(G - 2, -1, -1):
        sink = jnp.where(hh == i, sink_ref[layer, G * kvh + i], sink)
    m = jnp.maximum(jnp.maximum(jnp.max(s_new, axis=-1, keepdims=True),
                                jnp.max(s_old, axis=-1, keepdims=True)), sink)
    e_new = jnp.where(valid_new, jnp.exp(s_new - m), 0.0)
    e_old = jnp.where(valid_old, jnp.exp(s_old - m), 0.0)
    den = jnp.sum(e_new, axis=-1, keepdims=True) + jnp.sum(e_old, axis=-1, keepdims=True) + jnp.exp(sink - m)
    p_old = e_old / den
    o = _dot((e_new / den).astype(BF16), vn)
    for b in range(bb):
        v_t = vc_ref[b].astype(BF16)
        o = o + _dot_nt(jnp.where(bq == b, p_old, 0.0).astype(BF16), jnp.concatenate([v_t, v_t], axis=0))
    for jj in range(G // 2):
        o_j = _unstack_heads(o[2 * jj * R:(2 * jj + 2) * R])
        for t in range(steps):
            y_ref[t, :, jj * LANES:(jj + 1) * LANES] = o_j[t * bb:(t + 1) * bb]


def swa_sample(proj3, khat3, cache_kt, cache_vt, sinks, g2, tables, layer, bb):
    steps, B, _ = proj3.shape
    R = steps * bb
    qw = MIX_W // SWA_KV_HEADS
    tab = pl.BlockSpec((R, LANES), lambda g, h: (0, 0))
    cache = pl.BlockSpec((None, bb, None, SWA_HEAD_DIM, WINDOW), lambda g, h: (layer, g, h, 0, 0))
    return pl.pallas_call(
        functools.partial(_swa_sample_kernel, layer=layer, steps=steps, bb=bb),
        grid=(B // bb, SWA_KV_HEADS),
        in_specs=[
            pl.BlockSpec(memory_space=pltpu.SMEM),
            pl.BlockSpec((steps, bb, qw), lambda g, h: (0, g, COL_SQ // qw + h)),
            pl.BlockSpec((steps, bb, LANES), lambda g, h: (0, g, h // 2)),
            pl.BlockSpec((steps, bb, LANES), lambda g, h: (0, g, COL_SV // LANES + h // 2)),
            cache, cache,
            pl.BlockSpec((1, LANES), lambda g, h: (0, 0)), tab, tab, tab,
        ],
        out_specs=pl.BlockSpec((steps, bb, qw), lambda g, h: (0, g, h)),
        out_shape=jax.ShapeDtypeStruct((steps, B, MIX_W), F32),
        compiler_params=_params("parallel", "arbitrary"),
        name="swa_sample",
    )(sinks, proj3, khat3, proj3, cache_kt, cache_vt, g2, *tables)


def _mem_kv_kernel(x_ref, g_ref, w_ref, kg_ref, o_ref, xn_ref):
    j = pl.program_id(0)

    @pl.when(j == 0)
    def _():
        x = x_ref[...]
        ms = jnp.mean(x * x, axis=-1, keepdims=True)
        xn_ref[...] = (x * lax.rsqrt(ms + EPS) * g_ref[...]).astype(BF16)

    y = _dot(xn_ref[...], w_ref[...].astype(BF16))

    @pl.when(j < MEM_HEADS)
    def _():
        ms = jnp.mean(y * y, axis=-1, keepdims=True)
        o_ref[...] = y * lax.rsqrt(ms + EPS) * kg_ref[...]

    @pl.when(j >= MEM_HEADS)
    def _():
        o_ref[...] = y


def mem_kv(mem, mem_norm_g, w_mem_kv, mem_knorm_g, layer):
    M, K = mem.shape
    hd = MEM_HEAD_DIM
    return pl.pallas_call(
        _mem_kv_kernel,
        grid=(2 * MEM_HEADS,),
        in_specs=[
            pl.BlockSpec((M, K), lambda j: (0, 0)),
            pl.BlockSpec((None, 1, K), lambda j: (layer, 0, 0)),
            pl.BlockSpec((None, K, hd), lambda j: (layer, 0, j)),
            pl.BlockSpec((None, 1, hd), lambda j: (layer, 0, 0)),
        ],
        out_specs=pl.BlockSpec((M, hd), lambda j: (0, j)),
        out_shape=jax.ShapeDtypeStruct((M, 2 * MIX_W), F32),
        scratch_shapes=[pltpu.VMEM((M, K), BF16)],
        compiler_params=_params("arbitrary"),
        name="mem_kv",
    )(mem, mem_norm_g, w_mem_kv, mem_knorm_g)


def _mem_qnorm(q, g):
    ms = jnp.mean(q * q, axis=-1, keepdims=True)
    return q * lax.rsqrt(ms + EPS) * g * (MEM_HEAD_DIM ** -0.5)


def _softmax_rows(s):
    m = jnp.max(s, axis=-1, keepdims=True)
    e = jnp.exp(s - m)
    return e / jnp.sum(e, axis=-1, keepdims=True)


def _mem_prompt_kernel(q_ref, k_ref, v_ref, g_ref, y_ref):
    q = _mem_qnorm(q_ref[...], g_ref[...]).astype(BF16)
    p = _softmax_rows(_dot_nt(q, k_ref[...].astype(BF16)))
    y_ref[...] = _dot(p.astype(BF16), v_ref[...].astype(BF16)).astype(BF16)


def mem_attn_prompt(proj, kv, mem_qnorm_g, layer, tq):
    T = proj.shape[0]
    hd = MEM_HEAD_DIM
    return pl.pallas_call(
        _mem_prompt_kernel,
        grid=(T // tq, MEM_HEADS),
        in_specs=[
            pl.BlockSpec((tq, hd), lambda i, h: (i, COL_MQ // hd + h)),
            pl.BlockSpec((N_MEM, hd), lambda i, h: (0, h)),
            pl.BlockSpec((N_MEM, hd), lambda i, h: (0, MEM_HEADS + h)),
            pl.BlockSpec((None, 1, hd), lambda i, h: (layer, 0, 0)),
        ],
        out_specs=pl.BlockSpec((tq, hd), lambda i, h: (i, h)),
        out_shape=jax.ShapeDtypeStruct((T, MIX_W), BF16),
        compiler_params=_params("parallel", "arbitrary"),
        name="mem_prompt",
    )(proj, kv, kv, mem_qnorm_g)


def _mem_rows_view(c):
    L_, B_, M, H, hd = c.shape
    c = c.reshape(L_, B_, M, H, hd // LANES, LANES)
    return jnp.transpose(c, (0, 1, 2, 4, 3, 5)).reshape(L_, B_, M * H * (hd // LANES), LANES)


def _mem_head(c_ref, b, h):
    nt = MEM_HEAD_DIM // LANES
    parts = [c_ref[b, pl.ds(lt * MEM_HEADS + h, N_MEM, stride=nt * MEM_HEADS), :] for lt in range(nt)]
    return jnp.concatenate(parts, axis=1).astype(BF16)


def _mem_sample_kernel(q0_ref, q1_ref, q2_ref, q3_ref, k_ref, v_ref, g_ref, y_ref, *, steps, bb):
    R = steps * bb
    hd = MEM_HEAD_DIM
    bq = lax.broadcasted_iota(jnp.int32, (R, 1), 0) % bb
    for h, q_ref in enumerate((q0_ref, q1_ref, q2_ref, q3_ref)):
        q = _mem_qnorm(jnp.concatenate([q_ref[t] for t in range(steps)], axis=0), g_ref[...])
        s = None
        for b in range(bb):
            d = _dot_nt(jnp.where(bq == b, q, 0.0).astype(BF16), _mem_head(k_ref, b, h))
            s = d if s is None else s + d
        p = _softmax_rows(s)
        o = None
        for b in range(bb):
            d = _dot(jnp.where(bq == b, p, 0.0).astype(BF16), _mem_head(v_ref, b, h))
            o = d if o is None else o + d
        for t in range(steps):
            y_ref[t, :, h * hd:(h + 1) * hd] = o[t * bb:(t + 1) * bb]


def mem_attn_sample(proj3, cache_k, cache_v, mem_qnorm_g, layer, bb):
    steps, B, _ = proj3.shape
    hd = MEM_HEAD_DIM
    cache = pl.BlockSpec((None, bb) + cache_k.shape[2:], lambda g: (layer, g, 0, 0))

    def q_spec(h):
        cb = COL_MQ // hd + h
        return pl.BlockSpec((steps, bb, hd), lambda g: (0, g, cb))

    return pl.pallas_call(
        functools.partial(_mem_sample_kernel, steps=steps, bb=bb),
        grid=(B // bb,),
        in_specs=[q_spec(h) for h in range(MEM_HEADS)] + [
            cache, cache,
            pl.BlockSpec((None, 1, hd), lambda g: (layer, 0, 0)),
        ],
        out_specs=pl.BlockSpec((steps, bb, MIX_W), lambda g: (0, g, 0)),
        out_shape=jax.ShapeDtypeStruct((steps, B, MIX_W), F32),
        compiler_params=_params("arbitrary"),
        name="mem_sample",
    )(proj3, proj3, proj3, proj3, cache_k, cache_v, mem_qnorm_g)


def _row_tile(T, cap):
    t = cap
    while T % t:
        t //= 2
    return t


TM_STREAM = 2048
TM_DOWN = 1024
TM_LOCAL = 512


def _token_tail(x, xn, ys, layer, w_in, w_branch, w_o, norm2_g):
    T = x.shape[0]
    merged = merge_branches(xn, ys, w_in, w_branch, layer, _row_tile(T, 512), 512)
    h = matmul_res(merged, w_o, layer, x, _row_tile(T, TM_DOWN), 512)
    return h, prenorm(h, norm2_g, layer, _row_tile(T, TM_LOCAL))


def kernel(x_prompt, x_sample, mem_prompt, state_hgrn, cache_pool, cache_swa_k, cache_swa_v, state_conv, cache_mem_k, cache_mem_v, norm1_g, w_in, hgrn_lb, hgrn_norm_g, pool_w, pool_scale, swa_qnorm_g, swa_knorm_g, swa_sinks, mem_norm_g, w_mem_kv, mem_qnorm_g, mem_knorm_g, w_branch, w_o, norm2_g, w_up, conv_w, conv_b, w_down):
    depth = w_in.shape[0]
    bp, L, _ = x_prompt.shape
    B, steps, _ = x_sample.shape
    assert bp == 1
    kw = SWA_KV_HEADS * SWA_HEAD_DIM
    Ts = steps * B

    lb_all = jnp.cumsum(jax.nn.softmax(hgrn_lb.astype(F32), axis=0), axis=0)
    lb_all = lb_all - lb_all[:1]

    swa_bb = 8
    tab_p = _rope_tables(np.arange(L))
    tab_s = _rope_tables(np.repeat(PAST_LEN + np.arange(steps), B))
    tab_sb = _rope_tables(np.repeat(PAST_LEN + np.arange(steps), swa_bb))

    xp = x_prompt.reshape(L, D_MODEL)
    xs = jnp.transpose(x_sample, (1, 0, 2)).reshape(Ts, D_MODEL)
    mem = mem_prompt.reshape(N_MEM, D_MODEL)
    ckt_all = jnp.transpose(cache_swa_k, (0, 1, 3, 4, 2))
    cvt_all = jnp.transpose(cache_swa_v, (0, 1, 3, 4, 2))
    mk_rows = _mem_rows_view(cache_mem_k)
    mv_rows = _mem_rows_view(cache_mem_v)
    row3 = lambda a: a.reshape(depth, 1, a.shape[-1])
    norm1_g, norm2_g, pool_scale, conv_b = row3(norm1_g), row3(norm2_g), row3(pool_scale), row3(conv_b)
    mem_norm_g, mem_qnorm_g, mem_knorm_g = row3(mem_norm_g), row3(mem_qnorm_g), row3(mem_knorm_g)
    tm_p = _row_tile(L, TM_STREAM)
    tl_p = _row_tile(L, TM_LOCAL)

    outs = {k: [] for k in ("sp", "pp", "ps", "kp", "ks", "vp", "vs", "cp", "mk", "mv")}
    hgrn_states = None
    conv_states = None
    for l in range(depth):
        lb = lb_all[l].reshape(1, MIX_W)
        gn = hgrn_norm_g[l].reshape(1, MIX_W)
        gq2 = jnp.tile(swa_qnorm_g[l], 2).reshape(1, LANES)
        gk2 = jnp.tile(swa_knorm_g[l], 2).reshape(1, LANES)

        kv = mem_kv(mem, mem_norm_g, w_mem_kv, mem_knorm_g, l)

        xn = prenorm(xp, norm1_g, l, tl_p)
        proj = matmul_cols(xn, w_in, l, COL_GATE, tm_p, 512)
        ya, s_p = hgrn_prompt(proj, lb, gn, tl_p)
        yb = pool_prompt(proj, pool_w, pool_scale, l, tl_p)
        khat = swa_kprep(proj, gk2, tab_p, tl_p)
        yc = swa_prompt(proj, khat, swa_sinks, gq2, tab_p, l)
        ym = mem_attn_prompt(proj, kv, mem_qnorm_g, l, tl_p)
        h, hn = _token_tail(xp, xn, (ya, yb, yc, ym), l, w_in, w_branch, w_o, norm2_g)
        gact, a_tail = up_conv_prompt(hn, w_up, conv_w, conv_b, l, tm_p, 256)
        xp = matmul_res(gact, w_down, l, h, _row_tile(L, TM_DOWN), 256)

        outs["sp"].append(s_p[None])
        outs["pp"].append(proj[None, L - POOL_BUF:, COL_POOL:COL_POOL + MIX_W])
        outs["kp"].append(khat[None, L - WINDOW:].reshape(1, WINDOW, SWA_KV_HEADS, SWA_HEAD_DIM))
        outs["vp"].append(proj[None, L - WINDOW:, COL_SV:COL_SV + kw].reshape(1, WINDOW, SWA_KV_HEADS, SWA_HEAD_DIM))
        outs["cp"].append(a_tail[-1:, CONV_HIST - 2:])
        outs["mk"].append(kv[None, :, :MIX_W].reshape(1, N_MEM, MEM_HEADS, MEM_HEAD_DIM))
        outs["mv"].append(kv[None, :, MIX_W:].reshape(1, N_MEM, MEM_HEADS, MEM_HEAD_DIM))

        xn = prenorm(xs, norm1_g, l, Ts)
        proj_s = matmul_cols(xn, w_in, l, COL_GATE, Ts, 512)
        proj3 = proj_s.reshape(steps, B, COL_GATE)
        ya, hgrn_states = hgrn_sample(proj3, lb, gn, state_hgrn, l, LANES // steps, hgrn_states)
        yb = pool_sample(proj3, cache_pool, pool_w, pool_scale, l, 64)
        khat_s = swa_kprep(proj_s, gk2, tab_s, Ts)
        khat3 = khat_s.reshape(steps, B, kw)
        yc = swa_sample(proj3, khat3, ckt_all, cvt_all, swa_sinks, gq2, tab_sb, l, swa_bb)
        ym = mem_attn_sample(proj3, mk_rows, mv_rows, mem_qnorm_g, l, 8)
        ys = tuple(y.reshape(Ts, MIX_W).astype(BF16) for y in (ya, yb, yc, ym))
        h, hn = _token_tail(xs, xn, ys, l, w_in, w_branch, w_o, norm2_g)
        gact, conv_states = up_conv_sample(hn, w_up, state_conv, conv_w, conv_b, l, steps, 256, conv_states)
        xs = matmul_res(gact, w_down, l, h, Ts, 256)

        u_new = jnp.transpose(proj3[:, :, COL_POOL:COL_POOL + MIX_W], (1, 0, 2))
        outs["ps"].append(jnp.concatenate([cache_pool[l], u_new], axis=1)[:, -POOL_BUF:])
        k_new = jnp.transpose(khat3, (1, 0, 2)).reshape(B, steps, SWA_KV_HEADS, SWA_HEAD_DIM)
        v_new = jnp.transpose(proj3[:, :, COL_SV:COL_SV + kw], (1, 0, 2)).reshape(B, steps, SWA_KV_HEADS, SWA_HEAD_DIM)
        outs["ks"].append(jnp.concatenate([cache_swa_k[l], k_new], axis=1)[:, -WINDOW:])
        outs["vs"].append(jnp.concatenate([cache_swa_v[l], v_new], axis=1)[:, -WINDOW:])

    stk = lambda k: jnp.stack(outs[k], axis=0)
    y_prompt = xp.reshape(1, L, D_MODEL)
    y_sample = jnp.transpose(xs.reshape(steps, B, D_MODEL), (1, 0, 2))
    return (y_prompt, y_sample,
            stk("sp"), hgrn_states, stk("pp"), stk("ps"), stk("kp"), stk("ks"), stk("vp"), stk("vs"),
            stk("cp"), conv_states, jnp.concatenate(outs["mk"], axis=0)[:, None], jnp.concatenate(outs["mv"], axis=0)[:, None])
```

```python
import functools

import numpy as np
import jax
import jax.numpy as jnp
from jax import lax
from jax.experimental import pallas as pl
from jax.experimental.pallas import tpu as pltpu

F32 = jnp.float32
BF16 = jnp.bfloat16

D_MODEL = 2048
MIX_W = D_MODEL // 2
N_BRANCH = 4
A_DK = 128
A_HEADS = MIX_W // A_DK
POOL_WINDOWS = (2, 4, 8, 16)
POOL_GC = MIX_W // len(POOL_WINDOWS)
POOL_BUF = max(POOL_WINDOWS) - 1
SWA_HEAD_DIM = 64
SWA_Q_HEADS = MIX_W // SWA_HEAD_DIM
SWA_KV_HEADS = SWA_Q_HEADS // 4
WINDOW = 128
ROT_DIM = SWA_HEAD_DIM // 4
ROPE_THETA = 500000.0
N_MEM = 256
MEM_HEADS = 4
MEM_HEAD_DIM = MIX_W // MEM_HEADS
D_FF = 11 * D_MODEL // 4
EPS = 1e-6
PAST_LEN = 8192

COL_HQ, COL_HF, COL_HI, COL_HG = 0, MIX_W, 2 * MIX_W, 3 * MIX_W
COL_POOL = 4 * MIX_W
COL_SQ = 5 * MIX_W
COL_SK = 6 * MIX_W
COL_SV = COL_SK + SWA_KV_HEADS * SWA_HEAD_DIM
COL_MQ = COL_SV + SWA_KV_HEADS * SWA_HEAD_DIM
COL_GATE = COL_MQ + MIX_W
IN_COLS = COL_GATE + N_BRANCH * D_MODEL

LANES = 128
HGRN_CHUNK = 128
VMEM_LIMIT = 56 * 1024 * 1024
NEG_BIG = -1e30


def _params(*sem):
    return pltpu.CompilerParams(dimension_semantics=sem, vmem_limit_bytes=VMEM_LIMIT)


def _sigmoid(x):
    return 1.0 / (1.0 + jnp.exp(-x))


def _dot(a, b):
    return jnp.dot(a, b, preferred_element_type=F32)


def _dot_nt(a, b):
    return lax.dot_general(a, b, (((1,), (1,)), ((), ())), preferred_element_type=F32)


def _skip_ref(kernel_fn, idx):
    def wrapped(*refs):
        return kernel_fn(*refs[:idx], *refs[idx + 1:])
    return wrapped


def _layer_slab_call(kernel_fn, in_specs, args, slab_out, **kw):
    if slab_out is not None:
        idx = len(args)
        in_specs = list(in_specs) + [pl.BlockSpec(memory_space=pl.ANY)]
        args = list(args) + [slab_out]
        kernel_fn = _skip_ref(kernel_fn, idx)
        kw["input_output_aliases"] = {idx: len(kw["out_shape"]) - 1}
    return pl.pallas_call(kernel_fn, in_specs=in_specs, **kw)(*args)


def _rms_rows(x, g):
    ms = jnp.mean(x * x, axis=-1, keepdims=True)
    return x * lax.rsqrt(ms + EPS) * g


def _prenorm_kernel(x_ref, g_ref, o_ref):
    o_ref[...] = _rms_rows(x_ref[...], g_ref[...]).astype(BF16)


def prenorm(x, g, layer, tm):
    T, K = x.shape
    return pl.pallas_call(
        _prenorm_kernel,
        grid=(T // tm,),
        in_specs=[pl.BlockSpec((tm, K), lambda i: (i, 0)),
                  pl.BlockSpec((None, 1, K), lambda i: (layer, 0, 0))],
        out_specs=pl.BlockSpec((tm, K), lambda i: (i, 0)),
        out_shape=jax.ShapeDtypeStruct((T, K), BF16),
        compiler_params=_params("arbitrary"),
        name="prenorm",
    )(x, g)


def _matmul_kernel(a_ref, w_ref, o_ref):
    o_ref[...] = _dot(a_ref[...], w_ref[...].astype(BF16))


def matmul_cols(a, w, layer, n_cols, tm, tn):
    T, K = a.shape
    return pl.pallas_call(
        _matmul_kernel,
        grid=(T // tm, n_cols // tn),
        in_specs=[
            pl.BlockSpec((tm, K), lambda i, j: (i, 0)),
            pl.BlockSpec((None, K, tn), lambda i, j: (layer, 0, j)),
        ],
        out_specs=pl.BlockSpec((tm, tn), lambda i, j: (i, j)),
        out_shape=jax.ShapeDtypeStruct((T, n_cols), F32),
        compiler_params=_params("parallel", "arbitrary"),
        name="matmul_cols",
    )(a, w)


def _matmul_res_kernel(a_ref, w_ref, r_ref, o_ref):
    o_ref[...] = r_ref[...] + _dot(a_ref[...], w_ref[...].astype(BF16))


def matmul_res(a, w, layer, res, tm, tn):
    T, K = a.shape
    N = w.shape[2]
    return pl.pallas_call(
        _matmul_res_kernel,
        grid=(T // tm, N // tn),
        in_specs=[
            pl.BlockSpec((tm, K), lambda i, j: (i, 0)),
            pl.BlockSpec((None, K, tn), lambda i, j: (layer, 0, j)),
            pl.BlockSpec((tm, tn), lambda i, j: (i, j)),
        ],
        out_specs=pl.BlockSpec((tm, tn), lambda i, j: (i, j)),
        out_shape=jax.ShapeDtypeStruct((T, N), F32),
        compiler_params=_params("parallel", "arbitrary"),
        name="matmul_res",
    )(a, w, res)


def _merge_kernel(xn_ref, ya_ref, yb_ref, yc_ref, ym_ref, wg0_ref, wg1_ref, wg2_ref, wg3_ref, wb_ref, o_ref,
                  wg_s, wb_s):
    @pl.when(pl.program_id(1) == 0)
    def _():
        for n, wg_ref in enumerate((wg0_ref, wg1_ref, wg2_ref, wg3_ref)):
            wg_s[n] = wg_ref[...].astype(BF16)
            wb_s[n] = wb_ref[n].astype(BF16)

    xn = xn_ref[...]
    acc = None
    for n, y_ref in enumerate((ya_ref, yb_ref, yc_ref, ym_ref)):
        t = _sigmoid(_dot(xn, wg_s[n])) * _dot(y_ref[...], wb_s[n])
        acc = t if acc is None else acc + t
    o_ref[...] = acc.astype(BF16)


def merge_branches(xn, ys, w_in, w_branch, layer, tm, tn):
    T = xn.shape[0]
    once = pl.Buffered(1)
    y_spec = pl.BlockSpec((tm, MIX_W), lambda j, i: (i, 0))

    def gate_spec(n):
        off = (COL_GATE + n * D_MODEL) // tn
        return pl.BlockSpec((None, D_MODEL, tn), lambda j, i: (layer, 0, off + j), pipeline_mode=once)

    return pl.pallas_call(
        _merge_kernel,
        grid=(D_MODEL // tn, T // tm),
        in_specs=[pl.BlockSpec((tm, D_MODEL), lambda j, i: (i, 0))] + [y_spec] * 4
        + [gate_spec(n) for n in range(N_BRANCH)]
        + [pl.BlockSpec((None, N_BRANCH, MIX_W, tn), lambda j, i: (layer, 0, 0, j), pipeline_mode=once)],
        out_specs=pl.BlockSpec((tm, tn), lambda j, i: (i, j)),
        out_shape=jax.ShapeDtypeStruct((T, D_MODEL), BF16),
        scratch_shapes=[pltpu.VMEM((N_BRANCH, D_MODEL, tn), BF16), pltpu.VMEM((N_BRANCH, MIX_W, tn), BF16)],
        compiler_params=_params("arbitrary", "arbitrary"),
        name="merge_branches",
    )(xn, *ys, w_in, w_in, w_in, w_in, w_branch)


CONV_HIST = 8


def _gelu(x):
    return 0.5 * x * (1.0 + lax.erf(x * (2.0 ** -0.5)))


def _up_conv_prompt_kernel(xn_ref, wa_ref, wv_ref, cw_ref, cb_ref, g_ref, tail_ref, carry_ref, *, tm, rc):
    i, j = pl.program_id(0), pl.program_id(1)
    wa = wa_ref[...].astype(BF16)
    wv = wv_ref[...].astype(BF16)
    prev = jnp.where(i == 0, 0.0, carry_ref[j])
    row = lax.broadcasted_iota(jnp.int32, (rc, wa.shape[1]), 0)
    for c in range(tm // rc):
        sl = pl.ds(c * rc, rc)
        xn = xn_ref[sl, :]
        a = _dot(xn, wa)
        v = _dot(xn, wv)
        a1 = jnp.where(row == 0, prev[CONV_HIST - 1:CONV_HIST], pltpu.roll(a, 1, 0))
        a2 = jnp.where(row == 0, prev[CONV_HIST - 2:CONV_HIST - 1],
                       jnp.where(row == 1, prev[CONV_HIST - 1:CONV_HIST], pltpu.roll(a, 2, 0)))
        cc = cb_ref[...] + cw_ref[0:1, :] * a2 + cw_ref[1:2, :] * a1 + cw_ref[2:3, :] * a
        g_ref[sl, :] = (_gelu(cc) * v).astype(BF16)
        prev = a[rc - CONV_HIST:, :]
    carry_ref[j] = prev
    tail_ref[...] = prev


def up_conv_prompt(xn, w_up, conv_w, conv_b, layer, tm, tn):
    T, K = xn.shape
    nc = D_FF // tn
    return pl.pallas_call(
        functools.partial(_up_conv_prompt_kernel, tm=tm, rc=min(tm, TM_LOCAL)),
        grid=(T // tm, nc),
        in_specs=[
            pl.BlockSpec((tm, K), lambda i, j: (i, 0)),
            pl.BlockSpec((None, K, tn), lambda i, j: (layer, 0, j)),
            pl.BlockSpec((None, K, tn), lambda i, j: (layer, 0, nc + j)),
            pl.BlockSpec((None, 3, tn), lambda i, j: (layer, 0, j)),
            pl.BlockSpec((None, 1, tn), lambda i, j: (layer, 0, j)),
        ],
        out_specs=[
            pl.BlockSpec((tm, tn), lambda i, j: (i, j)),
            pl.BlockSpec((None, CONV_HIST, tn), lambda i, j: (i, 0, j)),
        ],
        out_shape=[
            jax.ShapeDtypeStruct((T, D_FF), BF16),
            jax.ShapeDtypeStruct((T // tm, CONV_HIST, D_FF), F32),
        ],
        scratch_shapes=[pltpu.VMEM((nc, CONV_HIST, tn), F32)],
        compiler_params=_params("arbitrary", "arbitrary"),
        name="up_conv_prompt",
    )(xn, w_up, w_up, conv_w, conv_b)


def _up_conv_sample_kernel(xn_ref, wa_ref, wv_ref, st_ref, cw_ref, cb_ref, g_ref, so_ref, *, steps, B):
    xn = xn_ref[...]
    a = _dot(xn, wa_ref[...].astype(BF16))
    v = _dot(xn, wv_ref[...].astype(BF16))
    hist = [st_ref[:, 0, :], st_ref[:, 1, :]] + [a[t * B:(t + 1) * B] for t in range(steps)]
    for t in range(steps):
        c = cb_ref[...] + cw_ref[0:1, :] * hist[t] + cw_ref[1:2, :] * hist[t + 1] + cw_ref[2:3, :] * hist[t + 2]
        g_ref[t * B:(t + 1) * B, :] = (_gelu(c) * v[t * B:(t + 1) * B]).astype(BF16)
    so_ref[:, 0, :] = hist[steps]
    so_ref[:, 1, :] = hist[steps + 1]


def up_conv_sample(xn, w_up, state_conv, conv_w, conv_b, layer, steps, tn, state_out):
    T, K = xn.shape
    B = T // steps
    nc = D_FF // tn
    in_specs = [
        pl.BlockSpec((T, K), lambda j: (0, 0)),
        pl.BlockSpec((None, K, tn), lambda j: (layer, 0, j)),
        pl.BlockSpec((None, K, tn), lambda j: (layer, 0, nc + j)),
        pl.BlockSpec((None, B, 2, tn), lambda j: (layer, 0, 0, j)),
        pl.BlockSpec((None, 3, tn), lambda j: (layer, 0, j)),
        pl.BlockSpec((None, 1, tn), lambda j: (layer, 0, j)),
    ]
    return _layer_slab_call(
        functools.partial(_up_conv_sample_kernel, steps=steps, B=B),
        in_specs, [xn, w_up, w_up, state_conv, conv_w, conv_b], state_out,
        grid=(nc,),
        out_specs=[
            pl.BlockSpec((T, tn), lambda j: (0, j)),
            pl.BlockSpec((None, B, 2, tn), lambda j: (layer, 0, 0, j)),
        ],
        out_shape=[
            jax.ShapeDtypeStruct((T, D_FF), BF16),
            jax.ShapeDtypeStruct(state_conv.shape, F32),
        ],
        compiler_params=_params("arbitrary"),
        name="up_conv_sample",
    )


def _hgrn_gates(q_in, z, lb):
    q = q_in * _sigmoid(q_in)
    e = jnp.exp(-jnp.abs(z))
    d = 1.0 + e
    log_sig = jnp.minimum(z, 0.0) - jnp.log(d)
    a1 = jnp.log(lb)
    a2 = jnp.log1p(-lb) + log_sig
    log_f = jnp.maximum(a1, a2) + jnp.log(1.0 + jnp.exp(-jnp.abs(a1 - a2)))
    k = (1.0 - lb) * (jnp.where(z >= 0.0, e, 1.0) / d)
    return q, log_f, k


def _hgrn_out(o, gate, gn):
    ms = jnp.mean(o * o, axis=-1, keepdims=True)
    return o * lax.rsqrt(ms + EPS) * gn * (gate * _sigmoid(gate))


def _cumsum_rows(x, tril):
    hi = x.astype(BF16)
    r1 = x - hi.astype(F32)
    mid = r1.astype(BF16)
    lo = (r1 - mid.astype(F32)).astype(BF16)
    return _dot(tril, hi) + _dot(tril, mid) + _dot(tril, lo)


def _block_row(x, blk, r):
    C = x.shape[0]
    x3 = x.reshape(C // blk, blk, LANES)
    return jnp.broadcast_to(x3[:, r:r + 1, :], (C // blk, blk, LANES)).reshape(C, LANES)


def _hgrn_pair_codes():
    t = np.arange(HGRN_CHUNK)[:, None]
    s = np.arange(HGRN_CHUNK)[None, :]
    level = np.floor(np.log2(np.maximum(t ^ s, 1))).astype(np.int32)
    return jnp.asarray(np.where(s > t, -1, np.where(s == t, 0, 1 + level)), jnp.int32)


def _boundary_row(b, m):
    if 2 * m >= 8:
        return _block_row(b, 2 * m, m - 1)
    r8 = lax.broadcasted_iota(jnp.int32, b.shape, 0) & 7
    if m == 2:
        return jnp.where(r8 < 4, _block_row(b, 8, 1), _block_row(b, 8, 5))
    return jnp.where(r8 < 2, _block_row(b, 8, 0),
                     jnp.where(r8 < 4, _block_row(b, 8, 2),
                               jnp.where(r8 < 6, _block_row(b, 8, 4), _block_row(b, 8, 6))))


def _hgrn_chunk(q, log_f, k, v, S, code):
    C = HGRN_CHUNK
    rowl = lax.broadcasted_iota(jnp.int32, (C, LANES), 0)
    b = _cumsum_rows(log_f, jnp.where(code >= 0, 1.0, 0.0).astype(BF16))

    att = jnp.where(code == 0, jnp.sum(q * k, axis=-1, keepdims=True), 0.0)
    m, level = 1, 1
    while m < C:
        d = b - _boundary_row(b, m)
        isq = (rowl & m) != 0
        x = (jnp.where(isq, q, k) * jnp.exp(jnp.where(isq, d, -d))).astype(BF16)
        att = jnp.where(code == level, _dot_nt(x, x), att)
        m *= 2
        level += 1

    vb = v.astype(BF16)
    o = _dot((q * jnp.exp(b)).astype(BF16), S.astype(BF16)) + _dot(att.astype(BF16), vb)
    bl = b[C - 1:C, :]
    kk = k * jnp.exp(bl - b)
    ecol = jnp.transpose(jnp.broadcast_to(jnp.exp(bl), (LANES, LANES)))
    s_new = ecol * S + _dot(jnp.transpose(kk).astype(BF16), vb)
    return o, s_new


def _hgrn_prompt_kernel(q_ref, f_ref, i_ref, g_ref, lb_ref, gn_ref, code_ref, y_ref, so_ref, s_ref, *, rows):
    @pl.when(pl.program_id(1) == 0)
    def _():
        s_ref[...] = jnp.zeros_like(s_ref)

    lb = lb_ref[...]
    gn = gn_ref[...]
    code = code_ref[...]
    for c in range(rows // HGRN_CHUNK):
        sl = pl.ds(c * HGRN_CHUNK, HGRN_CHUNK)
        q, log_f, k = _hgrn_gates(q_ref[sl, :], f_ref[sl, :], lb)
        o, s_new = _hgrn_chunk(q, log_f, k, i_ref[sl, :], s_ref[...], code)
        s_ref[...] = s_new
        y_ref[sl, :] = _hgrn_out(o, g_ref[sl, :], gn).astype(BF16)

    @pl.when(pl.program_id(1) == pl.num_programs(1) - 1)
    def _():
        so_ref[...] = s_ref[...]


def hgrn_prompt(proj, lb, gn, rows):
    T = proj.shape[0]

    def col(off):
        base = off // LANES
        return pl.BlockSpec((rows, LANES), lambda h, c: (c, base + h))

    vec = pl.BlockSpec((1, LANES), lambda h, c: (0, h))
    return pl.pallas_call(
        functools.partial(_hgrn_prompt_kernel, rows=rows),
        grid=(A_HEADS, T // rows),
        in_specs=[col(COL_HQ), col(COL_HF), col(COL_HI), col(COL_HG), vec, vec,
                  pl.BlockSpec((HGRN_CHUNK, HGRN_CHUNK), lambda h, c: (0, 0))],
        out_specs=[
            pl.BlockSpec((rows, LANES), lambda h, c: (c, h)),
            pl.BlockSpec((None, A_DK, LANES), lambda h, c: (h, 0, 0)),
        ],
        out_shape=[
            jax.ShapeDtypeStruct((T, MIX_W), BF16),
            jax.ShapeDtypeStruct((A_HEADS, A_DK, LANES), F32),
        ],
        scratch_shapes=[pltpu.VMEM((A_DK, LANES), F32)],
        compiler_params=_params("parallel", "arbitrary"),
        name="hgrn_prompt",
    )(proj, proj, proj, proj, lb, gn, _hgrn_pair_codes())


def _hgrn_sample_kernel(q_ref, f_ref, i_ref, g_ref, lb_ref, gn_ref, s_ref, y_ref, so_ref, *, steps, bb):
    lb = lb_ref[...]
    gn = gn_ref[...]
    qs, ks, vs, bs = [], [], [], []
    b = None
    for t in range(steps):
        q, log_f, k = _hgrn_gates(q_ref[t], f_ref[t], lb)
        b = log_f if b is None else b + log_f
        qs.append(q)
        ks.append(k)
        vs.append(i_ref[t])
        bs.append(b)
    intra = []
    for t in range(steps):
        acc = None
        for s in range(t + 1):
            w = jnp.sum(qs[t] * ks[s] * jnp.exp(bs[t] - bs[s]), axis=-1, keepdims=True)
            acc = w * vs[s] if acc is None else acc + w * vs[s]
        intra.append(acc)
    R = steps * bb
    q_stack = jnp.concatenate([qs[t] * jnp.exp(bs[t]) for t in range(steps)], axis=0).astype(BF16)
    k_stack = jnp.concatenate([ks[t] * jnp.exp(bs[-1] - bs[t]) for t in range(steps)], axis=0)
    v_stack = jnp.concatenate(vs, axis=0).astype(BF16)
    k_t = jnp.transpose(k_stack)
    f_pad = jnp.concatenate([jnp.exp(bs[-1])] + [jnp.zeros((R - bb, LANES), F32)], axis=0)
    f_t = jnp.transpose(f_pad)
    rowi = lax.broadcasted_iota(jnp.int32, (R, LANES), 0) % bb
    lanei = lax.broadcasted_iota(jnp.int32, (LANES, R), 1)

    def body(bi, o_acc):
        s_b = s_ref[bi]
        o_acc = jnp.where(rowi == bi, _dot(q_stack, s_b.astype(BF16)), o_acc)
        f_col = jnp.sum(jnp.where(lanei == bi, f_t, 0.0), axis=-1, keepdims=True)
        k_b = jnp.where(lanei % bb == bi, k_t, 0.0).astype(BF16)
        so_ref[bi] = f_col * s_b + _dot(k_b, v_stack)
        return o_acc

    o_inter = lax.fori_loop(0, bb, body, jnp.zeros((R, LANES), F32), unroll=4)
    for t in range(steps):
        o = o_inter[t * bb:(t + 1) * bb] + intra[t]
        y_ref[t] = _hgrn_out(o, g_ref[t], gn).astype(BF16)


def hgrn_sample(proj3, lb, gn, state, layer, bb, state_out):
    steps, B, _ = proj3.shape
    assert steps * bb == LANES

    def col(off):
        base = off // LANES
        return pl.BlockSpec((steps, bb, LANES), lambda g, h: (0, g, base + h))

    vec = pl.BlockSpec((1, LANES), lambda g, h: (0, h))
    slab = pl.BlockSpec((None, bb, None, A_DK, LANES), lambda g, h: (layer, g, h, 0, 0))
    return _layer_slab_call(
        functools.partial(_hgrn_sample_kernel, steps=steps, bb=bb),
        [col(COL_HQ), col(COL_HF), col(COL_HI), col(COL_HG), vec, vec, slab],
        [proj3, proj3, proj3, proj3, lb, gn, state], state_out,
        grid=(B // bb, A_HEADS),
        out_specs=[pl.BlockSpec((steps, bb, LANES), lambda g, h: (0, g, h)), slab],
        out_shape=[
            jax.ShapeDtypeStruct((steps, B, MIX_W), BF16),
            jax.ShapeDtypeStruct(state.shape, F32),
        ],
        compiler_params=_params("parallel", "arbitrary"),
        name="hgrn_sample",
    )


POOL_HIST = 16


def _pool_project(d, w_ref, sc_ref, g):
    sl = slice(g * POOL_GC, (g + 1) * POOL_GC)
    return _dot(d.astype(BF16), w_ref[g].astype(BF16)) * sc_ref[:, sl]


def _pool_prompt_kernel(u_ref, prev_ref, w_ref, sc_ref, y_ref, ext_ref, *, tm):
    i = pl.program_id(0)
    ext_ref[0:POOL_HIST, :] = jnp.where(i == 0, 0.0, prev_ref[...])
    ext_ref[POOL_HIST:, :] = u_ref[...]
    pos = i * tm + lax.broadcasted_iota(jnp.int32, (tm, 1), 0)
    for g, win in enumerate(POOL_WINDOWS):
        sl = slice(g * POOL_GC, (g + 1) * POOL_GC)
        acc = ext_ref[pl.ds(POOL_HIST, tm), sl]
        for j in range(1, win):
            acc = acc + ext_ref[pl.ds(POOL_HIST - j, tm), sl]
        cnt = jnp.minimum(pos + 1, win).astype(F32)
        d = acc / cnt - u_ref[:, sl]
        y_ref[:, sl] = _pool_project(d, w_ref, sc_ref, g).astype(BF16)


def pool_prompt(proj, pool_w, pool_scale, layer, tm):
    T = proj.shape[0]
    cb = COL_POOL // MIX_W
    return pl.pallas_call(
        functools.partial(_pool_prompt_kernel, tm=tm),
        grid=(T // tm,),
        in_specs=[
            pl.BlockSpec((tm, MIX_W), lambda i: (i, cb)),
            pl.BlockSpec((POOL_HIST, MIX_W), lambda i: (jnp.maximum(i * (tm // POOL_HIST) - 1, 0), cb)),
            pl.BlockSpec((None, len(POOL_WINDOWS), POOL_GC, POOL_GC), lambda i: (layer, 0, 0, 0)),
            pl.BlockSpec((None, 1, MIX_W), lambda i: (layer, 0, 0)),
        ],
        out_specs=pl.BlockSpec((tm, MIX_W), lambda i: (i, 0)),
        out_shape=jax.ShapeDtypeStruct((T, MIX_W), BF16),
        scratch_shapes=[pltpu.VMEM((POOL_HIST + tm, MIX_W), F32)],
        compiler_params=_params("arbitrary"),
        name="pool_prompt",
    )(proj, proj, pool_w, pool_scale)


def _pool_sample_kernel(u_ref, c_ref, w_ref, sc_ref, y_ref, *, steps):
    for t in range(steps):
        for g, win in enumerate(POOL_WINDOWS):
            sl = slice(g * POOL_GC, (g + 1) * POOL_GC)
            acc = u_ref[t, :, sl]
            for j in range(1, win):
                if j <= t:
                    acc = acc + u_ref[t - j, :, sl]
                else:
                    acc = acc + c_ref[:, POOL_BUF + t - j, sl]
            d = acc / float(win) - u_ref[t, :, sl]
            y_ref[t, :, sl] = _pool_project(d, w_ref, sc_ref, g).astype(BF16)


def pool_sample(proj3, cache_pool, pool_w, pool_scale, layer, bb):
    steps, B, _ = proj3.shape
    cb = COL_POOL // MIX_W
    return pl.pallas_call(
        functools.partial(_pool_sample_kernel, steps=steps),
        grid=(B // bb,),
        in_specs=[
            pl.BlockSpec((steps, bb, MIX_W), lambda g: (0, g, cb)),
            pl.BlockSpec((None, bb, POOL_BUF, MIX_W), lambda g: (layer, g, 0, 0)),
            pl.BlockSpec((None, len(POOL_WINDOWS), POOL_GC, POOL_GC), lambda g: (layer, 0, 0, 0)),
            pl.BlockSpec((None, 1, MIX_W), lambda g: (layer, 0, 0)),
        ],
        out_specs=pl.BlockSpec((steps, bb, MIX_W), lambda g: (0, g, 0)),
        out_shape=jax.ShapeDtypeStruct((steps, B, MIX_W), BF16),
        compiler_params=_params("arbitrary"),
        name="pool_sample",
    )(proj3, cache_pool, pool_w, pool_scale)


def _rope_tables(positions):
    half = ROT_DIM // 2
    inv = np.power(ROPE_THETA, -np.arange(0, ROT_DIM, 2, dtype=np.float64) / ROT_DIM)
    ang = np.asarray(positions, np.float64)[:, None] * inv[None, :]
    cos, sin = np.cos(ang), np.sin(ang)
    n = len(positions)
    ct = np.ones((n, LANES))
    sn = np.zeros((n, LANES))
    sp = np.zeros((n, LANES))
    for base in (0, SWA_HEAD_DIM):
        ct[:, base:base + half] = cos
        ct[:, base + half:base + ROT_DIM] = cos
        sn[:, base:base + half] = -sin
        sp[:, base + half:base + ROT_DIM] = sin
    return tuple(jnp.asarray(t, F32) for t in (ct, sn, sp))


def _head_norm_rope(x, g, ct, sn, sp):
    lane = lax.broadcasted_iota(jnp.int32, x.shape, 1)
    lo = lane < SWA_HEAD_DIM
    x2 = x * x
    ms_lo = jnp.sum(jnp.where(lo, x2, 0.0), axis=-1, keepdims=True) / SWA_HEAD_DIM
    ms_hi = jnp.sum(jnp.where(lo, 0.0, x2), axis=-1, keepdims=True) / SWA_HEAD_DIM
    xn = x * jnp.where(lo, lax.rsqrt(ms_lo + EPS), lax.rsqrt(ms_hi + EPS)) * g
    half = ROT_DIM // 2
    return xn * ct + pltpu.roll(xn, LANES - half, 1) * sn + pltpu.roll(xn, half, 1) * sp


def _kprep_kernel(k_ref, g_ref, ct_ref, sn_ref, sp_ref, o_ref):
    ct, sn, sp = ct_ref[...], sn_ref[...], sp_ref[...]
    for j in range(2):
        sl = slice(j * LANES, (j + 1) * LANES)
        o_ref[:, sl] = _head_norm_rope(k_ref[:, sl], g_ref[...], ct, sn, sp)


def swa_kprep(proj, g2, tables, tm):
    T = proj.shape[0]
    kw = SWA_KV_HEADS * SWA_HEAD_DIM
    tab = pl.BlockSpec((tm, LANES), lambda i: (i, 0))
    return pl.pallas_call(
        _kprep_kernel,
        grid=(T // tm,),
        in_specs=[pl.BlockSpec((tm, kw), lambda i: (i, COL_SK // kw)),
                  pl.BlockSpec((1, LANES), lambda i: (0, 0)), tab, tab, tab],
        out_specs=pl.BlockSpec((tm, kw), lambda i: (i, 0)),
        out_shape=jax.ShapeDtypeStruct((T, kw), F32),
        compiler_params=_params("arbitrary"),
        name="swa_kprep",
    )(proj, g2, *tables)


def _dup_head(x, parity):
    lane = lax.broadcasted_iota(jnp.int32, x.shape, 1)
    return jnp.where(lane // SWA_HEAD_DIM == parity, x, pltpu.roll(x, SWA_HEAD_DIM, 1))


def _stack_heads(q):
    lane = lax.broadcasted_iota(jnp.int32, q.shape, 1)
    lo = lane < SWA_HEAD_DIM
    return jnp.concatenate([jnp.where(lo, q, 0.0), jnp.where(lo, 0.0, q)], axis=0)


def _unstack_heads(o2):
    R = o2.shape[0] // 2
    lane = lax.broadcasted_iota(jnp.int32, (R, LANES), 1)
    return jnp.where(lane < SWA_HEAD_DIM, o2[:R], o2[R:])


def _swa_prompt_kernel(sink_ref, q_ref, kc_ref, kp_ref, vc_ref, vp_ref, g_ref, ct_ref, sn_ref, sp_ref, y_ref,
                       *, layer):
    i = pl.program_id(0)
    W = WINDOW
    ct, sn, sp = ct_ref[...], sn_ref[...], sp_ref[...]
    qi = lax.broadcasted_iota(jnp.int32, (2 * W, 2 * W), 0) % W
    ci = lax.broadcasted_iota(jnp.int32, (2 * W, 2 * W), 1)
    valid = (ci > qi) & (ci <= qi + W) & ((i > 0) | (ci >= W))
    top = lax.broadcasted_iota(jnp.int32, (2 * W, 1), 0) < W
    scale = SWA_HEAD_DIM ** -0.5
    for kvh in range(SWA_KV_HEADS):
        ksl = slice((kvh // 2) * LANES, (kvh // 2 + 1) * LANES)
        k_dup = _dup_head(jnp.concatenate([kp_ref[:, ksl], kc_ref[:, ksl]], axis=0), kvh % 2).astype(BF16)
        v_dup = _dup_head(jnp.concatenate([vp_ref[:, ksl], vc_ref[:, ksl]], axis=0), kvh % 2).astype(BF16)
        for jj in range(2):
            j = 2 * kvh + jj
            qsl = slice(j * LANES, (j + 1) * LANES)
            q = _head_norm_rope(q_ref[:, qsl], g_ref[...], ct, sn, sp) * scale
            s = _dot_nt(_stack_heads(q).astype(BF16), k_dup)
            sink = jnp.where(top, sink_ref[layer, 2 * j], sink_ref[layer, 2 * j + 1])
            s = jnp.where(valid, s, NEG_BIG)
            m = jnp.maximum(jnp.max(s, axis=-1, keepdims=True), sink)
            e = jnp.exp(s - m)
            p = e / (jnp.sum(e, axis=-1, keepdims=True) + jnp.exp(sink - m))
            y_ref[:, qsl] = _unstack_heads(_dot(p.astype(BF16), v_dup)).astype(BF16)


def swa_prompt(proj, khat, sinks, g2, tables, layer):
    T = proj.shape[0]
    W = WINDOW
    kw = SWA_KV_HEADS * SWA_HEAD_DIM
    tab = pl.BlockSpec((W, LANES), lambda i: (i, 0))
    prev = lambda i: jnp.maximum(i - 1, 0)
    return pl.pallas_call(
        functools.partial(_swa_prompt_kernel, layer=layer),
        grid=(T // W,),
        in_specs=[
            pl.BlockSpec(memory_space=pltpu.SMEM),
            pl.BlockSpec((W, MIX_W), lambda i: (i, COL_SQ // MIX_W)),
            pl.BlockSpec((W, kw), lambda i: (i, 0)),
            pl.BlockSpec((W, kw), lambda i: (prev(i), 0)),
            pl.BlockSpec((W, kw), lambda i: (i, COL_SV // kw)),
            pl.BlockSpec((W, kw), lambda i: (prev(i), COL_SV // kw)),
            pl.BlockSpec((1, LANES), lambda i: (0, 0)), tab, tab, tab,
        ],
        out_specs=pl.BlockSpec((W, MIX_W), lambda i: (i, 0)),
        out_shape=jax.ShapeDtypeStruct((T, MIX_W), BF16),
        compiler_params=_params("arbitrary"),
        name="swa_prompt",
    )(sinks, proj, khat, khat, proj, proj, g2, *tables)


def _swa_sample_kernel(sink_ref, q_ref, kn_ref, vn_ref, kc_ref, vc_ref, g_ref, ct_ref, sn_ref, sp_ref, y_ref,
                       *, layer, steps, bb):
    kvh = pl.program_id(1)
    parity = kvh % 2
    W = WINDOW
    R = steps * bb
    G = SWA_Q_HEADS // SWA_KV_HEADS
    ct, sn, sp = ct_ref[...], sn_ref[...], sp_ref[...]
    scale = SWA_HEAD_DIM ** -0.5
    r4 = lax.broadcasted_iota(jnp.int32, (G * R, 1), 0)
    hh = r4 // R
    tq = (r4 % R) // bb
    bq = r4 % bb
    c_new = lax.broadcasted_iota(jnp.int32, (1, R), 1)
    valid_new = (c_new % bb == bq) & (c_new // bb <= tq)
    c_old = lax.broadcasted_iota(jnp.int32, (1, W), 1)
    valid_old = c_old > tq
    kn = _dup_head(jnp.concatenate([kn_ref[t] for t in range(steps)], axis=0), parity).astype(BF16)
    vn = _dup_head(jnp.concatenate([vn_ref[t] for t in range(steps)], axis=0), parity).astype(BF16)
    qs = []
    for jj in range(G // 2):
        qsl = slice(jj * LANES, (jj + 1) * LANES)
        q = jnp.concatenate([q_ref[t, :, qsl] for t in range(steps)], axis=0)
        qs.append(_stack_heads(_head_norm_rope(q, g_ref[...], ct, sn, sp) * scale))
    q4 = jnp.concatenate(qs, axis=0)
    s_new = jnp.where(valid_new, _dot_nt(q4.astype(BF16), kn), NEG_BIG)
    s_old = None
    for b in range(bb):
        k_t = kc_ref[b].astype(BF16)
        d = _dot(jnp.where(bq == b, q4, 0.0).astype(BF16), jnp.concatenate([k_t, k_t], axis=0))
        s_old = d if s_old is None else s_old + d
    s_old = jnp.where(valid_old, s_old, NEG_BIG)
    sink = sink_ref[layer, G * kvh + G - 1]
    for i in range(G - 2, -1, -1):
        sink = jnp.where(hh == i, sink_ref[layer, G * kvh + i], sink)
    m = jnp.maximum(jnp.maximum(jnp.max(s_new, axis=-1, keepdims=True),
                                jnp.max(s_old, axis=-1, keepdims=True)), sink)
    e_new = jnp.exp(s_new - m)
    e_old = jnp.exp(s_old - m)
    den = jnp.sum(e_new, axis=-1, keepdims=True) + jnp.sum(e_old, axis=-1, keepdims=True) + jnp.exp(sink - m)
    p_old = e_old / den
    o = _dot((e_new / den).astype(BF16), vn)
    for b in range(bb):
        v_t = vc_ref[b].astype(BF16)
        o = o + _dot_nt(jnp.where(bq == b, p_old, 0.0).astype(BF16), jnp.concatenate([v_t, v_t], axis=0))
    for jj in range(G // 2):
        o_j = _unstack_heads(o[2 * jj * R:(2 * jj + 2) * R])
        for t in range(steps):
            y_ref[t, :, jj * LANES:(jj + 1) * LANES] = o_j[t * bb:(t + 1) * bb]


def swa_sample(proj3, khat3, cache_kt, cache_vt, sinks, g2, tables, layer, bb):
    steps, B, _ = proj3.shape
    R = steps * bb
    qw = MIX_W // SWA_KV_HEADS
    tab = pl.BlockSpec((R, LANES), lambda g, h: (0, 0))
    cache = pl.BlockSpec((None, bb, None, SWA_HEAD_DIM, WINDOW), lambda g, h: (layer, g, h, 0, 0))
    return pl.pallas_call(
        functools.partial(_swa_sample_kernel, layer=layer, steps=steps, bb=bb),
        grid=(B // bb, SWA_KV_HEADS),
        in_specs=[
            pl.BlockSpec(memory_space=pltpu.SMEM),
            pl.BlockSpec((steps, bb, qw), lambda g, h: (0, g, COL_SQ // qw + h)),
            pl.BlockSpec((steps, bb, LANES), lambda g, h: (0, g, h // 2)),
            pl.BlockSpec((steps, bb, LANES), lambda g, h: (0, g, COL_SV // LANES + h // 2)),
            cache, cache,
            pl.BlockSpec((1, LANES), lambda g, h: (0, 0)), tab, tab, tab,
        ],
        out_specs=pl.BlockSpec((steps, bb, qw), lambda g, h: (0, g, h)),
        out_shape=jax.ShapeDtypeStruct((steps, B, MIX_W), F32),
        compiler_params=_params("parallel", "arbitrary"),
        name="swa_sample",
    )(sinks, proj3, khat3, proj3, cache_kt, cache_vt, g2, *tables)


def _mem_kv_kernel(x_ref, g_ref, w_ref, kg_ref, o_ref, xn_ref):
    j = pl.program_id(0)

    @pl.when(j == 0)
    def _():
        x = x_ref[...]
        ms = jnp.mean(x * x, axis=-1, keepdims=True)
        xn_ref[...] = (x * lax.rsqrt(ms + EPS) * g_ref[...]).astype(BF16)

    y = _dot(xn_ref[...], w_ref[...].astype(BF16))

    @pl.when(j < MEM_HEADS)
    def _():
        ms = jnp.mean(y * y, axis=-1, keepdims=True)
        o_ref[...] = y * lax.rsqrt(ms + EPS) * kg_ref[...]

    @pl.when(j >= MEM_HEADS)
    def _():
        o_ref[...] = y


def mem_kv(mem, mem_norm_g, w_mem_kv, mem_knorm_g, layer):
    M, K = mem.shape
    hd = MEM_HEAD_DIM
    return pl.pallas_call(
        _mem_kv_kernel,
        grid=(2 * MEM_HEADS,),
        in_specs=[
            pl.BlockSpec((M, K), lambda j: (0, 0)),
            pl.BlockSpec((None, 1, K), lambda j: (layer, 0, 0)),
            pl.BlockSpec((None, K, hd), lambda j: (layer, 0, j)),
            pl.BlockSpec((None, 1, hd), lambda j: (layer, 0, 0)),
        ],
        out_specs=pl.BlockSpec((M, hd), lambda j: (0, j)),
        out_shape=jax.ShapeDtypeStruct((M, 2 * MIX_W), F32),
        scratch_shapes=[pltpu.VMEM((M, K), BF16)],
        compiler_params=_params("arbitrary"),
        name="mem_kv",
    )(mem, mem_norm_g, w_mem_kv, mem_knorm_g)


def _mem_qnorm(q, g):
    ms = jnp.mean(q * q, axis=-1, keepdims=True)
    return q * lax.rsqrt(ms + EPS) * g * (MEM_HEAD_DIM ** -0.5)


def _softmax_rows(s):
    m = jnp.max(s, axis=-1, keepdims=True)
    e = jnp.exp(s - m)
    return e / jnp.sum(e, axis=-1, keepdims=True)


def _mem_prompt_kernel(q0_ref, q1_ref, q2_ref, q3_ref, kv_ref, g_ref, y_ref):
    hd = MEM_HEAD_DIM
    for h, q_ref in enumerate((q0_ref, q1_ref, q2_ref, q3_ref)):
        q = _mem_qnorm(q_ref[...], g_ref[...]).astype(BF16)
        p = _softmax_rows(_dot_nt(q, kv_ref[:, h * hd:(h + 1) * hd].astype(BF16)))
        v = kv_ref[:, MIX_W + h * hd:MIX_W + (h + 1) * hd].astype(BF16)
        y_ref[:, h * hd:(h + 1) * hd] = _dot(p.astype(BF16), v).astype(BF16)


def mem_attn_prompt(proj, kv, mem_qnorm_g, layer, tq):
    T = proj.shape[0]
    hd = MEM_HEAD_DIM

    def q_spec(h):
        cb = COL_MQ // hd + h
        return pl.BlockSpec((tq, hd), lambda i: (i, cb))

    return pl.pallas_call(
        _mem_prompt_kernel,
        grid=(T // tq,),
        in_specs=[q_spec(h) for h in range(MEM_HEADS)] + [
            pl.BlockSpec((N_MEM, 2 * MIX_W), lambda i: (0, 0)),
            pl.BlockSpec((None, 1, hd), lambda i: (layer, 0, 0)),
        ],
        out_specs=pl.BlockSpec((tq, MIX_W), lambda i: (i, 0)),
        out_shape=jax.ShapeDtypeStruct((T, MIX_W), BF16),
        compiler_params=_params("arbitrary"),
        name="mem_prompt",
    )(proj, proj, proj, proj, kv, mem_qnorm_g)


def _mem_rows_view(c):
    L_, B_, M, H, hd = c.shape
    c = c.reshape(L_, B_, M, H, hd // LANES, LANES)
    return jnp.transpose(c, (0, 1, 2, 4, 3, 5)).reshape(L_, B_, M * H * (hd // LANES), LANES)


def _mem_head(c_ref, b, h):
    nt = MEM_HEAD_DIM // LANES
    parts = [c_ref[b, pl.ds(lt * MEM_HEADS + h, N_MEM, stride=nt * MEM_HEADS), :] for lt in range(nt)]
    return jnp.concatenate(parts, axis=1).astype(BF16)


def _mem_sample_kernel(q0_ref, q1_ref, q2_ref, q3_ref, k_ref, v_ref, g_ref, y_ref, *, steps, bb):
    R = steps * bb
    hd = MEM_HEAD_DIM
    bq = lax.broadcasted_iota(jnp.int32, (R, 1), 0) % bb
    for h, q_ref in enumerate((q0_ref, q1_ref, q2_ref, q3_ref)):
        q = _mem_qnorm(jnp.concatenate([q_ref[t] for t in range(steps)], axis=0), g_ref[...])
        s = None
        for b in range(bb):
            d = _dot_nt(jnp.where(bq == b, q, 0.0).astype(BF16), _mem_head(k_ref, b, h))
            s = d if s is None else s + d
        p = _softmax_rows(s)
        o = None
        for b in range(bb):
            d = _dot(jnp.where(bq == b, p, 0.0).astype(BF16), _mem_head(v_ref, b, h))
            o = d if o is None else o + d
        for t in range(steps):
            y_ref[t, :, h * hd:(h + 1) * hd] = o[t * bb:(t + 1) * bb]


def mem_attn_sample(proj3, cache_k, cache_v, mem_qnorm_g, layer, bb):
    steps, B, _ = proj3.shape
    hd = MEM_HEAD_DIM
    cache = pl.BlockSpec((None, bb) + cache_k.shape[2:], lambda g: (layer, g, 0, 0))

    def q_spec(h):
        cb = COL_MQ // hd + h
        return pl.BlockSpec((steps, bb, hd), lambda g: (0, g, cb))

    return pl.pallas_call(
        functools.partial(_mem_sample_kernel, steps=steps, bb=bb),
        grid=(B // bb,),
        in_specs=[q_spec(h) for h in range(MEM_HEADS)] + [
            cache, cache,
            pl.BlockSpec((None, 1, hd), lambda g: (layer, 0, 0)),
        ],
        out_specs=pl.BlockSpec((steps, bb, MIX_W), lambda g: (0, g, 0)),
        out_shape=jax.ShapeDtypeStruct((steps, B, MIX_W), F32),
        compiler_params=_params("arbitrary"),
        name="mem_sample",
    )(proj3, proj3, proj3, proj3, cache_k, cache_v, mem_qnorm_g)


def _row_tile(T, cap):
    t = cap
    while T % t:
        t //= 2
    return t


TM_STREAM = 2048
TM_DOWN = 1024
TM_LOCAL = 512


def _token_tail(x, xn, ys, layer, w_in, w_branch, w_o, norm2_g):
    T = x.shape[0]
    merged = merge_branches(xn, ys, w_in, w_branch, layer, _row_tile(T, 512), 512)
    h = matmul_res(merged, w_o, layer, x, _row_tile(T, TM_DOWN), 512)
    return h, prenorm(h, norm2_g, layer, _row_tile(T, TM_LOCAL))


def kernel(x_prompt, x_sample, mem_prompt, state_hgrn, cache_pool, cache_swa_k, cache_swa_v, state_conv, cache_mem_k, cache_mem_v, norm1_g, w_in, hgrn_lb, hgrn_norm_g, pool_w, pool_scale, swa_qnorm_g, swa_knorm_g, swa_sinks, mem_norm_g, w_mem_kv, mem_qnorm_g, mem_knorm_g, w_branch, w_o, norm2_g, w_up, conv_w, conv_b, w_down):
    depth = w_in.shape[0]
    bp, L, _ = x_prompt.shape
    B, steps, _ = x_sample.shape
    assert bp == 1
    kw = SWA_KV_HEADS * SWA_HEAD_DIM
    Ts = steps * B

    lb_all = jnp.cumsum(jax.nn.softmax(hgrn_lb.astype(F32), axis=0), axis=0)
    lb_all = lb_all - lb_all[:1]

    swa_bb = 8
    tab_p = _rope_tables(np.arange(L))
    tab_s = _rope_tables(np.repeat(PAST_LEN + np.arange(steps), B))
    tab_sb = _rope_tables(np.repeat(PAST_LEN + np.arange(steps), swa_bb))

    xp = x_prompt.reshape(L, D_MODEL)
    xs = jnp.transpose(x_sample, (1, 0, 2)).reshape(Ts, D_MODEL)
    mem = mem_prompt.reshape(N_MEM, D_MODEL)
    ckt_all = jnp.transpose(cache_swa_k, (0, 1, 3, 4, 2))
    cvt_all = jnp.transpose(cache_swa_v, (0, 1, 3, 4, 2))
    mk_rows = _mem_rows_view(cache_mem_k)
    mv_rows = _mem_rows_view(cache_mem_v)
    row3 = lambda a: a.reshape(depth, 1, a.shape[-1])
    norm1_g, norm2_g, pool_scale, conv_b = row3(norm1_g), row3(norm2_g), row3(pool_scale), row3(conv_b)
    mem_norm_g, mem_qnorm_g, mem_knorm_g = row3(mem_norm_g), row3(mem_qnorm_g), row3(mem_knorm_g)
    tm_p = _row_tile(L, TM_STREAM)
    tl_p = _row_tile(L, TM_LOCAL)

    outs = {k: [] for k in ("sp", "pp", "ps", "kp", "ks", "vp", "vs", "cp", "mk", "mv")}
    hgrn_states = None
    conv_states = None
    for l in range(depth):
        lb = lb_all[l].reshape(1, MIX_W)
        gn = hgrn_norm_g[l].reshape(1, MIX_W)
        gq2 = jnp.tile(swa_qnorm_g[l], 2).reshape(1, LANES)
        gk2 = jnp.tile(swa_knorm_g[l], 2).reshape(1, LANES)

        kv = mem_kv(mem, mem_norm_g, w_mem_kv, mem_knorm_g, l)

        xn = prenorm(xp, norm1_g, l, tl_p)
        proj = matmul_cols(xn, w_in, l, COL_GATE, tm_p, 512)
        ya, s_p = hgrn_prompt(proj, lb, gn, tl_p)
        yb = pool_prompt(proj, pool_w, pool_scale, l, tl_p)
        khat = swa_kprep(proj, gk2, tab_p, tl_p)
        yc = swa_prompt(proj, khat, swa_sinks, gq2, tab_p, l)
        ym = mem_attn_prompt(proj, kv, mem_qnorm_g, l, tl_p)
        h, hn = _token_tail(xp, xn, (ya, yb, yc, ym), l, w_in, w_branch, w_o, norm2_g)
        gact, a_tail = up_conv_prompt(hn, w_up, conv_w, conv_b, l, tm_p, 256)
        xp = matmul_res(gact, w_down, l, h, _row_tile(L, TM_DOWN), 256)

        outs["sp"].append(s_p[None])
        outs["pp"].append(proj[None, L - POOL_BUF:, COL_POOL:COL_POOL + MIX_W])
        outs["kp"].append(khat[None, L - WINDOW:].reshape(1, WINDOW, SWA_KV_HEADS, SWA_HEAD_DIM))
        outs["vp"].append(proj[None, L - WINDOW:, COL_SV:COL_SV + kw].reshape(1, WINDOW, SWA_KV_HEADS, SWA_HEAD_DIM))
        outs["cp"].append(a_tail[-1:, CONV_HIST - 2:])
        outs["mk"].append(kv[None, :, :MIX_W].reshape(1, N_MEM, MEM_HEADS, MEM_HEAD_DIM))
        outs["mv"].append(kv[None, :, MIX_W:].reshape(1, N_MEM, MEM_HEADS, MEM_HEAD_DIM))

        xn = prenorm(xs, norm1_g, l, Ts)
        proj_s = matmul_cols(xn, w_in, l, COL_GATE, Ts, 512)
        proj3 = proj_s.reshape(steps, B, COL_GATE)
        ya, hgrn_states = hgrn_sample(proj3, lb, gn, state_hgrn, l, LANES // steps, hgrn_states)
        yb = pool_sample(proj3, cache_pool, pool_w, pool_scale, l, 64)
        khat_s = swa_kprep(proj_s, gk2, tab_s, Ts)
        khat3 = khat_s.reshape(steps, B, kw)
        yc = swa_sample(proj3, khat3, ckt_all, cvt_all, swa_sinks, gq2, tab_sb, l, swa_bb)
        ym = mem_attn_sample(proj3, mk_rows, mv_rows, mem_qnorm_g, l, 8)
        ys = tuple(y.reshape(Ts, MIX_W).astype(BF16) for y in (ya, yb, yc, ym))
        h, hn = _token_tail(xs, xn, ys, l, w_in, w_branch, w_o, norm2_g)
        gact, conv_states = up_conv_sample(hn, w_up, state_conv, conv_w, conv_b, l, steps, 256, conv_states)
        xs = matmul_res(gact, w_down, l, h, Ts, 256)

        u_new = jnp.transpose(proj3[:, :, COL_POOL:COL_POOL + MIX_W], (1, 0, 2))
        outs["ps"].append(jnp.concatenate([cache_pool[l], u_new], axis=1)[:, -POOL_BUF:])
        k_new = jnp.transpose(khat3, (1, 0, 2)).reshape(B, steps, SWA_KV_HEADS, SWA_HEAD_DIM)
        v_new = jnp.transpose(proj3[:, :, COL_SV:COL_SV + kw], (1, 0, 2)).reshape(B, steps, SWA_KV_HEADS, SWA_HEAD_DIM)
        outs["ks"].append(jnp.concatenate([cache_swa_k[l], k_new], axis=1)[:, -WINDOW:])
        outs["vs"].append(jnp.concatenate([cache_swa_v[l], v_new], axis=1)[:, -WINDOW:])

    stk = lambda k: jnp.stack(outs[k], axis=0)
    y_prompt = xp.reshape(1, L, D_MODEL)
    y_sample = jnp.transpose(xs.reshape(steps, B, D_MODEL), (1, 0, 2))
    return (y_prompt, y_sample,
            stk("sp"), hgrn_states, stk("pp"), stk("ps"), stk("kp"), stk("ks"), stk("vp"), stk("vs"),
            stk("cp"), conv_states, jnp.concatenate(outs["mk"], axis=0)[:, None], jnp.concatenate(outs["mv"], axis=0)[:, None])
```

```python
import functools

import numpy as np
import jax
import jax.numpy as jnp
from jax import lax
from jax.experimental import pallas as pl
from jax.experimental.pallas import tpu as pltpu

F32 = jnp.float32
BF16 = jnp.bfloat16

D_MODEL = 2048
MIX_W = D_MODEL // 2
N_BRANCH = 4
A_DK = 128
A_HEADS = MIX_W // A_DK
POOL_WINDOWS = (2, 4, 8, 16)
POOL_GC = MIX_W // len(POOL_WINDOWS)
POOL_BUF = max(POOL_WINDOWS) - 1
SWA_HEAD_DIM = 64
SWA_Q_HEADS = MIX_W // SWA_HEAD_DIM
SWA_KV_HEADS = SWA_Q_HEADS // 4
WINDOW = 128
ROT_DIM = SWA_HEAD_DIM // 4
ROPE_THETA = 500000.0
N_MEM = 256
MEM_HEADS = 4
MEM_HEAD_DIM = MIX_W // MEM_HEADS
D_FF = 11 * D_MODEL // 4
EPS = 1e-6
PAST_LEN = 8192

COL_HQ, COL_HF, COL_HI, COL_HG = 0, MIX_W, 2 * MIX_W, 3 * MIX_W
COL_POOL = 4 * MIX_W
COL_SQ = 5 * MIX_W
COL_SK = 6 * MIX_W
COL_SV = COL_SK + SWA_KV_HEADS * SWA_HEAD_DIM
COL_MQ = COL_SV + SWA_KV_HEADS * SWA_HEAD_DIM
COL_GATE = COL_MQ + MIX_W
IN_COLS = COL_GATE + N_BRANCH * D_MODEL

LANES = 128
HGRN_CHUNK = 128
VMEM_LIMIT = 56 * 1024 * 1024
NEG_BIG = -1e30


def _params(*sem):
    return pltpu.CompilerParams(dimension_semantics=sem, vmem_limit_bytes=VMEM_LIMIT)


def _sigmoid(x):
    return 1.0 / (1.0 + jnp.exp(-x))


def _dot(a, b):
    return jnp.dot(a, b, preferred_element_type=F32)


def _dot_nt(a, b):
    return lax.dot_general(a, b, (((1,), (1,)), ((), ())), preferred_element_type=F32)


def _skip_ref(kernel_fn, idx):
    def wrapped(*refs):
        return kernel_fn(*refs[:idx], *refs[idx + 1:])
    return wrapped


def _layer_slab_call(kernel_fn, in_specs, args, slab_out, **kw):
    if slab_out is not None:
        idx = len(args)
        in_specs = list(in_specs) + [pl.BlockSpec(memory_space=pl.ANY)]
        args = list(args) + [slab_out]
        kernel_fn = _skip_ref(kernel_fn, idx)
        kw["input_output_aliases"] = {idx: len(kw["out_shape"]) - 1}
    return pl.pallas_call(kernel_fn, in_specs=in_specs, **kw)(*args)


def _rms_rows(x, g):
    ms = jnp.mean(x * x, axis=-1, keepdims=True)
    return x * lax.rsqrt(ms + EPS) * g


def _prenorm_kernel(x_ref, g_ref, o_ref):
    o_ref[...] = _rms_rows(x_ref[...], g_ref[...]).astype(BF16)


def prenorm(x, g, layer, tm):
    T, K = x.shape
    return pl.pallas_call(
        _prenorm_kernel,
        grid=(T // tm,),
        in_specs=[pl.BlockSpec((tm, K), lambda i: (i, 0)),
                  pl.BlockSpec((None, 1, K), lambda i: (layer, 0, 0))],
        out_specs=pl.BlockSpec((tm, K), lambda i: (i, 0)),
        out_shape=jax.ShapeDtypeStruct((T, K), BF16),
        compiler_params=_params("arbitrary"),
        name="prenorm",
    )(x, g)


def _matmul_kernel(a_ref, w_ref, o_ref):
    o_ref[...] = _dot(a_ref[...], w_ref[...].astype(BF16))


def matmul_cols(a, w, layer, n_cols, tm, tn):
    T, K = a.shape
    return pl.pallas_call(
        _matmul_kernel,
        grid=(T // tm, n_cols // tn),
        in_specs=[
            pl.BlockSpec((tm, K), lambda i, j: (i, 0)),
            pl.BlockSpec((None, K, tn), lambda i, j: (layer, 0, j)),
        ],
        out_specs=pl.BlockSpec((tm, tn), lambda i, j: (i, j)),
        out_shape=jax.ShapeDtypeStruct((T, n_cols), F32),
        compiler_params=_params("parallel", "arbitrary"),
        name="matmul_cols",
    )(a, w)


def _matmul_res_kernel(a_ref, w_ref, r_ref, o_ref):
    o_ref[...] = r_ref[...] + _dot(a_ref[...], w_ref[...].astype(BF16))


def matmul_res(a, w, layer, res, tm, tn):
    T, K = a.shape
    N = w.shape[2]
    return pl.pallas_call(
        _matmul_res_kernel,
        grid=(T // tm, N // tn),
        in_specs=[
            pl.BlockSpec((tm, K), lambda i, j: (i, 0)),
            pl.BlockSpec((None, K, tn), lambda i, j: (layer, 0, j)),
            pl.BlockSpec((tm, tn), lambda i, j: (i, j)),
        ],
        out_specs=pl.BlockSpec((tm, tn), lambda i, j: (i, j)),
        out_shape=jax.ShapeDtypeStruct((T, N), F32),
        compiler_params=_params("parallel", "arbitrary"),
        name="matmul_res",
    )(a, w, res)


def _res_norm_kernel(a_ref, w_ref, r_ref, g_ref, h_ref, hn_ref, w_s):
    @pl.when(pl.program_id(0) == 0)
    def _():
        w_s[...] = w_ref[...].astype(BF16)

    h = r_ref[...] + _dot(a_ref[...], w_s[...])
    h_ref[...] = h
    hn_ref[...] = _rms_rows(h, g_ref[...]).astype(BF16)


def matmul_res_norm(a, w, layer, res, g, tm):
    T, K = a.shape
    N = w.shape[2]
    return pl.pallas_call(
        _res_norm_kernel,
        grid=(T // tm,),
        in_specs=[
            pl.BlockSpec((tm, K), lambda i: (i, 0)),
            pl.BlockSpec((None, K, N), lambda i: (layer, 0, 0), pipeline_mode=pl.Buffered(1)),
            pl.BlockSpec((tm, N), lambda i: (i, 0)),
            pl.BlockSpec((None, 1, N), lambda i: (layer, 0, 0)),
        ],
        out_specs=[pl.BlockSpec((tm, N), lambda i: (i, 0)), pl.BlockSpec((tm, N), lambda i: (i, 0))],
        out_shape=[jax.ShapeDtypeStruct((T, N), F32), jax.ShapeDtypeStruct((T, N), BF16)],
        scratch_shapes=[pltpu.VMEM((K, N), BF16)],
        compiler_params=_params("arbitrary"),
        name="matmul_res_norm",
    )(a, w, res, g)


def _merge_kernel(xn_ref, ya_ref, yb_ref, yc_ref, ym_ref, wg0_ref, wg1_ref, wg2_ref, wg3_ref, wb_ref, o_ref,
                  wg_s, wb_s):
    @pl.when(pl.program_id(1) == 0)
    def _():
        for n, wg_ref in enumerate((wg0_ref, wg1_ref, wg2_ref, wg3_ref)):
            wg_s[n] = wg_ref[...].astype(BF16)
            wb_s[n] = wb_ref[n].astype(BF16)

    xn = xn_ref[...]
    acc = None
    for n, y_ref in enumerate((ya_ref, yb_ref, yc_ref, ym_ref)):
        t = _sigmoid(_dot(xn, wg_s[n])) * _dot(y_ref[...], wb_s[n])
        acc = t if acc is None else acc + t
    o_ref[...] = acc.astype(BF16)


def merge_branches(xn, ys, w_in, w_branch, layer, tm, tn):
    T = xn.shape[0]
    once = pl.Buffered(1)
    y_spec = pl.BlockSpec((tm, MIX_W), lambda j, i: (i, 0))

    def gate_spec(n):
        off = (COL_GATE + n * D_MODEL) // tn
        return pl.BlockSpec((None, D_MODEL, tn), lambda j, i: (layer, 0, off + j), pipeline_mode=once)

    return pl.pallas_call(
        _merge_kernel,
        grid=(D_MODEL // tn, T // tm),
        in_specs=[pl.BlockSpec((tm, D_MODEL), lambda j, i: (i, 0))] + [y_spec] * 4
        + [gate_spec(n) for n in range(N_BRANCH)]
        + [pl.BlockSpec((None, N_BRANCH, MIX_W, tn), lambda j, i: (layer, 0, 0, j), pipeline_mode=once)],
        out_specs=pl.BlockSpec((tm, tn), lambda j, i: (i, j)),
        out_shape=jax.ShapeDtypeStruct((T, D_MODEL), BF16),
        scratch_shapes=[pltpu.VMEM((N_BRANCH, D_MODEL, tn), BF16), pltpu.VMEM((N_BRANCH, MIX_W, tn), BF16)],
        compiler_params=_params("arbitrary", "arbitrary"),
        name="merge_branches",
    )(xn, *ys, w_in, w_in, w_in, w_in, w_branch)


CONV_HIST = 8


def _gelu(x):
    return 0.5 * x * (1.0 + lax.erf(x * (2.0 ** -0.5)))


def _up_conv_prompt_kernel(xn_ref, wa_ref, wv_ref, cw_ref, cb_ref, g_ref, tail_ref, carry_ref, *, tm, rc):
    i, j = pl.program_id(0), pl.program_id(1)
    wa = wa_ref[...].astype(BF16)
    wv = wv_ref[...].astype(BF16)
    prev = jnp.where(i == 0, 0.0, carry_ref[j])
    row = lax.broadcasted_iota(jnp.int32, (rc, wa.shape[1]), 0)
    for c in range(tm // rc):
        sl = pl.ds(c * rc, rc)
        xn = xn_ref[sl, :]
        a = _dot(xn, wa)
        v = _dot(xn, wv)
        a1 = jnp.where(row == 0, prev[CONV_HIST - 1:CONV_HIST], pltpu.roll(a, 1, 0))
        a2 = jnp.where(row == 0, prev[CONV_HIST - 2:CONV_HIST - 1],
                       jnp.where(row == 1, prev[CONV_HIST - 1:CONV_HIST], pltpu.roll(a, 2, 0)))
        cc = cb_ref[...] + cw_ref[0:1, :] * a2 + cw_ref[1:2, :] * a1 + cw_ref[2:3, :] * a
        g_ref[sl, :] = (_gelu(cc) * v).astype(BF16)
        prev = a[rc - CONV_HIST:, :]
    carry_ref[j] = prev
    tail_ref[...] = prev


def up_conv_prompt(xn, w_up, conv_w, conv_b, layer, tm, tn):
    T, K = xn.shape
    nc = D_FF // tn
    return pl.pallas_call(
        functools.partial(_up_conv_prompt_kernel, tm=tm, rc=min(tm, TM_LOCAL)),
        grid=(T // tm, nc),
        in_specs=[
            pl.BlockSpec((tm, K), lambda i, j: (i, 0)),
            pl.BlockSpec((None, K, tn), lambda i, j: (layer, 0, j)),
            pl.BlockSpec((None, K, tn), lambda i, j: (layer, 0, nc + j)),
            pl.BlockSpec((None, 3, tn), lambda i, j: (layer, 0, j)),
            pl.BlockSpec((None, 1, tn), lambda i, j: (layer, 0, j)),
        ],
        out_specs=[
            pl.BlockSpec((tm, tn), lambda i, j: (i, j)),
            pl.BlockSpec((None, CONV_HIST, tn), lambda i, j: (i, 0, j)),
        ],
        out_shape=[
            jax.ShapeDtypeStruct((T, D_FF), BF16),
            jax.ShapeDtypeStruct((T // tm, CONV_HIST, D_FF), F32),
        ],
        scratch_shapes=[pltpu.VMEM((nc, CONV_HIST, tn), F32)],
        compiler_params=_params("arbitrary", "arbitrary"),
        name="up_conv_prompt",
    )(xn, w_up, w_up, conv_w, conv_b)


def _up_conv_sample_kernel(xn_ref, wa_ref, wv_ref, st_ref, cw_ref, cb_ref, g_ref, so_ref, *, steps, B):
    xn = xn_ref[...]
    a = _dot(xn, wa_ref[...].astype(BF16))
    v = _dot(xn, wv_ref[...].astype(BF16))
    hist = [st_ref[:, 0, :], st_ref[:, 1, :]] + [a[t * B:(t + 1) * B] for t in range(steps)]
    for t in range(steps):
        c = cb_ref[...] + cw_ref[0:1, :] * hist[t] + cw_ref[1:2, :] * hist[t + 1] + cw_ref[2:3, :] * hist[t + 2]
        g_ref[t * B:(t + 1) * B, :] = (_gelu(c) * v[t * B:(t + 1) * B]).astype(BF16)
    so_ref[:, 0, :] = hist[steps]
    so_ref[:, 1, :] = hist[steps + 1]


def up_conv_sample(xn, w_up, state_conv, conv_w, conv_b, layer, steps, tn, state_out):
    T, K = xn.shape
    B = T // steps
    nc = D_FF // tn
    in_specs = [
        pl.BlockSpec((T, K), lambda j: (0, 0)),
        pl.BlockSpec((None, K, tn), lambda j: (layer, 0, j)),
        pl.BlockSpec((None, K, tn), lambda j: (layer, 0, nc + j)),
        pl.BlockSpec((None, B, 2, tn), lambda j: (layer, 0, 0, j)),
        pl.BlockSpec((None, 3, tn), lambda j: (layer, 0, j)),
        pl.BlockSpec((None, 1, tn), lambda j: (layer, 0, j)),
    ]
    return _layer_slab_call(
        functools.partial(_up_conv_sample_kernel, steps=steps, B=B),
        in_specs, [xn, w_up, w_up, state_conv, conv_w, conv_b], state_out,
        grid=(nc,),
        out_specs=[
            pl.BlockSpec((T, tn), lambda j: (0, j)),
            pl.BlockSpec((None, B, 2, tn), lambda j: (layer, 0, 0, j)),
        ],
        out_shape=[
            jax.ShapeDtypeStruct((T, D_FF), BF16),
            jax.ShapeDtypeStruct(state_conv.shape, F32),
        ],
        compiler_params=_params("arbitrary"),
        name="up_conv_sample",
    )


def _hgrn_gates(q_in, z, lb):
    q = q_in * _sigmoid(q_in)
    e = jnp.exp(-jnp.abs(z))
    d = 1.0 + e
    log_sig = jnp.minimum(z, 0.0) - jnp.log(d)
    a1 = jnp.log(lb)
    a2 = jnp.log1p(-lb) + log_sig
    log_f = jnp.maximum(a1, a2) + jnp.log(1.0 + jnp.exp(-jnp.abs(a1 - a2)))
    k = (1.0 - lb) * (jnp.where(z >= 0.0, e, 1.0) / d)
    return q, log_f, k


def _hgrn_out(o, gate, gn):
    ms = jnp.mean(o * o, axis=-1, keepdims=True)
    return o * lax.rsqrt(ms + EPS) * gn * (gate * _sigmoid(gate))


def _cumsum_rows(x, tril):
    hi = x.astype(BF16)
    r1 = x - hi.astype(F32)
    mid = r1.astype(BF16)
    lo = (r1 - mid.astype(F32)).astype(BF16)
    return _dot(tril, hi) + _dot(tril, mid) + _dot(tril, lo)


def _block_row(x, blk, r):
    C = x.shape[0]
    x3 = x.reshape(C // blk, blk, LANES)
    return jnp.broadcast_to(x3[:, r:r + 1, :], (C // blk, blk, LANES)).reshape(C, LANES)


def _hgrn_pair_codes():
    t = np.arange(HGRN_CHUNK)[:, None]
    s = np.arange(HGRN_CHUNK)[None, :]
    level = np.floor(np.log2(np.maximum(t ^ s, 1))).astype(np.int32)
    return jnp.asarray(np.where(s > t, -1, np.where(s == t, 0, 1 + level)), jnp.int32)


def _boundary_row(b, m):
    if 2 * m >= 8:
        return _block_row(b, 2 * m, m - 1)
    r8 = lax.broadcasted_iota(jnp.int32, b.shape, 0) & 7
    if m == 2:
        return jnp.where(r8 < 4, _block_row(b, 8, 1), _block_row(b, 8, 5))
    return jnp.where(r8 < 2, _block_row(b, 8, 0),
                     jnp.where(r8 < 4, _block_row(b, 8, 2),
                               jnp.where(r8 < 6, _block_row(b, 8, 4), _block_row(b, 8, 6))))


def _hgrn_chunk(q, log_f, k, v, S, code):
    C = HGRN_CHUNK
    rowl = lax.broadcasted_iota(jnp.int32, (C, LANES), 0)
    b = _cumsum_rows(log_f, jnp.where(code >= 0, 1.0, 0.0).astype(BF16))

    att = jnp.where(code == 0, jnp.sum(q * k, axis=-1, keepdims=True), 0.0)
    m, level = 1, 1
    while m < C:
        d = b - _boundary_row(b, m)
        isq = (rowl & m) != 0
        x = (jnp.where(isq, q, k) * jnp.exp(jnp.where(isq, d, -d))).astype(BF16)
        att = jnp.where(code == level, _dot_nt(x, x), att)
        m *= 2
        level += 1

    vb = v.astype(BF16)
    o = _dot((q * jnp.exp(b)).astype(BF16), S.astype(BF16)) + _dot(att.astype(BF16), vb)
    bl = b[C - 1:C, :]
    kk = k * jnp.exp(bl - b)
    ecol = jnp.transpose(jnp.broadcast_to(jnp.exp(bl), (LANES, LANES)))
    s_new = ecol * S + _dot(jnp.transpose(kk).astype(BF16), vb)
    return o, s_new


def _hgrn_prompt_kernel(q_ref, f_ref, i_ref, g_ref, lb_ref, gn_ref, code_ref, y_ref, so_ref, s_ref, *, rows):
    @pl.when(pl.program_id(1) == 0)
    def _():
        s_ref[...] = jnp.zeros_like(s_ref)

    lb = lb_ref[...]
    gn = gn_ref[...]
    code = code_ref[...]
    for c in range(rows // HGRN_CHUNK):
        sl = pl.ds(c * HGRN_CHUNK, HGRN_CHUNK)
        q, log_f, k = _hgrn_gates(q_ref[sl, :], f_ref[sl, :], lb)
        o, s_new = _hgrn_chunk(q, log_f, k, i_ref[sl, :], s_ref[...], code)
        s_ref[...] = s_new
        y_ref[sl, :] = _hgrn_out(o, g_ref[sl, :], gn).astype(BF16)

    @pl.when(pl.program_id(1) == pl.num_programs(1) - 1)
    def _():
        so_ref[...] = s_ref[...]


def hgrn_prompt(proj, lb, gn, rows):
    T = proj.shape[0]

    def col(off):
        base = off // LANES
        return pl.BlockSpec((rows, LANES), lambda h, c: (c, base + h))

    vec = pl.BlockSpec((1, LANES), lambda h, c: (0, h))
    return pl.pallas_call(
        functools.partial(_hgrn_prompt_kernel, rows=rows),
        grid=(A_HEADS, T // rows),
        in_specs=[col(COL_HQ), col(COL_HF), col(COL_HI), col(COL_HG), vec, vec,
                  pl.BlockSpec((HGRN_CHUNK, HGRN_CHUNK), lambda h, c: (0, 0))],
        out_specs=[
            pl.BlockSpec((rows, LANES), lambda h, c: (c, h)),
            pl.BlockSpec((None, A_DK, LANES), lambda h, c: (h, 0, 0)),
        ],
        out_shape=[
            jax.ShapeDtypeStruct((T, MIX_W), BF16),
            jax.ShapeDtypeStruct((A_HEADS, A_DK, LANES), F32),
        ],
        scratch_shapes=[pltpu.VMEM((A_DK, LANES), F32)],
        compiler_params=_params("parallel", "arbitrary"),
        name="hgrn_prompt",
    )(proj, proj, proj, proj, lb, gn, _hgrn_pair_codes())


def _hgrn_sample_kernel(q_ref, f_ref, i_ref, g_ref, lb_ref, gn_ref, s_ref, y_ref, so_ref, *, steps, bb):
    lb = lb_ref[...]
    gn = gn_ref[...]
    qs, ks, vs, bs = [], [], [], []
    b = None
    for t in range(steps):
        q, log_f, k = _hgrn_gates(q_ref[t], f_ref[t], lb)
        b = log_f if b is None else b + log_f
        qs.append(q)
        ks.append(k)
        vs.append(i_ref[t])
        bs.append(b)
    intra = []
    for t in range(steps):
        acc = None
        for s in range(t + 1):
            w = jnp.sum(qs[t] * ks[s] * jnp.exp(bs[t] - bs[s]), axis=-1, keepdims=True)
            acc = w * vs[s] if acc is None else acc + w * vs[s]
        intra.append(acc)
    R = steps * bb
    q_stack = jnp.concatenate([qs[t] * jnp.exp(bs[t]) for t in range(steps)], axis=0).astype(BF16)
    k_stack = jnp.concatenate([ks[t] * jnp.exp(bs[-1] - bs[t]) for t in range(steps)], axis=0)
    v_stack = jnp.concatenate(vs, axis=0).astype(BF16)
    k_t = jnp.transpose(k_stack)
    f_pad = jnp.concatenate([jnp.exp(bs[-1])] + [jnp.zeros((R - bb, LANES), F32)], axis=0)
    f_t = jnp.transpose(f_pad)
    rowi = lax.broadcasted_iota(jnp.int32, (R, LANES), 0) % bb
    lanei = lax.broadcasted_iota(jnp.int32, (LANES, R), 1)

    def body(bi, o_acc):
        s_b = s_ref[bi]
        o_acc = jnp.where(rowi == bi, _dot(q_stack, s_b.astype(BF16)), o_acc)
        f_col = jnp.sum(jnp.where(lanei == bi, f_t, 0.0), axis=-1, keepdims=True)
        k_b = jnp.where(lanei % bb == bi, k_t, 0.0).astype(BF16)
        so_ref[bi] = f_col * s_b + _dot(k_b, v_stack)
        return o_acc

    o_inter = lax.fori_loop(0, bb, body, jnp.zeros((R, LANES), F32), unroll=4)
    for t in range(steps):
        o = o_inter[t * bb:(t + 1) * bb] + intra[t]
        y_ref[t] = _hgrn_out(o, g_ref[t], gn).astype(BF16)


def hgrn_sample(proj3, lb, gn, state, layer, bb, state_out):
    steps, B, _ = proj3.shape
    assert steps * bb == LANES

    def col(off):
        base = off // LANES
        return pl.BlockSpec((steps, bb, LANES), lambda g, h: (0, g, base + h))

    vec = pl.BlockSpec((1, LANES), lambda g, h: (0, h))
    slab = pl.BlockSpec((None, bb, None, A_DK, LANES), lambda g, h: (layer, g, h, 0, 0))
    return _layer_slab_call(
        functools.partial(_hgrn_sample_kernel, steps=steps, bb=bb),
        [col(COL_HQ), col(COL_HF), col(COL_HI), col(COL_HG), vec, vec, slab],
        [proj3, proj3, proj3, proj3, lb, gn, state], state_out,
        grid=(B // bb, A_HEADS),
        out_specs=[pl.BlockSpec((steps, bb, LANES), lambda g, h: (0, g, h)), slab],
        out_shape=[
            jax.ShapeDtypeStruct((steps, B, MIX_W), BF16),
            jax.ShapeDtypeStruct(state.shape, F32),
        ],
        compiler_params=_params("parallel", "arbitrary"),
        name="hgrn_sample",
    )


POOL_HIST = 16


def _pool_project(d, w_ref, sc_ref, g):
    sl = slice(g * POOL_GC, (g + 1) * POOL_GC)
    return _dot(d.astype(BF16), w_ref[g].astype(BF16)) * sc_ref[:, sl]


def _pool_prompt_kernel(u_ref, prev_ref, w_ref, sc_ref, y_ref, ext_ref, *, tm):
    i = pl.program_id(0)
    ext_ref[0:POOL_HIST, :] = jnp.where(i == 0, 0.0, prev_ref[...])
    ext_ref[POOL_HIST:, :] = u_ref[...]
    pos = i * tm + lax.broadcasted_iota(jnp.int32, (tm, 1), 0)
    for g, win in enumerate(POOL_WINDOWS):
        sl = slice(g * POOL_GC, (g + 1) * POOL_GC)
        acc = ext_ref[pl.ds(POOL_HIST, tm), sl]
        for j in range(1, win):
            acc = acc + ext_ref[pl.ds(POOL_HIST - j, tm), sl]
        cnt = jnp.minimum(pos + 1, win).astype(F32)
        d = acc / cnt - u_ref[:, sl]
        y_ref[:, sl] = _pool_project(d, w_ref, sc_ref, g).astype(BF16)


def pool_prompt(proj, pool_w, pool_scale, layer, tm):
    T = proj.shape[0]
    cb = COL_POOL // MIX_W
    return pl.pallas_call(
        functools.partial(_pool_prompt_kernel, tm=tm),
        grid=(T // tm,),
        in_specs=[
            pl.BlockSpec((tm, MIX_W), lambda i: (i, cb)),
            pl.BlockSpec((POOL_HIST, MIX_W), lambda i: (jnp.maximum(i * (tm // POOL_HIST) - 1, 0), cb)),
            pl.BlockSpec((None, len(POOL_WINDOWS), POOL_GC, POOL_GC), lambda i: (layer, 0, 0, 0)),
            pl.BlockSpec((None, 1, MIX_W), lambda i: (layer, 0, 0)),
        ],
        out_specs=pl.BlockSpec((tm, MIX_W), lambda i: (i, 0)),
        out_shape=jax.ShapeDtypeStruct((T, MIX_W), BF16),
        scratch_shapes=[pltpu.VMEM((POOL_HIST + tm, MIX_W), F32)],
        compiler_params=_params("arbitrary"),
        name="pool_prompt",
    )(proj, proj, pool_w, pool_scale)


def _pool_sample_kernel(u_ref, c_ref, w_ref, sc_ref, y_ref, *, steps):
    for t in range(steps):
        for g, win in enumerate(POOL_WINDOWS):
            sl = slice(g * POOL_GC, (g + 1) * POOL_GC)
            acc = u_ref[t, :, sl]
            for j in range(1, win):
                if j <= t:
                    acc = acc + u_ref[t - j, :, sl]
                else:
                    acc = acc + c_ref[:, POOL_BUF + t - j, sl]
            d = acc / float(win) - u_ref[t, :, sl]
            y_ref[t, :, sl] = _pool_project(d, w_ref, sc_ref, g).astype(BF16)


def pool_sample(proj3, cache_pool, pool_w, pool_scale, layer, bb):
    steps, B, _ = proj3.shape
    cb = COL_POOL // MIX_W
    return pl.pallas_call(
        functools.partial(_pool_sample_kernel, steps=steps),
        grid=(B // bb,),
        in_specs=[
            pl.BlockSpec((steps, bb, MIX_W), lambda g: (0, g, cb)),
            pl.BlockSpec((None, bb, POOL_BUF, MIX_W), lambda g: (layer, g, 0, 0)),
            pl.BlockSpec((None, len(POOL_WINDOWS), POOL_GC, POOL_GC), lambda g: (layer, 0, 0, 0)),
            pl.BlockSpec((None, 1, MIX_W), lambda g: (layer, 0, 0)),
        ],
        out_specs=pl.BlockSpec((steps, bb, MIX_W), lambda g: (0, g, 0)),
        out_shape=jax.ShapeDtypeStruct((steps, B, MIX_W), BF16),
        compiler_params=_params("arbitrary"),
        name="pool_sample",
    )(proj3, cache_pool, pool_w, pool_scale)


def _rope_tables(positions):
    half = ROT_DIM // 2
    inv = np.power(ROPE_THETA, -np.arange(0, ROT_DIM, 2, dtype=np.float64) / ROT_DIM)
    ang = np.asarray(positions, np.float64)[:, None] * inv[None, :]
    cos, sin = np.cos(ang), np.sin(ang)
    n = len(positions)
    ct = np.ones((n, LANES))
    sn = np.zeros((n, LANES))
    sp = np.zeros((n, LANES))
    for base in (0, SWA_HEAD_DIM):
        ct[:, base:base + half] = cos
        ct[:, base + half:base + ROT_DIM] = cos
        sn[:, base:base + half] = -sin
        sp[:, base + half:base + ROT_DIM] = sin
    return tuple(jnp.asarray(t, F32) for t in (ct, sn, sp))


def _head_norm_rope(x, g, ct, sn, sp):
    lane = lax.broadcasted_iota(jnp.int32, x.shape, 1)
    lo = lane < SWA_HEAD_DIM
    x2 = x * x
    ms_lo = jnp.sum(jnp.where(lo, x2, 0.0), axis=-1, keepdims=True) / SWA_HEAD_DIM
    ms_hi = jnp.sum(jnp.where(lo, 0.0, x2), axis=-1, keepdims=True) / SWA_HEAD_DIM
    xn = x * jnp.where(lo, lax.rsqrt(ms_lo + EPS), lax.rsqrt(ms_hi + EPS)) * g
    half = ROT_DIM // 2
    return xn * ct + pltpu.roll(xn, LANES - half, 1) * sn + pltpu.roll(xn, half, 1) * sp


def _kprep_kernel(k_ref, g_ref, ct_ref, sn_ref, sp_ref, o_ref):
    ct, sn, sp = ct_ref[...], sn_ref[...], sp_ref[...]
    for j in range(2):
        sl = slice(j * LANES, (j + 1) * LANES)
        o_ref[:, sl] = _head_norm_rope(k_ref[:, sl], g_ref[...], ct, sn, sp)


def swa_kprep(proj, g2, tables, tm):
    T = proj.shape[0]
    kw = SWA_KV_HEADS * SWA_HEAD_DIM
    tab = pl.BlockSpec((tm, LANES), lambda i: (i, 0))
    return pl.pallas_call(
        _kprep_kernel,
        grid=(T // tm,),
        in_specs=[pl.BlockSpec((tm, kw), lambda i: (i, COL_SK // kw)),
                  pl.BlockSpec((1, LANES), lambda i: (0, 0)), tab, tab, tab],
        out_specs=pl.BlockSpec((tm, kw), lambda i: (i, 0)),
        out_shape=jax.ShapeDtypeStruct((T, kw), F32),
        compiler_params=_params("arbitrary"),
        name="swa_kprep",
    )(proj, g2, *tables)


def _dup_head(x, parity):
    lane = lax.broadcasted_iota(jnp.int32, x.shape, 1)
    return jnp.where(lane // SWA_HEAD_DIM == parity, x, pltpu.roll(x, SWA_HEAD_DIM, 1))


def _stack_heads(q):
    lane = lax.broadcasted_iota(jnp.int32, q.shape, 1)
    lo = lane < SWA_HEAD_DIM
    return jnp.concatenate([jnp.where(lo, q, 0.0), jnp.where(lo, 0.0, q)], axis=0)


def _unstack_heads(o2):
    R = o2.shape[0] // 2
    lane = lax.broadcasted_iota(jnp.int32, (R, LANES), 1)
    return jnp.where(lane < SWA_HEAD_DIM, o2[:R], o2[R:])


def _swa_prompt_kernel(sink_ref, q_ref, kc_ref, kp_ref, vc_ref, vp_ref, g_ref, ct_ref, sn_ref, sp_ref, y_ref,
                       *, layer):
    first = pl.program_id(0) == 0
    W = WINDOW
    ct, sn, sp = ct_ref[...], sn_ref[...], sp_ref[...]
    G = SWA_Q_HEADS // SWA_KV_HEADS
    r4 = lax.broadcasted_iota(jnp.int32, (G * W, 1), 0)
    hh = r4 // W
    ci = lax.broadcasted_iota(jnp.int32, (1, W), 1)
    cur = ci <= r4 % W
    scale = SWA_HEAD_DIM ** -0.5
    scores, sinks = [], []
    for kvh in range(SWA_KV_HEADS):
        ksl = slice((kvh // 2) * LANES, (kvh // 2 + 1) * LANES)
        kc, kp = (_dup_head(r[:, ksl], kvh % 2).astype(BF16) for r in (kc_ref, kp_ref))
        qs = []
        for jj in range(G // 2):
            qsl = slice((2 * kvh + jj) * LANES, (2 * kvh + jj + 1) * LANES)
            qs.append(_stack_heads(_head_norm_rope(q_ref[:, qsl], g_ref[...], ct, sn, sp) * scale))
        q = jnp.concatenate(qs, axis=0).astype(BF16)
        s_prev = jnp.where(first, NEG_BIG, _dot_nt(q, kp))
        scores.append(jnp.where(cur, _dot_nt(q, kc), s_prev))
        sink = sink_ref[layer, G * kvh + G - 1]
        for i in range(G - 2, -1, -1):
            sink = jnp.where(hh == i, sink_ref[layer, G * kvh + i], sink)
        sinks.append(sink)
    probs = []
    for s, sink in zip(scores, sinks):
        m = jnp.maximum(jnp.max(s, axis=-1, keepdims=True), sink)
        e = jnp.exp(s - m)
        probs.append(e * (1.0 / (jnp.sum(e, axis=-1, keepdims=True) + jnp.exp(sink - m))))
    for kvh, p in enumerate(probs):
        ksl = slice((kvh // 2) * LANES, (kvh // 2 + 1) * LANES)
        vc, vp = (_dup_head(r[:, ksl], kvh % 2).astype(BF16) for r in (vc_ref, vp_ref))
        o = _dot(jnp.where(cur, p, 0.0).astype(BF16), vc) + _dot(jnp.where(cur, 0.0, p).astype(BF16), vp)
        for jj in range(G // 2):
            qsl = slice((2 * kvh + jj) * LANES, (2 * kvh + jj + 1) * LANES)
            y_ref[:, qsl] = _unstack_heads(o[2 * jj * W:(2 * jj + 2) * W]).astype(BF16)


def swa_prompt(proj, khat, sinks, g2, tables, layer):
    T = proj.shape[0]
    W = WINDOW
    kw = SWA_KV_HEADS * SWA_HEAD_DIM
    tab = pl.BlockSpec((W, LANES), lambda i: (i, 0))
    prev = lambda i: jnp.maximum(i - 1, 0)
    return pl.pallas_call(
        functools.partial(_swa_prompt_kernel, layer=layer),
        grid=(T // W,),
        in_specs=[
            pl.BlockSpec(memory_space=pltpu.SMEM),
            pl.BlockSpec((W, MIX_W), lambda i: (i, COL_SQ // MIX_W)),
            pl.BlockSpec((W, kw), lambda i: (i, 0)),
            pl.BlockSpec((W, kw), lambda i: (prev(i), 0)),
            pl.BlockSpec((W, kw), lambda i: (i, COL_SV // kw)),
            pl.BlockSpec((W, kw), lambda i: (prev(i), COL_SV // kw)),
            pl.BlockSpec((1, LANES), lambda i: (0, 0)), tab, tab, tab,
        ],
        out_specs=pl.BlockSpec((W, MIX_W), lambda i: (i, 0)),
        out_shape=jax.ShapeDtypeStruct((T, MIX_W), BF16),
        compiler_params=_params("arbitrary"),
        name="swa_prompt",
    )(sinks, proj, khat, khat, proj, proj, g2, *tables)


def _swa_sample_kernel(sink_ref, q_ref, kn_ref, vn_ref, kc_ref, vc_ref, g_ref, ct_ref, sn_ref, sp_ref, y_ref,
                       *, layer, steps, bb):
    kvh = pl.program_id(1)
    parity = kvh % 2
    W = WINDOW
    R = steps * bb
    G = SWA_Q_HEADS // SWA_KV_HEADS
    ct, sn, sp = ct_ref[...], sn_ref[...], sp_ref[...]
    scale = SWA_HEAD_DIM ** -0.5
    r4 = lax.broadcasted_iota(jnp.int32, (G * R, 1), 0)
    hh = r4 // R
    tq = (r4 % R) // bb
    bq = r4 % bb
    c_new = lax.broadcasted_iota(jnp.int32, (1, R), 1)
    valid_new = (c_new % bb == bq) & (c_new // bb <= tq)
    c_old = lax.broadcasted_iota(jnp.int32, (1, W), 1)
    valid_old = c_old > tq
    kn = _dup_head(jnp.concatenate([kn_ref[t] for t in range(steps)], axis=0), parity).astype(BF16)
    vn = _dup_head(jnp.concatenate([vn_ref[t] for t in range(steps)], axis=0), parity).astype(BF16)
    qs = []
    for jj in range(G // 2):
        qsl = slice(jj * LANES, (jj + 1) * LANES)
        q = jnp.concatenate([q_ref[t, :, qsl] for t in range(steps)], axis=0)
        qs.append(_stack_heads(_head_norm_rope(q, g_ref[...], ct, sn, sp) * scale))
    q4 = jnp.concatenate(qs, axis=0)
    s_new = jnp.where(valid_new, _dot_nt(q4.astype(BF16), kn), NEG_BIG)
    s_old = None
    for b in range(0, bb, 2):
        lhs = jnp.concatenate([jnp.where(bq == b + i, q4, 0.0).astype(BF16) for i in range(2)], axis=1)
        k_t = [kc_ref[b + i].astype(BF16) for i in range(2)]
        d = _dot(lhs, jnp.concatenate([k_t[0], k_t[0], k_t[1], k_t[1]], axis=0))
        s_old = d if s_old is None else s_old + d
    s_old = jnp.where(valid_old, s_old, NEG_BIG)
    sink = sink_ref[layer, G * kvh + G - 1]
    for i in range(G - 2, -1, -1):
        sink = jnp.where(hh == i, sink_ref[layer, G * kvh + i], sink)
    m = jnp.maximum(jnp.maximum(jnp.max(s_new, axis=-1, keepdims=True),
                                jnp.max(s_old, axis=-1, keepdims=True)), sink)
    e_new = jnp.exp(s_new - m)
    e_old = jnp.exp(s_old - m)
    den = jnp.sum(e_new, axis=-1, keepdims=True) + jnp.sum(e_old, axis=-1, keepdims=True) + jnp.exp(sink - m)
    p_old = e_old / den
    o = _dot((e_new / den).astype(BF16), vn)
    for b in range(0, bb, 2):
        lhs = jnp.concatenate([jnp.where(bq == b + i, p_old, 0.0).astype(BF16) for i in range(2)], axis=1)
        v_t = [vc_ref[b + i].astype(BF16) for i in range(2)]
        rhs = jnp.concatenate([jnp.concatenate([v_t[i], v_t[i]], axis=0) for i in range(2)], axis=1)
        o = o + _dot_nt(lhs, rhs)
    for jj in range(G // 2):
        o_j = _unstack_heads(o[2 * jj * R:(2 * jj + 2) * R])
        for t in range(steps):
            y_ref[t, :, jj * LANES:(jj + 1) * LANES] = o_j[t * bb:(t + 1) * bb]


def swa_sample(proj3, khat3, cache_kt, cache_vt, sinks, g2, tables, layer, bb):
    steps, B, _ = proj3.shape
    R = steps * bb
    qw = MIX_W // SWA_KV_HEADS
    tab = pl.BlockSpec((R, LANES), lambda g, h: (0, 0))
    cache = pl.BlockSpec((None, bb, None, SWA_HEAD_DIM, WINDOW), lambda g, h: (layer, g, h, 0, 0))
    return pl.pallas_call(
        functools.partial(_swa_sample_kernel, layer=layer, steps=steps, bb=bb),
        grid=(B // bb, SWA_KV_HEADS),
        in_specs=[
            pl.BlockSpec(memory_space=pltpu.SMEM),
            pl.BlockSpec((steps, bb, qw), lambda g, h: (0, g, COL_SQ // qw + h)),
            pl.BlockSpec((steps, bb, LANES), lambda g, h: (0, g, h // 2)),
            pl.BlockSpec((steps, bb, LANES), lambda g, h: (0, g, COL_SV // LANES + h // 2)),
            cache, cache,
            pl.BlockSpec((1, LANES), lambda g, h: (0, 0)), tab, tab, tab,
        ],
        out_specs=pl.BlockSpec((steps, bb, qw), lambda g, h: (0, g, h)),
        out_shape=jax.ShapeDtypeStruct((steps, B, MIX_W), F32),
        compiler_params=_params("parallel", "arbitrary"),
        name="swa_sample",
    )(sinks, proj3, khat3, proj3, cache_kt, cache_vt, g2, *tables)


def _mem_kv_kernel(x_ref, g_ref, w_ref, kg_ref, o_ref, xn_ref):
    j = pl.program_id(0)

    @pl.when(j == 0)
    def _():
        x = x_ref[...]
        ms = jnp.mean(x * x, axis=-1, keepdims=True)
        xn_ref[...] = (x * lax.rsqrt(ms + EPS) * g_ref[...]).astype(BF16)

    y = _dot(xn_ref[...], w_ref[...].astype(BF16))

    @pl.when(j < MEM_HEADS)
    def _():
        ms = jnp.mean(y * y, axis=-1, keepdims=True)
        o_ref[...] = y * lax.rsqrt(ms + EPS) * kg_ref[...]

    @pl.when(j >= MEM_HEADS)
    def _():
        o_ref[...] = y


def mem_kv(mem, mem_norm_g, w_mem_kv, mem_knorm_g, layer):
    M, K = mem.shape
    hd = MEM_HEAD_DIM
    return pl.pallas_call(
        _mem_kv_kernel,
        grid=(2 * MEM_HEADS,),
        in_specs=[
            pl.BlockSpec((M, K), lambda j: (0, 0)),
            pl.BlockSpec((None, 1, K), lambda j: (layer, 0, 0)),
            pl.BlockSpec((None, K, hd), lambda j: (layer, 0, j)),
            pl.BlockSpec((None, 1, hd), lambda j: (layer, 0, 0)),
        ],
        out_specs=pl.BlockSpec((M, hd), lambda j: (0, j)),
        out_shape=jax.ShapeDtypeStruct((M, 2 * MIX_W), F32),
        scratch_shapes=[pltpu.VMEM((M, K), BF16)],
        compiler_params=_params("arbitrary"),
        name="mem_kv",
    )(mem, mem_norm_g, w_mem_kv, mem_knorm_g)


def _mem_qnorm(q, g):
    ms = jnp.mean(q * q, axis=-1, keepdims=True)
    return q * lax.rsqrt(ms + EPS) * g * (MEM_HEAD_DIM ** -0.5)


def _softmax_rows(s):
    m = jnp.max(s, axis=-1, keepdims=True)
    e = jnp.exp(s - m)
    return e / jnp.sum(e, axis=-1, keepdims=True)


def _mem_prompt_kernel(q0_ref, q1_ref, q2_ref, q3_ref, kv_ref, g_ref, y_ref):
    hd = MEM_HEAD_DIM
    for h, q_ref in enumerate((q0_ref, q1_ref, q2_ref, q3_ref)):
        q = _mem_qnorm(q_ref[...], g_ref[...]).astype(BF16)
        p = _softmax_rows(_dot_nt(q, kv_ref[:, h * hd:(h + 1) * hd].astype(BF16)))
        v = kv_ref[:, MIX_W + h * hd:MIX_W + (h + 1) * hd].astype(BF16)
        y_ref[:, h * hd:(h + 1) * hd] = _dot(p.astype(BF16), v).astype(BF16)


def mem_attn_prompt(proj, kv, mem_qnorm_g, layer, tq):
    T = proj.shape[0]
    hd = MEM_HEAD_DIM

    def q_spec(h):
        cb = COL_MQ // hd + h
        return pl.BlockSpec((tq, hd), lambda i: (i, cb))

    return pl.pallas_call(
        _mem_prompt_kernel,
        grid=(T // tq,),
        in_specs=[q_spec(h) for h in range(MEM_HEADS)] + [
            pl.BlockSpec((N_MEM, 2 * MIX_W), lambda i: (0, 0)),
            pl.BlockSpec((None, 1, hd), lambda i: (layer, 0, 0)),
        ],
        out_specs=pl.BlockSpec((tq, MIX_W), lambda i: (i, 0)),
        out_shape=jax.ShapeDtypeStruct((T, MIX_W), BF16),
        compiler_params=_params("arbitrary"),
        name="mem_prompt",
    )(proj, proj, proj, proj, kv, mem_qnorm_g)


def _mem_rows_view(c):
    L_, B_, M, H, hd = c.shape
    c = c.reshape(L_, B_, M, H, hd // LANES, LANES)
    return jnp.transpose(c, (0, 1, 2, 4, 3, 5)).reshape(L_, B_, M * H * (hd // LANES), LANES)


def _mem_head(c_ref, b, h):
    nt = MEM_HEAD_DIM // LANES
    parts = [c_ref[b, pl.ds(lt * MEM_HEADS + h, N_MEM, stride=nt * MEM_HEADS), :] for lt in range(nt)]
    return jnp.concatenate(parts, axis=1).astype(BF16)


def _mem_sample_kernel(q0_ref, q1_ref, q2_ref, q3_ref, k_ref, v_ref, g_ref, y_ref, *, steps, bb):
    R = steps * bb
    hd = MEM_HEAD_DIM
    bq = lax.broadcasted_iota(jnp.int32, (R, 1), 0) % bb
    for h, q_ref in enumerate((q0_ref, q1_ref, q2_ref, q3_ref)):
        q = _mem_qnorm(jnp.concatenate([q_ref[t] for t in range(steps)], axis=0), g_ref[...])
        s = None
        for b in range(bb):
            d = _dot_nt(jnp.where(bq == b, q, 0.0).astype(BF16), _mem_head(k_ref, b, h))
            s = d if s is None else s + d
        p = _softmax_rows(s)
        o = None
        for b in range(bb):
            d = _dot(jnp.where(bq == b, p, 0.0).astype(BF16), _mem_head(v_ref, b, h))
            o = d if o is None else o + d
        for t in range(steps):
            y_ref[t, :, h * hd:(h + 1) * hd] = o[t * bb:(t + 1) * bb]


def mem_attn_sample(proj3, cache_k, cache_v, mem_qnorm_g, layer, bb):
    steps, B, _ = proj3.shape
    hd = MEM_HEAD_DIM
    cache = pl.BlockSpec((None, bb) + cache_k.shape[2:], lambda g: (layer, g, 0, 0))

    def q_spec(h):
        cb = COL_MQ // hd + h
        return pl.BlockSpec((steps, bb, hd), lambda g: (0, g, cb))

    return pl.pallas_call(
        functools.partial(_mem_sample_kernel, steps=steps, bb=bb),
        grid=(B // bb,),
        in_specs=[q_spec(h) for h in range(MEM_HEADS)] + [
            cache, cache,
            pl.BlockSpec((None, 1, hd), lambda g: (layer, 0, 0)),
        ],
        out_specs=pl.BlockSpec((steps, bb, MIX_W), lambda g: (0, g, 0)),
        out_shape=jax.ShapeDtypeStruct((steps, B, MIX_W), F32),
        compiler_params=_params("arbitrary"),
        name="mem_sample",
    )(proj3, proj3, proj3, proj3, cache_k, cache_v, mem_qnorm_g)


def _row_tile(T, cap):
    t = cap
    while T % t:
        t //= 2
    return t


TM_STREAM = 2048
TM_DOWN = 1024
TM_LOCAL = 512


def _token_tail(x, xn, ys, layer, w_in, w_branch, w_o, norm2_g):
    T = x.shape[0]
    merged = merge_branches(xn, ys, w_in, w_branch, layer, _row_tile(T, 512), 512)
    return matmul_res_norm(merged, w_o, layer, x, norm2_g, _row_tile(T, 256))


def kernel(x_prompt, x_sample, mem_prompt, state_hgrn, cache_pool, cache_swa_k, cache_swa_v, state_conv, cache_mem_k, cache_mem_v, norm1_g, w_in, hgrn_lb, hgrn_norm_g, pool_w, pool_scale, swa_qnorm_g, swa_knorm_g, swa_sinks, mem_norm_g, w_mem_kv, mem_qnorm_g, mem_knorm_g, w_branch, w_o, norm2_g, w_up, conv_w, conv_b, w_down):
    depth = w_in.shape[0]
    bp, L, _ = x_prompt.shape
    B, steps, _ = x_sample.shape
    assert bp == 1
    kw = SWA_KV_HEADS * SWA_HEAD_DIM
    Ts = steps * B

    lb_all = jnp.cumsum(jax.nn.softmax(hgrn_lb.astype(F32), axis=0), axis=0)
    lb_all = lb_all - lb_all[:1]

    swa_bb = 8
    tab_p = _rope_tables(np.arange(L))
    tab_s = _rope_tables(np.repeat(PAST_LEN + np.arange(steps), B))
    tab_sb = _rope_tables(np.repeat(PAST_LEN + np.arange(steps), swa_bb))

    xp = x_prompt.reshape(L, D_MODEL)
    xs = jnp.transpose(x_sample, (1, 0, 2)).reshape(Ts, D_MODEL)
    mem = mem_prompt.reshape(N_MEM, D_MODEL)
    ckt_all = jnp.transpose(cache_swa_k, (0, 1, 3, 4, 2))
    cvt_all = jnp.transpose(cache_swa_v, (0, 1, 3, 4, 2))
    mk_rows = _mem_rows_view(cache_mem_k)
    mv_rows = _mem_rows_view(cache_mem_v)
    row3 = lambda a: a.reshape(depth, 1, a.shape[-1])
    norm1_g, norm2_g, pool_scale, conv_b = row3(norm1_g), row3(norm2_g), row3(pool_scale), row3(conv_b)
    mem_norm_g, mem_qnorm_g, mem_knorm_g = row3(mem_norm_g), row3(mem_qnorm_g), row3(mem_knorm_g)
    tm_p = _row_tile(L, TM_STREAM)
    tl_p = _row_tile(L, TM_LOCAL)

    outs = {k: [] for k in ("sp", "pp", "ps", "kp", "ks", "vp", "vs", "cp", "mk", "mv")}
    hgrn_states = None
    conv_states = None
    for l in range(depth):
        lb = lb_all[l].reshape(1, MIX_W)
        gn = hgrn_norm_g[l].reshape(1, MIX_W)
        gq2 = jnp.tile(swa_qnorm_g[l], 2).reshape(1, LANES)
        gk2 = jnp.tile(swa_knorm_g[l], 2).reshape(1, LANES)

        kv = mem_kv(mem, mem_norm_g, w_mem_kv, mem_knorm_g, l)

        xn = prenorm(xp, norm1_g, l, tl_p)
        proj = matmul_cols(xn, w_in, l, COL_GATE, tm_p, 512)
        ya, s_p = hgrn_prompt(proj, lb, gn, tl_p)
        yb = pool_prompt(proj, pool_w, pool_scale, l, tl_p)
        khat = swa_kprep(proj, gk2, tab_p, tl_p)
        yc = swa_prompt(proj, khat, swa_sinks, gq2, tab_p, l)
        ym = mem_attn_prompt(proj, kv, mem_qnorm_g, l, tl_p)
        h, hn = _token_tail(xp, xn, (ya, yb, yc, ym), l, w_in, w_branch, w_o, norm2_g)
        gact, a_tail = up_conv_prompt(hn, w_up, conv_w, conv_b, l, tm_p, 256)
        xp = matmul_res(gact, w_down, l, h, _row_tile(L, TM_DOWN), 256)

        outs["sp"].append(s_p[None])
        outs["pp"].append(proj[None, L - POOL_BUF:, COL_POOL:COL_POOL + MIX_W])
        outs["kp"].append(khat[None, L - WINDOW:].reshape(1, WINDOW, SWA_KV_HEADS, SWA_HEAD_DIM))
        outs["vp"].append(proj[None, L - WINDOW:, COL_SV:COL_SV + kw].reshape(1, WINDOW, SWA_KV_HEADS, SWA_HEAD_DIM))
        outs["cp"].append(a_tail[-1:, CONV_HIST - 2:])
        outs["mk"].append(kv[None, :, :MIX_W].reshape(1, N_MEM, MEM_HEADS, MEM_HEAD_DIM))
        outs["mv"].append(kv[None, :, MIX_W:].reshape(1, N_MEM, MEM_HEADS, MEM_HEAD_DIM))

        xn = prenorm(xs, norm1_g, l, Ts)
        proj_s = matmul_cols(xn, w_in, l, COL_GATE, Ts, 512)
        proj3 = proj_s.reshape(steps, B, COL_GATE)
        ya, hgrn_states = hgrn_sample(proj3, lb, gn, state_hgrn, l, LANES // steps, hgrn_states)
        yb = pool_sample(proj3, cache_pool, pool_w, pool_scale, l, 64)
        khat_s = swa_kprep(proj_s, gk2, tab_s, Ts)
        khat3 = khat_s.reshape(steps, B, kw)
        yc = swa_sample(proj3, khat3, ckt_all, cvt_all, swa_sinks, gq2, tab_sb, l, swa_bb)
        ym = mem_attn_sample(proj3, mk_rows, mv_rows, mem_qnorm_g, l, 8)
        ys = tuple(y.reshape(Ts, MIX_W).astype(BF16) for y in (ya, yb, yc, ym))
        h, hn = _token_tail(xs, xn, ys, l, w_in, w_branch, w_o, norm2_g)
        gact, conv_states = up_conv_sample(hn, w_up, state_conv, conv_w, conv_b, l, steps, 256, conv_states)
        xs = matmul_res(gact, w_down, l, h, Ts, 256)

        u_new = jnp.transpose(proj3[:, :, COL_POOL:COL_POOL + MIX_W], (1, 0, 2))
        outs["ps"].append(jnp.concatenate([cache_pool[l], u_new], axis=1)[:, -POOL_BUF:])
        k_new = jnp.transpose(khat3, (1, 0, 2)).reshape(B, steps, SWA_KV_HEADS, SWA_HEAD_DIM)
        v_new = jnp.transpose(proj3[:, :, COL_SV:COL_SV + kw], (1, 0, 2)).reshape(B, steps, SWA_KV_HEADS, SWA_HEAD_DIM)
        outs["ks"].append(jnp.concatenate([cache_swa_k[l], k_new], axis=1)[:, -WINDOW:])
        outs["vs"].append(jnp.concatenate([cache_swa_v[l], v_new], axis=1)[:, -WINDOW:])

    stk = lambda k: jnp.stack(outs[k], axis=0)
    y_prompt = xp.reshape(1, L, D_MODEL)
    y_sample = jnp.transpose(xs.reshape(steps, B, D_MODEL), (1, 0, 2))
    return (y_prompt, y_sample,
            stk("sp"), hgrn_states, stk("pp"), stk("ps"), stk("kp"), stk("ks"), stk("vp"), stk("vs"),
            stk("cp"), conv_states, jnp.concatenate(outs["mk"], axis=0)[:, None], jnp.concatenate(outs["mv"], axis=0)[:, None])
```

```python
import functools

import numpy as np
import jax
import jax.numpy as jnp
from jax import lax
from jax.experimental import pallas as pl
from jax.experimental.pallas import tpu as pltpu

F32 = jnp.float32
BF16 = jnp.bfloat16

D_MODEL = 2048
MIX_W = D_MODEL // 2
N_BRANCH = 4
A_DK = 128
A_HEADS = MIX_W // A_DK
POOL_WINDOWS = (2, 4, 8, 16)
POOL_GC = MIX_W // len(POOL_WINDOWS)
POOL_BUF = max(POOL_WINDOWS) - 1
SWA_HEAD_DIM = 64
SWA_Q_HEADS = MIX_W // SWA_HEAD_DIM
SWA_KV_HEADS = SWA_Q_HEADS // 4
WINDOW = 128
ROT_DIM = SWA_HEAD_DIM // 4
ROPE_THETA = 500000.0
N_MEM = 256
MEM_HEADS = 4
MEM_HEAD_DIM = MIX_W // MEM_HEADS
D_FF = 11 * D_MODEL // 4
EPS = 1e-6
PAST_LEN = 8192

COL_HQ, COL_HF, COL_HI, COL_HG = 0, MIX_W, 2 * MIX_W, 3 * MIX_W
COL_POOL = 4 * MIX_W
COL_SQ = 5 * MIX_W
COL_SK = 6 * MIX_W
COL_SV = COL_SK + SWA_KV_HEADS * SWA_HEAD_DIM
COL_MQ = COL_SV + SWA_KV_HEADS * SWA_HEAD_DIM
COL_GATE = COL_MQ + MIX_W
IN_COLS = COL_GATE + N_BRANCH * D_MODEL

LANES = 128
HGRN_CHUNK = 128
VMEM_LIMIT = 56 * 1024 * 1024
NEG_BIG = -1e30


def _params(*sem):
    return pltpu.CompilerParams(dimension_semantics=sem, vmem_limit_bytes=VMEM_LIMIT)


def _sigmoid(x):
    return 0.5 * jnp.tanh(0.5 * x) + 0.5


def _dot(a, b):
    return jnp.dot(a, b, preferred_element_type=F32)


def _dot_nt(a, b):
    return lax.dot_general(a, b, (((1,), (1,)), ((), ())), preferred_element_type=F32)


def _skip_ref(kernel_fn, idx):
    def wrapped(*refs):
        return kernel_fn(*refs[:idx], *refs[idx + 1:])
    return wrapped


def _layer_slab_call(make_kernel, in_specs, args, slab_out, layer, slab_block, slab_index, out_specs, **kw):
    n_layers = kw["out_shape"][-1].shape[0]
    if slab_out is None:
        spec = pl.BlockSpec((n_layers,) + slab_block, lambda *g: (0,) + slab_index(*g))
        return pl.pallas_call(make_kernel(layer), in_specs=in_specs, out_specs=list(out_specs) + [spec], **kw)(*args)
    spec = pl.BlockSpec((1,) + slab_block, lambda *g: (layer,) + slab_index(*g))
    idx = len(args)
    return pl.pallas_call(
        _skip_ref(make_kernel(0), idx),
        in_specs=list(in_specs) + [pl.BlockSpec(memory_space=pl.ANY)],
        out_specs=list(out_specs) + [spec],
        input_output_aliases={idx: len(kw["out_shape"]) - 1},
        **kw)(*args, slab_out)


def _zero_other_slabs(so_ref, own):
    for l in range(so_ref.shape[0]):
        if l != own:
            so_ref[l] = jnp.zeros(so_ref.shape[1:], so_ref.dtype)


def _rms_rows(x, g):
    ms = jnp.mean(x * x, axis=-1, keepdims=True)
    return x * lax.rsqrt(ms + EPS) * g


def _prenorm_kernel(x_ref, g_ref, o_ref):
    o_ref[...] = _rms_rows(x_ref[...], g_ref[...]).astype(BF16)


def prenorm(x, g, layer, tm):
    T, K = x.shape
    return pl.pallas_call(
        _prenorm_kernel,
        grid=(T // tm,),
        in_specs=[pl.BlockSpec((tm, K), lambda i: (i, 0)),
                  pl.BlockSpec((None, 1, K), lambda i: (layer, 0, 0))],
        out_specs=pl.BlockSpec((tm, K), lambda i: (i, 0)),
        out_shape=jax.ShapeDtypeStruct((T, K), BF16),
        compiler_params=_params("arbitrary"),
        name="prenorm",
    )(x, g)


def _matmul_kernel(a_ref, w_ref, o_ref):
    o_ref[...] = _dot(a_ref[...], w_ref[...].astype(BF16))


def matmul_cols(a, w, layer, n_cols, tm, tn):
    T, K = a.shape
    return pl.pallas_call(
        _matmul_kernel,
        grid=(T // tm, n_cols // tn),
        in_specs=[
            pl.BlockSpec((tm, K), lambda i, j: (i, 0)),
            pl.BlockSpec((None, K, tn), lambda i, j: (layer, 0, j)),
        ],
        out_specs=pl.BlockSpec((tm, tn), lambda i, j: (i, j)),
        out_shape=jax.ShapeDtypeStruct((T, n_cols), F32),
        compiler_params=_params("parallel", "arbitrary"),
        name="matmul_cols",
    )(a, w)


def _matmul_res_kernel(a_ref, w_ref, r_ref, o_ref):
    o_ref[...] = r_ref[...] + _dot(a_ref[...], w_ref[...].astype(BF16))


def matmul_res(a, w, layer, res, tm, tn):
    T, K = a.shape
    N = w.shape[2]
    return pl.pallas_call(
        _matmul_res_kernel,
        grid=(T // tm, N // tn),
        in_specs=[
            pl.BlockSpec((tm, K), lambda i, j: (i, 0)),
            pl.BlockSpec((None, K, tn), lambda i, j: (layer, 0, j)),
            pl.BlockSpec((tm, tn), lambda i, j: (i, j)),
        ],
        out_specs=pl.BlockSpec((tm, tn), lambda i, j: (i, j)),
        out_shape=jax.ShapeDtypeStruct((T, N), F32),
        compiler_params=_params("parallel", "arbitrary"),
        name="matmul_res",
    )(a, w, res)


def _res_norm_kernel(a_ref, w_ref, r_ref, g_ref, h_ref, hn_ref, w_s):
    @pl.when(pl.program_id(0) == 0)
    def _():
        w_s[...] = w_ref[...].astype(BF16)

    h = r_ref[...] + _dot(a_ref[...], w_s[...])
    h_ref[...] = h
    hn_ref[...] = _rms_rows(h, g_ref[...]).astype(BF16)


def matmul_res_norm(a, w, layer, res, g, tm):
    T, K = a.shape
    N = w.shape[2]
    return pl.pallas_call(
        _res_norm_kernel,
        grid=(T // tm,),
        in_specs=[
            pl.BlockSpec((tm, K), lambda i: (i, 0)),
            pl.BlockSpec((None, K, N), lambda i: (layer, 0, 0), pipeline_mode=pl.Buffered(1)),
            pl.BlockSpec((tm, N), lambda i: (i, 0)),
            pl.BlockSpec((None, 1, N), lambda i: (layer, 0, 0)),
        ],
        out_specs=[pl.BlockSpec((tm, N), lambda i: (i, 0)), pl.BlockSpec((tm, N), lambda i: (i, 0))],
        out_shape=[jax.ShapeDtypeStruct((T, N), F32), jax.ShapeDtypeStruct((T, N), BF16)],
        scratch_shapes=[pltpu.VMEM((K, N), BF16)],
        compiler_params=_params("arbitrary"),
        name="matmul_res_norm",
    )(a, w, res, g)


def _merge_kernel(xn_ref, ya_ref, yb_ref, yc_ref, ym_ref, wg0_ref, wg1_ref, wg2_ref, wg3_ref, wb_ref, o_ref,
                  wg_s, wb_s):
    @pl.when(pl.program_id(1) == 0)
    def _():
        for n, wg_ref in enumerate((wg0_ref, wg1_ref, wg2_ref, wg3_ref)):
            wg_s[n] = wg_ref[...].astype(BF16)
            wb_s[n] = wb_ref[n].astype(BF16)

    xn = xn_ref[...]
    acc = None
    for n, y_ref in enumerate((ya_ref, yb_ref, yc_ref, ym_ref)):
        t = _sigmoid(_dot(xn, wg_s[n])) * _dot(y_ref[...], wb_s[n])
        acc = t if acc is None else acc + t
    o_ref[...] = acc.astype(BF16)


def merge_branches(xn, ys, w_in, w_branch, layer, tm, tn):
    T = xn.shape[0]
    once = pl.Buffered(1)
    y_spec = pl.BlockSpec((tm, MIX_W), lambda j, i: (i, 0))

    def gate_spec(n):
        off = (COL_GATE + n * D_MODEL) // tn
        return pl.BlockSpec((None, D_MODEL, tn), lambda j, i: (layer, 0, off + j), pipeline_mode=once)

    return pl.pallas_call(
        _merge_kernel,
        grid=(D_MODEL // tn, T // tm),
        in_specs=[pl.BlockSpec((tm, D_MODEL), lambda j, i: (i, 0))] + [y_spec] * 4
        + [gate_spec(n) for n in range(N_BRANCH)]
        + [pl.BlockSpec((None, N_BRANCH, MIX_W, tn), lambda j, i: (layer, 0, 0, j), pipeline_mode=once)],
        out_specs=pl.BlockSpec((tm, tn), lambda j, i: (i, j)),
        out_shape=jax.ShapeDtypeStruct((T, D_MODEL), BF16),
        scratch_shapes=[pltpu.VMEM((N_BRANCH, D_MODEL, tn), BF16), pltpu.VMEM((N_BRANCH, MIX_W, tn), BF16)],
        compiler_params=_params("arbitrary", "arbitrary"),
        name="merge_branches",
    )(xn, *ys, w_in, w_in, w_in, w_in, w_branch)


CONV_HIST = 8


def _gelu(x):
    return 0.5 * x * (1.0 + lax.erf(x * (2.0 ** -0.5)))


def _up_conv_prompt_kernel(xn_ref, wa_ref, wv_ref, cw_ref, cb_ref, g_ref, tail_ref, carry_ref, *, tm, rc):
    i, j = pl.program_id(0), pl.program_id(1)
    wa = wa_ref[...].astype(BF16)
    wv = wv_ref[...].astype(BF16)
    prev = jnp.where(i == 0, 0.0, carry_ref[j])
    row = lax.broadcasted_iota(jnp.int32, (rc, wa.shape[1]), 0)
    for c in range(tm // rc):
        sl = pl.ds(c * rc, rc)
        xn = xn_ref[sl, :]
        a = _dot(xn, wa)
        v = _dot(xn, wv)
        a1 = jnp.where(row == 0, prev[CONV_HIST - 1:CONV_HIST], pltpu.roll(a, 1, 0))
        a2 = jnp.where(row == 0, prev[CONV_HIST - 2:CONV_HIST - 1],
                       jnp.where(row == 1, prev[CONV_HIST - 1:CONV_HIST], pltpu.roll(a, 2, 0)))
        cc = cb_ref[...] + cw_ref[0:1, :] * a2 + cw_ref[1:2, :] * a1 + cw_ref[2:3, :] * a
        g_ref[sl, :] = (_gelu(cc) * v).astype(BF16)
        prev = a[rc - CONV_HIST:, :]
    carry_ref[j] = prev
    tail_ref[...] = prev


def up_conv_prompt(xn, w_up, conv_w, conv_b, layer, tm, tn):
    T, K = xn.shape
    nc = D_FF // tn
    return pl.pallas_call(
        functools.partial(_up_conv_prompt_kernel, tm=tm, rc=min(tm, TM_LOCAL)),
        grid=(T // tm, nc),
        in_specs=[
            pl.BlockSpec((tm, K), lambda i, j: (i, 0)),
            pl.BlockSpec((None, K, tn), lambda i, j: (layer, 0, j)),
            pl.BlockSpec((None, K, tn), lambda i, j: (layer, 0, nc + j)),
            pl.BlockSpec((None, 3, tn), lambda i, j: (layer, 0, j)),
            pl.BlockSpec((None, 1, tn), lambda i, j: (layer, 0, j)),
        ],
        out_specs=[
            pl.BlockSpec((tm, tn), lambda i, j: (i, j)),
            pl.BlockSpec((None, CONV_HIST, tn), lambda i, j: (i, 0, j)),
        ],
        out_shape=[
            jax.ShapeDtypeStruct((T, D_FF), BF16),
            jax.ShapeDtypeStruct((T // tm, CONV_HIST, D_FF), F32),
        ],
        scratch_shapes=[pltpu.VMEM((nc, CONV_HIST, tn), F32)],
        compiler_params=_params("arbitrary", "arbitrary"),
        name="up_conv_prompt",
    )(xn, w_up, w_up, conv_w, conv_b)


def _up_conv_sample_kernel(xn_ref, wa_ref, wv_ref, st_ref, cw_ref, cb_ref, g_ref, so_ref, *, steps, B, own):
    _zero_other_slabs(so_ref, own)
    xn = xn_ref[...]
    a = _dot(xn, wa_ref[...].astype(BF16))
    v = _dot(xn, wv_ref[...].astype(BF16))
    hist = [st_ref[:, 0, :], st_ref[:, 1, :]] + [a[t * B:(t + 1) * B] for t in range(steps)]
    for t in range(steps):
        c = cb_ref[...] + cw_ref[0:1, :] * hist[t] + cw_ref[1:2, :] * hist[t + 1] + cw_ref[2:3, :] * hist[t + 2]
        g_ref[t * B:(t + 1) * B, :] = (_gelu(c) * v[t * B:(t + 1) * B]).astype(BF16)
    so_ref[own, :, 0, :] = hist[steps]
    so_ref[own, :, 1, :] = hist[steps + 1]


def up_conv_sample(xn, w_up, state_conv, conv_w, conv_b, layer, steps, tn, state_out):
    T, K = xn.shape
    B = T // steps
    nc = D_FF // tn
    in_specs = [
        pl.BlockSpec((T, K), lambda j: (0, 0)),
        pl.BlockSpec((None, K, tn), lambda j: (layer, 0, j)),
        pl.BlockSpec((None, K, tn), lambda j: (layer, 0, nc + j)),
        pl.BlockSpec((None, B, 2, tn), lambda j: (layer, 0, 0, j)),
        pl.BlockSpec((None, 3, tn), lambda j: (layer, 0, j)),
        pl.BlockSpec((None, 1, tn), lambda j: (layer, 0, j)),
    ]
    return _layer_slab_call(
        lambda own: functools.partial(_up_conv_sample_kernel, steps=steps, B=B, own=own),
        in_specs, [xn, w_up, w_up, state_conv, conv_w, conv_b], state_out, layer,
        (B, 2, tn), lambda j: (0, 0, j),
        [pl.BlockSpec((T, tn), lambda j: (0, j))],
        grid=(nc,),
        out_shape=[
            jax.ShapeDtypeStruct((T, D_FF), BF16),
            jax.ShapeDtypeStruct(state_conv.shape, F32),
        ],
        compiler_params=_params("arbitrary"),
        name="up_conv_sample",
    )


def _hgrn_gates(q_in, z, lb):
    q = q_in * _sigmoid(q_in)
    log_sig = jnp.minimum(z, 0.0) - jnp.log(1.0 + jnp.exp(-jnp.abs(z)))
    a1 = jnp.log(lb)
    a2 = jnp.log1p(-lb) + log_sig
    log_f = jnp.maximum(a1, a2) + jnp.log(1.0 + jnp.exp(-jnp.abs(a1 - a2)))
    k = (1.0 - lb) * _sigmoid(-z)
    return q, log_f, k


def _hgrn_out(o, gate, gn):
    ms = jnp.mean(o * o, axis=-1, keepdims=True)
    return o * lax.rsqrt(ms + EPS) * gn * (gate * _sigmoid(gate))


def _cumsum_rows(x, tril):
    hi = x.astype(BF16)
    r1 = x - hi.astype(F32)
    mid = r1.astype(BF16)
    lo = (r1 - mid.astype(F32)).astype(BF16)
    return _dot(tril, hi) + _dot(tril, mid) + _dot(tril, lo)


def _block_row(x, blk, r):
    C = x.shape[0]
    x3 = x.reshape(C // blk, blk, LANES)
    return jnp.broadcast_to(x3[:, r:r + 1, :], (C // blk, blk, LANES)).reshape(C, LANES)


def _hgrn_pair_codes():
    t = np.arange(HGRN_CHUNK)[:, None]
    s = np.arange(HGRN_CHUNK)[None, :]
    level = np.floor(np.log2(np.maximum(t ^ s, 1))).astype(np.int32)
    return jnp.asarray(np.where(s > t, -1, np.where(s == t, 0, 1 + level)), jnp.int32)


def _boundary_row(b, m):
    if 2 * m >= 8:
        return _block_row(b, 2 * m, m - 1)
    r8 = lax.broadcasted_iota(jnp.int32, b.shape, 0) & 7
    if m == 2:
        return jnp.where(r8 < 4, _block_row(b, 8, 1), _block_row(b, 8, 5))
    return jnp.where(r8 < 2, _block_row(b, 8, 0),
                     jnp.where(r8 < 4, _block_row(b, 8, 2),
                               jnp.where(r8 < 6, _block_row(b, 8, 4), _block_row(b, 8, 6))))


def _hgrn_chunk(q, log_f, k, v, S, code):
    C = HGRN_CHUNK
    rowl = lax.broadcasted_iota(jnp.int32, (C, LANES), 0)
    b = _cumsum_rows(log_f, jnp.where(code >= 0, 1.0, 0.0).astype(BF16))

    att = jnp.where(code == 0, jnp.sum(q * k, axis=-1, keepdims=True), 0.0)
    m, level = 1, 1
    while m < C:
        d = b - _boundary_row(b, m)
        isq = (rowl & m) != 0
        x = (jnp.where(isq, q, k) * jnp.exp(jnp.where(isq, d, -d))).astype(BF16)
        att = jnp.where(code == level, _dot_nt(x, x), att)
        m *= 2
        level += 1

    vb = v.astype(BF16)
    o = _dot((q * jnp.exp(b)).astype(BF16), S.astype(BF16)) + _dot(att.astype(BF16), vb)
    bl = b[C - 1:C, :]
    kk = k * jnp.exp(bl - b)
    ecol = jnp.transpose(jnp.broadcast_to(jnp.exp(bl), (LANES, LANES)))
    s_new = ecol * S + _dot(jnp.transpose(kk).astype(BF16), vb)
    return o, s_new


def _hgrn_prompt_kernel(q_ref, f_ref, i_ref, g_ref, lb_ref, gn_ref, code_ref, y_ref, so_ref, s_ref, *, rows):
    @pl.when(pl.program_id(1) == 0)
    def _():
        s_ref[...] = jnp.zeros_like(s_ref)

    lb = lb_ref[...]
    gn = gn_ref[...]
    code = code_ref[...]
    for c in range(rows // HGRN_CHUNK):
        sl = pl.ds(c * HGRN_CHUNK, HGRN_CHUNK)
        q, log_f, k = _hgrn_gates(q_ref[sl, :], f_ref[sl, :], lb)
        o, s_new = _hgrn_chunk(q, log_f, k, i_ref[sl, :], s_ref[...], code)
        s_ref[...] = s_new
        y_ref[sl, :] = _hgrn_out(o, g_ref[sl, :], gn).astype(BF16)

    @pl.when(pl.program_id(1) == pl.num_programs(1) - 1)
    def _():
        so_ref[...] = s_ref[...]


def hgrn_prompt(proj, lb, gn, rows):
    T = proj.shape[0]

    def col(off):
        base = off // LANES
        return pl.BlockSpec((rows, LANES), lambda h, c: (c, base + h))

    vec = pl.BlockSpec((1, LANES), lambda h, c: (0, h))
    return pl.pallas_call(
        functools.partial(_hgrn_prompt_kernel, rows=rows),
        grid=(A_HEADS, T // rows),
        in_specs=[col(COL_HQ), col(COL_HF), col(COL_HI), col(COL_HG), vec, vec,
                  pl.BlockSpec((HGRN_CHUNK, HGRN_CHUNK), lambda h, c: (0, 0))],
        out_specs=[
            pl.BlockSpec((rows, LANES), lambda h, c: (c, h)),
            pl.BlockSpec((None, A_DK, LANES), lambda h, c: (h, 0, 0)),
        ],
        out_shape=[
            jax.ShapeDtypeStruct((T, MIX_W), BF16),
            jax.ShapeDtypeStruct((A_HEADS, A_DK, LANES), F32),
        ],
        scratch_shapes=[pltpu.VMEM((A_DK, LANES), F32)],
        compiler_params=_params("parallel", "arbitrary"),
        name="hgrn_prompt",
    )(proj, proj, proj, proj, lb, gn, _hgrn_pair_codes())


def _hgrn_sample_kernel(q_ref, f_ref, i_ref, g_ref, lb_ref, gn_ref, s_ref, y_ref, so_ref, *, steps, bb, own):
    _zero_other_slabs(so_ref, own)
    lb = lb_ref[...]
    gn = gn_ref[...]
    qs, ks, vs, bs = [], [], [], []
    b = None
    for t in range(steps):
        q, log_f, k = _hgrn_gates(q_ref[t], f_ref[t], lb)
        b = log_f if b is None else b + log_f
        qs.append(q)
        ks.append(k)
        vs.append(i_ref[t])
        bs.append(b)
    intra = []
    for t in range(steps):
        acc = None
        for s in range(t + 1):
            w = jnp.sum(qs[t] * ks[s] * jnp.exp(bs[t] - bs[s]), axis=-1, keepdims=True)
            acc = w * vs[s] if acc is None else acc + w * vs[s]
        intra.append(acc)
    R = steps * bb
    q_stack = jnp.concatenate([qs[t] * jnp.exp(bs[t]) for t in range(steps)], axis=0).astype(BF16)
    k_stack = jnp.concatenate([ks[t] * jnp.exp(bs[-1] - bs[t]) for t in range(steps)], axis=0)
    v_stack = jnp.concatenate(vs, axis=0).astype(BF16)
    k_t = jnp.transpose(k_stack)
    f_pad = jnp.concatenate([jnp.exp(bs[-1])] + [jnp.zeros((R - bb, LANES), F32)], axis=0)
    f_t = jnp.transpose(f_pad)
    rowi = lax.broadcasted_iota(jnp.int32, (R, LANES), 0) % bb
    lanei = lax.broadcasted_iota(jnp.int32, (LANES, R), 1)

    def body(bi, o_acc):
        s_b = s_ref[bi]
        o_acc = jnp.where(rowi == bi, _dot(q_stack, s_b.astype(BF16)), o_acc)
        f_col = jnp.sum(jnp.where(lanei == bi, f_t, 0.0), axis=-1, keepdims=True)
        k_b = jnp.where(lanei % bb == bi, k_t, 0.0).astype(BF16)
        so_ref[own, bi] = f_col * s_b + _dot(k_b, v_stack)
        return o_acc

    o_inter = lax.fori_loop(0, bb, body, jnp.zeros((R, LANES), F32), unroll=4)
    for t in range(steps):
        o = o_inter[t * bb:(t + 1) * bb] + intra[t]
        y_ref[t] = _hgrn_out(o, g_ref[t], gn).astype(BF16)


def hgrn_sample(proj3, lb, gn, state, layer, bb, state_out):
    steps, B, _ = proj3.shape
    assert steps * bb == LANES

    def col(off):
        base = off // LANES
        return pl.BlockSpec((steps, bb, LANES), lambda g, h: (0, g, base + h))

    vec = pl.BlockSpec((1, LANES), lambda g, h: (0, h))
    slab = pl.BlockSpec((None, bb, None, A_DK, LANES), lambda g, h: (layer, g, h, 0, 0))
    return _layer_slab_call(
        lambda own: functools.partial(_hgrn_sample_kernel, steps=steps, bb=bb, own=own),
        [col(COL_HQ), col(COL_HF), col(COL_HI), col(COL_HG), vec, vec, slab],
        [proj3, proj3, proj3, proj3, lb, gn, state], state_out, layer,
        (bb, None, A_DK, LANES), lambda g, h: (g, h, 0, 0),
        [pl.BlockSpec((steps, bb, LANES), lambda g, h: (0, g, h))],
        grid=(B // bb, A_HEADS),
        out_shape=[
            jax.ShapeDtypeStruct((steps, B, MIX_W), BF16),
            jax.ShapeDtypeStruct(state.shape, F32),
        ],
        compiler_params=_params("parallel", "arbitrary"),
        name="hgrn_sample",
    )


POOL_HIST = 16


def _pool_project(d, w_ref, sc_ref, g):
    sl = slice(g * POOL_GC, (g + 1) * POOL_GC)
    return _dot(d.astype(BF16), w_ref[g].astype(BF16)) * sc_ref[:, sl]


def _pool_prompt_kernel(u_ref, prev_ref, w_ref, sc_ref, y_ref, ext_ref, *, tm):
    i = pl.program_id(0)
    ext_ref[0:POOL_HIST, :] = jnp.where(i == 0, 0.0, prev_ref[...])
    ext_ref[POOL_HIST:, :] = u_ref[...]
    pos = i * tm + lax.broadcasted_iota(jnp.int32, (tm, 1), 0)
    for g, win in enumerate(POOL_WINDOWS):
        sl = slice(g * POOL_GC, (g + 1) * POOL_GC)
        acc = ext_ref[pl.ds(POOL_HIST, tm), sl]
        for j in range(1, win):
            acc = acc + ext_ref[pl.ds(POOL_HIST - j, tm), sl]
        cnt = jnp.minimum(pos + 1, win).astype(F32)
        d = acc / cnt - u_ref[:, sl]
        y_ref[:, sl] = _pool_project(d, w_ref, sc_ref, g).astype(BF16)


def pool_prompt(proj, pool_w, pool_scale, layer, tm):
    T = proj.shape[0]
    cb = COL_POOL // MIX_W
    return pl.pallas_call(
        functools.partial(_pool_prompt_kernel, tm=tm),
        grid=(T // tm,),
        in_specs=[
            pl.BlockSpec((tm, MIX_W), lambda i: (i, cb)),
            pl.BlockSpec((POOL_HIST, MIX_W), lambda i: (jnp.maximum(i * (tm // POOL_HIST) - 1, 0), cb)),
            pl.BlockSpec((None, len(POOL_WINDOWS), POOL_GC, POOL_GC), lambda i: (layer, 0, 0, 0)),
            pl.BlockSpec((None, 1, MIX_W), lambda i: (layer, 0, 0)),
        ],
        out_specs=pl.BlockSpec((tm, MIX_W), lambda i: (i, 0)),
        out_shape=jax.ShapeDtypeStruct((T, MIX_W), BF16),
        scratch_shapes=[pltpu.VMEM((POOL_HIST + tm, MIX_W), F32)],
        compiler_params=_params("arbitrary"),
        name="pool_prompt",
    )(proj, proj, pool_w, pool_scale)


def _pool_sample_kernel(u_ref, c_ref, w_ref, sc_ref, y_ref, *, steps):
    for t in range(steps):
        for g, win in enumerate(POOL_WINDOWS):
            sl = slice(g * POOL_GC, (g + 1) * POOL_GC)
            acc = u_ref[t, :, sl]
            for j in range(1, win):
                if j <= t:
                    acc = acc + u_ref[t - j, :, sl]
                else:
                    acc = acc + c_ref[POOL_BUF + t - j, :, sl]
            d = acc / float(win) - u_ref[t, :, sl]
            y_ref[t, :, sl] = _pool_project(d, w_ref, sc_ref, g).astype(BF16)


def pool_sample(proj3, cache_pool, pool_w, pool_scale, layer, bb):
    steps, B, _ = proj3.shape
    cb = COL_POOL // MIX_W
    return pl.pallas_call(
        functools.partial(_pool_sample_kernel, steps=steps),
        grid=(B // bb,),
        in_specs=[
            pl.BlockSpec((steps, bb, MIX_W), lambda g: (0, g, cb)),
            pl.BlockSpec((None, POOL_BUF, bb, MIX_W), lambda g: (layer, 0, g, 0)),
            pl.BlockSpec((None, len(POOL_WINDOWS), POOL_GC, POOL_GC), lambda g: (layer, 0, 0, 0)),
            pl.BlockSpec((None, 1, MIX_W), lambda g: (layer, 0, 0)),
        ],
        out_specs=pl.BlockSpec((steps, bb, MIX_W), lambda g: (0, g, 0)),
        out_shape=jax.ShapeDtypeStruct((steps, B, MIX_W), BF16),
        compiler_params=_params("arbitrary"),
        name="pool_sample",
    )(proj3, cache_pool, pool_w, pool_scale)


def _rope_tables(positions):
    half = ROT_DIM // 2
    inv = np.power(ROPE_THETA, -np.arange(0, ROT_DIM, 2, dtype=np.float64) / ROT_DIM)
    ang = np.asarray(positions, np.float64)[:, None] * inv[None, :]
    cos, sin = np.cos(ang), np.sin(ang)
    n = len(positions)
    ct = np.ones((n, LANES))
    sn = np.zeros((n, LANES))
    sp = np.zeros((n, LANES))
    for base in (0, SWA_HEAD_DIM):
        ct[:, base:base + half] = cos
        ct[:, base + half:base + ROT_DIM] = cos
        sn[:, base:base + half] = -sin
        sp[:, base + half:base + ROT_DIM] = sin
    return tuple(jnp.asarray(t, F32) for t in (ct, sn, sp))


def _head_norm_rope(x, g, ct, sn, sp):
    lane = lax.broadcasted_iota(jnp.int32, x.shape, 1)
    lo = lane < SWA_HEAD_DIM
    x2 = x * x
    ms_lo = jnp.sum(jnp.where(lo, x2, 0.0), axis=-1, keepdims=True) / SWA_HEAD_DIM
    ms_hi = jnp.sum(jnp.where(lo, 0.0, x2), axis=-1, keepdims=True) / SWA_HEAD_DIM
    xn = x * jnp.where(lo, lax.rsqrt(ms_lo + EPS), lax.rsqrt(ms_hi + EPS)) * g
    half = ROT_DIM // 2
    return xn * ct + pltpu.roll(xn, LANES - half, 1) * sn + pltpu.roll(xn, half, 1) * sp


def _kprep_kernel(k_ref, g_ref, ct_ref, sn_ref, sp_ref, o_ref):
    ct, sn, sp = ct_ref[...], sn_ref[...], sp_ref[...]
    for j in range(2):
        sl = slice(j * LANES, (j + 1) * LANES)
        o_ref[:, sl] = _head_norm_rope(k_ref[:, sl], g_ref[...], ct, sn, sp)


def swa_kprep(proj, g2, tables, tm):
    T = proj.shape[0]
    kw = SWA_KV_HEADS * SWA_HEAD_DIM
    tab = pl.BlockSpec((tm, LANES), lambda i: (i, 0))
    return pl.pallas_call(
        _kprep_kernel,
        grid=(T // tm,),
        in_specs=[pl.BlockSpec((tm, kw), lambda i: (i, COL_SK // kw)),
                  pl.BlockSpec((1, LANES), lambda i: (0, 0)), tab, tab, tab],
        out_specs=pl.BlockSpec((tm, kw), lambda i: (i, 0)),
        out_shape=jax.ShapeDtypeStruct((T, kw), F32),
        compiler_params=_params("arbitrary"),
        name="swa_kprep",
    )(proj, g2, *tables)


def _dup_head(x, parity):
    lane = lax.broadcasted_iota(jnp.int32, x.shape, 1)
    return jnp.where(lane // SWA_HEAD_DIM == parity, x, pltpu.roll(x, SWA_HEAD_DIM, 1))


def _stack_heads(q):
    lane = lax.broadcasted_iota(jnp.int32, q.shape, 1)
    lo = lane < SWA_HEAD_DIM
    return jnp.concatenate([jnp.where(lo, q, 0.0), jnp.where(lo, 0.0, q)], axis=0)


def _unstack_heads(o2):
    R = o2.shape[0] // 2
    lane = lax.broadcasted_iota(jnp.int32, (R, LANES), 1)
    return jnp.where(lane < SWA_HEAD_DIM, o2[:R], o2[R:])


def _swa_prompt_kernel(sink_ref, q_ref, kc_ref, kp_ref, vc_ref, vp_ref, g_ref, ct_ref, sn_ref, sp_ref, y_ref,
                       *, layer):
    first = pl.program_id(0) == 0
    W = WINDOW
    ct, sn, sp = ct_ref[...], sn_ref[...], sp_ref[...]
    G = SWA_Q_HEADS // SWA_KV_HEADS
    r4 = lax.broadcasted_iota(jnp.int32, (G * W, 1), 0)
    hh = r4 // W
    ci = lax.broadcasted_iota(jnp.int32, (1, W), 1)
    cur = ci <= r4 % W
    scale = SWA_HEAD_DIM ** -0.5
    scores, sinks = [], []
    for kvh in range(SWA_KV_HEADS):
        ksl = slice((kvh // 2) * LANES, (kvh // 2 + 1) * LANES)
        kc, kp = (_dup_head(r[:, ksl], kvh % 2).astype(BF16) for r in (kc_ref, kp_ref))
        qs = []
        for jj in range(G // 2):
            qsl = slice((2 * kvh + jj) * LANES, (2 * kvh + jj + 1) * LANES)
            qs.append(_stack_heads(_head_norm_rope(q_ref[:, qsl], g_ref[...], ct, sn, sp) * scale))
        q = jnp.concatenate(qs, axis=0).astype(BF16)
        s_prev = jnp.where(first, NEG_BIG, _dot_nt(q, kp))
        scores.append(jnp.where(cur, _dot_nt(q, kc), s_prev))
        sink = sink_ref[layer, G * kvh + G - 1]
        for i in range(G - 2, -1, -1):
            sink = jnp.where(hh == i, sink_ref[layer, G * kvh + i], sink)
        sinks.append(sink)
    probs = []
    for s, sink in zip(scores, sinks):
        m = jnp.maximum(jnp.max(s, axis=-1, keepdims=True), sink)
        e = jnp.exp(s - m)
        probs.append(e * (1.0 / (jnp.sum(e, axis=-1, keepdims=True) + jnp.exp(sink - m))))
    for kvh, p in enumerate(probs):
        ksl = slice((kvh // 2) * LANES, (kvh // 2 + 1) * LANES)
        vc, vp = (_dup_head(r[:, ksl], kvh % 2).astype(BF16) for r in (vc_ref, vp_ref))
        o = _dot(jnp.where(cur, p, 0.0).astype(BF16), vc) + _dot(jnp.where(cur, 0.0, p).astype(BF16), vp)
        for jj in range(G // 2):
            qsl = slice((2 * kvh + jj) * LANES, (2 * kvh + jj + 1) * LANES)
            y_ref[:, qsl] = _unstack_heads(o[2 * jj * W:(2 * jj + 2) * W]).astype(BF16)


def swa_prompt(proj, khat, sinks, g2, tables, layer):
    T = proj.shape[0]
    W = WINDOW
    kw = SWA_KV_HEADS * SWA_HEAD_DIM
    tab = pl.BlockSpec((W, LANES), lambda i: (i, 0))
    prev = lambda i: jnp.maximum(i - 1, 0)
    return pl.pallas_call(
        functools.partial(_swa_prompt_kernel, layer=layer),
        grid=(T // W,),
        in_specs=[
            pl.BlockSpec(memory_space=pltpu.SMEM),
            pl.BlockSpec((W, MIX_W), lambda i: (i, COL_SQ // MIX_W)),
            pl.BlockSpec((W, kw), lambda i: (i, 0)),
            pl.BlockSpec((W, kw), lambda i: (prev(i), 0)),
            pl.BlockSpec((W, kw), lambda i: (i, COL_SV // kw)),
            pl.BlockSpec((W, kw), lambda i: (prev(i), COL_SV // kw)),
            pl.BlockSpec((1, LANES), lambda i: (0, 0)), tab, tab, tab,
        ],
        out_specs=pl.BlockSpec((W, MIX_W), lambda i: (i, 0)),
        out_shape=jax.ShapeDtypeStruct((T, MIX_W), BF16),
        compiler_params=_params("arbitrary"),
        name="swa_prompt",
    )(sinks, proj, khat, khat, proj, proj, g2, *tables)


def _swa_sample_kernel(sink_ref, q_ref, kn_ref, vn_ref, kc_ref, vc_ref, g_ref, ct_ref, sn_ref, sp_ref, y_ref,
                       *, layer, steps, bb):
    kvh = pl.program_id(1)
    parity = kvh % 2
    W = WINDOW
    R = steps * bb
    G = SWA_Q_HEADS // SWA_KV_HEADS
    ct, sn, sp = ct_ref[...], sn_ref[...], sp_ref[...]
    scale = SWA_HEAD_DIM ** -0.5
    r4 = lax.broadcasted_iota(jnp.int32, (G * R, 1), 0)
    hh = r4 // R
    tq = (r4 % R) // bb
    bq = r4 % bb
    c_new = lax.broadcasted_iota(jnp.int32, (1, R), 1)
    valid_new = (c_new % bb == bq) & (c_new // bb <= tq)
    c_old = lax.broadcasted_iota(jnp.int32, (1, W), 1)
    valid_old = c_old > tq
    kn = _dup_head(jnp.concatenate([kn_ref[t] for t in range(steps)], axis=0), parity).astype(BF16)
    vn = _dup_head(jnp.concatenate([vn_ref[t] for t in range(steps)], axis=0), parity).astype(BF16)
    qs = []
    for jj in range(G // 2):
        qsl = slice(jj * LANES, (jj + 1) * LANES)
        q = jnp.concatenate([q_ref[t, :, qsl] for t in range(steps)], axis=0)
        qs.append(_stack_heads(_head_norm_rope(q, g_ref[...], ct, sn, sp) * scale))
    q4 = jnp.concatenate(qs, axis=0)
    s_new = jnp.where(valid_new, _dot_nt(q4.astype(BF16), kn), NEG_BIG)
    s_old = None
    for b in range(0, bb, 2):
        lhs = jnp.concatenate([jnp.where(bq == b + i, q4, 0.0).astype(BF16) for i in range(2)], axis=1)
        k_t = [kc_ref[b + i].astype(BF16) for i in range(2)]
        d = _dot(lhs, jnp.concatenate([k_t[0], k_t[0], k_t[1], k_t[1]], axis=0))
        s_old = d if s_old is None else s_old + d
    s_old = jnp.where(valid_old, s_old, NEG_BIG)
    sink = sink_ref[layer, G * kvh + G - 1]
    for i in range(G - 2, -1, -1):
        sink = jnp.where(hh == i, sink_ref[layer, G * kvh + i], sink)
    m = jnp.maximum(jnp.maximum(jnp.max(s_new, axis=-1, keepdims=True),
                                jnp.max(s_old, axis=-1, keepdims=True)), sink)
    e_new = jnp.exp(s_new - m)
    e_old = jnp.exp(s_old - m)
    den = jnp.sum(e_new, axis=-1, keepdims=True) + jnp.sum(e_old, axis=-1, keepdims=True) + jnp.exp(sink - m)
    p_old = e_old / den
    o = _dot((e_new / den).astype(BF16), vn)
    for b in range(0, bb, 2):
        lhs = jnp.concatenate([jnp.where(bq == b + i, p_old, 0.0).astype(BF16) for i in range(2)], axis=1)
        v_t = [vc_ref[b + i].astype(BF16) for i in range(2)]
        rhs = jnp.concatenate([jnp.concatenate([v_t[i], v_t[i]], axis=0) for i in range(2)], axis=1)
        o = o + _dot_nt(lhs, rhs)
    for jj in range(G // 2):
        o_j = _unstack_heads(o[2 * jj * R:(2 * jj + 2) * R])
        for t in range(steps):
            y_ref[t, :, jj * LANES:(jj + 1) * LANES] = o_j[t * bb:(t + 1) * bb]


def swa_sample(proj3, khat3, cache_kt, cache_vt, sinks, g2, tables, layer, bb):
    steps, B, _ = proj3.shape
    R = steps * bb
    qw = MIX_W // SWA_KV_HEADS
    tab = pl.BlockSpec((R, LANES), lambda g, h: (0, 0))
    cache = pl.BlockSpec((None, bb, None, SWA_HEAD_DIM, WINDOW), lambda g, h: (layer, g, h, 0, 0))
    return pl.pallas_call(
        functools.partial(_swa_sample_kernel, layer=layer, steps=steps, bb=bb),
        grid=(B // bb, SWA_KV_HEADS),
        in_specs=[
            pl.BlockSpec(memory_space=pltpu.SMEM),
            pl.BlockSpec((steps, bb, qw), lambda g, h: (0, g, COL_SQ // qw + h)),
            pl.BlockSpec((steps, bb, LANES), lambda g, h: (0, g, h // 2)),
            pl.BlockSpec((steps, bb, LANES), lambda g, h: (0, g, COL_SV // LANES + h // 2)),
            cache, cache,
            pl.BlockSpec((1, LANES), lambda g, h: (0, 0)), tab, tab, tab,
        ],
        out_specs=pl.BlockSpec((steps, bb, qw), lambda g, h: (0, g, h)),
        out_shape=jax.ShapeDtypeStruct((steps, B, MIX_W), F32),
        compiler_params=_params("parallel", "arbitrary"),
        name="swa_sample",
    )(sinks, proj3, khat3, proj3, cache_kt, cache_vt, g2, *tables)


def _mem_kv_kernel(x_ref, g_ref, w_ref, kg_ref, o_ref, xn_ref):
    j = pl.program_id(0)

    @pl.when(j == 0)
    def _():
        x = x_ref[...]
        ms = jnp.mean(x * x, axis=-1, keepdims=True)
        xn_ref[...] = (x * lax.rsqrt(ms + EPS) * g_ref[...]).astype(BF16)

    y = _dot(xn_ref[...], w_ref[...].astype(BF16))

    @pl.when(j < MEM_HEADS)
    def _():
        ms = jnp.mean(y * y, axis=-1, keepdims=True)
        o_ref[...] = y * lax.rsqrt(ms + EPS) * kg_ref[...]

    @pl.when(j >= MEM_HEADS)
    def _():
        o_ref[...] = y


def mem_kv(mem, mem_norm_g, w_mem_kv, mem_knorm_g, layer):
    M, K = mem.shape
    hd = MEM_HEAD_DIM
    return pl.pallas_call(
        _mem_kv_kernel,
        grid=(2 * MEM_HEADS,),
        in_specs=[
            pl.BlockSpec((M, K), lambda j: (0, 0)),
            pl.BlockSpec((None, 1, K), lambda j: (layer, 0, 0)),
            pl.BlockSpec((None, K, hd), lambda j: (layer, 0, j)),
            pl.BlockSpec((None, 1, hd), lambda j: (layer, 0, 0)),
        ],
        out_specs=pl.BlockSpec((M, hd), lambda j: (0, j)),
        out_shape=jax.ShapeDtypeStruct((M, 2 * MIX_W), F32),
        scratch_shapes=[pltpu.VMEM((M, K), BF16)],
        compiler_params=_params("arbitrary"),
        name="mem_kv",
    )(mem, mem_norm_g, w_mem_kv, mem_knorm_g)


def _mem_qnorm(q, g):
    ms = jnp.mean(q * q, axis=-1, keepdims=True)
    return q * lax.rsqrt(ms + EPS) * g * (MEM_HEAD_DIM ** -0.5)


def _softmax_rows(s):
    m = jnp.max(s, axis=-1, keepdims=True)
    e = jnp.exp(s - m)
    return e / jnp.sum(e, axis=-1, keepdims=True)


def _mem_prompt_kernel(q0_ref, q1_ref, q2_ref, q3_ref, kv_ref, g_ref, y_ref):
    hd = MEM_HEAD_DIM
    for h, q_ref in enumerate((q0_ref, q1_ref, q2_ref, q3_ref)):
        q = _mem_qnorm(q_ref[...], g_ref[...]).astype(BF16)
        p = _softmax_rows(_dot_nt(q, kv_ref[:, h * hd:(h + 1) * hd].astype(BF16)))
        v = kv_ref[:, MIX_W + h * hd:MIX_W + (h + 1) * hd].astype(BF16)
        y_ref[:, h * hd:(h + 1) * hd] = _dot(p.astype(BF16), v).astype(BF16)


def mem_attn_prompt(proj, kv, mem_qnorm_g, layer, tq):
    T = proj.shape[0]
    hd = MEM_HEAD_DIM

    def q_spec(h):
        cb = COL_MQ // hd + h
        return pl.BlockSpec((tq, hd), lambda i: (i, cb))

    return pl.pallas_call(
        _mem_prompt_kernel,
        grid=(T // tq,),
        in_specs=[q_spec(h) for h in range(MEM_HEADS)] + [
            pl.BlockSpec((N_MEM, 2 * MIX_W), lambda i: (0, 0)),
            pl.BlockSpec((None, 1, hd), lambda i: (layer, 0, 0)),
        ],
        out_specs=pl.BlockSpec((tq, MIX_W), lambda i: (i, 0)),
        out_shape=jax.ShapeDtypeStruct((T, MIX_W), BF16),
        compiler_params=_params("arbitrary"),
        name="mem_prompt",
    )(proj, proj, proj, proj, kv, mem_qnorm_g)


def _mem_rows_view(c):
    L_, B_, M, H, hd = c.shape
    c = c.reshape(L_, B_, M, H, hd // LANES, LANES)
    return jnp.transpose(c, (0, 1, 2, 4, 3, 5)).reshape(L_, B_, M * H * (hd // LANES), LANES)


def _mem_head(c_ref, b, h):
    nt = MEM_HEAD_DIM // LANES
    parts = [c_ref[b, pl.ds(lt * MEM_HEADS + h, N_MEM, stride=nt * MEM_HEADS), :] for lt in range(nt)]
    return jnp.concatenate(parts, axis=1).astype(BF16)


def _mem_sample_kernel(q0_ref, q1_ref, q2_ref, q3_ref, k_ref, v_ref, g_ref, y_ref, *, steps, bb):
    R = steps * bb
    hd = MEM_HEAD_DIM
    bq = lax.broadcasted_iota(jnp.int32, (R, 1), 0) % bb
    for h, q_ref in enumerate((q0_ref, q1_ref, q2_ref, q3_ref)):
        q = _mem_qnorm(jnp.concatenate([q_ref[t] for t in range(steps)], axis=0), g_ref[...])
        s = None
        for b in range(bb):
            d = _dot_nt(jnp.where(bq == b, q, 0.0).astype(BF16), _mem_head(k_ref, b, h))
            s = d if s is None else s + d
        p = _softmax_rows(s)
        o = None
        for b in range(bb):
            d = _dot(jnp.where(bq == b, p, 0.0).astype(BF16), _mem_head(v_ref, b, h))
            o = d if o is None else o + d
        for t in range(steps):
            y_ref[t, :, h * hd:(h + 1) * hd] = o[t * bb:(t + 1) * bb]


def mem_attn_sample(proj3, cache_k, cache_v, mem_qnorm_g, layer, bb):
    steps, B, _ = proj3.shape
    hd = MEM_HEAD_DIM
    cache = pl.BlockSpec((None, bb) + cache_k.shape[2:], lambda g: (layer, g, 0, 0))

    def q_spec(h):
        cb = COL_MQ // hd + h
        return pl.BlockSpec((steps, bb, hd), lambda g: (0, g, cb))

    return pl.pallas_call(
        functools.partial(_mem_sample_kernel, steps=steps, bb=bb),
        grid=(B // bb,),
        in_specs=[q_spec(h) for h in range(MEM_HEADS)] + [
            cache, cache,
            pl.BlockSpec((None, 1, hd), lambda g: (layer, 0, 0)),
        ],
        out_specs=pl.BlockSpec((steps, bb, MIX_W), lambda g: (0, g, 0)),
        out_shape=jax.ShapeDtypeStruct((steps, B, MIX_W), F32),
        compiler_params=_params("arbitrary"),
        name="mem_sample",
    )(proj3, proj3, proj3, proj3, cache_k, cache_v, mem_qnorm_g)


def _row_tile(T, cap):
    t = cap
    while T % t:
        t //= 2
    return t


TM_STREAM = 2048
TM_DOWN = 1024
TM_LOCAL = 512


def _token_tail(x, xn, ys, layer, w_in, w_branch, w_o, norm2_g):
    T = x.shape[0]
    merged = merge_branches(xn, ys, w_in, w_branch, layer, _row_tile(T, 512), 512)
    return matmul_res_norm(merged, w_o, layer, x, norm2_g, _row_tile(T, 256))


def kernel(x_prompt, x_sample, mem_prompt, state_hgrn, cache_pool, cache_swa_k, cache_swa_v, state_conv, cache_mem_k, cache_mem_v, norm1_g, w_in, hgrn_lb, hgrn_norm_g, pool_w, pool_scale, swa_qnorm_g, swa_knorm_g, swa_sinks, mem_norm_g, w_mem_kv, mem_qnorm_g, mem_knorm_g, w_branch, w_o, norm2_g, w_up, conv_w, conv_b, w_down):
    depth = w_in.shape[0]
    bp, L, _ = x_prompt.shape
    B, steps, _ = x_sample.shape
    assert bp == 1
    kw = SWA_KV_HEADS * SWA_HEAD_DIM
    Ts = steps * B

    lb_all = jnp.cumsum(jax.nn.softmax(hgrn_lb.astype(F32), axis=0), axis=0)
    lb_all = lb_all - lb_all[:1]

    swa_bb = 8
    tab_p = _rope_tables(np.arange(L))
    tab_s = _rope_tables(np.repeat(PAST_LEN + np.arange(steps), B))
    tab_sb = _rope_tables(np.repeat(PAST_LEN + np.arange(steps), swa_bb))

    xp = x_prompt.reshape(L, D_MODEL)
    xs = jnp.transpose(x_sample, (1, 0, 2)).reshape(Ts, D_MODEL)
    mem = mem_prompt.reshape(N_MEM, D_MODEL)
    ckt_all = jnp.transpose(cache_swa_k, (0, 1, 3, 4, 2))
    cvt_all = jnp.transpose(cache_swa_v, (0, 1, 3, 4, 2))
    mk_rows = _mem_rows_view(cache_mem_k)
    mv_rows = _mem_rows_view(cache_mem_v)
    cpool_v = jnp.transpose(cache_pool, (0, 2, 1, 3))
    row3 = lambda a: a.reshape(depth, 1, a.shape[-1])
    norm1_g, norm2_g, pool_scale, conv_b = row3(norm1_g), row3(norm2_g), row3(pool_scale), row3(conv_b)
    mem_norm_g, mem_qnorm_g, mem_knorm_g = row3(mem_norm_g), row3(mem_qnorm_g), row3(mem_knorm_g)
    tm_p = _row_tile(L, TM_STREAM)
    tl_p = _row_tile(L, TM_LOCAL)

    outs = {k: [] for k in ("sp", "pp", "ps", "kp", "ks", "vp", "vs", "cp", "mk", "mv")}
    hgrn_states = None
    conv_states = None
    for l in range(depth):
        lb = lb_all[l].reshape(1, MIX_W)
        gn = hgrn_norm_g[l].reshape(1, MIX_W)
        gq2 = jnp.tile(swa_qnorm_g[l], 2).reshape(1, LANES)
        gk2 = jnp.tile(swa_knorm_g[l], 2).reshape(1, LANES)

        kv = mem_kv(mem, mem_norm_g, w_mem_kv, mem_knorm_g, l)

        xn = prenorm(xp, norm1_g, l, tl_p)
        proj = matmul_cols(xn, w_in, l, COL_GATE, tm_p, 512)
        ya, s_p = hgrn_prompt(proj, lb, gn, tl_p)
        yb = pool_prompt(proj, pool_w, pool_scale, l, tl_p)
        khat = swa_kprep(proj, gk2, tab_p, tl_p)
        yc = swa_prompt(proj, khat, swa_sinks, gq2, tab_p, l)
        ym = mem_attn_prompt(proj, kv, mem_qnorm_g, l, tl_p)
        h, hn = _token_tail(xp, xn, (ya, yb, yc, ym), l, w_in, w_branch, w_o, norm2_g)
        gact, a_tail = up_conv_prompt(hn, w_up, conv_w, conv_b, l, tm_p, 256)
        xp = matmul_res(gact, w_down, l, h, _row_tile(L, TM_DOWN), 256)

        outs["sp"].append(s_p[None])
        outs["pp"].append(proj[None, L - POOL_BUF:, COL_POOL:COL_POOL + MIX_W])
        outs["kp"].append(khat[None, L - WINDOW:].reshape(1, WINDOW, SWA_KV_HEADS, SWA_HEAD_DIM))
        outs["vp"].append(proj[None, L - WINDOW:, COL_SV:COL_SV + kw].reshape(1, WINDOW, SWA_KV_HEADS, SWA_HEAD_DIM))
        outs["cp"].append(a_tail[-1:, CONV_HIST - 2:])
        outs["mk"].append(kv[None, :, :MIX_W].reshape(1, N_MEM, MEM_HEADS, MEM_HEAD_DIM))
        outs["mv"].append(kv[None, :, MIX_W:].reshape(1, N_MEM, MEM_HEADS, MEM_HEAD_DIM))

        xn = prenorm(xs, norm1_g, l, Ts)
        proj_s = matmul_cols(xn, w_in, l, COL_GATE, Ts, 512)
        proj3 = proj_s.reshape(steps, B, COL_GATE)
        ya, hgrn_states = hgrn_sample(proj3, lb, gn, state_hgrn, l, LANES // steps, hgrn_states)
        yb = pool_sample(proj3, cpool_v, pool_w, pool_scale, l, 64)
        khat_s = swa_kprep(proj_s, gk2, tab_s, Ts)
        khat3 = khat_s.reshape(steps, B, kw)
        yc = swa_sample(proj3, khat3, ckt_all, cvt_all, swa_sinks, gq2, tab_sb, l, swa_bb)
        ym = mem_attn_sample(proj3, mk_rows, mv_rows, mem_qnorm_g, l, 8)
        ys = tuple(y.reshape(Ts, MIX_W).astype(BF16) for y in (ya, yb, yc, ym))
        h, hn = _token_tail(xs, xn, ys, l, w_in, w_branch, w_o, norm2_g)
        gact, conv_states = up_conv_sample(hn, w_up, state_conv, conv_w, conv_b, l, steps, 256, conv_states)
        xs = matmul_res(gact, w_down, l, h, Ts, 256)

        outs["ps"].append(proj3[:, :, COL_POOL:COL_POOL + MIX_W])
        to_window_minor = lambda a: jnp.transpose(a.reshape(steps, B, SWA_KV_HEADS, SWA_HEAD_DIM), (1, 2, 3, 0))
        outs["ks"].append(to_window_minor(khat3))
        outs["vs"].append(to_window_minor(proj3[:, :, COL_SV:COL_SV + kw]))

    stk = lambda k: jnp.stack(outs[k], axis=0)
    pool_s = jnp.transpose(jnp.concatenate([cpool_v[:, steps:], stk("ps")], axis=1), (0, 2, 1, 3))
    swa_k_s = jnp.transpose(jnp.concatenate([ckt_all[..., steps:], stk("ks")], axis=-1), (0, 1, 4, 2, 3))
    swa_v_s = jnp.transpose(jnp.concatenate([cvt_all[..., steps:], stk("vs")], axis=-1), (0, 1, 4, 2, 3))
    y_prompt = xp.reshape(1, L, D_MODEL)
    y_sample = jnp.transpose(xs.reshape(steps, B, D_MODEL), (1, 0, 2))
    return (y_prompt, y_sample,
            stk("sp"), hgrn_states, stk("pp"), pool_s, stk("kp"), swa_k_s, stk("vp"), swa_v_s,
            stk("cp"), conv_states, jnp.concatenate(outs["mk"], axis=0)[:, None], jnp.concatenate(outs["mv"], axis=0)[:, None])
```

```python
import functools

import numpy as np
import jax
import jax.numpy as jnp
from jax import lax
from jax.experimental import pallas as pl
from jax.experimental.pallas import tpu as pltpu

F32 = jnp.float32
BF16 = jnp.bfloat16

D_MODEL = 2048
MIX_W = D_MODEL // 2
N_BRANCH = 4
A_DK = 128
A_HEADS = MIX_W // A_DK
POOL_WINDOWS = (2, 4, 8, 16)
POOL_GC = MIX_W // len(POOL_WINDOWS)
POOL_BUF = max(POOL_WINDOWS) - 1
SWA_HEAD_DIM = 64
SWA_Q_HEADS = MIX_W // SWA_HEAD_DIM
SWA_KV_HEADS = SWA_Q_HEADS // 4
WINDOW = 128
ROT_DIM = SWA_HEAD_DIM // 4
ROPE_THETA = 500000.0
N_MEM = 256
MEM_HEADS = 4
MEM_HEAD_DIM = MIX_W // MEM_HEADS
D_FF = 11 * D_MODEL // 4
EPS = 1e-6
PAST_LEN = 8192

COL_HQ, COL_HF, COL_HI, COL_HG = 0, MIX_W, 2 * MIX_W, 3 * MIX_W
COL_POOL = 4 * MIX_W
COL_SQ = 5 * MIX_W
COL_SK = 6 * MIX_W
COL_SV = COL_SK + SWA_KV_HEADS * SWA_HEAD_DIM
COL_MQ = COL_SV + SWA_KV_HEADS * SWA_HEAD_DIM
COL_GATE = COL_MQ + MIX_W
IN_COLS = COL_GATE + N_BRANCH * D_MODEL

LANES = 128
HGRN_CHUNK = 128
VMEM_LIMIT = 56 * 1024 * 1024
NEG_BIG = -1e30


def _params(*sem):
    return pltpu.CompilerParams(dimension_semantics=sem, vmem_limit_bytes=VMEM_LIMIT)


def _sigmoid(x):
    return 0.5 * jnp.tanh(0.5 * x) + 0.5


def _dot(a, b):
    return jnp.dot(a, b, preferred_element_type=F32)


def _dot_nt(a, b):
    return lax.dot_general(a, b, (((1,), (1,)), ((), ())), preferred_element_type=F32)


def _skip_ref(kernel_fn, idx):
    def wrapped(*refs):
        return kernel_fn(*refs[:idx], *refs[idx + 1:])
    return wrapped


def _layer_slab_call(make_kernel, in_specs, args, slab_out, layer, slab_block, slab_index, out_specs, **kw):
    n_layers = kw["out_shape"][-1].shape[0]
    if slab_out is None:
        spec = pl.BlockSpec((n_layers,) + slab_block, lambda *g: (0,) + slab_index(*g))
        return pl.pallas_call(make_kernel(layer), in_specs=in_specs, out_specs=list(out_specs) + [spec], **kw)(*args)
    spec = pl.BlockSpec((1,) + slab_block, lambda *g: (layer,) + slab_index(*g))
    idx = len(args)
    return pl.pallas_call(
        _skip_ref(make_kernel(0), idx),
        in_specs=list(in_specs) + [pl.BlockSpec(memory_space=pl.ANY)],
        out_specs=list(out_specs) + [spec],
        input_output_aliases={idx: len(kw["out_shape"]) - 1},
        **kw)(*args, slab_out)


def _zero_other_slabs(so_ref, own):
    for l in range(so_ref.shape[0]):
        if l != own:
            so_ref[l] = jnp.zeros(so_ref.shape[1:], so_ref.dtype)


def _rms_rows(x, g):
    ms = jnp.mean(x * x, axis=-1, keepdims=True)
    return x * lax.rsqrt(ms + EPS) * g


def _prenorm_kernel(x_ref, g_ref, o_ref):
    o_ref[...] = _rms_rows(x_ref[...], g_ref[...]).astype(BF16)


def prenorm(x, g, layer, tm):
    T, K = x.shape
    return pl.pallas_call(
        _prenorm_kernel,
        grid=(T // tm,),
        in_specs=[pl.BlockSpec((tm, K), lambda i: (i, 0)),
                  pl.BlockSpec((None, 1, K), lambda i: (layer, 0, 0))],
        out_specs=pl.BlockSpec((tm, K), lambda i: (i, 0)),
        out_shape=jax.ShapeDtypeStruct((T, K), BF16),
        compiler_params=_params("arbitrary"),
        name="prenorm",
    )(x, g)


def _matmul_kernel(a_ref, w_ref, o_ref):
    o_ref[...] = _dot(a_ref[...], w_ref[...].astype(BF16))


def matmul_cols(a, w, layer, n_cols, tm, tn):
    T, K = a.shape
    return pl.pallas_call(
        _matmul_kernel,
        grid=(T // tm, n_cols // tn),
        in_specs=[
            pl.BlockSpec((tm, K), lambda i, j: (i, 0)),
            pl.BlockSpec((None, K, tn), lambda i, j: (layer, 0, j)),
        ],
        out_specs=pl.BlockSpec((tm, tn), lambda i, j: (i, j)),
        out_shape=jax.ShapeDtypeStruct((T, n_cols), F32),
        compiler_params=_params("parallel", "arbitrary"),
        name="matmul_cols",
    )(a, w)


def _matmul_res_kernel(a_ref, w_ref, r_ref, o_ref):
    o_ref[...] = r_ref[...] + _dot(a_ref[...], w_ref[...].astype(BF16))


def matmul_res(a, w, layer, res, tm, tn):
    T, K = a.shape
    N = w.shape[2]
    return pl.pallas_call(
        _matmul_res_kernel,
        grid=(T // tm, N // tn),
        in_specs=[
            pl.BlockSpec((tm, K), lambda i, j: (i, 0)),
            pl.BlockSpec((None, K, tn), lambda i, j: (layer, 0, j)),
            pl.BlockSpec((tm, tn), lambda i, j: (i, j)),
        ],
        out_specs=pl.BlockSpec((tm, tn), lambda i, j: (i, j)),
        out_shape=jax.ShapeDtypeStruct((T, N), F32),
        compiler_params=_params("parallel", "arbitrary"),
        name="matmul_res",
    )(a, w, res)


def _res_norm_kernel(a_ref, w_ref, r_ref, g_ref, h_ref, hn_ref, w_s):
    @pl.when(pl.program_id(0) == 0)
    def _():
        w_s[...] = w_ref[...].astype(BF16)

    h = r_ref[...] + _dot(a_ref[...], w_s[...])
    h_ref[...] = h
    hn_ref[...] = _rms_rows(h, g_ref[...]).astype(BF16)


def matmul_res_norm(a, w, layer, res, g, tm):
    T, K = a.shape
    N = w.shape[2]
    return pl.pallas_call(
        _res_norm_kernel,
        grid=(T // tm,),
        in_specs=[
            pl.BlockSpec((tm, K), lambda i: (i, 0)),
            pl.BlockSpec((None, K, N), lambda i: (layer, 0, 0), pipeline_mode=pl.Buffered(1)),
            pl.BlockSpec((tm, N), lambda i: (i, 0)),
            pl.BlockSpec((None, 1, N), lambda i: (layer, 0, 0)),
        ],
        out_specs=[pl.BlockSpec((tm, N), lambda i: (i, 0)), pl.BlockSpec((tm, N), lambda i: (i, 0))],
        out_shape=[jax.ShapeDtypeStruct((T, N), F32), jax.ShapeDtypeStruct((T, N), BF16)],
        scratch_shapes=[pltpu.VMEM((K, N), BF16)],
        compiler_params=_params("arbitrary"),
        name="matmul_res_norm",
    )(a, w, res, g)


def _merge_kernel(xn_ref, ya_ref, yb_ref, yc_ref, ym_ref, wg0_ref, wg1_ref, wg2_ref, wg3_ref, wb_ref, o_ref,
                  wg_s, wb_s):
    @pl.when(pl.program_id(1) == 0)
    def _():
        for n, wg_ref in enumerate((wg0_ref, wg1_ref, wg2_ref, wg3_ref)):
            wg_s[n] = wg_ref[...].astype(BF16)
            wb_s[n] = wb_ref[n].astype(BF16)

    xn = xn_ref[...]
    acc = None
    for n, y_ref in enumerate((ya_ref, yb_ref, yc_ref, ym_ref)):
        t = _sigmoid(_dot(xn, wg_s[n])) * _dot(y_ref[...], wb_s[n])
        acc = t if acc is None else acc + t
    o_ref[...] = acc.astype(BF16)


def merge_branches(xn, ys, w_in, w_branch, layer, tm, tn):
    T = xn.shape[0]
    once = pl.Buffered(1)
    y_spec = pl.BlockSpec((tm, MIX_W), lambda j, i: (i, 0))

    def gate_spec(n):
        off = (COL_GATE + n * D_MODEL) // tn
        return pl.BlockSpec((None, D_MODEL, tn), lambda j, i: (layer, 0, off + j), pipeline_mode=once)

    return pl.pallas_call(
        _merge_kernel,
        grid=(D_MODEL // tn, T // tm),
        in_specs=[pl.BlockSpec((tm, D_MODEL), lambda j, i: (i, 0))] + [y_spec] * 4
        + [gate_spec(n) for n in range(N_BRANCH)]
        + [pl.BlockSpec((None, N_BRANCH, MIX_W, tn), lambda j, i: (layer, 0, 0, j), pipeline_mode=once)],
        out_specs=pl.BlockSpec((tm, tn), lambda j, i: (i, j)),
        out_shape=jax.ShapeDtypeStruct((T, D_MODEL), BF16),
        scratch_shapes=[pltpu.VMEM((N_BRANCH, D_MODEL, tn), BF16), pltpu.VMEM((N_BRANCH, MIX_W, tn), BF16)],
        compiler_params=_params("arbitrary", "arbitrary"),
        name="merge_branches",
    )(xn, *ys, w_in, w_in, w_in, w_in, w_branch)


CONV_HIST = 8


def _gelu(x):
    return 0.5 * x * (1.0 + lax.erf(x * (2.0 ** -0.5)))


def _up_conv_prompt_kernel(xn_ref, wa_ref, wv_ref, cw_ref, cb_ref, g_ref, tail_ref, carry_ref, *, tm, rc):
    i, j = pl.program_id(0), pl.program_id(1)
    wa = wa_ref[...].astype(BF16)
    wv = wv_ref[...].astype(BF16)
    prev = jnp.where(i == 0, 0.0, carry_ref[j])
    row = lax.broadcasted_iota(jnp.int32, (rc, wa.shape[1]), 0)
    for c in range(tm // rc):
        sl = pl.ds(c * rc, rc)
        xn = xn_ref[sl, :]
        a = _dot(xn, wa)
        v = _dot(xn, wv)
        a1 = jnp.where(row == 0, prev[CONV_HIST - 1:CONV_HIST], pltpu.roll(a, 1, 0))
        a2 = jnp.where(row == 0, prev[CONV_HIST - 2:CONV_HIST - 1],
                       jnp.where(row == 1, prev[CONV_HIST - 1:CONV_HIST], pltpu.roll(a, 2, 0)))
        cc = cb_ref[...] + cw_ref[0:1, :] * a2 + cw_ref[1:2, :] * a1 + cw_ref[2:3, :] * a
        g_ref[sl, :] = (_gelu(cc) * v).astype(BF16)
        prev = a[rc - CONV_HIST:, :]
    carry_ref[j] = prev
    tail_ref[...] = prev


def up_conv_prompt(xn, w_up, conv_w, conv_b, layer, tm, tn):
    T, K = xn.shape
    nc = D_FF // tn
    return pl.pallas_call(
        functools.partial(_up_conv_prompt_kernel, tm=tm, rc=min(tm, 1024)),
        grid=(T // tm, nc),
        in_specs=[
            pl.BlockSpec((tm, K), lambda i, j: (i, 0)),
            pl.BlockSpec((None, K, tn), lambda i, j: (layer, 0, j)),
            pl.BlockSpec((None, K, tn), lambda i, j: (layer, 0, nc + j)),
            pl.BlockSpec((None, 3, tn), lambda i, j: (layer, 0, j)),
            pl.BlockSpec((None, 1, tn), lambda i, j: (layer, 0, j)),
        ],
        out_specs=[
            pl.BlockSpec((tm, tn), lambda i, j: (i, j)),
            pl.BlockSpec((None, CONV_HIST, tn), lambda i, j: (i, 0, j)),
        ],
        out_shape=[
            jax.ShapeDtypeStruct((T, D_FF), BF16),
            jax.ShapeDtypeStruct((T // tm, CONV_HIST, D_FF), F32),
        ],
        scratch_shapes=[pltpu.VMEM((nc, CONV_HIST, tn), F32)],
        compiler_params=_params("arbitrary", "arbitrary"),
        name="up_conv_prompt",
    )(xn, w_up, w_up, conv_w, conv_b)


def _up_conv_sample_kernel(xn_ref, wa_ref, wv_ref, st_ref, cw_ref, cb_ref, g_ref, so_ref, *, steps, B, own):
    _zero_other_slabs(so_ref, own)
    xn = xn_ref[...]
    a = _dot(xn, wa_ref[...].astype(BF16))
    v = _dot(xn, wv_ref[...].astype(BF16))
    hist = [st_ref[:, 0, :], st_ref[:, 1, :]] + [a[t * B:(t + 1) * B] for t in range(steps)]
    for t in range(steps):
        c = cb_ref[...] + cw_ref[0:1, :] * hist[t] + cw_ref[1:2, :] * hist[t + 1] + cw_ref[2:3, :] * hist[t + 2]
        g_ref[t * B:(t + 1) * B, :] = (_gelu(c) * v[t * B:(t + 1) * B]).astype(BF16)
    so_ref[own, :, 0, :] = hist[steps]
    so_ref[own, :, 1, :] = hist[steps + 1]


def up_conv_sample(xn, w_up, state_conv, conv_w, conv_b, layer, steps, tn, state_out):
    T, K = xn.shape
    B = T // steps
    nc = D_FF // tn
    in_specs = [
        pl.BlockSpec((T, K), lambda j: (0, 0)),
        pl.BlockSpec((None, K, tn), lambda j: (layer, 0, j)),
        pl.BlockSpec((None, K, tn), lambda j: (layer, 0, nc + j)),
        pl.BlockSpec((None, B, 2, tn), lambda j: (layer, 0, 0, j)),
        pl.BlockSpec((None, 3, tn), lambda j: (layer, 0, j)),
        pl.BlockSpec((None, 1, tn), lambda j: (layer, 0, j)),
    ]
    return _layer_slab_call(
        lambda own: functools.partial(_up_conv_sample_kernel, steps=steps, B=B, own=own),
        in_specs, [xn, w_up, w_up, state_conv, conv_w, conv_b], state_out, layer,
        (B, 2, tn), lambda j: (0, 0, j),
        [pl.BlockSpec((T, tn), lambda j: (0, j))],
        grid=(nc,),
        out_shape=[
            jax.ShapeDtypeStruct((T, D_FF), BF16),
            jax.ShapeDtypeStruct(state_conv.shape, F32),
        ],
        compiler_params=_params("arbitrary"),
        name="up_conv_sample",
    )


def _hgrn_gates(q_in, z, lb):
    q = q_in * _sigmoid(q_in)
    log_sig = jnp.minimum(z, 0.0) - jnp.log(1.0 + jnp.exp(-jnp.abs(z)))
    a1 = jnp.log(lb)
    a2 = jnp.log1p(-lb) + log_sig
    log_f = jnp.maximum(a1, a2) + jnp.log(1.0 + jnp.exp(-jnp.abs(a1 - a2)))
    k = (1.0 - lb) * _sigmoid(-z)
    return q, log_f, k


def _hgrn_out(o, gate, gn):
    ms = jnp.mean(o * o, axis=-1, keepdims=True)
    return o * lax.rsqrt(ms + EPS) * gn * (gate * _sigmoid(gate))


def _cumsum_rows(x, tril):
    hi = x.astype(BF16)
    r1 = x - hi.astype(F32)
    mid = r1.astype(BF16)
    lo = (r1 - mid.astype(F32)).astype(BF16)
    return _dot(tril, hi) + _dot(tril, mid) + _dot(tril, lo)


def _block_row(x, blk, r):
    C = x.shape[0]
    x3 = x.reshape(C // blk, blk, LANES)
    return jnp.broadcast_to(x3[:, r:r + 1, :], (C // blk, blk, LANES)).reshape(C, LANES)


def _hgrn_pair_codes():
    t = np.arange(HGRN_CHUNK)[:, None]
    s = np.arange(HGRN_CHUNK)[None, :]
    level = np.floor(np.log2(np.maximum(t ^ s, 1))).astype(np.int32)
    return jnp.asarray(np.where(s > t, -1, np.where(s == t, 0, 1 + level)), jnp.int32)


def _boundary_row(b, m):
    if 2 * m >= 8:
        return _block_row(b, 2 * m, m - 1)
    r8 = lax.broadcasted_iota(jnp.int32, b.shape, 0) & 7
    if m == 2:
        return jnp.where(r8 < 4, _block_row(b, 8, 1), _block_row(b, 8, 5))
    return jnp.where(r8 < 2, _block_row(b, 8, 0),
                     jnp.where(r8 < 4, _block_row(b, 8, 2),
                               jnp.where(r8 < 6, _block_row(b, 8, 4), _block_row(b, 8, 6))))


def _hgrn_attention(q, k, b, code):
    C = HGRN_CHUNK
    rowl = lax.broadcasted_iota(jnp.int32, (C, LANES), 0)
    att = jnp.where(code == 0, jnp.sum(q * k, axis=-1, keepdims=True), 0.0)
    m, level = 1, 1
    while m < C:
        d = b - _boundary_row(b, m)
        isq = (rowl & m) != 0
        x = (jnp.where(isq, q, k) * jnp.exp(jnp.where(isq, d, -d))).astype(BF16)
        att = jnp.where(code == level, _dot_nt(x, x), att)
        m *= 2
        level += 1
    return att


def _hgrn_apply(q, k, v, b, att, S):
    C = HGRN_CHUNK
    vb = v.astype(BF16)
    o = _dot((q * jnp.exp(b)).astype(BF16), S.astype(BF16)) + _dot(att.astype(BF16), vb)
    bl = b[C - 1:C, :]
    kk = k * jnp.exp(bl - b)
    ecol = jnp.transpose(jnp.broadcast_to(jnp.exp(bl), (LANES, LANES)))
    return o, ecol * S + _dot(jnp.transpose(kk).astype(BF16), vb)


def _hgrn_prompt_kernel(q_ref, f_ref, i_ref, g_ref, lb_ref, gn_ref, code_ref, y_ref, so_ref, s_ref, *, rows):
    @pl.when(pl.program_id(1) == 0)
    def _():
        s_ref[...] = jnp.zeros_like(s_ref)

    C = HGRN_CHUNK
    code = code_ref[...]
    tril = jnp.where(code >= 0, 1.0, 0.0).astype(BF16)
    q, log_f, k = _hgrn_gates(q_ref[...], f_ref[...], lb_ref[...])
    chunks = [slice(c * C, (c + 1) * C) for c in range(rows // C)]
    bs = [_cumsum_rows(log_f[sl], tril) for sl in chunks]
    atts = [_hgrn_attention(q[sl], k[sl], b, code) for sl, b in zip(chunks, bs)]
    S = s_ref[...]
    outs = []
    for sl, b, att in zip(chunks, bs, atts):
        o, S = _hgrn_apply(q[sl], k[sl], i_ref[sl, :], b, att, S)
        outs.append(o)
    s_ref[...] = S
    y_ref[...] = _hgrn_out(jnp.concatenate(outs, axis=0), g_ref[...], gn_ref[...]).astype(BF16)

    @pl.when(pl.program_id(1) == pl.num_programs(1) - 1)
    def _():
        so_ref[...] = s_ref[...]


def hgrn_prompt(proj, lb, gn, rows):
    T = proj.shape[0]

    def col(off):
        base = off // LANES
        return pl.BlockSpec((rows, LANES), lambda h, c: (c, base + h))

    vec = pl.BlockSpec((1, LANES), lambda h, c: (0, h))
    return pl.pallas_call(
        functools.partial(_hgrn_prompt_kernel, rows=rows),
        grid=(A_HEADS, T // rows),
        in_specs=[col(COL_HQ), col(COL_HF), col(COL_HI), col(COL_HG), vec, vec,
                  pl.BlockSpec((HGRN_CHUNK, HGRN_CHUNK), lambda h, c: (0, 0))],
        out_specs=[
            pl.BlockSpec((rows, LANES), lambda h, c: (c, h)),
            pl.BlockSpec((None, A_DK, LANES), lambda h, c: (h, 0, 0)),
        ],
        out_shape=[
            jax.ShapeDtypeStruct((T, MIX_W), BF16),
            jax.ShapeDtypeStruct((A_HEADS, A_DK, LANES), F32),
        ],
        scratch_shapes=[pltpu.VMEM((A_DK, LANES), F32)],
        compiler_params=_params("parallel", "arbitrary"),
        name="hgrn_prompt",
    )(proj, proj, proj, proj, lb, gn, _hgrn_pair_codes())


def _hgrn_sample_kernel(q_ref, f_ref, i_ref, g_ref, lb_ref, gn_ref, s_ref, y_ref, so_ref, *, steps, bb, own):
    _zero_other_slabs(so_ref, own)
    lb = lb_ref[...]
    gn = gn_ref[...]
    qs, ks, vs, bs = [], [], [], []
    b = None
    for t in range(steps):
        q, log_f, k = _hgrn_gates(q_ref[t], f_ref[t], lb)
        b = log_f if b is None else b + log_f
        qs.append(q)
        ks.append(k)
        vs.append(i_ref[t])
        bs.append(b)
    intra = []
    for t in range(steps):
        acc = None
        for s in range(t + 1):
            w = jnp.sum(qs[t] * ks[s] * jnp.exp(bs[t] - bs[s]), axis=-1, keepdims=True)
            acc = w * vs[s] if acc is None else acc + w * vs[s]
        intra.append(acc)
    R = steps * bb
    q_stack = jnp.concatenate([qs[t] * jnp.exp(bs[t]) for t in range(steps)], axis=0).astype(BF16)
    k_stack = jnp.concatenate([ks[t] * jnp.exp(bs[-1] - bs[t]) for t in range(steps)], axis=0)
    v_stack = jnp.concatenate(vs, axis=0).astype(BF16)
    k_t = jnp.transpose(k_stack)
    f_pad = jnp.concatenate([jnp.exp(bs[-1])] + [jnp.zeros((R - bb, LANES), F32)], axis=0)
    f_t = jnp.transpose(f_pad)
    rowi = lax.broadcasted_iota(jnp.int32, (R, LANES), 0) % bb
    lanei = lax.broadcasted_iota(jnp.int32, (LANES, R), 1)

    def body(bi, o_acc):
        s_b = s_ref[bi]
        o_acc = jnp.where(rowi == bi, _dot(q_stack, s_b.astype(BF16)), o_acc)
        f_col = jnp.sum(jnp.where(lanei == bi, f_t, 0.0), axis=-1, keepdims=True)
        k_b = jnp.where(lanei % bb == bi, k_t, 0.0).astype(BF16)
        so_ref[own, bi] = f_col * s_b + _dot(k_b, v_stack)
        return o_acc

    o_inter = lax.fori_loop(0, bb, body, jnp.zeros((R, LANES), F32), unroll=4)
    for t in range(steps):
        o = o_inter[t * bb:(t + 1) * bb] + intra[t]
        y_ref[t] = _hgrn_out(o, g_ref[t], gn).astype(BF16)


def hgrn_sample(proj3, lb, gn, state, layer, bb, state_out):
    steps, B, _ = proj3.shape
    assert steps * bb == LANES

    def col(off):
        base = off // LANES
        return pl.BlockSpec((steps, bb, LANES), lambda g, h: (0, g, base + h))

    vec = pl.BlockSpec((1, LANES), lambda g, h: (0, h))
    slab = pl.BlockSpec((None, bb, None, A_DK, LANES), lambda g, h: (layer, g, h, 0, 0))
    return _layer_slab_call(
        lambda own: functools.partial(_hgrn_sample_kernel, steps=steps, bb=bb, own=own),
        [col(COL_HQ), col(COL_HF), col(COL_HI), col(COL_HG), vec, vec, slab],
        [proj3, proj3, proj3, proj3, lb, gn, state], state_out, layer,
        (bb, None, A_DK, LANES), lambda g, h: (g, h, 0, 0),
        [pl.BlockSpec((steps, bb, LANES), lambda g, h: (0, g, h))],
        grid=(B // bb, A_HEADS),
        out_shape=[
            jax.ShapeDtypeStruct((steps, B, MIX_W), BF16),
            jax.ShapeDtypeStruct(state.shape, F32),
        ],
        compiler_params=_params("parallel", "arbitrary"),
        name="hgrn_sample",
    )


POOL_HIST = 16


def _pool_project(d, w_ref, sc_ref, g):
    sl = slice(g * POOL_GC, (g + 1) * POOL_GC)
    return _dot(d.astype(BF16), w_ref[g].astype(BF16)) * sc_ref[:, sl]


def _pool_prompt_kernel(u_ref, prev_ref, w_ref, sc_ref, y_ref, ext_ref, *, tm):
    i = pl.program_id(0)
    ext_ref[0:POOL_HIST, :] = jnp.where(i == 0, 0.0, prev_ref[...])
    ext_ref[POOL_HIST:, :] = u_ref[...]
    pos = i * tm + lax.broadcasted_iota(jnp.int32, (tm, 1), 0)
    for g, win in enumerate(POOL_WINDOWS):
        sl = slice(g * POOL_GC, (g + 1) * POOL_GC)
        acc = ext_ref[pl.ds(POOL_HIST, tm), sl]
        for j in range(1, win):
            acc = acc + ext_ref[pl.ds(POOL_HIST - j, tm), sl]
        cnt = jnp.minimum(pos + 1, win).astype(F32)
        d = acc / cnt - u_ref[:, sl]
        y_ref[:, sl] = _pool_project(d, w_ref, sc_ref, g).astype(BF16)


def pool_prompt(proj, pool_w, pool_scale, layer, tm):
    T = proj.shape[0]
    cb = COL_POOL // MIX_W
    return pl.pallas_call(
        functools.partial(_pool_prompt_kernel, tm=tm),
        grid=(T // tm,),
        in_specs=[
            pl.BlockSpec((tm, MIX_W), lambda i: (i, cb)),
            pl.BlockSpec((POOL_HIST, MIX_W), lambda i: (jnp.maximum(i * (tm // POOL_HIST) - 1, 0), cb)),
            pl.BlockSpec((None, len(POOL_WINDOWS), POOL_GC, POOL_GC), lambda i: (layer, 0, 0, 0)),
            pl.BlockSpec((None, 1, MIX_W), lambda i: (layer, 0, 0)),
        ],
        out_specs=pl.BlockSpec((tm, MIX_W), lambda i: (i, 0)),
        out_shape=jax.ShapeDtypeStruct((T, MIX_W), BF16),
        scratch_shapes=[pltpu.VMEM((POOL_HIST + tm, MIX_W), F32)],
        compiler_params=_params("arbitrary"),
        name="pool_prompt",
    )(proj, proj, pool_w, pool_scale)


def _pool_sample_kernel(u_ref, c_ref, w_ref, sc_ref, y_ref, *, steps):
    for t in range(steps):
        for g, win in enumerate(POOL_WINDOWS):
            sl = slice(g * POOL_GC, (g + 1) * POOL_GC)
            acc = u_ref[t, :, sl]
            for j in range(1, win):
                if j <= t:
                    acc = acc + u_ref[t - j, :, sl]
                else:
                    acc = acc + c_ref[POOL_BUF + t - j, :, sl]
            d = acc / float(win) - u_ref[t, :, sl]
            y_ref[t, :, sl] = _pool_project(d, w_ref, sc_ref, g).astype(BF16)


def pool_sample(proj3, cache_pool, pool_w, pool_scale, layer, bb):
    steps, B, _ = proj3.shape
    cb = COL_POOL // MIX_W
    return pl.pallas_call(
        functools.partial(_pool_sample_kernel, steps=steps),
        grid=(B // bb,),
        in_specs=[
            pl.BlockSpec((steps, bb, MIX_W), lambda g: (0, g, cb)),
            pl.BlockSpec((None, POOL_BUF, bb, MIX_W), lambda g: (layer, 0, g, 0)),
            pl.BlockSpec((None, len(POOL_WINDOWS), POOL_GC, POOL_GC), lambda g: (layer, 0, 0, 0)),
            pl.BlockSpec((None, 1, MIX_W), lambda g: (layer, 0, 0)),
        ],
        out_specs=pl.BlockSpec((steps, bb, MIX_W), lambda g: (0, g, 0)),
        out_shape=jax.ShapeDtypeStruct((steps, B, MIX_W), BF16),
        compiler_params=_params("arbitrary"),
        name="pool_sample",
    )(proj3, cache_pool, pool_w, pool_scale)


def _rope_tables(positions):
    half = ROT_DIM // 2
    inv = np.power(ROPE_THETA, -np.arange(0, ROT_DIM, 2, dtype=np.float64) / ROT_DIM)
    ang = np.asarray(positions, np.float64)[:, None] * inv[None, :]
    cos, sin = np.cos(ang), np.sin(ang)
    n = len(positions)
    ct = np.ones((n, LANES))
    sn = np.zeros((n, LANES))
    sp = np.zeros((n, LANES))
    for base in (0, SWA_HEAD_DIM):
        ct[:, base:base + half] = cos
        ct[:, base + half:base + ROT_DIM] = cos
        sn[:, base:base + half] = -sin
        sp[:, base + half:base + ROT_DIM] = sin
    return tuple(jnp.asarray(t, F32) for t in (ct, sn, sp))


def _head_norm_rope(x, g, ct, sn, sp):
    lane = lax.broadcasted_iota(jnp.int32, x.shape, 1)
    lo = lane < SWA_HEAD_DIM
    x2 = x * x
    ms_lo = jnp.sum(jnp.where(lo, x2, 0.0), axis=-1, keepdims=True) / SWA_HEAD_DIM
    ms_hi = jnp.sum(jnp.where(lo, 0.0, x2), axis=-1, keepdims=True) / SWA_HEAD_DIM
    xn = x * jnp.where(lo, lax.rsqrt(ms_lo + EPS), lax.rsqrt(ms_hi + EPS)) * g
    half = ROT_DIM // 2
    return xn * ct + pltpu.roll(xn, LANES - half, 1) * sn + pltpu.roll(xn, half, 1) * sp


def _kprep_kernel(k_ref, g_ref, ct_ref, sn_ref, sp_ref, o_ref):
    ct, sn, sp = ct_ref[...], sn_ref[...], sp_ref[...]
    for j in range(2):
        sl = slice(j * LANES, (j + 1) * LANES)
        o_ref[:, sl] = _head_norm_rope(k_ref[:, sl], g_ref[...], ct, sn, sp)


def swa_kprep(proj, g2, tables, tm):
    T = proj.shape[0]
    kw = SWA_KV_HEADS * SWA_HEAD_DIM
    tab = pl.BlockSpec((tm, LANES), lambda i: (i, 0))
    return pl.pallas_call(
        _kprep_kernel,
        grid=(T // tm,),
        in_specs=[pl.BlockSpec((tm, kw), lambda i: (i, COL_SK // kw)),
                  pl.BlockSpec((1, LANES), lambda i: (0, 0)), tab, tab, tab],
        out_specs=pl.BlockSpec((tm, kw), lambda i: (i, 0)),
        out_shape=jax.ShapeDtypeStruct((T, kw), F32),
        compiler_params=_params("arbitrary"),
        name="swa_kprep",
    )(proj, g2, *tables)


def _dup_head(x, parity):
    lane = lax.broadcasted_iota(jnp.int32, x.shape, 1)
    return jnp.where(lane // SWA_HEAD_DIM == parity, x, pltpu.roll(x, SWA_HEAD_DIM, 1))


def _stack_heads(q):
    lane = lax.broadcasted_iota(jnp.int32, q.shape, 1)
    lo = lane < SWA_HEAD_DIM
    return jnp.concatenate([jnp.where(lo, q, 0.0), jnp.where(lo, 0.0, q)], axis=0)


def _unstack_heads(o2):
    R = o2.shape[0] // 2
    lane = lax.broadcasted_iota(jnp.int32, (R, LANES), 1)
    return jnp.where(lane < SWA_HEAD_DIM, o2[:R], o2[R:])


def _swa_prompt_kernel(sink_ref, q_ref, kc_ref, kp_ref, vc_ref, vp_ref, g_ref, ct_ref, sn_ref, sp_ref, y_ref,
                       *, layer, nb):
    first = pl.program_id(0) == 0
    W = WINDOW
    G = SWA_Q_HEADS // SWA_KV_HEADS
    r4 = lax.broadcasted_iota(jnp.int32, (G * W, 1), 0)
    hh = r4 // W
    ci = lax.broadcasted_iota(jnp.int32, (1, W), 1)
    cur = ci <= r4 % W
    scale = SWA_HEAD_DIM ** -0.5

    def head_blocks(cur_ref, prev_ref, kvh):
        ksl = slice((kvh // 2) * LANES, (kvh // 2 + 1) * LANES)
        x = _dup_head(jnp.concatenate([prev_ref[:, ksl], cur_ref[:, ksl]], axis=0), kvh % 2).astype(BF16)
        return [x[j * W:(j + 1) * W] for j in range(nb + 1)]

    scores, sinks = [], []
    for kvh in range(SWA_KV_HEADS):
        kb = head_blocks(kc_ref, kp_ref, kvh)
        sink = sink_ref[layer, G * kvh + G - 1]
        for i in range(G - 2, -1, -1):
            sink = jnp.where(hh == i, sink_ref[layer, G * kvh + i], sink)
        for blk in range(nb):
            rows = slice(blk * W, (blk + 1) * W)
            qs = []
            for jj in range(G // 2):
                qsl = slice((2 * kvh + jj) * LANES, (2 * kvh + jj + 1) * LANES)
                qn = _head_norm_rope(q_ref[rows, qsl], g_ref[...], ct_ref[rows, :], sn_ref[rows, :], sp_ref[rows, :])
                qs.append(_stack_heads(qn * scale))
            q = jnp.concatenate(qs, axis=0).astype(BF16)
            s_prev = _dot_nt(q, kb[blk])
            if blk == 0:
                s_prev = jnp.where(first, NEG_BIG, s_prev)
            scores.append(jnp.where(cur, _dot_nt(q, kb[blk + 1]), s_prev))
            sinks.append(sink)
    probs = []
    for s, sink in zip(scores, sinks):
        m = jnp.maximum(jnp.max(s, axis=-1, keepdims=True), sink)
        e = jnp.exp(s - m)
        probs.append(e * (1.0 / (jnp.sum(e, axis=-1, keepdims=True) + jnp.exp(sink - m))))
    for kvh in range(SWA_KV_HEADS):
        vb = head_blocks(vc_ref, vp_ref, kvh)
        for blk in range(nb):
            p = probs[kvh * nb + blk]
            o = (_dot(jnp.where(cur, p, 0.0).astype(BF16), vb[blk + 1])
                 + _dot(jnp.where(cur, 0.0, p).astype(BF16), vb[blk]))
            for jj in range(G // 2):
                qsl = slice((2 * kvh + jj) * LANES, (2 * kvh + jj + 1) * LANES)
                y_ref[blk * W:(blk + 1) * W, qsl] = _unstack_heads(o[2 * jj * W:(2 * jj + 2) * W]).astype(BF16)


def swa_prompt(proj, khat, sinks, g2, tables, layer, nb):
    T = proj.shape[0]
    W = WINDOW
    tq = nb * W
    kw = SWA_KV_HEADS * SWA_HEAD_DIM
    tab = pl.BlockSpec((tq, LANES), lambda i: (i, 0))
    prev = lambda i: jnp.maximum(i * nb - 1, 0)
    return pl.pallas_call(
        functools.partial(_swa_prompt_kernel, layer=layer, nb=nb),
        grid=(T // tq,),
        in_specs=[
            pl.BlockSpec(memory_space=pltpu.SMEM),
            pl.BlockSpec((tq, MIX_W), lambda i: (i, COL_SQ // MIX_W)),
            pl.BlockSpec((tq, kw), lambda i: (i, 0)),
            pl.BlockSpec((W, kw), lambda i: (prev(i), 0)),
            pl.BlockSpec((tq, kw), lambda i: (i, COL_SV // kw)),
            pl.BlockSpec((W, kw), lambda i: (prev(i), COL_SV // kw)),
            pl.BlockSpec((1, LANES), lambda i: (0, 0)), tab, tab, tab,
        ],
        out_specs=pl.BlockSpec((tq, MIX_W), lambda i: (i, 0)),
        out_shape=jax.ShapeDtypeStruct((T, MIX_W), BF16),
        compiler_params=_params("arbitrary"),
        name="swa_prompt",
    )(sinks, proj, khat, khat, proj, proj, g2, *tables)


def _swa_sample_kernel(sink_ref, q_ref, kn_ref, vn_ref, kc_ref, vc_ref, g_ref, ct_ref, sn_ref, sp_ref, y_ref,
                       *, layer, steps, bb):
    kvh = pl.program_id(1)
    parity = kvh % 2
    W = WINDOW
    R = steps * bb
    G = SWA_Q_HEADS // SWA_KV_HEADS
    ct, sn, sp = ct_ref[...], sn_ref[...], sp_ref[...]
    scale = SWA_HEAD_DIM ** -0.5
    r4 = lax.broadcasted_iota(jnp.int32, (G * R, 1), 0)
    hh = r4 // R
    tq = (r4 % R) // bb
    bq = r4 % bb
    c_new = lax.broadcasted_iota(jnp.int32, (1, R), 1)
    valid_new = (c_new % bb == bq) & (c_new // bb <= tq)
    c_old = lax.broadcasted_iota(jnp.int32, (1, W), 1)
    valid_old = c_old > tq
    kn = _dup_head(jnp.concatenate([kn_ref[t] for t in range(steps)], axis=0), parity).astype(BF16)
    vn = _dup_head(jnp.concatenate([vn_ref[t] for t in range(steps)], axis=0), parity).astype(BF16)
    qs = []
    for jj in range(G // 2):
        qsl = slice(jj * LANES, (jj + 1) * LANES)
        q = jnp.concatenate([q_ref[t, :, qsl] for t in range(steps)], axis=0)
        qs.append(_stack_heads(_head_norm_rope(q, g_ref[...], ct, sn, sp) * scale))
    q4 = jnp.concatenate(qs, axis=0)
    s_new = jnp.where(valid_new, _dot_nt(q4.astype(BF16), kn), NEG_BIG)
    s_old = None
    for b in range(0, bb, 2):
        lhs = jnp.concatenate([jnp.where(bq == b + i, q4, 0.0).astype(BF16) for i in range(2)], axis=1)
        k_t = [kc_ref[b + i].astype(BF16) for i in range(2)]
        d = _dot(lhs, jnp.concatenate([k_t[0], k_t[0], k_t[1], k_t[1]], axis=0))
        s_old = d if s_old is None else s_old + d
    s_old = jnp.where(valid_old, s_old, NEG_BIG)
    sink = sink_ref[layer, G * kvh + G - 1]
    for i in range(G - 2, -1, -1):
        sink = jnp.where(hh == i, sink_ref[layer, G * kvh + i], sink)
    m = jnp.maximum(jnp.maximum(jnp.max(s_new, axis=-1, keepdims=True),
                                jnp.max(s_old, axis=-1, keepdims=True)), sink)
    e_new = jnp.exp(s_new - m)
    e_old = jnp.exp(s_old - m)
    den = jnp.sum(e_new, axis=-1, keepdims=True) + jnp.sum(e_old, axis=-1, keepdims=True) + jnp.exp(sink - m)
    p_old = e_old / den
    o = _dot((e_new / den).astype(BF16), vn)
    for b in range(0, bb, 2):
        lhs = jnp.concatenate([jnp.where(bq == b + i, p_old, 0.0).astype(BF16) for i in range(2)], axis=1)
        v_t = [vc_ref[b + i].astype(BF16) for i in range(2)]
        rhs = jnp.concatenate([jnp.concatenate([v_t[i], v_t[i]], axis=0) for i in range(2)], axis=1)
        o = o + _dot_nt(lhs, rhs)
    for jj in range(G // 2):
        o_j = _unstack_heads(o[2 * jj * R:(2 * jj + 2) * R])
        for t in range(steps):
            y_ref[t, :, jj * LANES:(jj + 1) * LANES] = o_j[t * bb:(t + 1) * bb]


def swa_sample(proj3, khat3, cache_kt, cache_vt, sinks, g2, tables, layer, bb):
    steps, B, _ = proj3.shape
    R = steps * bb
    qw = MIX_W // SWA_KV_HEADS
    tab = pl.BlockSpec((R, LANES), lambda g, h: (0, 0))
    cache = pl.BlockSpec((None, bb, None, SWA_HEAD_DIM, WINDOW), lambda g, h: (layer, g, h, 0, 0))
    return pl.pallas_call(
        functools.partial(_swa_sample_kernel, layer=layer, steps=steps, bb=bb),
        grid=(B // bb, SWA_KV_HEADS),
        in_specs=[
            pl.BlockSpec(memory_space=pltpu.SMEM),
            pl.BlockSpec((steps, bb, qw), lambda g, h: (0, g, COL_SQ // qw + h)),
            pl.BlockSpec((steps, bb, LANES), lambda g, h: (0, g, h // 2)),
            pl.BlockSpec((steps, bb, LANES), lambda g, h: (0, g, COL_SV // LANES + h // 2)),
            cache, cache,
            pl.BlockSpec((1, LANES), lambda g, h: (0, 0)), tab, tab, tab,
        ],
        out_specs=pl.BlockSpec((steps, bb, qw), lambda g, h: (0, g, h)),
        out_shape=jax.ShapeDtypeStruct((steps, B, MIX_W), F32),
        compiler_params=_params("parallel", "arbitrary"),
        name="swa_sample",
    )(sinks, proj3, khat3, proj3, cache_kt, cache_vt, g2, *tables)


def _mem_kv_kernel(x_ref, g_ref, w_ref, kg_ref, o_ref, xn_ref):
    j = pl.program_id(0)

    @pl.when(j == 0)
    def _():
        x = x_ref[...]
        ms = jnp.mean(x * x, axis=-1, keepdims=True)
        xn_ref[...] = (x * lax.rsqrt(ms + EPS) * g_ref[...]).astype(BF16)

    y = _dot(xn_ref[...], w_ref[...].astype(BF16))

    @pl.when(j < MEM_HEADS)
    def _():
        ms = jnp.mean(y * y, axis=-1, keepdims=True)
        o_ref[...] = y * lax.rsqrt(ms + EPS) * kg_ref[...]

    @pl.when(j >= MEM_HEADS)
    def _():
        o_ref[...] = y


def mem_kv(mem, mem_norm_g, w_mem_kv, mem_knorm_g, layer):
    M, K = mem.shape
    hd = MEM_HEAD_DIM
    return pl.pallas_call(
        _mem_kv_kernel,
        grid=(2 * MEM_HEADS,),
        in_specs=[
            pl.BlockSpec((M, K), lambda j: (0, 0)),
            pl.BlockSpec((None, 1, K), lambda j: (layer, 0, 0)),
            pl.BlockSpec((None, K, hd), lambda j: (layer, 0, j)),
            pl.BlockSpec((None, 1, hd), lambda j: (layer, 0, 0)),
        ],
        out_specs=pl.BlockSpec((M, hd), lambda j: (0, j)),
        out_shape=jax.ShapeDtypeStruct((M, 2 * MIX_W), F32),
        scratch_shapes=[pltpu.VMEM((M, K), BF16)],
        compiler_params=_params("arbitrary"),
        name="mem_kv",
    )(mem, mem_norm_g, w_mem_kv, mem_knorm_g)


def _mem_qnorm(q, g):
    ms = jnp.mean(q * q, axis=-1, keepdims=True)
    return q * lax.rsqrt(ms + EPS) * g * (MEM_HEAD_DIM ** -0.5)


def _softmax_rows(s):
    m = jnp.max(s, axis=-1, keepdims=True)
    e = jnp.exp(s - m)
    return e / jnp.sum(e, axis=-1, keepdims=True)


def _mem_prompt_kernel(q0_ref, q1_ref, q2_ref, q3_ref, kv_ref, g_ref, y_ref):
    hd = MEM_HEAD_DIM
    for h, q_ref in enumerate((q0_ref, q1_ref, q2_ref, q3_ref)):
        q = _mem_qnorm(q_ref[...], g_ref[...]).astype(BF16)
        p = _softmax_rows(_dot_nt(q, kv_ref[:, h * hd:(h + 1) * hd].astype(BF16)))
        v = kv_ref[:, MIX_W + h * hd:MIX_W + (h + 1) * hd].astype(BF16)
        y_ref[:, h * hd:(h + 1) * hd] = _dot(p.astype(BF16), v).astype(BF16)


def mem_attn_prompt(proj, kv, mem_qnorm_g, layer, tq):
    T = proj.shape[0]
    hd = MEM_HEAD_DIM

    def q_spec(h):
        cb = COL_MQ // hd + h
        return pl.BlockSpec((tq, hd), lambda i: (i, cb))

    return pl.pallas_call(
        _mem_prompt_kernel,
        grid=(T // tq,),
        in_specs=[q_spec(h) for h in range(MEM_HEADS)] + [
            pl.BlockSpec((N_MEM, 2 * MIX_W), lambda i: (0, 0)),
            pl.BlockSpec((None, 1, hd), lambda i: (layer, 0, 0)),
        ],
        out_specs=pl.BlockSpec((tq, MIX_W), lambda i: (i, 0)),
        out_shape=jax.ShapeDtypeStruct((T, MIX_W), BF16),
        compiler_params=_params("arbitrary"),
        name="mem_prompt",
    )(proj, proj, proj, proj, kv, mem_qnorm_g)


def _mem_rows_view(c):
    L_, B_, M, H, hd = c.shape
    c = c.reshape(L_, B_, M, H, hd // LANES, LANES)
    return jnp.transpose(c, (0, 1, 2, 4, 3, 5)).reshape(L_, B_, M * H * (hd // LANES), LANES)


def _mem_head(c_ref, b, h):
    nt = MEM_HEAD_DIM // LANES
    parts = [c_ref[b, pl.ds(lt * MEM_HEADS + h, N_MEM, stride=nt * MEM_HEADS), :] for lt in range(nt)]
    return jnp.concatenate(parts, axis=1).astype(BF16)


def _mem_sample_kernel(q0_ref, q1_ref, q2_ref, q3_ref, k_ref, v_ref, g_ref, y_ref, *, steps, bb):
    R = steps * bb
    hd = MEM_HEAD_DIM
    bq = lax.broadcasted_iota(jnp.int32, (R, 1), 0) % bb
    for h, q_ref in enumerate((q0_ref, q1_ref, q2_ref, q3_ref)):
        q = _mem_qnorm(jnp.concatenate([q_ref[t] for t in range(steps)], axis=0), g_ref[...])
        s = None
        for b in range(bb):
            d = _dot_nt(jnp.where(bq == b, q, 0.0).astype(BF16), _mem_head(k_ref, b, h))
            s = d if s is None else s + d
        p = _softmax_rows(s)
        o = None
        for b in range(bb):
            d = _dot(jnp.where(bq == b, p, 0.0).astype(BF16), _mem_head(v_ref, b, h))
            o = d if o is None else o + d
        for t in range(steps):
            y_ref[t, :, h * hd:(h + 1) * hd] = o[t * bb:(t + 1) * bb]


def mem_attn_sample(proj3, cache_k, cache_v, mem_qnorm_g, layer, bb):
    steps, B, _ = proj3.shape
    hd = MEM_HEAD_DIM
    cache = pl.BlockSpec((None, bb) + cache_k.shape[2:], lambda g: (layer, g, 0, 0))

    def q_spec(h):
        cb = COL_MQ // hd + h
        return pl.BlockSpec((steps, bb, hd), lambda g: (0, g, cb))

    return pl.pallas_call(
        functools.partial(_mem_sample_kernel, steps=steps, bb=bb),
        grid=(B // bb,),
        in_specs=[q_spec(h) for h in range(MEM_HEADS)] + [
            cache, cache,
            pl.BlockSpec((None, 1, hd), lambda g: (layer, 0, 0)),
        ],
        out_specs=pl.BlockSpec((steps, bb, MIX_W), lambda g: (0, g, 0)),
        out_shape=jax.ShapeDtypeStruct((steps, B, MIX_W), F32),
        compiler_params=_params("arbitrary"),
        name="mem_sample",
    )(proj3, proj3, proj3, proj3, cache_k, cache_v, mem_qnorm_g)


def _row_tile(T, cap):
    t = cap
    while T % t:
        t //= 2
    return t


TM_STREAM = 2048
TM_DOWN = 1024
TM_LOCAL = 512


def _token_tail(x, xn, ys, layer, w_in, w_branch, w_o, norm2_g):
    T = x.shape[0]
    merged = merge_branches(xn, ys, w_in, w_branch, layer, _row_tile(T, 512), 512)
    return matmul_res_norm(merged, w_o, layer, x, norm2_g, _row_tile(T, 256))


def kernel(x_prompt, x_sample, mem_prompt, state_hgrn, cache_pool, cache_swa_k, cache_swa_v, state_conv, cache_mem_k, cache_mem_v, norm1_g, w_in, hgrn_lb, hgrn_norm_g, pool_w, pool_scale, swa_qnorm_g, swa_knorm_g, swa_sinks, mem_norm_g, w_mem_kv, mem_qnorm_g, mem_knorm_g, w_branch, w_o, norm2_g, w_up, conv_w, conv_b, w_down):
    depth = w_in.shape[0]
    bp, L, _ = x_prompt.shape
    B, steps, _ = x_sample.shape
    assert bp == 1
    kw = SWA_KV_HEADS * SWA_HEAD_DIM
    Ts = steps * B

    lb_all = jnp.cumsum(jax.nn.softmax(hgrn_lb.astype(F32), axis=0), axis=0)
    lb_all = lb_all - lb_all[:1]

    swa_bb = 8
    tab_p = _rope_tables(np.arange(L))
    tab_s = _rope_tables(np.repeat(PAST_LEN + np.arange(steps), B))
    tab_sb = _rope_tables(np.repeat(PAST_LEN + np.arange(steps), swa_bb))

    xp = x_prompt.reshape(L, D_MODEL)
    xs = jnp.transpose(x_sample, (1, 0, 2)).reshape(Ts, D_MODEL)
    mem = mem_prompt.reshape(N_MEM, D_MODEL)
    ckt_all = jnp.transpose(cache_swa_k, (0, 1, 3, 4, 2))
    cvt_all = jnp.transpose(cache_swa_v, (0, 1, 3, 4, 2))
    mk_rows = _mem_rows_view(cache_mem_k)
    mv_rows = _mem_rows_view(cache_mem_v)
    cpool_v = jnp.transpose(cache_pool, (0, 2, 1, 3))
    row3 = lambda a: a.reshape(depth, 1, a.shape[-1])
    norm1_g, norm2_g, pool_scale, conv_b = row3(norm1_g), row3(norm2_g), row3(pool_scale), row3(conv_b)
    mem_norm_g, mem_qnorm_g, mem_knorm_g = row3(mem_norm_g), row3(mem_qnorm_g), row3(mem_knorm_g)
    tm_p = _row_tile(L, TM_STREAM)
    tl_p = _row_tile(L, TM_LOCAL)

    outs = {k: [] for k in ("sp", "pp", "ps", "kp", "ks", "vp", "vs", "cp", "mk", "mv")}
    hgrn_states = None
    conv_states = None
    for l in range(depth):
        lb = lb_all[l].reshape(1, MIX_W)
        gn = hgrn_norm_g[l].reshape(1, MIX_W)
        gq2 = jnp.tile(swa_qnorm_g[l], 2).reshape(1, LANES)
        gk2 = jnp.tile(swa_knorm_g[l], 2).reshape(1, LANES)

        kv = mem_kv(mem, mem_norm_g, w_mem_kv, mem_knorm_g, l)

        xn = prenorm(xp, norm1_g, l, tl_p)
        proj = matmul_cols(xn, w_in, l, COL_GATE, tm_p, 512)
        ya, s_p = hgrn_prompt(proj, lb, gn, _row_tile(L, 2048))
        yb = pool_prompt(proj, pool_w, pool_scale, l, tl_p)
        khat = swa_kprep(proj, gk2, tab_p, tl_p)
        yc = swa_prompt(proj, khat, swa_sinks, gq2, tab_p, l, 4)
        ym = mem_attn_prompt(proj, kv, mem_qnorm_g, l, tl_p)
        h, hn = _token_tail(xp, xn, (ya, yb, yc, ym), l, w_in, w_branch, w_o, norm2_g)
        gact, a_tail = up_conv_prompt(hn, w_up, conv_w, conv_b, l, tm_p, 512)
        xp = matmul_res(gact, w_down, l, h, _row_tile(L, TM_DOWN), 256)

        outs["sp"].append(s_p[None])
        outs["pp"].append(proj[None, L - POOL_BUF:, COL_POOL:COL_POOL + MIX_W])
        outs["kp"].append(khat[None, L - WINDOW:].reshape(1, WINDOW, SWA_KV_HEADS, SWA_HEAD_DIM))
        outs["vp"].append(proj[None, L - WINDOW:, COL_SV:COL_SV + kw].reshape(1, WINDOW, SWA_KV_HEADS, SWA_HEAD_DIM))
        outs["cp"].append(a_tail[-1:, CONV_HIST - 2:])
        outs["mk"].append(kv[None, :, :MIX_W].reshape(1, N_MEM, MEM_HEADS, MEM_HEAD_DIM))
        outs["mv"].append(kv[None, :, MIX_W:].reshape(1, N_MEM, MEM_HEADS, MEM_HEAD_DIM))

        xn = prenorm(xs, norm1_g, l, Ts)
        proj_s = matmul_cols(xn, w_in, l, COL_GATE, Ts, 512)
        proj3 = proj_s.reshape(steps, B, COL_GATE)
        ya, hgrn_states = hgrn_sample(proj3, lb, gn, state_hgrn, l, LANES // steps, hgrn_states)
        yb = pool_sample(proj3, cpool_v, pool_w, pool_scale, l, 64)
        khat_s = swa_kprep(proj_s, gk2, tab_s, Ts)
        khat3 = khat_s.reshape(steps, B, kw)
        yc = swa_sample(proj3, khat3, ckt_all, cvt_all, swa_sinks, gq2, tab_sb, l, swa_bb)
        ym = mem_attn_sample(proj3, mk_rows, mv_rows, mem_qnorm_g, l, 8)
        ys = tuple(y.reshape(Ts, MIX_W).astype(BF16) for y in (ya, yb, yc, ym))
        h, hn = _token_tail(xs, xn, ys, l, w_in, w_branch, w_o, norm2_g)
        gact, conv_states = up_conv_sample(hn, w_up, state_conv, conv_w, conv_b, l, steps, 256, conv_states)
        xs = matmul_res(gact, w_down, l, h, Ts, 256)

        outs["ps"].append(proj3[:, :, COL_POOL:COL_POOL + MIX_W])
        to_window_minor = lambda a: jnp.transpose(a.reshape(steps, B, SWA_KV_HEADS, SWA_HEAD_DIM), (1, 2, 3, 0))
        outs["ks"].append(to_window_minor(khat3))
        outs["vs"].append(to_window_minor(proj3[:, :, COL_SV:COL_SV + kw]))

    stk = lambda k: jnp.stack(outs[k], axis=0)
    pool_s = jnp.transpose(jnp.concatenate([cpool_v[:, steps:], stk("ps")], axis=1), (0, 2, 1, 3))
    def slide_window(old, new):
        lead = [(0, 0, 0)] * (old.ndim - 1)
        shifted = lax.pad(old, jnp.zeros((), old.dtype), lead + [(-steps, steps, 0)])
        tail = lax.pad(new, jnp.zeros((), old.dtype), lead + [(WINDOW - steps, 0, 0)])
        pos = lax.broadcasted_iota(jnp.int32, old.shape, old.ndim - 1)
        return jnp.transpose(jnp.where(pos < WINDOW - steps, shifted, tail), (0, 1, 4, 2, 3))

    swa_k_s = slide_window(ckt_all, stk("ks"))
    swa_v_s = slide_window(cvt_all, stk("vs"))
    y_prompt = xp.reshape(1, L, D_MODEL)
    y_sample = jnp.transpose(xs.reshape(steps, B, D_MODEL), (1, 0, 2))
    return (y_prompt, y_sample,
            stk("sp"), hgrn_states, stk("pp"), pool_s, stk("kp"), swa_k_s, stk("vp"), swa_v_s,
            stk("cp"), conv_states, jnp.concatenate(outs["mk"], axis=0)[:, None], jnp.concatenate(outs["mv"], axis=0)[:, None])
```

```python
import functools

import numpy as np
import jax
import jax.numpy as jnp
from jax import lax
from jax.experimental import pallas as pl
from jax.experimental.pallas import tpu as pltpu

F32 = jnp.float32
BF16 = jnp.bfloat16

D_MODEL = 2048
MIX_W = D_MODEL // 2
N_BRANCH = 4
A_DK = 128
A_HEADS = MIX_W // A_DK
POOL_WINDOWS = (2, 4, 8, 16)
POOL_GC = MIX_W // len(POOL_WINDOWS)
POOL_BUF = max(POOL_WINDOWS) - 1
SWA_HEAD_DIM = 64
SWA_Q_HEADS = MIX_W // SWA_HEAD_DIM
SWA_KV_HEADS = SWA_Q_HEADS // 4
WINDOW = 128
ROT_DIM = SWA_HEAD_DIM // 4
ROPE_THETA = 500000.0
N_MEM = 256
MEM_HEADS = 4
MEM_HEAD_DIM = MIX_W // MEM_HEADS
D_FF = 11 * D_MODEL // 4
EPS = 1e-6
PAST_LEN = 8192

COL_HQ, COL_HF, COL_HI, COL_HG = 0, MIX_W, 2 * MIX_W, 3 * MIX_W
COL_POOL = 4 * MIX_W
COL_SQ = 5 * MIX_W
COL_SK = 6 * MIX_W
COL_SV = COL_SK + SWA_KV_HEADS * SWA_HEAD_DIM
COL_MQ = COL_SV + SWA_KV_HEADS * SWA_HEAD_DIM
COL_GATE = COL_MQ + MIX_W
IN_COLS = COL_GATE + N_BRANCH * D_MODEL

LANES = 128
HGRN_CHUNK = 128
VMEM_LIMIT = 56 * 1024 * 1024
NEG_BIG = -1e30


def _params(*sem):
    return pltpu.CompilerParams(dimension_semantics=sem, vmem_limit_bytes=VMEM_LIMIT)


def _sigmoid(x):
    return 0.5 * jnp.tanh(0.5 * x) + 0.5


def _dot(a, b):
    return jnp.dot(a, b, preferred_element_type=F32)


def _dot_nt(a, b):
    return lax.dot_general(a, b, (((1,), (1,)), ((), ())), preferred_element_type=F32)


def _skip_ref(kernel_fn, idx):
    def wrapped(*refs):
        return kernel_fn(*refs[:idx], *refs[idx + 1:])
    return wrapped


def _layer_slab_call(make_kernel, in_specs, args, slab_out, layer, slab_block, slab_index, out_specs, **kw):
    n_layers = kw["out_shape"][-1].shape[0]
    if slab_out is None:
        spec = pl.BlockSpec((n_layers,) + slab_block, lambda *g: (0,) + slab_index(*g))
        return pl.pallas_call(make_kernel(layer), in_specs=in_specs, out_specs=list(out_specs) + [spec], **kw)(*args)
    spec = pl.BlockSpec((1,) + slab_block, lambda *g: (layer,) + slab_index(*g))
    idx = len(args)
    return pl.pallas_call(
        _skip_ref(make_kernel(0), idx),
        in_specs=list(in_specs) + [pl.BlockSpec(memory_space=pl.ANY)],
        out_specs=list(out_specs) + [spec],
        input_output_aliases={idx: len(kw["out_shape"]) - 1},
        **kw)(*args, slab_out)


def _zero_other_slabs(so_ref, own):
    for l in range(so_ref.shape[0]):
        if l != own:
            so_ref[l] = jnp.zeros(so_ref.shape[1:], so_ref.dtype)


def _rms_rows(x, g):
    ms = jnp.mean(x * x, axis=-1, keepdims=True)
    return x * lax.rsqrt(ms + EPS) * g


def _prenorm_kernel(x_ref, g_ref, o_ref):
    o_ref[...] = _rms_rows(x_ref[...], g_ref[...]).astype(BF16)


def prenorm(x, g, layer, tm):
    T, K = x.shape
    return pl.pallas_call(
        _prenorm_kernel,
        grid=(T // tm,),
        in_specs=[pl.BlockSpec((tm, K), lambda i: (i, 0)),
                  pl.BlockSpec((None, 1, K), lambda i: (layer, 0, 0))],
        out_specs=pl.BlockSpec((tm, K), lambda i: (i, 0)),
        out_shape=jax.ShapeDtypeStruct((T, K), BF16),
        compiler_params=_params("arbitrary"),
        name="prenorm",
    )(x, g)


def _matmul_kernel(a_ref, w_ref, o_ref):
    o_ref[...] = _dot(a_ref[...], w_ref[...].astype(BF16))


def matmul_cols(a, w, layer, n_cols, tm, tn):
    T, K = a.shape
    return pl.pallas_call(
        _matmul_kernel,
        grid=(T // tm, n_cols // tn),
        in_specs=[
            pl.BlockSpec((tm, K), lambda i, j: (i, 0)),
            pl.BlockSpec((None, K, tn), lambda i, j: (layer, 0, j)),
        ],
        out_specs=pl.BlockSpec((tm, tn), lambda i, j: (i, j)),
        out_shape=jax.ShapeDtypeStruct((T, n_cols), F32),
        compiler_params=_params("parallel", "arbitrary"),
        name="matmul_cols",
    )(a, w)


def _matmul_res_kernel(a_ref, w_ref, r_ref, o_ref):
    o_ref[...] = r_ref[...] + _dot(a_ref[...], w_ref[...].astype(BF16))


def matmul_res(a, w, layer, res, tm, tn):
    T, K = a.shape
    N = w.shape[2]
    return pl.pallas_call(
        _matmul_res_kernel,
        grid=(T // tm, N // tn),
        in_specs=[
            pl.BlockSpec((tm, K), lambda i, j: (i, 0)),
            pl.BlockSpec((None, K, tn), lambda i, j: (layer, 0, j)),
            pl.BlockSpec((tm, tn), lambda i, j: (i, j)),
        ],
        out_specs=pl.BlockSpec((tm, tn), lambda i, j: (i, j)),
        out_shape=jax.ShapeDtypeStruct((T, N), F32),
        compiler_params=_params("parallel", "arbitrary"),
        name="matmul_res",
    )(a, w, res)


def _res_norm_kernel(a_ref, w_ref, r_ref, g_ref, h_ref, hn_ref, w_s):
    @pl.when(pl.program_id(0) == 0)
    def _():
        w_s[...] = w_ref[...].astype(BF16)

    h = r_ref[...] + _dot(a_ref[...], w_s[...])
    h_ref[...] = h
    hn_ref[...] = _rms_rows(h, g_ref[...]).astype(BF16)


def matmul_res_norm(a, w, layer, res, g, tm):
    T, K = a.shape
    N = w.shape[2]
    return pl.pallas_call(
        _res_norm_kernel,
        grid=(T // tm,),
        in_specs=[
            pl.BlockSpec((tm, K), lambda i: (i, 0)),
            pl.BlockSpec((None, K, N), lambda i: (layer, 0, 0), pipeline_mode=pl.Buffered(1)),
            pl.BlockSpec((tm, N), lambda i: (i, 0)),
            pl.BlockSpec((None, 1, N), lambda i: (layer, 0, 0)),
        ],
        out_specs=[pl.BlockSpec((tm, N), lambda i: (i, 0)), pl.BlockSpec((tm, N), lambda i: (i, 0))],
        out_shape=[jax.ShapeDtypeStruct((T, N), F32), jax.ShapeDtypeStruct((T, N), BF16)],
        scratch_shapes=[pltpu.VMEM((K, N), BF16)],
        compiler_params=_params("arbitrary"),
        name="matmul_res_norm",
    )(a, w, res, g)


def _merge_kernel(xn_ref, ya_ref, yb_ref, yc_ref, ym_ref, wg0_ref, wg1_ref, wg2_ref, wg3_ref, wb_ref, o_ref,
                  wg_s, wb_s):
    @pl.when(pl.program_id(1) == 0)
    def _():
        for n, wg_ref in enumerate((wg0_ref, wg1_ref, wg2_ref, wg3_ref)):
            wg_s[n] = wg_ref[...].astype(BF16)
            wb_s[n] = wb_ref[n].astype(BF16)

    xn = xn_ref[...]
    acc = None
    for n, y_ref in enumerate((ya_ref, yb_ref, yc_ref, ym_ref)):
        t = _sigmoid(_dot(xn, wg_s[n])) * _dot(y_ref[...], wb_s[n])
        acc = t if acc is None else acc + t
    o_ref[...] = acc.astype(BF16)


def merge_branches(xn, ys, w_in, w_branch, layer, tm, tn):
    T = xn.shape[0]
    once = pl.Buffered(1)
    y_spec = pl.BlockSpec((tm, MIX_W), lambda j, i: (i, 0))

    def gate_spec(n):
        off = (COL_GATE + n * D_MODEL) // tn
        return pl.BlockSpec((None, D_MODEL, tn), lambda j, i: (layer, 0, off + j), pipeline_mode=once)

    return pl.pallas_call(
        _merge_kernel,
        grid=(D_MODEL // tn, T // tm),
        in_specs=[pl.BlockSpec((tm, D_MODEL), lambda j, i: (i, 0))] + [y_spec] * 4
        + [gate_spec(n) for n in range(N_BRANCH)]
        + [pl.BlockSpec((None, N_BRANCH, MIX_W, tn), lambda j, i: (layer, 0, 0, j), pipeline_mode=once)],
        out_specs=pl.BlockSpec((tm, tn), lambda j, i: (i, j)),
        out_shape=jax.ShapeDtypeStruct((T, D_MODEL), BF16),
        scratch_shapes=[pltpu.VMEM((N_BRANCH, D_MODEL, tn), BF16), pltpu.VMEM((N_BRANCH, MIX_W, tn), BF16)],
        compiler_params=_params("arbitrary", "arbitrary"),
        name="merge_branches",
    )(xn, *ys, w_in, w_in, w_in, w_in, w_branch)


CONV_HIST = 8


def _gelu(x):
    return 0.5 * x * (1.0 + lax.erf(x * (2.0 ** -0.5)))


def _up_conv_prompt_kernel(xn_ref, wa_ref, wv_ref, cw_ref, cb_ref, g_ref, tail_ref, carry_ref, *, tm, rc):
    i, j = pl.program_id(0), pl.program_id(1)
    wa = wa_ref[...].astype(BF16)
    wv = wv_ref[...].astype(BF16)
    prev = jnp.where(i == 0, 0.0, carry_ref[j])
    row = lax.broadcasted_iota(jnp.int32, (rc, wa.shape[1]), 0)
    for c in range(tm // rc):
        sl = pl.ds(c * rc, rc)
        xn = xn_ref[sl, :]
        a = _dot(xn, wa)
        v = _dot(xn, wv)
        a1 = jnp.where(row == 0, prev[CONV_HIST - 1:CONV_HIST], pltpu.roll(a, 1, 0))
        a2 = jnp.where(row == 0, prev[CONV_HIST - 2:CONV_HIST - 1],
                       jnp.where(row == 1, prev[CONV_HIST - 1:CONV_HIST], pltpu.roll(a, 2, 0)))
        cc = cb_ref[...] + cw_ref[0:1, :] * a2 + cw_ref[1:2, :] * a1 + cw_ref[2:3, :] * a
        g_ref[sl, :] = (_gelu(cc) * v).astype(BF16)
        prev = a[rc - CONV_HIST:, :]
    carry_ref[j] = prev
    tail_ref[...] = prev


def up_conv_prompt(xn, w_up, conv_w, conv_b, layer, tm, tn):
    T, K = xn.shape
    nc = D_FF // tn
    return pl.pallas_call(
        functools.partial(_up_conv_prompt_kernel, tm=tm, rc=min(tm, 1024)),
        grid=(T // tm, nc),
        in_specs=[
            pl.BlockSpec((tm, K), lambda i, j: (i, 0)),
            pl.BlockSpec((None, K, tn), lambda i, j: (layer, 0, j)),
            pl.BlockSpec((None, K, tn), lambda i, j: (layer, 0, nc + j)),
            pl.BlockSpec((None, 3, tn), lambda i, j: (layer, 0, j)),
            pl.BlockSpec((None, 1, tn), lambda i, j: (layer, 0, j)),
        ],
        out_specs=[
            pl.BlockSpec((tm, tn), lambda i, j: (i, j)),
            pl.BlockSpec((None, CONV_HIST, tn), lambda i, j: (i, 0, j)),
        ],
        out_shape=[
            jax.ShapeDtypeStruct((T, D_FF), BF16),
            jax.ShapeDtypeStruct((T // tm, CONV_HIST, D_FF), F32),
        ],
        scratch_shapes=[pltpu.VMEM((nc, CONV_HIST, tn), F32)],
        compiler_params=_params("arbitrary", "arbitrary"),
        name="up_conv_prompt",
    )(xn, w_up, w_up, conv_w, conv_b)


def _up_conv_sample_kernel(xn_ref, wa_ref, wv_ref, st_ref, cw_ref, cb_ref, g_ref, so_ref, *, steps, B, own):
    _zero_other_slabs(so_ref, own)
    xn = xn_ref[...]
    a = _dot(xn, wa_ref[...].astype(BF16))
    v = _dot(xn, wv_ref[...].astype(BF16))
    hist = [st_ref[:, 0, :], st_ref[:, 1, :]] + [a[t * B:(t + 1) * B] for t in range(steps)]
    for t in range(steps):
        c = cb_ref[...] + cw_ref[0:1, :] * hist[t] + cw_ref[1:2, :] * hist[t + 1] + cw_ref[2:3, :] * hist[t + 2]
        g_ref[t * B:(t + 1) * B, :] = (_gelu(c) * v[t * B:(t + 1) * B]).astype(BF16)
    so_ref[own, :, 0, :] = hist[steps]
    so_ref[own, :, 1, :] = hist[steps + 1]


def up_conv_sample(xn, w_up, state_conv, conv_w, conv_b, layer, steps, tn, state_out):
    T, K = xn.shape
    B = T // steps
    nc = D_FF // tn
    in_specs = [
        pl.BlockSpec((T, K), lambda j: (0, 0)),
        pl.BlockSpec((None, K, tn), lambda j: (layer, 0, j)),
        pl.BlockSpec((None, K, tn), lambda j: (layer, 0, nc + j)),
        pl.BlockSpec((None, B, 2, tn), lambda j: (layer, 0, 0, j)),
        pl.BlockSpec((None, 3, tn), lambda j: (layer, 0, j)),
        pl.BlockSpec((None, 1, tn), lambda j: (layer, 0, j)),
    ]
    return _layer_slab_call(
        lambda own: functools.partial(_up_conv_sample_kernel, steps=steps, B=B, own=own),
        in_specs, [xn, w_up, w_up, state_conv, conv_w, conv_b], state_out, layer,
        (B, 2, tn), lambda j: (0, 0, j),
        [pl.BlockSpec((T, tn), lambda j: (0, j))],
        grid=(nc,),
        out_shape=[
            jax.ShapeDtypeStruct((T, D_FF), BF16),
            jax.ShapeDtypeStruct(state_conv.shape, F32),
        ],
        compiler_params=_params("arbitrary"),
        name="up_conv_sample",
    )


def _hgrn_gates(q_in, z, lb):
    q = q_in * _sigmoid(q_in)
    log_sig = jnp.minimum(z, 0.0) - jnp.log(1.0 + jnp.exp(-jnp.abs(z)))
    a1 = jnp.log(lb)
    a2 = jnp.log1p(-lb) + log_sig
    log_f = jnp.maximum(a1, a2) + jnp.log(1.0 + jnp.exp(-jnp.abs(a1 - a2)))
    k = (1.0 - lb) * _sigmoid(-z)
    return q, log_f, k


def _hgrn_out(o, gate, gn):
    ms = jnp.mean(o * o, axis=-1, keepdims=True)
    return o * lax.rsqrt(ms + EPS) * gn * (gate * _sigmoid(gate))


def _cumsum_rows(x, tril):
    hi = x.astype(BF16)
    r1 = x - hi.astype(F32)
    mid = r1.astype(BF16)
    lo = (r1 - mid.astype(F32)).astype(BF16)
    return _dot(tril, hi) + _dot(tril, mid) + _dot(tril, lo)


def _block_row(x, blk, r):
    C = x.shape[0]
    x3 = x.reshape(C // blk, blk, LANES)
    return jnp.broadcast_to(x3[:, r:r + 1, :], (C // blk, blk, LANES)).reshape(C, LANES)


def _hgrn_pair_codes():
    t = np.arange(HGRN_CHUNK)[:, None]
    s = np.arange(HGRN_CHUNK)[None, :]
    level = np.floor(np.log2(np.maximum(t ^ s, 1))).astype(np.int32)
    return jnp.asarray(np.where(s > t, -1, np.where(s == t, 0, 1 + level)), jnp.int32)


def _boundary_row(b, m):
    if 2 * m >= 8:
        return _block_row(b, 2 * m, m - 1)
    r8 = lax.broadcasted_iota(jnp.int32, b.shape, 0) & 7
    if m == 2:
        return jnp.where(r8 < 4, _block_row(b, 8, 1), _block_row(b, 8, 5))
    return jnp.where(r8 < 2, _block_row(b, 8, 0),
                     jnp.where(r8 < 4, _block_row(b, 8, 2),
                               jnp.where(r8 < 6, _block_row(b, 8, 4), _block_row(b, 8, 6))))


def _hgrn_attention(q, k, b, code):
    C = HGRN_CHUNK
    rowl = lax.broadcasted_iota(jnp.int32, (C, LANES), 0)
    att = jnp.where(code == 0, jnp.sum(q * k, axis=-1, keepdims=True), 0.0)
    m, level = 1, 1
    while m < C:
        d = b - _boundary_row(b, m)
        isq = (rowl & m) != 0
        x = (jnp.where(isq, q, k) * jnp.exp(jnp.where(isq, d, -d))).astype(BF16)
        att = jnp.where(code == level, _dot_nt(x, x), att)
        m *= 2
        level += 1
    return att


def _hgrn_apply(q, k, v, b, att, S):
    C = HGRN_CHUNK
    vb = v.astype(BF16)
    o = _dot((q * jnp.exp(b)).astype(BF16), S.astype(BF16)) + _dot(att.astype(BF16), vb)
    bl = b[C - 1:C, :]
    kk = k * jnp.exp(bl - b)
    ecol = jnp.transpose(jnp.broadcast_to(jnp.exp(bl), (LANES, LANES)))
    return o, ecol * S + _dot(jnp.transpose(kk).astype(BF16), vb)


def _hgrn_prompt_kernel(q_ref, f_ref, i_ref, g_ref, lb_ref, gn_ref, code_ref, y_ref, so_ref, s_ref, *, rows):
    @pl.when(pl.program_id(1) == 0)
    def _():
        s_ref[...] = jnp.zeros_like(s_ref)

    C = HGRN_CHUNK
    code = code_ref[...]
    tril = jnp.where(code >= 0, 1.0, 0.0).astype(BF16)
    q, log_f, k = _hgrn_gates(q_ref[...], f_ref[...], lb_ref[...])
    chunks = [slice(c * C, (c + 1) * C) for c in range(rows // C)]
    bs = [_cumsum_rows(log_f[sl], tril) for sl in chunks]
    atts = [_hgrn_attention(q[sl], k[sl], b, code) for sl, b in zip(chunks, bs)]
    S = s_ref[...]
    outs = []
    for sl, b, att in zip(chunks, bs, atts):
        o, S = _hgrn_apply(q[sl], k[sl], i_ref[sl, :], b, att, S)
        outs.append(o)
    s_ref[...] = S
    y_ref[...] = _hgrn_out(jnp.concatenate(outs, axis=0), g_ref[...], gn_ref[...]).astype(BF16)

    @pl.when(pl.program_id(1) == pl.num_programs(1) - 1)
    def _():
        so_ref[...] = s_ref[...]


def hgrn_prompt(proj, lb, gn, rows):
    T = proj.shape[0]

    def col(off):
        base = off // LANES
        return pl.BlockSpec((rows, LANES), lambda h, c: (c, base + h))

    vec = pl.BlockSpec((1, LANES), lambda h, c: (0, h))
    return pl.pallas_call(
        functools.partial(_hgrn_prompt_kernel, rows=rows),
        grid=(A_HEADS, T // rows),
        in_specs=[col(COL_HQ), col(COL_HF), col(COL_HI), col(COL_HG), vec, vec,
                  pl.BlockSpec((HGRN_CHUNK, HGRN_CHUNK), lambda h, c: (0, 0))],
        out_specs=[
            pl.BlockSpec((rows, LANES), lambda h, c: (c, h)),
            pl.BlockSpec((None, A_DK, LANES), lambda h, c: (h, 0, 0)),
        ],
        out_shape=[
            jax.ShapeDtypeStruct((T, MIX_W), BF16),
            jax.ShapeDtypeStruct((A_HEADS, A_DK, LANES), F32),
        ],
        scratch_shapes=[pltpu.VMEM((A_DK, LANES), F32)],
        compiler_params=_params("parallel", "arbitrary"),
        name="hgrn_prompt",
    )(proj, proj, proj, proj, lb, gn, _hgrn_pair_codes())


def _hgrn_sample_kernel(q_ref, f_ref, i_ref, g_ref, lb_ref, gn_ref, s_ref, y_ref, so_ref, *, steps, bb, own):
    _zero_other_slabs(so_ref, own)
    lb = lb_ref[...]
    gn = gn_ref[...]
    qs, ks, vs, bs = [], [], [], []
    b = None
    for t in range(steps):
        q, log_f, k = _hgrn_gates(q_ref[t], f_ref[t], lb)
        b = log_f if b is None else b + log_f
        qs.append(q)
        ks.append(k)
        vs.append(i_ref[t])
        bs.append(b)
    intra = []
    for t in range(steps):
        acc = None
        for s in range(t + 1):
            w = jnp.sum(qs[t] * ks[s] * jnp.exp(bs[t] - bs[s]), axis=-1, keepdims=True)
            acc = w * vs[s] if acc is None else acc + w * vs[s]
        intra.append(acc)
    R = steps * bb
    q_stack = jnp.concatenate([qs[t] * jnp.exp(bs[t]) for t in range(steps)], axis=0).astype(BF16)
    k_stack = jnp.concatenate([ks[t] * jnp.exp(bs[-1] - bs[t]) for t in range(steps)], axis=0)
    v_stack = jnp.concatenate(vs, axis=0).astype(BF16)
    k_t = jnp.transpose(k_stack)
    f_pad = jnp.concatenate([jnp.exp(bs[-1])] + [jnp.zeros((R - bb, LANES), F32)], axis=0)
    f_t = jnp.transpose(f_pad)
    rowi = lax.broadcasted_iota(jnp.int32, (R, LANES), 0) % bb
    lanei = lax.broadcasted_iota(jnp.int32, (LANES, R), 1)

    def body(bi, o_acc):
        s_b = s_ref[bi]
        o_acc = jnp.where(rowi == bi, _dot(q_stack, s_b.astype(BF16)), o_acc)
        f_col = jnp.sum(jnp.where(lanei == bi, f_t, 0.0), axis=-1, keepdims=True)
        k_b = jnp.where(lanei % bb == bi, k_t, 0.0).astype(BF16)
        so_ref[own, bi] = f_col * s_b + _dot(k_b, v_stack)
        return o_acc

    o_inter = lax.fori_loop(0, bb, body, jnp.zeros((R, LANES), F32), unroll=16)
    for t in range(steps):
        o = o_inter[t * bb:(t + 1) * bb] + intra[t]
        y_ref[t] = _hgrn_out(o, g_ref[t], gn).astype(BF16)


def hgrn_sample(proj3, lb, gn, state, layer, bb, state_out):
    steps, B, _ = proj3.shape
    assert steps * bb == LANES

    def col(off):
        base = off // LANES
        return pl.BlockSpec((steps, bb, LANES), lambda g, h: (0, g, base + h))

    vec = pl.BlockSpec((1, LANES), lambda g, h: (0, h))
    slab = pl.BlockSpec((None, bb, None, A_DK, LANES), lambda g, h: (layer, g, h, 0, 0))
    return _layer_slab_call(
        lambda own: functools.partial(_hgrn_sample_kernel, steps=steps, bb=bb, own=own),
        [col(COL_HQ), col(COL_HF), col(COL_HI), col(COL_HG), vec, vec, slab],
        [proj3, proj3, proj3, proj3, lb, gn, state], state_out, layer,
        (bb, None, A_DK, LANES), lambda g, h: (g, h, 0, 0),
        [pl.BlockSpec((steps, bb, LANES), lambda g, h: (0, g, h))],
        grid=(B // bb, A_HEADS),
        out_shape=[
            jax.ShapeDtypeStruct((steps, B, MIX_W), BF16),
            jax.ShapeDtypeStruct(state.shape, F32),
        ],
        compiler_params=_params("parallel", "arbitrary"),
        name="hgrn_sample",
    )


POOL_HIST = 32


def _pool_project(d, w_ref, sc_ref, g):
    sl = slice(g * POOL_GC, (g + 1) * POOL_GC)
    return _dot(d.astype(BF16), w_ref[g].astype(BF16)) * sc_ref[:, sl]


def _pool_prompt_kernel(u_ref, prev_ref, w_ref, sc_ref, y_ref, ext_ref, sum_ref, *, tm):
    i = pl.program_id(0)
    H = POOL_HIST
    ext_ref[0:H, :] = jnp.where(i == 0, 0.0, prev_ref[...])
    ext_ref[H:, :] = u_ref[...]
    sum_ref[0:H // 2, :] = jnp.zeros((H // 2, MIX_W), F32)
    pos = i * tm + lax.broadcasted_iota(jnp.int32, (tm, 1), 0)
    src, w = ext_ref, 1
    for g, win in enumerate(POOL_WINDOWS):
        lanes = slice(g * POOL_GC, MIX_W)
        while w < win:
            n = H // 2 + tm
            sum_ref[pl.ds(H // 2, n), lanes] = src[pl.ds(H // 2, n), lanes] + src[pl.ds(H // 2 - w, n), lanes]
            src, w = sum_ref, 2 * w
        sl = slice(g * POOL_GC, (g + 1) * POOL_GC)
        cnt = jnp.minimum(pos + 1, win).astype(F32)
        d = src[pl.ds(H, tm), sl] / cnt - u_ref[:, sl]
        y_ref[:, sl] = _pool_project(d, w_ref, sc_ref, g).astype(BF16)


def pool_prompt(proj, pool_w, pool_scale, layer, tm):
    T = proj.shape[0]
    cb = COL_POOL // MIX_W
    return pl.pallas_call(
        functools.partial(_pool_prompt_kernel, tm=tm),
        grid=(T // tm,),
        in_specs=[
            pl.BlockSpec((tm, MIX_W), lambda i: (i, cb)),
            pl.BlockSpec((POOL_HIST, MIX_W), lambda i: (jnp.maximum(i * (tm // POOL_HIST) - 1, 0), cb)),
            pl.BlockSpec((None, len(POOL_WINDOWS), POOL_GC, POOL_GC), lambda i: (layer, 0, 0, 0)),
            pl.BlockSpec((None, 1, MIX_W), lambda i: (layer, 0, 0)),
        ],
        out_specs=pl.BlockSpec((tm, MIX_W), lambda i: (i, 0)),
        out_shape=jax.ShapeDtypeStruct((T, MIX_W), BF16),
        scratch_shapes=[pltpu.VMEM((POOL_HIST + tm, MIX_W), F32), pltpu.VMEM((POOL_HIST + tm, MIX_W), F32)],
        compiler_params=_params("arbitrary"),
        name="pool_prompt",
    )(proj, proj, pool_w, pool_scale)


def _pool_sample_kernel(u_ref, c_ref, w_ref, sc_ref, y_ref, *, steps):
    for t in range(steps):
        for g, win in enumerate(POOL_WINDOWS):
            sl = slice(g * POOL_GC, (g + 1) * POOL_GC)
            acc = u_ref[t, :, sl]
            for j in range(1, win):
                if j <= t:
                    acc = acc + u_ref[t - j, :, sl]
                else:
                    acc = acc + c_ref[POOL_BUF + t - j, :, sl]
            d = acc / float(win) - u_ref[t, :, sl]
            y_ref[t, :, sl] = _pool_project(d, w_ref, sc_ref, g).astype(BF16)


def pool_sample(proj3, cache_pool, pool_w, pool_scale, layer, bb):
    steps, B, _ = proj3.shape
    cb = COL_POOL // MIX_W
    return pl.pallas_call(
        functools.partial(_pool_sample_kernel, steps=steps),
        grid=(B // bb,),
        in_specs=[
            pl.BlockSpec((steps, bb, MIX_W), lambda g: (0, g, cb)),
            pl.BlockSpec((None, POOL_BUF, bb, MIX_W), lambda g: (layer, 0, g, 0)),
            pl.BlockSpec((None, len(POOL_WINDOWS), POOL_GC, POOL_GC), lambda g: (layer, 0, 0, 0)),
            pl.BlockSpec((None, 1, MIX_W), lambda g: (layer, 0, 0)),
        ],
        out_specs=pl.BlockSpec((steps, bb, MIX_W), lambda g: (0, g, 0)),
        out_shape=jax.ShapeDtypeStruct((steps, B, MIX_W), BF16),
        compiler_params=_params("arbitrary"),
        name="pool_sample",
    )(proj3, cache_pool, pool_w, pool_scale)


def _rope_tables(positions):
    half = ROT_DIM // 2
    inv = np.power(ROPE_THETA, -np.arange(0, ROT_DIM, 2, dtype=np.float64) / ROT_DIM)
    ang = np.asarray(positions, np.float64)[:, None] * inv[None, :]
    cos, sin = np.cos(ang), np.sin(ang)
    n = len(positions)
    ct = np.ones((n, LANES))
    sn = np.zeros((n, LANES))
    sp = np.zeros((n, LANES))
    for base in (0, SWA_HEAD_DIM):
        ct[:, base:base + half] = cos
        ct[:, base + half:base + ROT_DIM] = cos
        sn[:, base:base + half] = -sin
        sp[:, base + half:base + ROT_DIM] = sin
    return tuple(jnp.asarray(t, F32) for t in (ct, sn, sp))


def _head_norm_rope(x, g, ct, sn, sp):
    lane = lax.broadcasted_iota(jnp.int32, x.shape, 1)
    lo = lane < SWA_HEAD_DIM
    x2 = x * x
    ms_lo = jnp.sum(jnp.where(lo, x2, 0.0), axis=-1, keepdims=True) / SWA_HEAD_DIM
    ms_hi = jnp.sum(jnp.where(lo, 0.0, x2), axis=-1, keepdims=True) / SWA_HEAD_DIM
    xn = x * jnp.where(lo, lax.rsqrt(ms_lo + EPS), lax.rsqrt(ms_hi + EPS)) * g
    half = ROT_DIM // 2
    return xn * ct + pltpu.roll(xn, LANES - half, 1) * sn + pltpu.roll(xn, half, 1) * sp


def _kprep_kernel(k_ref, g_ref, ct_ref, sn_ref, sp_ref, o_ref):
    ct, sn, sp = ct_ref[...], sn_ref[...], sp_ref[...]
    for j in range(2):
        sl = slice(j * LANES, (j + 1) * LANES)
        o_ref[:, sl] = _head_norm_rope(k_ref[:, sl], g_ref[...], ct, sn, sp)


def swa_kprep(proj, g2, tables, tm):
    T = proj.shape[0]
    kw = SWA_KV_HEADS * SWA_HEAD_DIM
    tab = pl.BlockSpec((tm, LANES), lambda i: (i, 0))
    return pl.pallas_call(
        _kprep_kernel,
        grid=(T // tm,),
        in_specs=[pl.BlockSpec((tm, kw), lambda i: (i, COL_SK // kw)),
                  pl.BlockSpec((1, LANES), lambda i: (0, 0)), tab, tab, tab],
        out_specs=pl.BlockSpec((tm, kw), lambda i: (i, 0)),
        out_shape=jax.ShapeDtypeStruct((T, kw), F32),
        compiler_params=_params("arbitrary"),
        name="swa_kprep",
    )(proj, g2, *tables)


def _dup_head(x, parity):
    lane = lax.broadcasted_iota(jnp.int32, x.shape, 1)
    return jnp.where(lane // SWA_HEAD_DIM == parity, x, pltpu.roll(x, SWA_HEAD_DIM, 1))


def _stack_heads(q):
    lane = lax.broadcasted_iota(jnp.int32, q.shape, 1)
    lo = lane < SWA_HEAD_DIM
    return jnp.concatenate([jnp.where(lo, q, 0.0), jnp.where(lo, 0.0, q)], axis=0)


def _unstack_heads(o2):
    R = o2.shape[0] // 2
    lane = lax.broadcasted_iota(jnp.int32, (R, LANES), 1)
    return jnp.where(lane < SWA_HEAD_DIM, o2[:R], o2[R:])


def _swa_prompt_kernel(sink_ref, q_ref, kc_ref, kp_ref, vc_ref, vp_ref, g_ref, ct_ref, sn_ref, sp_ref, y_ref,
                       *, layer, nb):
    first = pl.program_id(0) == 0
    W = WINDOW
    G = SWA_Q_HEADS // SWA_KV_HEADS
    r4 = lax.broadcasted_iota(jnp.int32, (G * W, 1), 0)
    hh = r4 // W
    ci = lax.broadcasted_iota(jnp.int32, (1, W), 1)
    cur = ci <= r4 % W
    scale = SWA_HEAD_DIM ** -0.5

    def head_blocks(cur_ref, prev_ref, kvh):
        ksl = slice((kvh // 2) * LANES, (kvh // 2 + 1) * LANES)
        x = _dup_head(jnp.concatenate([prev_ref[:, ksl], cur_ref[:, ksl]], axis=0), kvh % 2).astype(BF16)
        return [x[j * W:(j + 1) * W] for j in range(nb + 1)]

    scores, sinks = [], []
    for kvh in range(SWA_KV_HEADS):
        kb = head_blocks(kc_ref, kp_ref, kvh)
        sink = sink_ref[layer, G * kvh + G - 1]
        for i in range(G - 2, -1, -1):
            sink = jnp.where(hh == i, sink_ref[layer, G * kvh + i], sink)
        for blk in range(nb):
            rows = slice(blk * W, (blk + 1) * W)
            qs = []
            for jj in range(G // 2):
                qsl = slice((2 * kvh + jj) * LANES, (2 * kvh + jj + 1) * LANES)
                qn = _head_norm_rope(q_ref[rows, qsl], g_ref[...], ct_ref[rows, :], sn_ref[rows, :], sp_ref[rows, :])
                qs.append(_stack_heads(qn * scale))
            q = jnp.concatenate(qs, axis=0).astype(BF16)
            s_prev = _dot_nt(q, kb[blk])
            if blk == 0:
                s_prev = jnp.where(first, NEG_BIG, s_prev)
            scores.append(jnp.where(cur, _dot_nt(q, kb[blk + 1]), s_prev))
            sinks.append(sink)
    probs = []
    for s, sink in zip(scores, sinks):
        m = jnp.maximum(jnp.max(s, axis=-1, keepdims=True), sink)
        e = jnp.exp(s - m)
        probs.append(e * (1.0 / (jnp.sum(e, axis=-1, keepdims=True) + jnp.exp(sink - m))))
    for kvh in range(SWA_KV_HEADS):
        vb = head_blocks(vc_ref, vp_ref, kvh)
        for blk in range(nb):
            p = probs[kvh * nb + blk]
            o = (_dot(jnp.where(cur, p, 0.0).astype(BF16), vb[blk + 1])
                 + _dot(jnp.where(cur, 0.0, p).astype(BF16), vb[blk]))
            for jj in range(G // 2):
                qsl = slice((2 * kvh + jj) * LANES, (2 * kvh + jj + 1) * LANES)
                y_ref[blk * W:(blk + 1) * W, qsl] = _unstack_heads(o[2 * jj * W:(2 * jj + 2) * W]).astype(BF16)


def swa_prompt(proj, khat, sinks, g2, tables, layer, nb):
    T = proj.shape[0]
    W = WINDOW
    tq = nb * W
    kw = SWA_KV_HEADS * SWA_HEAD_DIM
    tab = pl.BlockSpec((tq, LANES), lambda i: (i, 0))
    prev = lambda i: jnp.maximum(i * nb - 1, 0)
    return pl.pallas_call(
        functools.partial(_swa_prompt_kernel, layer=layer, nb=nb),
        grid=(T // tq,),
        in_specs=[
            pl.BlockSpec(memory_space=pltpu.SMEM),
            pl.BlockSpec((tq, MIX_W), lambda i: (i, COL_SQ // MIX_W)),
            pl.BlockSpec((tq, kw), lambda i: (i, 0)),
            pl.BlockSpec((W, kw), lambda i: (prev(i), 0)),
            pl.BlockSpec((tq, kw), lambda i: (i, COL_SV // kw)),
            pl.BlockSpec((W, kw), lambda i: (prev(i), COL_SV // kw)),
            pl.BlockSpec((1, LANES), lambda i: (0, 0)), tab, tab, tab,
        ],
        out_specs=pl.BlockSpec((tq, MIX_W), lambda i: (i, 0)),
        out_shape=jax.ShapeDtypeStruct((T, MIX_W), BF16),
        compiler_params=_params("arbitrary"),
        name="swa_prompt",
    )(sinks, proj, khat, khat, proj, proj, g2, *tables)


def _swa_sample_kernel(sink_ref, q_ref, kn_ref, vn_ref, kc_ref, vc_ref, g_ref, ct_ref, sn_ref, sp_ref, y_ref,
                       *, layer, steps, bb):
    kvh = pl.program_id(1)
    parity = kvh % 2
    W = WINDOW
    R = steps * bb
    G = SWA_Q_HEADS // SWA_KV_HEADS
    ct, sn, sp = ct_ref[...], sn_ref[...], sp_ref[...]
    scale = SWA_HEAD_DIM ** -0.5
    r4 = lax.broadcasted_iota(jnp.int32, (G * R, 1), 0)
    hh = r4 // R
    tq = (r4 % R) // bb
    bq = r4 % bb
    c_new = lax.broadcasted_iota(jnp.int32, (1, R), 1)
    valid_new = (c_new % bb == bq) & (c_new // bb <= tq)
    c_old = lax.broadcasted_iota(jnp.int32, (1, W), 1)
    valid_old = c_old > tq
    kn = _dup_head(jnp.concatenate([kn_ref[t] for t in range(steps)], axis=0), parity).astype(BF16)
    vn = _dup_head(jnp.concatenate([vn_ref[t] for t in range(steps)], axis=0), parity).astype(BF16)
    qs = []
    for jj in range(G // 2):
        qsl = slice(jj * LANES, (jj + 1) * LANES)
        q = jnp.concatenate([q_ref[t, :, qsl] for t in range(steps)], axis=0)
        qs.append(_stack_heads(_head_norm_rope(q, g_ref[...], ct, sn, sp) * scale))
    q4 = jnp.concatenate(qs, axis=0)
    s_new = jnp.where(valid_new, _dot_nt(q4.astype(BF16), kn), NEG_BIG)
    s_old = None
    for b in range(0, bb, 2):
        lhs = jnp.concatenate([jnp.where(bq == b + i, q4, 0.0).astype(BF16) for i in range(2)], axis=1)
        k_t = [kc_ref[b + i].astype(BF16) for i in range(2)]
        d = _dot(lhs, jnp.concatenate([k_t[0], k_t[0], k_t[1], k_t[1]], axis=0))
        s_old = d if s_old is None else s_old + d
    s_old = jnp.where(valid_old, s_old, NEG_BIG)
    sink = sink_ref[layer, G * kvh + G - 1]
    for i in range(G - 2, -1, -1):
        sink = jnp.where(hh == i, sink_ref[layer, G * kvh + i], sink)
    m = jnp.maximum(jnp.maximum(jnp.max(s_new, axis=-1, keepdims=True),
                                jnp.max(s_old, axis=-1, keepdims=True)), sink)
    e_new = jnp.exp(s_new - m)
    e_old = jnp.exp(s_old - m)
    den = jnp.sum(e_new, axis=-1, keepdims=True) + jnp.sum(e_old, axis=-1, keepdims=True) + jnp.exp(sink - m)
    p_old = e_old / den
    o = _dot((e_new / den).astype(BF16), vn)
    for b in range(0, bb, 2):
        lhs = jnp.concatenate([jnp.where(bq == b + i, p_old, 0.0).astype(BF16) for i in range(2)], axis=1)
        v_t = [vc_ref[b + i].astype(BF16) for i in range(2)]
        rhs = jnp.concatenate([jnp.concatenate([v_t[i], v_t[i]], axis=0) for i in range(2)], axis=1)
        o = o + _dot_nt(lhs, rhs)
    for jj in range(G // 2):
        o_j = _unstack_heads(o[2 * jj * R:(2 * jj + 2) * R])
        for t in range(steps):
            y_ref[t, :, jj * LANES:(jj + 1) * LANES] = o_j[t * bb:(t + 1) * bb]


def swa_sample(proj3, khat3, cache_kt, cache_vt, sinks, g2, tables, layer, bb):
    steps, B, _ = proj3.shape
    R = steps * bb
    qw = MIX_W // SWA_KV_HEADS
    tab = pl.BlockSpec((R, LANES), lambda g, h: (0, 0))
    cache = pl.BlockSpec((None, bb, None, SWA_HEAD_DIM, WINDOW), lambda g, h: (layer, g, h, 0, 0))
    return pl.pallas_call(
        functools.partial(_swa_sample_kernel, layer=layer, steps=steps, bb=bb),
        grid=(B // bb, SWA_KV_HEADS),
        in_specs=[
            pl.BlockSpec(memory_space=pltpu.SMEM),
            pl.BlockSpec((steps, bb, qw), lambda g, h: (0, g, COL_SQ // qw + h)),
            pl.BlockSpec((steps, bb, LANES), lambda g, h: (0, g, h // 2)),
            pl.BlockSpec((steps, bb, LANES), lambda g, h: (0, g, COL_SV // LANES + h // 2)),
            cache, cache,
            pl.BlockSpec((1, LANES), lambda g, h: (0, 0)), tab, tab, tab,
        ],
        out_specs=pl.BlockSpec((steps, bb, qw), lambda g, h: (0, g, h)),
        out_shape=jax.ShapeDtypeStruct((steps, B, MIX_W), F32),
        compiler_params=_params("parallel", "arbitrary"),
        name="swa_sample",
    )(sinks, proj3, khat3, proj3, cache_kt, cache_vt, g2, *tables)


def _mem_kv_kernel(x_ref, g_ref, w_ref, kg_ref, o_ref, xn_ref):
    j = pl.program_id(0)

    @pl.when(j == 0)
    def _():
        x = x_ref[...]
        ms = jnp.mean(x * x, axis=-1, keepdims=True)
        xn_ref[...] = (x * lax.rsqrt(ms + EPS) * g_ref[...]).astype(BF16)

    y = _dot(xn_ref[...], w_ref[...].astype(BF16))

    @pl.when(j < MEM_HEADS)
    def _():
        ms = jnp.mean(y * y, axis=-1, keepdims=True)
        o_ref[...] = y * lax.rsqrt(ms + EPS) * kg_ref[...]

    @pl.when(j >= MEM_HEADS)
    def _():
        o_ref[...] = y


def mem_kv(mem, mem_norm_g, w_mem_kv, mem_knorm_g, layer):
    M, K = mem.shape
    hd = MEM_HEAD_DIM
    return pl.pallas_call(
        _mem_kv_kernel,
        grid=(2 * MEM_HEADS,),
        in_specs=[
            pl.BlockSpec((M, K), lambda j: (0, 0)),
            pl.BlockSpec((None, 1, K), lambda j: (layer, 0, 0)),
            pl.BlockSpec((None, K, hd), lambda j: (layer, 0, j)),
            pl.BlockSpec((None, 1, hd), lambda j: (layer, 0, 0)),
        ],
        out_specs=pl.BlockSpec((M, hd), lambda j: (0, j)),
        out_shape=jax.ShapeDtypeStruct((M, 2 * MIX_W), F32),
        scratch_shapes=[pltpu.VMEM((M, K), BF16)],
        compiler_params=_params("arbitrary"),
        name="mem_kv",
    )(mem, mem_norm_g, w_mem_kv, mem_knorm_g)


def _mem_qnorm(q, g):
    ms = jnp.mean(q * q, axis=-1, keepdims=True)
    return q * lax.rsqrt(ms + EPS) * g * (MEM_HEAD_DIM ** -0.5)


def _softmax_rows(s):
    m = jnp.max(s, axis=-1, keepdims=True)
    e = jnp.exp(s - m)
    return e / jnp.sum(e, axis=-1, keepdims=True)


def _mem_prompt_kernel(q0_ref, q1_ref, q2_ref, q3_ref, kv_ref, g_ref, y_ref):
    hd = MEM_HEAD_DIM
    for h, q_ref in enumerate((q0_ref, q1_ref, q2_ref, q3_ref)):
        q = _mem_qnorm(q_ref[...], g_ref[...]).astype(BF16)
        p = _softmax_rows(_dot_nt(q, kv_ref[:, h * hd:(h + 1) * hd].astype(BF16)))
        v = kv_ref[:, MIX_W + h * hd:MIX_W + (h + 1) * hd].astype(BF16)
        y_ref[:, h * hd:(h + 1) * hd] = _dot(p.astype(BF16), v).astype(BF16)


def mem_attn_prompt(proj, kv, mem_qnorm_g, layer, tq):
    T = proj.shape[0]
    hd = MEM_HEAD_DIM

    def q_spec(h):
        cb = COL_MQ // hd + h
        return pl.BlockSpec((tq, hd), lambda i: (i, cb))

    return pl.pallas_call(
        _mem_prompt_kernel,
        grid=(T // tq,),
        in_specs=[q_spec(h) for h in range(MEM_HEADS)] + [
            pl.BlockSpec((N_MEM, 2 * MIX_W), lambda i: (0, 0)),
            pl.BlockSpec((None, 1, hd), lambda i: (layer, 0, 0)),
        ],
        out_specs=pl.BlockSpec((tq, MIX_W), lambda i: (i, 0)),
        out_shape=jax.ShapeDtypeStruct((T, MIX_W), BF16),
        compiler_params=_params("arbitrary"),
        name="mem_prompt",
    )(proj, proj, proj, proj, kv, mem_qnorm_g)


def _mem_rows_view(c):
    L_, B_, M, H, hd = c.shape
    c = c.reshape(L_, B_, M, H, hd // LANES, LANES)
    return jnp.transpose(c, (0, 1, 2, 4, 3, 5)).reshape(L_, B_, M * H * (hd // LANES), LANES)


def _mem_head(c_ref, b, h):
    nt = MEM_HEAD_DIM // LANES
    parts = [c_ref[b, pl.ds(lt * MEM_HEADS + h, N_MEM, stride=nt * MEM_HEADS), :] for lt in range(nt)]
    return jnp.concatenate(parts, axis=1).astype(BF16)


def _mem_sample_kernel(q0_ref, q1_ref, q2_ref, q3_ref, k_ref, v_ref, g_ref, y_ref, *, steps, bb):
    R = steps * bb
    hd = MEM_HEAD_DIM
    bq = lax.broadcasted_iota(jnp.int32, (R, 1), 0) % bb
    for h, q_ref in enumerate((q0_ref, q1_ref, q2_ref, q3_ref)):
        q = _mem_qnorm(jnp.concatenate([q_ref[t] for t in range(steps)], axis=0), g_ref[...])
        s = None
        for b in range(bb):
            d = _dot_nt(jnp.where(bq == b, q, 0.0).astype(BF16), _mem_head(k_ref, b, h))
            s = d if s is None else s + d
        p = _softmax_rows(s)
        o = None
        for b in range(bb):
            d = _dot(jnp.where(bq == b, p, 0.0).astype(BF16), _mem_head(v_ref, b, h))
            o = d if o is None else o + d
        for t in range(steps):
            y_ref[t, :, h * hd:(h + 1) * hd] = o[t * bb:(t + 1) * bb]


def mem_attn_sample(proj3, cache_k, cache_v, mem_qnorm_g, layer, bb):
    steps, B, _ = proj3.shape
    hd = MEM_HEAD_DIM
    cache = pl.BlockSpec((None, bb) + cache_k.shape[2:], lambda g: (layer, g, 0, 0))

    def q_spec(h):
        cb = COL_MQ // hd + h
        return pl.BlockSpec((steps, bb, hd), lambda g: (0, g, cb))

    return pl.pallas_call(
        functools.partial(_mem_sample_kernel, steps=steps, bb=bb),
        grid=(B // bb,),
        in_specs=[q_spec(h) for h in range(MEM_HEADS)] + [
            cache, cache,
            pl.BlockSpec((None, 1, hd), lambda g: (layer, 0, 0)),
        ],
        out_specs=pl.BlockSpec((steps, bb, MIX_W), lambda g: (0, g, 0)),
        out_shape=jax.ShapeDtypeStruct((steps, B, MIX_W), F32),
        compiler_params=_params("arbitrary"),
        name="mem_sample",
    )(proj3, proj3, proj3, proj3, cache_k, cache_v, mem_qnorm_g)


def _row_tile(T, cap):
    t = cap
    while T % t:
        t //= 2
    return t


TM_STREAM = 2048
TM_DOWN = 1024
TM_LOCAL = 512


def _token_tail(x, xn, ys, layer, w_in, w_branch, w_o, norm2_g):
    T = x.shape[0]
    merged = merge_branches(xn, ys, w_in, w_branch, layer, _row_tile(T, 512), 512)
    return matmul_res_norm(merged, w_o, layer, x, norm2_g, _row_tile(T, 256))


def kernel(x_prompt, x_sample, mem_prompt, state_hgrn, cache_pool, cache_swa_k, cache_swa_v, state_conv, cache_mem_k, cache_mem_v, norm1_g, w_in, hgrn_lb, hgrn_norm_g, pool_w, pool_scale, swa_qnorm_g, swa_knorm_g, swa_sinks, mem_norm_g, w_mem_kv, mem_qnorm_g, mem_knorm_g, w_branch, w_o, norm2_g, w_up, conv_w, conv_b, w_down):
    depth = w_in.shape[0]
    bp, L, _ = x_prompt.shape
    B, steps, _ = x_sample.shape
    assert bp == 1
    kw = SWA_KV_HEADS * SWA_HEAD_DIM
    Ts = steps * B

    lb_all = jnp.cumsum(jax.nn.softmax(hgrn_lb.astype(F32), axis=0), axis=0)
    lb_all = lb_all - lb_all[:1]

    swa_bb = 8
    tab_p = _rope_tables(np.arange(L))
    tab_s = _rope_tables(np.repeat(PAST_LEN + np.arange(steps), B))
    tab_sb = _rope_tables(np.repeat(PAST_LEN + np.arange(steps), swa_bb))

    xp = x_prompt.reshape(L, D_MODEL)
    xs = jnp.transpose(x_sample, (1, 0, 2)).reshape(Ts, D_MODEL)
    mem = mem_prompt.reshape(N_MEM, D_MODEL)
    ckt_all = jnp.transpose(cache_swa_k, (0, 1, 3, 4, 2))
    cvt_all = jnp.transpose(cache_swa_v, (0, 1, 3, 4, 2))
    mk_rows = _mem_rows_view(cache_mem_k)
    mv_rows = _mem_rows_view(cache_mem_v)
    cpool_v = jnp.transpose(cache_pool, (0, 2, 1, 3))
    row3 = lambda a: a.reshape(depth, 1, a.shape[-1])
    norm1_g, norm2_g, pool_scale, conv_b = row3(norm1_g), row3(norm2_g), row3(pool_scale), row3(conv_b)
    mem_norm_g, mem_qnorm_g, mem_knorm_g = row3(mem_norm_g), row3(mem_qnorm_g), row3(mem_knorm_g)
    tm_p = _row_tile(L, TM_STREAM)
    tl_p = _row_tile(L, TM_LOCAL)

    outs = {k: [] for k in ("sp", "pp", "ps", "kp", "ks", "vp", "vs", "cp", "mk", "mv")}
    hgrn_states = None
    conv_states = None
    for l in range(depth):
        lb = lb_all[l].reshape(1, MIX_W)
        gn = hgrn_norm_g[l].reshape(1, MIX_W)
        gq2 = jnp.tile(swa_qnorm_g[l], 2).reshape(1, LANES)
        gk2 = jnp.tile(swa_knorm_g[l], 2).reshape(1, LANES)

        kv = mem_kv(mem, mem_norm_g, w_mem_kv, mem_knorm_g, l)

        xn = prenorm(xp, norm1_g, l, tl_p)
        proj = matmul_cols(xn, w_in, l, COL_GATE, tm_p, 512)
        ya, s_p = hgrn_prompt(proj, lb, gn, _row_tile(L, 2048))
        yb = pool_prompt(proj, pool_w, pool_scale, l, tl_p)
        khat = swa_kprep(proj, gk2, tab_p, tl_p)
        yc = swa_prompt(proj, khat, swa_sinks, gq2, tab_p, l, _row_tile(L, 8 * WINDOW) // WINDOW)
        ym = mem_attn_prompt(proj, kv, mem_qnorm_g, l, tl_p)
        h, hn = _token_tail(xp, xn, (ya, yb, yc, ym), l, w_in, w_branch, w_o, norm2_g)
        gact, a_tail = up_conv_prompt(hn, w_up, conv_w, conv_b, l, tm_p, 512)
        xp = matmul_res(gact, w_down, l, h, _row_tile(L, TM_DOWN), 256)

        outs["sp"].append(s_p[None])
        outs["pp"].append(proj[None, L - POOL_BUF:, COL_POOL:COL_POOL + MIX_W])
        outs["kp"].append(khat[None, L - WINDOW:].reshape(1, WINDOW, SWA_KV_HEADS, SWA_HEAD_DIM))
        outs["vp"].append(proj[None, L - WINDOW:, COL_SV:COL_SV + kw].reshape(1, WINDOW, SWA_KV_HEADS, SWA_HEAD_DIM))
        outs["cp"].append(a_tail[-1:, CONV_HIST - 2:])
        outs["mk"].append(kv[None, :, :MIX_W].reshape(1, N_MEM, MEM_HEADS, MEM_HEAD_DIM))
        outs["mv"].append(kv[None, :, MIX_W:].reshape(1, N_MEM, MEM_HEADS, MEM_HEAD_DIM))

        xn = prenorm(xs, norm1_g, l, Ts)
        proj_s = matmul_cols(xn, w_in, l, COL_GATE, Ts, 512)
        proj3 = proj_s.reshape(steps, B, COL_GATE)
        ya, hgrn_states = hgrn_sample(proj3, lb, gn, state_hgrn, l, LANES // steps, hgrn_states)
        yb = pool_sample(proj3, cpool_v, pool_w, pool_scale, l, 64)
        khat_s = swa_kprep(proj_s, gk2, tab_s, Ts)
        khat3 = khat_s.reshape(steps, B, kw)
        yc = swa_sample(proj3, khat3, ckt_all, cvt_all, swa_sinks, gq2, tab_sb, l, swa_bb)
        ym = mem_attn_sample(proj3, mk_rows, mv_rows, mem_qnorm_g, l, 8)
        ys = tuple(y.reshape(Ts, MIX_W).astype(BF16) for y in (ya, yb, yc, ym))
        h, hn = _token_tail(xs, xn, ys, l, w_in, w_branch, w_o, norm2_g)
        gact, conv_states = up_conv_sample(hn, w_up, state_conv, conv_w, conv_b, l, steps, 512, conv_states)
        xs = matmul_res(gact, w_down, l, h, Ts, 256)

        outs["ps"].append(proj3[:, :, COL_POOL:COL_POOL + MIX_W])
        to_window_minor = lambda a: jnp.transpose(a.reshape(steps, B, SWA_KV_HEADS, SWA_HEAD_DIM), (1, 2, 3, 0))
        outs["ks"].append(to_window_minor(khat3))
        outs["vs"].append(to_window_minor(proj3[:, :, COL_SV:COL_SV + kw]))

    stk = lambda k: jnp.stack(outs[k], axis=0)
    pool_s = jnp.transpose(jnp.concatenate([cpool_v[:, steps:], stk("ps")], axis=1), (0, 2, 1, 3))
    def slide_window(old, new):
        lead = [(0, 0, 0)] * (old.ndim - 1)
        shifted = lax.pad(old, jnp.zeros((), old.dtype), lead + [(-steps, steps, 0)])
        tail = lax.pad(new, jnp.zeros((), old.dtype), lead + [(WINDOW - steps, 0, 0)])
        pos = lax.broadcasted_iota(jnp.int32, old.shape, old.ndim - 1)
        return jnp.transpose(jnp.where(pos < WINDOW - steps, shifted, tail), (0, 1, 4, 2, 3))

    swa_k_s = slide_window(ckt_all, stk("ks"))
    swa_v_s = slide_window(cvt_all, stk("vs"))
    y_prompt = xp.reshape(1, L, D_MODEL)
    y_sample = jnp.transpose(xs.reshape(steps, B, D_MODEL), (1, 0, 2))
    return (y_prompt, y_sample,
            stk("sp"), hgrn_states, stk("pp"), pool_s, stk("kp"), swa_k_s, stk("vp"), swa_v_s,
            stk("cp"), conv_states, jnp.concatenate(outs["mk"], axis=0)[:, None], jnp.concatenate(outs["mv"], axis=0)[:, None])
```

```python
import functools

import numpy as np
import jax
import jax.numpy as jnp
from jax import lax
from jax.experimental import pallas as pl
from jax.experimental.pallas import tpu as pltpu

F32 = jnp.float32
BF16 = jnp.bfloat16

D_MODEL = 2048
MIX_W = D_MODEL // 2
N_BRANCH = 4
A_DK = 128
A_HEADS = MIX_W // A_DK
POOL_WINDOWS = (2, 4, 8, 16)
POOL_GC = MIX_W // len(POOL_WINDOWS)
POOL_BUF = max(POOL_WINDOWS) - 1
SWA_HEAD_DIM = 64
SWA_Q_HEADS = MIX_W // SWA_HEAD_DIM
SWA_KV_HEADS = SWA_Q_HEADS // 4
WINDOW = 128
ROT_DIM = SWA_HEAD_DIM // 4
ROPE_THETA = 500000.0
N_MEM = 256
MEM_HEADS = 4
MEM_HEAD_DIM = MIX_W // MEM_HEADS
D_FF = 11 * D_MODEL // 4
EPS = 1e-6
PAST_LEN = 8192

COL_HQ, COL_HF, COL_HI, COL_HG = 0, MIX_W, 2 * MIX_W, 3 * MIX_W
COL_POOL = 4 * MIX_W
COL_SQ = 5 * MIX_W
COL_SK = 6 * MIX_W
COL_SV = COL_SK + SWA_KV_HEADS * SWA_HEAD_DIM
COL_MQ = COL_SV + SWA_KV_HEADS * SWA_HEAD_DIM
COL_GATE = COL_MQ + MIX_W
IN_COLS = COL_GATE + N_BRANCH * D_MODEL

LANES = 128
V7X_VMEM_BYTES = 64 * 1024 * 1024
VMEM_LIMIT = V7X_VMEM_BYTES * 7 // 8
HGRN_CHUNK = 128
NEG_BIG = -1e30
LOG2_E = 1.4426950408889634


def _params(*sem):
    return pltpu.CompilerParams(dimension_semantics=sem, vmem_limit_bytes=VMEM_LIMIT)


def _sigmoid(x):
    return 0.5 * jnp.tanh(0.5 * x) + 0.5


def _dot(a, b):
    return jnp.dot(a, b, preferred_element_type=F32)


def _dot_nt(a, b):
    return lax.dot_general(a, b, (((1,), (1,)), ((), ())), preferred_element_type=F32)


def _skip_ref(kernel_fn, idx):
    def wrapped(*refs):
        return kernel_fn(*refs[:idx], *refs[idx + 1:])
    return wrapped


def _layer_slab_call(make_kernel, in_specs, args, slab_out, layer, slab_block, slab_index, out_specs, **kw):
    n_layers = kw["out_shape"][-1].shape[0]
    if slab_out is None:
        spec = pl.BlockSpec((n_layers,) + slab_block, lambda *g: (0,) + slab_index(*g))
        return pl.pallas_call(make_kernel(layer), in_specs=in_specs, out_specs=list(out_specs) + [spec], **kw)(*args)
    spec = pl.BlockSpec((1,) + slab_block, lambda *g: (layer,) + slab_index(*g))
    idx = len(args)
    return pl.pallas_call(
        _skip_ref(make_kernel(0), idx),
        in_specs=list(in_specs) + [pl.BlockSpec(memory_space=pl.ANY)],
        out_specs=list(out_specs) + [spec],
        input_output_aliases={idx: len(kw["out_shape"]) - 1},
        **kw)(*args, slab_out)


def _zero_other_slabs(so_ref, own):
    for l in range(so_ref.shape[0]):
        if l != own:
            so_ref[l] = jnp.zeros(so_ref.shape[1:], so_ref.dtype)


def _rms_rows(x, g):
    ms = jnp.mean(x * x, axis=-1, keepdims=True)
    return x * lax.rsqrt(ms + EPS) * g


def _prenorm_kernel(x_ref, g_ref, o_ref):
    o_ref[...] = _rms_rows(x_ref[...], g_ref[...]).astype(BF16)


def prenorm(x, g, layer, tm):
    T, K = x.shape
    return pl.pallas_call(
        _prenorm_kernel,
        grid=(T // tm,),
        in_specs=[pl.BlockSpec((tm, K), lambda i: (i, 0)),
                  pl.BlockSpec((None, 1, K), lambda i: (layer, 0, 0))],
        out_specs=pl.BlockSpec((tm, K), lambda i: (i, 0)),
        out_shape=jax.ShapeDtypeStruct((T, K), BF16),
        compiler_params=_params("arbitrary"),
        name="prenorm",
    )(x, g)


def _matmul_kernel(a_ref, w_ref, o_ref):
    o_ref[...] = _dot(a_ref[...], w_ref[...].astype(BF16))


def matmul_cols(a, w, layer, n_cols, tm, tn):
    T, K = a.shape
    return pl.pallas_call(
        _matmul_kernel,
        grid=(T // tm, n_cols // tn),
        in_specs=[
            pl.BlockSpec((tm, K), lambda i, j: (i, 0)),
            pl.BlockSpec((None, K, tn), lambda i, j: (layer, 0, j)),
        ],
        out_specs=pl.BlockSpec((tm, tn), lambda i, j: (i, j)),
        out_shape=jax.ShapeDtypeStruct((T, n_cols), F32),
        compiler_params=_params("parallel", "arbitrary"),
        name="matmul_cols",
    )(a, w)


def _matmul_res_kernel(a_ref, w_ref, r_ref, o_ref):
    o_ref[...] = r_ref[...] + _dot(a_ref[...], w_ref[...].astype(BF16))


def matmul_res(a, w, layer, res, tm, tn):
    T, K = a.shape
    N = w.shape[2]
    return pl.pallas_call(
        _matmul_res_kernel,
        grid=(T // tm, N // tn),
        in_specs=[
            pl.BlockSpec((tm, K), lambda i, j: (i, 0)),
            pl.BlockSpec((None, K, tn), lambda i, j: (layer, 0, j)),
            pl.BlockSpec((tm, tn), lambda i, j: (i, j)),
        ],
        out_specs=pl.BlockSpec((tm, tn), lambda i, j: (i, j)),
        out_shape=jax.ShapeDtypeStruct((T, N), F32),
        compiler_params=_params("parallel", "arbitrary"),
        name="matmul_res",
    )(a, w, res)


def _res_norm_kernel(a_ref, w_ref, r_ref, g_ref, h_ref, hn_ref, w_s):
    @pl.when(pl.program_id(0) == 0)
    def _():
        w_s[...] = w_ref[...].astype(BF16)

    h = r_ref[...] + _dot(a_ref[...], w_s[...])
    h_ref[...] = h
    hn_ref[...] = _rms_rows(h, g_ref[...]).astype(BF16)


def matmul_res_norm(a, w, layer, res, g, tm):
    T, K = a.shape
    N = w.shape[2]
    return pl.pallas_call(
        _res_norm_kernel,
        grid=(T // tm,),
        in_specs=[
            pl.BlockSpec((tm, K), lambda i: (i, 0)),
            pl.BlockSpec((None, K, N), lambda i: (layer, 0, 0), pipeline_mode=pl.Buffered(1)),
            pl.BlockSpec((tm, N), lambda i: (i, 0)),
            pl.BlockSpec((None, 1, N), lambda i: (layer, 0, 0)),
        ],
        out_specs=[pl.BlockSpec((tm, N), lambda i: (i, 0)), pl.BlockSpec((tm, N), lambda i: (i, 0))],
        out_shape=[jax.ShapeDtypeStruct((T, N), F32), jax.ShapeDtypeStruct((T, N), BF16)],
        scratch_shapes=[pltpu.VMEM((K, N), BF16)],
        compiler_params=_params("arbitrary"),
        name="matmul_res_norm",
    )(a, w, res, g)


def _merge_kernel(xn_ref, ya_ref, yb_ref, yc_ref, ym_ref, wg0_ref, wg1_ref, wg2_ref, wg3_ref, wb_ref, o_ref,
                  wg_s, wb_s):
    @pl.when(pl.program_id(1) == 0)
    def _():
        for n, wg_ref in enumerate((wg0_ref, wg1_ref, wg2_ref, wg3_ref)):
            wg_s[n] = wg_ref[...].astype(BF16)
            wb_s[n] = wb_ref[n].astype(BF16)

    xn = xn_ref[...]
    acc = None
    for n, y_ref in enumerate((ya_ref, yb_ref, yc_ref, ym_ref)):
        t = _sigmoid(_dot(xn, wg_s[n])) * _dot(y_ref[...], wb_s[n])
        acc = t if acc is None else acc + t
    o_ref[...] = acc.astype(BF16)


def merge_branches(xn, ys, w_in, w_branch, layer, tm, tn):
    T = xn.shape[0]
    once = pl.Buffered(1)
    y_spec = pl.BlockSpec((tm, MIX_W), lambda j, i: (i, 0))

    def gate_spec(n):
        off = (COL_GATE + n * D_MODEL) // tn
        return pl.BlockSpec((None, D_MODEL, tn), lambda j, i: (layer, 0, off + j), pipeline_mode=once)

    return pl.pallas_call(
        _merge_kernel,
        grid=(D_MODEL // tn, T // tm),
        in_specs=[pl.BlockSpec((tm, D_MODEL), lambda j, i: (i, 0))] + [y_spec] * 4
        + [gate_spec(n) for n in range(N_BRANCH)]
        + [pl.BlockSpec((None, N_BRANCH, MIX_W, tn), lambda j, i: (layer, 0, 0, j), pipeline_mode=once)],
        out_specs=pl.BlockSpec((tm, tn), lambda j, i: (i, j)),
        out_shape=jax.ShapeDtypeStruct((T, D_MODEL), BF16),
        scratch_shapes=[pltpu.VMEM((N_BRANCH, D_MODEL, tn), BF16), pltpu.VMEM((N_BRANCH, MIX_W, tn), BF16)],
        compiler_params=_params("arbitrary", "arbitrary"),
        name="merge_branches",
    )(xn, *ys, w_in, w_in, w_in, w_in, w_branch)


CONV_HIST = 8


def _gelu(x):
    return 0.5 * x * (1.0 + lax.erf(x * (2.0 ** -0.5)))


def _up_conv_prompt_kernel(xn_ref, wa_ref, wv_ref, cw_ref, cb_ref, g_ref, tail_ref, carry_ref, *, tm, rc):
    i, j = pl.program_id(0), pl.program_id(1)
    wa = wa_ref[...].astype(BF16)
    wv = wv_ref[...].astype(BF16)
    prev = jnp.where(i == 0, 0.0, carry_ref[j])
    row = lax.broadcasted_iota(jnp.int32, (rc, wa.shape[1]), 0)
    for c in range(tm // rc):
        sl = pl.ds(c * rc, rc)
        xn = xn_ref[sl, :]
        a = _dot(xn, wa)
        v = _dot(xn, wv)
        a1 = jnp.where(row == 0, prev[CONV_HIST - 1:CONV_HIST], pltpu.roll(a, 1, 0))
        a2 = jnp.where(row == 0, prev[CONV_HIST - 2:CONV_HIST - 1],
                       jnp.where(row == 1, prev[CONV_HIST - 1:CONV_HIST], pltpu.roll(a, 2, 0)))
        cc = cb_ref[...] + cw_ref[0:1, :] * a2 + cw_ref[1:2, :] * a1 + cw_ref[2:3, :] * a
        g_ref[sl, :] = (_gelu(cc) * v).astype(BF16)
        prev = a[rc - CONV_HIST:, :]
    carry_ref[j] = prev
    tail_ref[...] = prev


def up_conv_prompt(xn, w_up, conv_w, conv_b, layer, tm, tn):
    T, K = xn.shape
    nc = D_FF // tn
    return pl.pallas_call(
        functools.partial(_up_conv_prompt_kernel, tm=tm, rc=min(tm, UP_PIECE_ROWS)),
        grid=(T // tm, nc),
        in_specs=[
            pl.BlockSpec((tm, K), lambda i, j: (i, 0)),
            pl.BlockSpec((None, K, tn), lambda i, j: (layer, 0, j)),
            pl.BlockSpec((None, K, tn), lambda i, j: (layer, 0, nc + j)),
            pl.BlockSpec((None, 3, tn), lambda i, j: (layer, 0, j)),
            pl.BlockSpec((None, 1, tn), lambda i, j: (layer, 0, j)),
        ],
        out_specs=[
            pl.BlockSpec((tm, tn), lambda i, j: (i, j)),
            pl.BlockSpec((None, CONV_HIST, tn), lambda i, j: (i, 0, j)),
        ],
        out_shape=[
            jax.ShapeDtypeStruct((T, D_FF), BF16),
            jax.ShapeDtypeStruct((T // tm, CONV_HIST, D_FF), F32),
        ],
        scratch_shapes=[pltpu.VMEM((nc, CONV_HIST, tn), F32)],
        compiler_params=_params("arbitrary", "arbitrary"),
        name="up_conv_prompt",
    )(xn, w_up, w_up, conv_w, conv_b)


def _up_conv_sample_kernel(xn_ref, wa_ref, wv_ref, st_ref, cw_ref, cb_ref, g_ref, so_ref, *, steps, B, own):
    _zero_other_slabs(so_ref, own)
    xn = xn_ref[...]
    a = _dot(xn, wa_ref[...].astype(BF16))
    v = _dot(xn, wv_ref[...].astype(BF16))
    hist = [st_ref[:, 0, :], st_ref[:, 1, :]] + [a[t * B:(t + 1) * B] for t in range(steps)]
    for t in range(steps):
        c = cb_ref[...] + cw_ref[0:1, :] * hist[t] + cw_ref[1:2, :] * hist[t + 1] + cw_ref[2:3, :] * hist[t + 2]
        g_ref[t * B:(t + 1) * B, :] = (_gelu(c) * v[t * B:(t + 1) * B]).astype(BF16)
    so_ref[own, :, 0, :] = hist[steps]
    so_ref[own, :, 1, :] = hist[steps + 1]


def up_conv_sample(xn, w_up, state_conv, conv_w, conv_b, layer, steps, tn, state_out):
    T, K = xn.shape
    B = T // steps
    nc = D_FF // tn
    in_specs = [
        pl.BlockSpec((T, K), lambda j: (0, 0)),
        pl.BlockSpec((None, K, tn), lambda j: (layer, 0, j)),
        pl.BlockSpec((None, K, tn), lambda j: (layer, 0, nc + j)),
        pl.BlockSpec((None, B, 2, tn), lambda j: (layer, 0, 0, j)),
        pl.BlockSpec((None, 3, tn), lambda j: (layer, 0, j)),
        pl.BlockSpec((None, 1, tn), lambda j: (layer, 0, j)),
    ]
    return _layer_slab_call(
        lambda own: functools.partial(_up_conv_sample_kernel, steps=steps, B=B, own=own),
        in_specs, [xn, w_up, w_up, state_conv, conv_w, conv_b], state_out, layer,
        (B, 2, tn), lambda j: (0, 0, j),
        [pl.BlockSpec((T, tn), lambda j: (0, j))],
        grid=(nc,),
        out_shape=[
            jax.ShapeDtypeStruct((T, D_FF), BF16),
            jax.ShapeDtypeStruct(state_conv.shape, F32),
        ],
        compiler_params=_params("arbitrary"),
        name="up_conv_sample",
    )


def _hgrn_gates(q_in, z, lb):
    q = q_in * _sigmoid(q_in)
    log_sig = jnp.minimum(z, 0.0) - jnp.log(1.0 + jnp.exp(-jnp.abs(z)))
    a1 = jnp.log(lb)
    a2 = jnp.log1p(-lb) + log_sig
    log_f = jnp.maximum(a1, a2) + jnp.log(1.0 + jnp.exp(-jnp.abs(a1 - a2)))
    k = (1.0 - lb) * _sigmoid(-z)
    return q, log_f, k


def _hgrn_out(o, gate, gn):
    ms = jnp.mean(o * o, axis=-1, keepdims=True)
    return o * lax.rsqrt(ms + EPS) * gn * (gate * _sigmoid(gate))


def _cumsum_rows(x, tril):
    hi = x.astype(BF16)
    r1 = x - hi.astype(F32)
    mid = r1.astype(BF16)
    lo = (r1 - mid.astype(F32)).astype(BF16)
    return _dot(tril, hi) + _dot(tril, mid) + _dot(tril, lo)


def _block_row(x, blk, r):
    C = x.shape[0]
    x3 = x.reshape(C // blk, blk, LANES)
    return jnp.broadcast_to(x3[:, r:r + 1, :], (C // blk, blk, LANES)).reshape(C, LANES)


def _hgrn_pair_codes():
    t = np.arange(HGRN_CHUNK)[:, None]
    s = np.arange(HGRN_CHUNK)[None, :]
    level = np.floor(np.log2(np.maximum(t ^ s, 1))).astype(np.int32)
    return jnp.asarray(np.where(s > t, -1, np.where(s == t, 0, 1 + level)), jnp.int32)


def _boundary_row(b, m):
    if 2 * m >= 8:
        return _block_row(b, 2 * m, m - 1)
    r8 = lax.broadcasted_iota(jnp.int32, b.shape, 0) & 7
    if m == 2:
        return jnp.where(r8 < 4, _block_row(b, 8, 1), _block_row(b, 8, 5))
    return jnp.where((r8 & 1) == 0, b, pltpu.roll(b, 1, 0))


def _hgrn_attention(q, k, b, code):
    C = HGRN_CHUNK
    rowl = lax.broadcasted_iota(jnp.int32, (C, LANES), 0)
    att = jnp.where(code == 0, jnp.sum(q * k, axis=-1, keepdims=True), 0.0)
    m, level = 1, 1
    while m < C:
        d = b - _boundary_row(b, m)
        isq = (rowl & m) != 0
        x = (jnp.where(isq, q, k) * jnp.exp2(jnp.abs(d) * (-LOG2_E))).astype(BF16)
        att = jnp.where(code == level, _dot_nt(x, x), att)
        m *= 2
        level += 1
    return att


def _hgrn_apply(q, k, v, b, att, S):
    C = HGRN_CHUNK
    vb = v.astype(BF16)
    o = _dot((q * jnp.exp(b)).astype(BF16), S.astype(BF16)) + _dot(att.astype(BF16), vb)
    bl = b[C - 1:C, :]
    kk = k * jnp.exp(bl - b)
    ecol = jnp.transpose(jnp.broadcast_to(jnp.exp(bl), (LANES, LANES)))
    return o, ecol * S + _dot(jnp.transpose(kk).astype(BF16), vb)


def _hgrn_prompt_kernel(q_ref, f_ref, i_ref, g_ref, lb_ref, gn_ref, code_ref, y_ref, so_ref, s_ref, *, rows):
    @pl.when(pl.program_id(1) == 0)
    def _():
        s_ref[...] = jnp.zeros_like(s_ref)

    C = HGRN_CHUNK
    code = code_ref[...]
    tril = jnp.where(code >= 0, 1.0, 0.0).astype(BF16)
    q, log_f, k = _hgrn_gates(q_ref[...], f_ref[...], lb_ref[...])
    chunks = [slice(c * C, (c + 1) * C) for c in range(rows // C)]
    bs = [_cumsum_rows(log_f[sl], tril) for sl in chunks]
    atts = [_hgrn_attention(q[sl], k[sl], b, code) for sl, b in zip(chunks, bs)]
    S = s_ref[...]
    outs = []
    for sl, b, att in zip(chunks, bs, atts):
        o, S = _hgrn_apply(q[sl], k[sl], i_ref[sl, :], b, att, S)
        outs.append(o)
    s_ref[...] = S
    y_ref[...] = _hgrn_out(jnp.concatenate(outs, axis=0), g_ref[...], gn_ref[...]).astype(BF16)

    @pl.when(pl.program_id(1) == pl.num_programs(1) - 1)
    def _():
        so_ref[...] = s_ref[...]


def hgrn_prompt(proj, lb, gn, rows):
    T = proj.shape[0]

    def col(off):
        base = off // LANES
        return pl.BlockSpec((rows, LANES), lambda h, c: (c, base + h))

    vec = pl.BlockSpec((1, LANES), lambda h, c: (0, h))
    return pl.pallas_call(
        functools.partial(_hgrn_prompt_kernel, rows=rows),
        grid=(A_HEADS, T // rows),
        in_specs=[col(COL_HQ), col(COL_HF), col(COL_HI), col(COL_HG), vec, vec,
                  pl.BlockSpec((HGRN_CHUNK, HGRN_CHUNK), lambda h, c: (0, 0))],
        out_specs=[
            pl.BlockSpec((rows, LANES), lambda h, c: (c, h)),
            pl.BlockSpec((None, A_DK, LANES), lambda h, c: (h, 0, 0)),
        ],
        out_shape=[
            jax.ShapeDtypeStruct((T, MIX_W), BF16),
            jax.ShapeDtypeStruct((A_HEADS, A_DK, LANES), F32),
        ],
        scratch_shapes=[pltpu.VMEM((A_DK, LANES), F32)],
        compiler_params=_params("parallel", "arbitrary"),
        name="hgrn_prompt",
    )(proj, proj, proj, proj, lb, gn, _hgrn_pair_codes())


def _hgrn_sample_kernel(q_ref, f_ref, i_ref, g_ref, lb_ref, gn_ref, s_ref, y_ref, so_ref, *, steps, bb, own):
    _zero_other_slabs(so_ref, own)
    lb = lb_ref[...]
    gn = gn_ref[...]
    qs, ks, vs, bs = [], [], [], []
    b = None
    for t in range(steps):
        q, log_f, k = _hgrn_gates(q_ref[t], f_ref[t], lb)
        b = log_f if b is None else b + log_f
        qs.append(q)
        ks.append(k)
        vs.append(i_ref[t])
        bs.append(b)
    intra = []
    for t in range(steps):
        acc = None
        for s in range(t + 1):
            w = jnp.sum(qs[t] * ks[s] * jnp.exp(bs[t] - bs[s]), axis=-1, keepdims=True)
            acc = w * vs[s] if acc is None else acc + w * vs[s]
        intra.append(acc)
    R = steps * bb
    q_stack = jnp.concatenate([qs[t] * jnp.exp(bs[t]) for t in range(steps)], axis=0).astype(BF16)
    k_stack = jnp.concatenate([ks[t] * jnp.exp(bs[-1] - bs[t]) for t in range(steps)], axis=0)
    v_stack = jnp.concatenate(vs, axis=0).astype(BF16)
    k_t = jnp.transpose(k_stack)
    f_pad = jnp.concatenate([jnp.exp(bs[-1])] + [jnp.zeros((R - bb, LANES), F32)], axis=0)
    f_t = jnp.transpose(f_pad)
    rowi = lax.broadcasted_iota(jnp.int32, (R, LANES), 0) % bb
    lanei = lax.broadcasted_iota(jnp.int32, (LANES, R), 1)

    def body(bi, o_acc):
        s_b = s_ref[bi]
        o_acc = jnp.where(rowi == bi, _dot(q_stack, s_b.astype(BF16)), o_acc)
        f_col = jnp.sum(jnp.where(lanei == bi, f_t, 0.0), axis=-1, keepdims=True)
        k_b = jnp.where(lanei % bb == bi, k_t, 0.0).astype(BF16)
        so_ref[own, bi] = f_col * s_b + _dot(k_b, v_stack)
        return o_acc

    o_inter = lax.fori_loop(0, bb, body, jnp.zeros((R, LANES), F32), unroll=HGRN_BATCH_UNROLL)
    for t in range(steps):
        o = o_inter[t * bb:(t + 1) * bb] + intra[t]
        y_ref[t] = _hgrn_out(o, g_ref[t], gn).astype(BF16)


def hgrn_sample(proj3, lb, gn, state, layer, bb, state_out):
    steps, B, _ = proj3.shape
    assert steps * bb == LANES

    def col(off):
        base = off // LANES
        return pl.BlockSpec((steps, bb, LANES), lambda g, h: (0, g, base + h))

    vec = pl.BlockSpec((1, LANES), lambda g, h: (0, h))
    slab = pl.BlockSpec((None, bb, None, A_DK, LANES), lambda g, h: (layer, g, h, 0, 0))
    return _layer_slab_call(
        lambda own: functools.partial(_hgrn_sample_kernel, steps=steps, bb=bb, own=own),
        [col(COL_HQ), col(COL_HF), col(COL_HI), col(COL_HG), vec, vec, slab],
        [proj3, proj3, proj3, proj3, lb, gn, state], state_out, layer,
        (bb, None, A_DK, LANES), lambda g, h: (g, h, 0, 0),
        [pl.BlockSpec((steps, bb, LANES), lambda g, h: (0, g, h))],
        grid=(B // bb, A_HEADS),
        out_shape=[
            jax.ShapeDtypeStruct((steps, B, MIX_W), BF16),
            jax.ShapeDtypeStruct(state.shape, F32),
        ],
        compiler_params=_params("parallel", "arbitrary"),
        name="hgrn_sample",
    )


POOL_HIST = 32


def _pool_project(d, w_ref, sc_ref, g):
    sl = slice(g * POOL_GC, (g + 1) * POOL_GC)
    return _dot(d.astype(BF16), w_ref[g].astype(BF16)) * sc_ref[:, sl]


def _pool_prompt_kernel(u_ref, prev_ref, w_ref, sc_ref, y_ref, ext_ref, sum_ref, *, tm):
    i = pl.program_id(0)
    H = POOL_HIST
    ext_ref[0:H, :] = jnp.where(i == 0, 0.0, prev_ref[...])
    ext_ref[H:, :] = u_ref[...]
    sum_ref[0:H // 2, :] = jnp.zeros((H // 2, MIX_W), F32)
    pos = i * tm + lax.broadcasted_iota(jnp.int32, (tm, 1), 0)
    src, w = ext_ref, 1
    for g, win in enumerate(POOL_WINDOWS):
        lanes = slice(g * POOL_GC, MIX_W)
        while w < win:
            n = H // 2 + tm
            sum_ref[pl.ds(H // 2, n), lanes] = src[pl.ds(H // 2, n), lanes] + src[pl.ds(H // 2 - w, n), lanes]
            src, w = sum_ref, 2 * w
        sl = slice(g * POOL_GC, (g + 1) * POOL_GC)
        cnt = jnp.minimum(pos + 1, win).astype(F32)
        d = src[pl.ds(H, tm), sl] / cnt - u_ref[:, sl]
        y_ref[:, sl] = _pool_project(d, w_ref, sc_ref, g).astype(BF16)


def pool_prompt(proj, pool_w, pool_scale, layer, tm):
    T = proj.shape[0]
    cb = COL_POOL // MIX_W
    return pl.pallas_call(
        functools.partial(_pool_prompt_kernel, tm=tm),
        grid=(T // tm,),
        in_specs=[
            pl.BlockSpec((tm, MIX_W), lambda i: (i, cb)),
            pl.BlockSpec((POOL_HIST, MIX_W), lambda i: (jnp.maximum(i * (tm // POOL_HIST) - 1, 0), cb)),
            pl.BlockSpec((None, len(POOL_WINDOWS), POOL_GC, POOL_GC), lambda i: (layer, 0, 0, 0)),
            pl.BlockSpec((None, 1, MIX_W), lambda i: (layer, 0, 0)),
        ],
        out_specs=pl.BlockSpec((tm, MIX_W), lambda i: (i, 0)),
        out_shape=jax.ShapeDtypeStruct((T, MIX_W), BF16),
        scratch_shapes=[pltpu.VMEM((POOL_HIST + tm, MIX_W), F32), pltpu.VMEM((POOL_HIST + tm, MIX_W), F32)],
        compiler_params=_params("arbitrary"),
        name="pool_prompt",
    )(proj, proj, pool_w, pool_scale)


def _pool_sample_kernel(u_ref, c_ref, w_ref, sc_ref, y_ref, *, steps):
    for t in range(steps):
        for g, win in enumerate(POOL_WINDOWS):
            sl = slice(g * POOL_GC, (g + 1) * POOL_GC)
            acc = u_ref[t, :, sl]
            for j in range(1, win):
                if j <= t:
                    acc = acc + u_ref[t - j, :, sl]
                else:
                    acc = acc + c_ref[POOL_BUF + t - j, :, sl]
            d = acc / float(win) - u_ref[t, :, sl]
            y_ref[t, :, sl] = _pool_project(d, w_ref, sc_ref, g).astype(BF16)


def pool_sample(proj3, cache_pool, pool_w, pool_scale, layer, bb):
    steps, B, _ = proj3.shape
    cb = COL_POOL // MIX_W
    return pl.pallas_call(
        functools.partial(_pool_sample_kernel, steps=steps),
        grid=(B // bb,),
        in_specs=[
            pl.BlockSpec((steps, bb, MIX_W), lambda g: (0, g, cb)),
            pl.BlockSpec((None, POOL_BUF, bb, MIX_W), lambda g: (layer, 0, g, 0)),
            pl.BlockSpec((None, len(POOL_WINDOWS), POOL_GC, POOL_GC), lambda g: (layer, 0, 0, 0)),
            pl.BlockSpec((None, 1, MIX_W), lambda g: (layer, 0, 0)),
        ],
        out_specs=pl.BlockSpec((steps, bb, MIX_W), lambda g: (0, g, 0)),
        out_shape=jax.ShapeDtypeStruct((steps, B, MIX_W), BF16),
        compiler_params=_params("arbitrary"),
        name="pool_sample",
    )(proj3, cache_pool, pool_w, pool_scale)


def _rope_tables(positions):
    half = ROT_DIM // 2
    inv = np.power(ROPE_THETA, -np.arange(0, ROT_DIM, 2, dtype=np.float64) / ROT_DIM)
    ang = np.asarray(positions, np.float64)[:, None] * inv[None, :]
    cos, sin = np.cos(ang), np.sin(ang)
    n = len(positions)
    ct = np.ones((n, LANES))
    sn = np.zeros((n, LANES))
    sp = np.zeros((n, LANES))
    for base in (0, SWA_HEAD_DIM):
        ct[:, base:base + half] = cos
        ct[:, base + half:base + ROT_DIM] = cos
        sn[:, base:base + half] = -sin
        sp[:, base + half:base + ROT_DIM] = sin
    return tuple(jnp.asarray(t, F32) for t in (ct, sn, sp))


def _head_norm_rope(x, g, ct, sn, sp):
    lane = lax.broadcasted_iota(jnp.int32, x.shape, 1)
    lo = lane < SWA_HEAD_DIM
    x2 = x * x
    ms_lo = jnp.sum(jnp.where(lo, x2, 0.0), axis=-1, keepdims=True) / SWA_HEAD_DIM
    ms_hi = jnp.sum(jnp.where(lo, 0.0, x2), axis=-1, keepdims=True) / SWA_HEAD_DIM
    xn = x * jnp.where(lo, lax.rsqrt(ms_lo + EPS), lax.rsqrt(ms_hi + EPS)) * g
    half = ROT_DIM // 2
    return xn * ct + pltpu.roll(xn, LANES - half, 1) * sn + pltpu.roll(xn, half, 1) * sp


def _kprep_kernel(k_ref, g_ref, ct_ref, sn_ref, sp_ref, o_ref):
    ct, sn, sp = ct_ref[...], sn_ref[...], sp_ref[...]
    for j in range(2):
        sl = slice(j * LANES, (j + 1) * LANES)
        o_ref[:, sl] = _head_norm_rope(k_ref[:, sl], g_ref[...], ct, sn, sp)


def swa_kprep(proj, g2, tables, tm):
    T = proj.shape[0]
    kw = SWA_KV_HEADS * SWA_HEAD_DIM
    tab = pl.BlockSpec((tm, LANES), lambda i: (i, 0))
    return pl.pallas_call(
        _kprep_kernel,
        grid=(T // tm,),
        in_specs=[pl.BlockSpec((tm, kw), lambda i: (i, COL_SK // kw)),
                  pl.BlockSpec((1, LANES), lambda i: (0, 0)), tab, tab, tab],
        out_specs=pl.BlockSpec((tm, kw), lambda i: (i, 0)),
        out_shape=jax.ShapeDtypeStruct((T, kw), F32),
        compiler_params=_params("arbitrary"),
        name="swa_kprep",
    )(proj, g2, *tables)


def _dup_head(x, parity):
    lane = lax.broadcasted_iota(jnp.int32, x.shape, 1)
    return jnp.where(lane // SWA_HEAD_DIM == parity, x, pltpu.roll(x, SWA_HEAD_DIM, 1))


def _stack_heads(q):
    lane = lax.broadcasted_iota(jnp.int32, q.shape, 1)
    lo = lane < SWA_HEAD_DIM
    return jnp.concatenate([jnp.where(lo, q, 0.0), jnp.where(lo, 0.0, q)], axis=0)


def _unstack_heads(o2):
    R = o2.shape[0] // 2
    lane = lax.broadcasted_iota(jnp.int32, (R, LANES), 1)
    return jnp.where(lane < SWA_HEAD_DIM, o2[:R], o2[R:])


def _swa_prompt_kernel(sink_ref, q_ref, kc_ref, kp_ref, vc_ref, vp_ref, g_ref, ct_ref, sn_ref, sp_ref, y_ref,
                       *, layer, nb):
    first = pl.program_id(0) == 0
    W = WINDOW
    G = SWA_Q_HEADS // SWA_KV_HEADS
    r4 = lax.broadcasted_iota(jnp.int32, (G * W, 1), 0)
    hh = r4 // W
    ci = lax.broadcasted_iota(jnp.int32, (1, W), 1)
    cur = ci <= r4 % W
    scale = SWA_HEAD_DIM ** -0.5

    def head_blocks(cur_ref, prev_ref, kvh):
        ksl = slice((kvh // 2) * LANES, (kvh // 2 + 1) * LANES)
        x = _dup_head(jnp.concatenate([prev_ref[:, ksl], cur_ref[:, ksl]], axis=0), kvh % 2).astype(BF16)
        return [x[j * W:(j + 1) * W] for j in range(nb + 1)]

    scores, sinks = [], []
    for kvh in range(SWA_KV_HEADS):
        kb = head_blocks(kc_ref, kp_ref, kvh)
        sink = sink_ref[layer, G * kvh + G - 1]
        for i in range(G - 2, -1, -1):
            sink = jnp.where(hh == i, sink_ref[layer, G * kvh + i], sink)
        for blk in range(nb):
            rows = slice(blk * W, (blk + 1) * W)
            qs = []
            for jj in range(G // 2):
                qsl = slice((2 * kvh + jj) * LANES, (2 * kvh + jj + 1) * LANES)
                qn = _head_norm_rope(q_ref[rows, qsl], g_ref[...], ct_ref[rows, :], sn_ref[rows, :], sp_ref[rows, :])
                qs.append(_stack_heads(qn * scale))
            q = jnp.concatenate(qs, axis=0).astype(BF16)
            s_prev = _dot_nt(q, kb[blk])
            if blk == 0:
                s_prev = jnp.where(first, NEG_BIG, s_prev)
            scores.append(jnp.where(cur, _dot_nt(q, kb[blk + 1]), s_prev))
            sinks.append(sink)
    probs = []
    for s, sink in zip(scores, sinks):
        m = jnp.maximum(jnp.max(s, axis=-1, keepdims=True), sink)
        e = jnp.exp(s - m)
        probs.append(e * (1.0 / (jnp.sum(e, axis=-1, keepdims=True) + jnp.exp(sink - m))))
    for kvh in range(SWA_KV_HEADS):
        vb = head_blocks(vc_ref, vp_ref, kvh)
        for blk in range(nb):
            p = probs[kvh * nb + blk]
            o = (_dot(jnp.where(cur, p, 0.0).astype(BF16), vb[blk + 1])
                 + _dot(jnp.where(cur, 0.0, p).astype(BF16), vb[blk]))
            for jj in range(G // 2):
                qsl = slice((2 * kvh + jj) * LANES, (2 * kvh + jj + 1) * LANES)
                y_ref[blk * W:(blk + 1) * W, qsl] = _unstack_heads(o[2 * jj * W:(2 * jj + 2) * W]).astype(BF16)


def swa_prompt(proj, khat, sinks, g2, tables, layer, nb):
    T = proj.shape[0]
    W = WINDOW
    tq = nb * W
    kw = SWA_KV_HEADS * SWA_HEAD_DIM
    tab = pl.BlockSpec((tq, LANES), lambda i: (i, 0))
    prev = lambda i: jnp.maximum(i * nb - 1, 0)
    return pl.pallas_call(
        functools.partial(_swa_prompt_kernel, layer=layer, nb=nb),
        grid=(T // tq,),
        in_specs=[
            pl.BlockSpec(memory_space=pltpu.SMEM),
            pl.BlockSpec((tq, MIX_W), lambda i: (i, COL_SQ // MIX_W)),
            pl.BlockSpec((tq, kw), lambda i: (i, 0)),
            pl.BlockSpec((W, kw), lambda i: (prev(i), 0)),
            pl.BlockSpec((tq, kw), lambda i: (i, COL_SV // kw)),
            pl.BlockSpec((W, kw), lambda i: (prev(i), COL_SV // kw)),
            pl.BlockSpec((1, LANES), lambda i: (0, 0)), tab, tab, tab,
        ],
        out_specs=pl.BlockSpec((tq, MIX_W), lambda i: (i, 0)),
        out_shape=jax.ShapeDtypeStruct((T, MIX_W), BF16),
        compiler_params=_params("arbitrary"),
        name="swa_prompt",
    )(sinks, proj, khat, khat, proj, proj, g2, *tables)


def _swa_sample_kernel(sink_ref, q_ref, kn_ref, vn_ref, kc_ref, vc_ref, g_ref, ct_ref, sn_ref, sp_ref, y_ref,
                       *, layer, steps, bb):
    kvh = pl.program_id(1)
    parity = kvh % 2
    W = WINDOW
    R = steps * bb
    G = SWA_Q_HEADS // SWA_KV_HEADS
    ct, sn, sp = ct_ref[...], sn_ref[...], sp_ref[...]
    scale = SWA_HEAD_DIM ** -0.5
    r4 = lax.broadcasted_iota(jnp.int32, (G * R, 1), 0)
    hh = r4 // R
    tq = (r4 % R) // bb
    bq = r4 % bb
    c_new = lax.broadcasted_iota(jnp.int32, (1, R), 1)
    valid_new = (c_new % bb == bq) & (c_new // bb <= tq)
    c_old = lax.broadcasted_iota(jnp.int32, (1, W), 1)
    valid_old = c_old > tq
    kn = _dup_head(jnp.concatenate([kn_ref[t] for t in range(steps)], axis=0), parity).astype(BF16)
    vn = _dup_head(jnp.concatenate([vn_ref[t] for t in range(steps)], axis=0), parity).astype(BF16)
    qs = []
    for jj in range(G // 2):
        qsl = slice(jj * LANES, (jj + 1) * LANES)
        q = jnp.concatenate([q_ref[t, :, qsl] for t in range(steps)], axis=0)
        qs.append(_stack_heads(_head_norm_rope(q, g_ref[...], ct, sn, sp) * scale))
    q4 = jnp.concatenate(qs, axis=0)
    s_new = jnp.where(valid_new, _dot_nt(q4.astype(BF16), kn), NEG_BIG)
    s_old = None
    for b in range(0, bb, 2):
        lhs = jnp.concatenate([jnp.where(bq == b + i, q4, 0.0).astype(BF16) for i in range(2)], axis=1)
        k_t = [kc_ref[b + i].astype(BF16) for i in range(2)]
        d = _dot(lhs, jnp.concatenate([k_t[0], k_t[0], k_t[1], k_t[1]], axis=0))
        s_old = d if s_old is None else s_old + d
    s_old = jnp.where(valid_old, s_old, NEG_BIG)
    sink = sink_ref[layer, G * kvh + G - 1]
    for i in range(G - 2, -1, -1):
        sink = jnp.where(hh == i, sink_ref[layer, G * kvh + i], sink)
    m = jnp.maximum(jnp.maximum(jnp.max(s_new, axis=-1, keepdims=True),
                                jnp.max(s_old, axis=-1, keepdims=True)), sink)
    e_new = jnp.exp(s_new - m)
    e_old = jnp.exp(s_old - m)
    den = jnp.sum(e_new, axis=-1, keepdims=True) + jnp.sum(e_old, axis=-1, keepdims=True) + jnp.exp(sink - m)
    p_old = e_old / den
    o = _dot((e_new / den).astype(BF16), vn)
    for b in range(0, bb, 2):
        lhs = jnp.concatenate([jnp.where(bq == b + i, p_old, 0.0).astype(BF16) for i in range(2)], axis=1)
        v_t = [vc_ref[b + i].astype(BF16) for i in range(2)]
        rhs = jnp.concatenate([jnp.concatenate([v_t[i], v_t[i]], axis=0) for i in range(2)], axis=1)
        o = o + _dot_nt(lhs, rhs)
    for jj in range(G // 2):
        o_j = _unstack_heads(o[2 * jj * R:(2 * jj + 2) * R])
        for t in range(steps):
            y_ref[t, :, jj * LANES:(jj + 1) * LANES] = o_j[t * bb:(t + 1) * bb]


def swa_sample(proj3, khat3, cache_kt, cache_vt, sinks, g2, tables, layer, bb):
    steps, B, _ = proj3.shape
    R = steps * bb
    qw = MIX_W // SWA_KV_HEADS
    tab = pl.BlockSpec((R, LANES), lambda g, h: (0, 0))
    cache = pl.BlockSpec((None, bb, None, SWA_HEAD_DIM, WINDOW), lambda g, h: (layer, g, h, 0, 0))
    return pl.pallas_call(
        functools.partial(_swa_sample_kernel, layer=layer, steps=steps, bb=bb),
        grid=(B // bb, SWA_KV_HEADS),
        in_specs=[
            pl.BlockSpec(memory_space=pltpu.SMEM),
            pl.BlockSpec((steps, bb, qw), lambda g, h: (0, g, COL_SQ // qw + h)),
            pl.BlockSpec((steps, bb, LANES), lambda g, h: (0, g, h // 2)),
            pl.BlockSpec((steps, bb, LANES), lambda g, h: (0, g, COL_SV // LANES + h // 2)),
            cache, cache,
            pl.BlockSpec((1, LANES), lambda g, h: (0, 0)), tab, tab, tab,
        ],
        out_specs=pl.BlockSpec((steps, bb, qw), lambda g, h: (0, g, h)),
        out_shape=jax.ShapeDtypeStruct((steps, B, MIX_W), F32),
        compiler_params=_params("parallel", "arbitrary"),
        name="swa_sample",
    )(sinks, proj3, khat3, proj3, cache_kt, cache_vt, g2, *tables)


def _mem_kv_kernel(x_ref, g_ref, w_ref, kg_ref, o_ref, xn_ref):
    j = pl.program_id(0)

    @pl.when(j == 0)
    def _():
        x = x_ref[...]
        ms = jnp.mean(x * x, axis=-1, keepdims=True)
        xn_ref[...] = (x * lax.rsqrt(ms + EPS) * g_ref[...]).astype(BF16)

    y = _dot(xn_ref[...], w_ref[...].astype(BF16))

    @pl.when(j < MEM_HEADS)
    def _():
        ms = jnp.mean(y * y, axis=-1, keepdims=True)
        o_ref[...] = y * lax.rsqrt(ms + EPS) * kg_ref[...]

    @pl.when(j >= MEM_HEADS)
    def _():
        o_ref[...] = y


def mem_kv(mem, mem_norm_g, w_mem_kv, mem_knorm_g, layer):
    M, K = mem.shape
    hd = MEM_HEAD_DIM
    return pl.pallas_call(
        _mem_kv_kernel,
        grid=(2 * MEM_HEADS,),
        in_specs=[
            pl.BlockSpec((M, K), lambda j: (0, 0)),
            pl.BlockSpec((None, 1, K), lambda j: (layer, 0, 0)),
            pl.BlockSpec((None, K, hd), lambda j: (layer, 0, j)),
            pl.BlockSpec((None, 1, hd), lambda j: (layer, 0, 0)),
        ],
        out_specs=pl.BlockSpec((M, hd), lambda j: (0, j)),
        out_shape=jax.ShapeDtypeStruct((M, 2 * MIX_W), F32),
        scratch_shapes=[pltpu.VMEM((M, K), BF16)],
        compiler_params=_params("arbitrary"),
        name="mem_kv",
    )(mem, mem_norm_g, w_mem_kv, mem_knorm_g)


def _mem_qnorm(q, g):
    ms = jnp.mean(q * q, axis=-1, keepdims=True)
    return q * lax.rsqrt(ms + EPS) * g * (MEM_HEAD_DIM ** -0.5)


def _softmax_rows(s):
    m = jnp.max(s, axis=-1, keepdims=True)
    e = jnp.exp(s - m)
    return e / jnp.sum(e, axis=-1, keepdims=True)


def _mem_prompt_kernel(q0_ref, q1_ref, q2_ref, q3_ref, kv_ref, g_ref, y_ref):
    hd = MEM_HEAD_DIM
    for h, q_ref in enumerate((q0_ref, q1_ref, q2_ref, q3_ref)):
        q = _mem_qnorm(q_ref[...], g_ref[...]).astype(BF16)
        p = _softmax_rows(_dot_nt(q, kv_ref[:, h * hd:(h + 1) * hd].astype(BF16)))
        v = kv_ref[:, MIX_W + h * hd:MIX_W + (h + 1) * hd].astype(BF16)
        y_ref[:, h * hd:(h + 1) * hd] = _dot(p.astype(BF16), v).astype(BF16)


def mem_attn_prompt(proj, kv, mem_qnorm_g, layer, tq):
    T = proj.shape[0]
    hd = MEM_HEAD_DIM

    def q_spec(h):
        cb = COL_MQ // hd + h
        return pl.BlockSpec((tq, hd), lambda i: (i, cb))

    return pl.pallas_call(
        _mem_prompt_kernel,
        grid=(T // tq,),
        in_specs=[q_spec(h) for h in range(MEM_HEADS)] + [
            pl.BlockSpec((N_MEM, 2 * MIX_W), lambda i: (0, 0)),
            pl.BlockSpec((None, 1, hd), lambda i: (layer, 0, 0)),
        ],
        out_specs=pl.BlockSpec((tq, MIX_W), lambda i: (i, 0)),
        out_shape=jax.ShapeDtypeStruct((T, MIX_W), BF16),
        compiler_params=_params("arbitrary"),
        name="mem_prompt",
    )(proj, proj, proj, proj, kv, mem_qnorm_g)


def _mem_rows_view(c):
    L_, B_, M, H, hd = c.shape
    c = c.reshape(L_, B_, M, H, hd // LANES, LANES)
    return jnp.transpose(c, (0, 1, 2, 4, 3, 5)).reshape(L_, B_, M * H * (hd // LANES), LANES)


def _mem_head(c_ref, b, h):
    nt = MEM_HEAD_DIM // LANES
    parts = [c_ref[b, pl.ds(lt * MEM_HEADS + h, N_MEM, stride=nt * MEM_HEADS), :] for lt in range(nt)]
    return jnp.concatenate(parts, axis=1).astype(BF16)


def _mem_sample_kernel(q0_ref, q1_ref, q2_ref, q3_ref, k_ref, v_ref, g_ref, y_ref, *, steps, bb):
    R = steps * bb
    hd = MEM_HEAD_DIM
    bq = lax.broadcasted_iota(jnp.int32, (R, 1), 0) % bb
    for h, q_ref in enumerate((q0_ref, q1_ref, q2_ref, q3_ref)):
        q = _mem_qnorm(jnp.concatenate([q_ref[t] for t in range(steps)], axis=0), g_ref[...])
        s = None
        for b in range(bb):
            d = _dot_nt(jnp.where(bq == b, q, 0.0).astype(BF16), _mem_head(k_ref, b, h))
            s = d if s is None else s + d
        p = _softmax_rows(s)
        o = None
        for b in range(bb):
            d = _dot(jnp.where(bq == b, p, 0.0).astype(BF16), _mem_head(v_ref, b, h))
            o = d if o is None else o + d
        for t in range(steps):
            y_ref[t, :, h * hd:(h + 1) * hd] = o[t * bb:(t + 1) * bb]


def mem_attn_sample(proj3, cache_k, cache_v, mem_qnorm_g, layer, bb):
    steps, B, _ = proj3.shape
    hd = MEM_HEAD_DIM
    cache = pl.BlockSpec((None, bb) + cache_k.shape[2:], lambda g: (layer, g, 0, 0))

    def q_spec(h):
        cb = COL_MQ // hd + h
        return pl.BlockSpec((steps, bb, hd), lambda g: (0, g, cb))

    return pl.pallas_call(
        functools.partial(_mem_sample_kernel, steps=steps, bb=bb),
        grid=(B // bb,),
        in_specs=[q_spec(h) for h in range(MEM_HEADS)] + [
            cache, cache,
            pl.BlockSpec((None, 1, hd), lambda g: (layer, 0, 0)),
        ],
        out_specs=pl.BlockSpec((steps, bb, MIX_W), lambda g: (0, g, 0)),
        out_shape=jax.ShapeDtypeStruct((steps, B, MIX_W), F32),
        compiler_params=_params("arbitrary"),
        name="mem_sample",
    )(proj3, proj3, proj3, proj3, cache_k, cache_v, mem_qnorm_g)


def _row_tile(T, cap):
    t = cap
    while T % t:
        t //= 2
    return t


TM_STREAM = 2048
TM_DOWN = 1024
TM_LOCAL = 512
TM_MERGE = 512
TM_RESIDENT = 256
TN_STREAM = 512
TN_DOWN = 256
UP_PIECE_ROWS = 1024
HGRN_ROWS = 2048
HGRN_BATCH_UNROLL = 16
SWA_BLOCKS = 8
POOL_BATCH = 64
ATTN_BATCH = 8


def _token_tail(x, xn, ys, layer, w_in, w_branch, w_o, norm2_g):
    T = x.shape[0]
    merged = merge_branches(xn, ys, w_in, w_branch, layer, _row_tile(T, TM_MERGE), TN_STREAM)
    return matmul_res_norm(merged, w_o, layer, x, norm2_g, _row_tile(T, TM_RESIDENT))


def kernel(x_prompt, x_sample, mem_prompt, state_hgrn, cache_pool, cache_swa_k, cache_swa_v, state_conv, cache_mem_k, cache_mem_v, norm1_g, w_in, hgrn_lb, hgrn_norm_g, pool_w, pool_scale, swa_qnorm_g, swa_knorm_g, swa_sinks, mem_norm_g, w_mem_kv, mem_qnorm_g, mem_knorm_g, w_branch, w_o, norm2_g, w_up, conv_w, conv_b, w_down):
    depth = w_in.shape[0]
    bp, L, _ = x_prompt.shape
    B, steps, _ = x_sample.shape
    assert bp == 1
    kw = SWA_KV_HEADS * SWA_HEAD_DIM
    Ts = steps * B

    lb_all = jnp.cumsum(jax.nn.softmax(hgrn_lb.astype(F32), axis=0), axis=0)
    lb_all = lb_all - lb_all[:1]

    tab_p = _rope_tables(np.arange(L))
    tab_s = _rope_tables(np.repeat(PAST_LEN + np.arange(steps), B))
    tab_sb = _rope_tables(np.repeat(PAST_LEN + np.arange(steps), ATTN_BATCH))

    xp = x_prompt.reshape(L, D_MODEL)
    xs = jnp.transpose(x_sample, (1, 0, 2)).reshape(Ts, D_MODEL)
    mem = mem_prompt.reshape(N_MEM, D_MODEL)
    ckt_all = jnp.transpose(cache_swa_k, (0, 1, 3, 4, 2))
    cvt_all = jnp.transpose(cache_swa_v, (0, 1, 3, 4, 2))
    mk_rows = _mem_rows_view(cache_mem_k)
    mv_rows = _mem_rows_view(cache_mem_v)
    cpool_v = jnp.transpose(cache_pool, (0, 2, 1, 3))
    row3 = lambda a: a.reshape(depth, 1, a.shape[-1])
    norm1_g, norm2_g, pool_scale, conv_b = row3(norm1_g), row3(norm2_g), row3(pool_scale), row3(conv_b)
    mem_norm_g, mem_qnorm_g, mem_knorm_g = row3(mem_norm_g), row3(mem_qnorm_g), row3(mem_knorm_g)
    tm_p = _row_tile(L, TM_STREAM)
    tl_p = _row_tile(L, TM_LOCAL)

    outs = {k: [] for k in ("sp", "pp", "ps", "kp", "ks", "vp", "vs", "cp", "mk", "mv")}
    hgrn_states = None
    conv_states = None
    for l in range(depth):
        lb = lb_all[l].reshape(1, MIX_W)
        gn = hgrn_norm_g[l].reshape(1, MIX_W)
        gq2 = jnp.tile(swa_qnorm_g[l], 2).reshape(1, LANES)
        gk2 = jnp.tile(swa_knorm_g[l], 2).reshape(1, LANES)

        kv = mem_kv(mem, mem_norm_g, w_mem_kv, mem_knorm_g, l)

        xn = prenorm(xp, norm1_g, l, tl_p)
        proj = matmul_cols(xn, w_in, l, COL_GATE, tm_p, TN_STREAM)
        ya, s_p = hgrn_prompt(proj, lb, gn, _row_tile(L, HGRN_ROWS))
        yb = pool_prompt(proj, pool_w, pool_scale, l, tl_p)
        khat = swa_kprep(proj, gk2, tab_p, tl_p)
        yc = swa_prompt(proj, khat, swa_sinks, gq2, tab_p, l, _row_tile(L, SWA_BLOCKS * WINDOW) // WINDOW)
        ym = mem_attn_prompt(proj, kv, mem_qnorm_g, l, tl_p)
        h, hn = _token_tail(xp, xn, (ya, yb, yc, ym), l, w_in, w_branch, w_o, norm2_g)
        gact, a_tail = up_conv_prompt(hn, w_up, conv_w, conv_b, l, tm_p, TN_STREAM)
        xp = matmul_res(gact, w_down, l, h, _row_tile(L, TM_DOWN), TN_DOWN)

        outs["sp"].append(s_p[None])
        outs["pp"].append(proj[None, L - POOL_BUF:, COL_POOL:COL_POOL + MIX_W])
        outs["kp"].append(khat[None, L - WINDOW:].reshape(1, WINDOW, SWA_KV_HEADS, SWA_HEAD_DIM))
        outs["vp"].append(proj[None, L - WINDOW:, COL_SV:COL_SV + kw].reshape(1, WINDOW, SWA_KV_HEADS, SWA_HEAD_DIM))
        outs["cp"].append(a_tail[-1:, CONV_HIST - 2:])
        outs["mk"].append(kv[None, :, :MIX_W].reshape(1, N_MEM, MEM_HEADS, MEM_HEAD_DIM))
        outs["mv"].append(kv[None, :, MIX_W:].reshape(1, N_MEM, MEM_HEADS, MEM_HEAD_DIM))

        xn = prenorm(xs, norm1_g, l, Ts)
        proj_s = matmul_cols(xn, w_in, l, COL_GATE, Ts, TN_STREAM)
        proj3 = proj_s.reshape(steps, B, COL_GATE)
        ya, hgrn_states = hgrn_sample(proj3, lb, gn, state_hgrn, l, LANES // steps, hgrn_states)
        yb = pool_sample(proj3, cpool_v, pool_w, pool_scale, l, POOL_BATCH)
        khat_s = swa_kprep(proj_s, gk2, tab_s, Ts)
        khat3 = khat_s.reshape(steps, B, kw)
        yc = swa_sample(proj3, khat3, ckt_all, cvt_all, swa_sinks, gq2, tab_sb, l, ATTN_BATCH)
        ym = mem_attn_sample(proj3, mk_rows, mv_rows, mem_qnorm_g, l, ATTN_BATCH)
        ys = tuple(y.reshape(Ts, MIX_W).astype(BF16) for y in (ya, yb, yc, ym))
        h, hn = _token_tail(xs, xn, ys, l, w_in, w_branch, w_o, norm2_g)
        gact, conv_states = up_conv_sample(hn, w_up, state_conv, conv_w, conv_b, l, steps, TN_STREAM, conv_states)
        xs = matmul_res(gact, w_down, l, h, Ts, TN_DOWN)

        outs["ps"].append(proj3[:, :, COL_POOL:COL_POOL + MIX_W])
        to_window_minor = lambda a: jnp.transpose(a.reshape(steps, B, SWA_KV_HEADS, SWA_HEAD_DIM), (1, 2, 3, 0))
        outs["ks"].append(to_window_minor(khat3))
        outs["vs"].append(to_window_minor(proj3[:, :, COL_SV:COL_SV + kw]))

    stk = lambda k: jnp.stack(outs[k], axis=0)
    pool_s = jnp.transpose(jnp.concatenate([cpool_v[:, steps:], stk("ps")], axis=1), (0, 2, 1, 3))
    def slide_window(old, new):
        lead = [(0, 0, 0)] * (old.ndim - 1)
        shifted = lax.pad(old, jnp.zeros((), old.dtype), lead + [(-steps, steps, 0)])
        tail = lax.pad(new, jnp.zeros((), old.dtype), lead + [(WINDOW - steps, 0, 0)])
        pos = lax.broadcasted_iota(jnp.int32, old.shape, old.ndim - 1)
        return jnp.transpose(jnp.where(pos < WINDOW - steps, shifted, tail), (0, 1, 4, 2, 3))

    swa_k_s = slide_window(ckt_all, stk("ks"))
    swa_v_s = slide_window(cvt_all, stk("vs"))
    y_prompt = xp.reshape(1, L, D_MODEL)
    y_sample = jnp.transpose(xs.reshape(steps, B, D_MODEL), (1, 0, 2))
    return (y_prompt, y_sample,
            stk("sp"), hgrn_states, stk("pp"), pool_s, stk("kp"), swa_k_s, stk("vp"), swa_v_s,
            stk("cp"), conv_states, jnp.concatenate(outs["mk"], axis=0)[:, None], jnp.concatenate(outs["mv"], axis=0)[:, None])
```

```python
import functools

import numpy as np
import jax
import jax.numpy as jnp
from jax import lax
from jax.experimental import pallas as pl
from jax.experimental.pallas import tpu as pltpu

F32 = jnp.float32
BF16 = jnp.bfloat16

D_MODEL = 2048
MIX_W = D_MODEL // 2
N_BRANCH = 4
A_DK = 128
A_HEADS = MIX_W // A_DK
POOL_WINDOWS = (2, 4, 8, 16)
POOL_GC = MIX_W // len(POOL_WINDOWS)
POOL_BUF = max(POOL_WINDOWS) - 1
SWA_HEAD_DIM = 64
SWA_Q_HEADS = MIX_W // SWA_HEAD_DIM
SWA_KV_HEADS = SWA_Q_HEADS // 4
WINDOW = 128
ROT_DIM = SWA_HEAD_DIM // 4
ROPE_THETA = 500000.0
N_MEM = 256
MEM_HEADS = 4
MEM_HEAD_DIM = MIX_W // MEM_HEADS
D_FF = 11 * D_MODEL // 4
EPS = 1e-6
PAST_LEN = 8192

COL_HQ, COL_HF, COL_HI, COL_HG = 0, MIX_W, 2 * MIX_W, 3 * MIX_W
COL_POOL = 4 * MIX_W
COL_SQ = 5 * MIX_W
COL_SK = 6 * MIX_W
COL_SV = COL_SK + SWA_KV_HEADS * SWA_HEAD_DIM
COL_MQ = COL_SV + SWA_KV_HEADS * SWA_HEAD_DIM
COL_GATE = COL_MQ + MIX_W
IN_COLS = COL_GATE + N_BRANCH * D_MODEL

LANES = 128
V7X_VMEM_BYTES = 64 * 1024 * 1024
VMEM_LIMIT = V7X_VMEM_BYTES * 7 // 8
HGRN_CHUNK = 128
NEG_BIG = -1e30
LOG2_E = 1.4426950408889634


def _params(*sem):
    return pltpu.CompilerParams(dimension_semantics=sem, vmem_limit_bytes=VMEM_LIMIT)


def _sigmoid(x):
    return 0.5 * jnp.tanh(0.5 * x) + 0.5


def _dot(a, b):
    return jnp.dot(a, b, preferred_element_type=F32)


def _dot_nt(a, b):
    return lax.dot_general(a, b, (((1,), (1,)), ((), ())), preferred_element_type=F32)


def _skip_ref(kernel_fn, idx):
    def wrapped(*refs):
        return kernel_fn(*refs[:idx], *refs[idx + 1:])
    return wrapped


def _layer_slab_call(make_kernel, in_specs, args, slab_out, layer, slab_block, slab_index, out_specs, **kw):
    n_layers = kw["out_shape"][-1].shape[0]
    if slab_out is None:
        spec = pl.BlockSpec((n_layers,) + slab_block, lambda *g: (0,) + slab_index(*g))
        return pl.pallas_call(make_kernel(layer), in_specs=in_specs, out_specs=list(out_specs) + [spec], **kw)(*args)
    spec = pl.BlockSpec((1,) + slab_block, lambda *g: (layer,) + slab_index(*g))
    idx = len(args)
    return pl.pallas_call(
        _skip_ref(make_kernel(0), idx),
        in_specs=list(in_specs) + [pl.BlockSpec(memory_space=pl.ANY)],
        out_specs=list(out_specs) + [spec],
        input_output_aliases={idx: len(kw["out_shape"]) - 1},
        **kw)(*args, slab_out)


def _zero_other_slabs(so_ref, own):
    for l in range(so_ref.shape[0]):
        if l != own:
            so_ref[l] = jnp.zeros(so_ref.shape[1:], so_ref.dtype)


def _rms_rows(x, g):
    ms = jnp.mean(x * x, axis=-1, keepdims=True)
    return x * lax.rsqrt(ms + EPS) * g


def _prenorm_kernel(x_ref, g_ref, o_ref):
    o_ref[...] = _rms_rows(x_ref[...], g_ref[...]).astype(BF16)


def prenorm(x, g, layer, tm):
    T, K = x.shape
    return pl.pallas_call(
        _prenorm_kernel,
        grid=(T // tm,),
        in_specs=[pl.BlockSpec((tm, K), lambda i: (i, 0)),
                  pl.BlockSpec((None, 1, K), lambda i: (layer, 0, 0))],
        out_specs=pl.BlockSpec((tm, K), lambda i: (i, 0)),
        out_shape=jax.ShapeDtypeStruct((T, K), BF16),
        compiler_params=_params("arbitrary"),
        name="prenorm",
    )(x, g)


def _matmul_kernel(a_ref, w_ref, o_ref):
    o_ref[...] = _dot(a_ref[...], w_ref[...].astype(BF16))


def matmul_cols(a, w, layer, n_cols, tm, tn):
    T, K = a.shape
    return pl.pallas_call(
        _matmul_kernel,
        grid=(T // tm, n_cols // tn),
        in_specs=[
            pl.BlockSpec((tm, K), lambda i, j: (i, 0)),
            pl.BlockSpec((None, K, tn), lambda i, j: (layer, 0, j)),
        ],
        out_specs=pl.BlockSpec((tm, tn), lambda i, j: (i, j)),
        out_shape=jax.ShapeDtypeStruct((T, n_cols), F32),
        compiler_params=_params("parallel", "arbitrary"),
        name="matmul_cols",
    )(a, w)


def _matmul_res_kernel(a_ref, w_ref, r_ref, o_ref):
    o_ref[...] = r_ref[...] + _dot(a_ref[...], w_ref[...].astype(BF16))


def matmul_res(a, w, layer, res, tm, tn):
    T, K = a.shape
    N = w.shape[2]
    return pl.pallas_call(
        _matmul_res_kernel,
        grid=(T // tm, N // tn),
        in_specs=[
            pl.BlockSpec((tm, K), lambda i, j: (i, 0)),
            pl.BlockSpec((None, K, tn), lambda i, j: (layer, 0, j)),
            pl.BlockSpec((tm, tn), lambda i, j: (i, j)),
        ],
        out_specs=pl.BlockSpec((tm, tn), lambda i, j: (i, j)),
        out_shape=jax.ShapeDtypeStruct((T, N), F32),
        compiler_params=_params("parallel", "arbitrary"),
        name="matmul_res",
    )(a, w, res)


def _res_norm_kernel(a_ref, w_ref, r_ref, g_ref, h_ref, hn_ref, w_s):
    @pl.when(pl.program_id(0) == 0)
    def _():
        w_s[...] = w_ref[...].astype(BF16)

    h = r_ref[...] + _dot(a_ref[...], w_s[...])
    h_ref[...] = h
    hn_ref[...] = _rms_rows(h, g_ref[...]).astype(BF16)


def matmul_res_norm(a, w, layer, res, g, tm):
    T, K = a.shape
    N = w.shape[2]
    return pl.pallas_call(
        _res_norm_kernel,
        grid=(T // tm,),
        in_specs=[
            pl.BlockSpec((tm, K), lambda i: (i, 0)),
            pl.BlockSpec((None, K, N), lambda i: (layer, 0, 0), pipeline_mode=pl.Buffered(1)),
            pl.BlockSpec((tm, N), lambda i: (i, 0)),
            pl.BlockSpec((None, 1, N), lambda i: (layer, 0, 0)),
        ],
        out_specs=[pl.BlockSpec((tm, N), lambda i: (i, 0)), pl.BlockSpec((tm, N), lambda i: (i, 0))],
        out_shape=[jax.ShapeDtypeStruct((T, N), F32), jax.ShapeDtypeStruct((T, N), BF16)],
        scratch_shapes=[pltpu.VMEM((K, N), BF16)],
        compiler_params=_params("arbitrary"),
        name="matmul_res_norm",
    )(a, w, res, g)


def _merge_kernel(xn_ref, ya_ref, yb_ref, yc_ref, ym_ref, wg0_ref, wg1_ref, wg2_ref, wg3_ref, wb_ref, o_ref,
                  wg_s, wb_s):
    @pl.when(pl.program_id(1) == 0)
    def _():
        for n, wg_ref in enumerate((wg0_ref, wg1_ref, wg2_ref, wg3_ref)):
            wg_s[n] = wg_ref[...].astype(BF16)
            wb_s[n] = wb_ref[n].astype(BF16)

    xn = xn_ref[...]
    acc = None
    for n, y_ref in enumerate((ya_ref, yb_ref, yc_ref, ym_ref)):
        t = _sigmoid(_dot(xn, wg_s[n])) * _dot(y_ref[...], wb_s[n])
        acc = t if acc is None else acc + t
    o_ref[...] = acc.astype(BF16)


def merge_branches(xn, ys, w_in, w_branch, layer, tm, tn):
    T = xn.shape[0]
    once = pl.Buffered(1) if T // tm > 1 else None
    y_spec = pl.BlockSpec((tm, MIX_W), lambda j, i: (i, 0))

    def gate_spec(n):
        off = (COL_GATE + n * D_MODEL) // tn
        return pl.BlockSpec((None, D_MODEL, tn), lambda j, i: (layer, 0, off + j), pipeline_mode=once)

    return pl.pallas_call(
        _merge_kernel,
        grid=(D_MODEL // tn, T // tm),
        in_specs=[pl.BlockSpec((tm, D_MODEL), lambda j, i: (i, 0))] + [y_spec] * 4
        + [gate_spec(n) for n in range(N_BRANCH)]
        + [pl.BlockSpec((None, N_BRANCH, MIX_W, tn), lambda j, i: (layer, 0, 0, j), pipeline_mode=once)],
        out_specs=pl.BlockSpec((tm, tn), lambda j, i: (i, j)),
        out_shape=jax.ShapeDtypeStruct((T, D_MODEL), BF16),
        scratch_shapes=[pltpu.VMEM((N_BRANCH, D_MODEL, tn), BF16), pltpu.VMEM((N_BRANCH, MIX_W, tn), BF16)],
        compiler_params=_params("arbitrary", "arbitrary"),
        name="merge_branches",
    )(xn, *ys, w_in, w_in, w_in, w_in, w_branch)


CONV_HIST = 8


def _gelu(x):
    return 0.5 * x * (1.0 + lax.erf(x * (2.0 ** -0.5)))


def _up_conv_prompt_kernel(xn_ref, wa_ref, wv_ref, cw_ref, cb_ref, g_ref, tail_ref, carry_ref, *, tm, rc):
    i, j = pl.program_id(0), pl.program_id(1)
    wa = wa_ref[...].astype(BF16)
    wv = wv_ref[...].astype(BF16)
    prev = jnp.where(i == 0, 0.0, carry_ref[j])
    row = lax.broadcasted_iota(jnp.int32, (rc, wa.shape[1]), 0)
    for c in range(tm // rc):
        sl = pl.ds(c * rc, rc)
        xn = xn_ref[sl, :]
        a = _dot(xn, wa)
        v = _dot(xn, wv)
        a1 = jnp.where(row == 0, prev[CONV_HIST - 1:CONV_HIST], pltpu.roll(a, 1, 0))
        a2 = jnp.where(row == 0, prev[CONV_HIST - 2:CONV_HIST - 1],
                       jnp.where(row == 1, prev[CONV_HIST - 1:CONV_HIST], pltpu.roll(a, 2, 0)))
        cc = cb_ref[...] + cw_ref[0:1, :] * a2 + cw_ref[1:2, :] * a1 + cw_ref[2:3, :] * a
        g_ref[sl, :] = (_gelu(cc) * v).astype(BF16)
        prev = a[rc - CONV_HIST:, :]
    carry_ref[j] = prev
    tail_ref[...] = prev


def up_conv_prompt(xn, w_up, conv_w, conv_b, layer, tm, tn):
    T, K = xn.shape
    nc = D_FF // tn
    return pl.pallas_call(
        functools.partial(_up_conv_prompt_kernel, tm=tm, rc=min(tm, UP_PIECE_ROWS)),
        grid=(T // tm, nc),
        in_specs=[
            pl.BlockSpec((tm, K), lambda i, j: (i, 0)),
            pl.BlockSpec((None, K, tn), lambda i, j: (layer, 0, j)),
            pl.BlockSpec((None, K, tn), lambda i, j: (layer, 0, nc + j)),
            pl.BlockSpec((None, 3, tn), lambda i, j: (layer, 0, j)),
            pl.BlockSpec((None, 1, tn), lambda i, j: (layer, 0, j)),
        ],
        out_specs=[
            pl.BlockSpec((tm, tn), lambda i, j: (i, j)),
            pl.BlockSpec((None, CONV_HIST, tn), lambda i, j: (i, 0, j)),
        ],
        out_shape=[
            jax.ShapeDtypeStruct((T, D_FF), BF16),
            jax.ShapeDtypeStruct((T // tm, CONV_HIST, D_FF), F32),
        ],
        scratch_shapes=[pltpu.VMEM((nc, CONV_HIST, tn), F32)],
        compiler_params=_params("arbitrary", "arbitrary"),
        name="up_conv_prompt",
    )(xn, w_up, w_up, conv_w, conv_b)


def _up_conv_sample_kernel(xn_ref, wa_ref, wv_ref, st_ref, cw_ref, cb_ref, g_ref, so_ref, *, steps, B, own):
    _zero_other_slabs(so_ref, own)
    xn = xn_ref[...]
    a = _dot(xn, wa_ref[...].astype(BF16))
    v = _dot(xn, wv_ref[...].astype(BF16))
    hist = [st_ref[:, 0, :], st_ref[:, 1, :]] + [a[t * B:(t + 1) * B] for t in range(steps)]
    for t in range(steps):
        c = cb_ref[...] + cw_ref[0:1, :] * hist[t] + cw_ref[1:2, :] * hist[t + 1] + cw_ref[2:3, :] * hist[t + 2]
        g_ref[t * B:(t + 1) * B, :] = (_gelu(c) * v[t * B:(t + 1) * B]).astype(BF16)
    so_ref[own, :, 0, :] = hist[steps]
    so_ref[own, :, 1, :] = hist[steps + 1]


def up_conv_sample(xn, w_up, state_conv, conv_w, conv_b, layer, steps, tn, state_out):
    T, K = xn.shape
    B = T // steps
    nc = D_FF // tn
    in_specs = [
        pl.BlockSpec((T, K), lambda j: (0, 0)),
        pl.BlockSpec((None, K, tn), lambda j: (layer, 0, j)),
        pl.BlockSpec((None, K, tn), lambda j: (layer, 0, nc + j)),
        pl.BlockSpec((None, B, 2, tn), lambda j: (layer, 0, 0, j)),
        pl.BlockSpec((None, 3, tn), lambda j: (layer, 0, j)),
        pl.BlockSpec((None, 1, tn), lambda j: (layer, 0, j)),
    ]
    return _layer_slab_call(
        lambda own: functools.partial(_up_conv_sample_kernel, steps=steps, B=B, own=own),
        in_specs, [xn, w_up, w_up, state_conv, conv_w, conv_b], state_out, layer,
        (B, 2, tn), lambda j: (0, 0, j),
        [pl.BlockSpec((T, tn), lambda j: (0, j))],
        grid=(nc,),
        out_shape=[
            jax.ShapeDtypeStruct((T, D_FF), BF16),
            jax.ShapeDtypeStruct(state_conv.shape, F32),
        ],
        compiler_params=_params("arbitrary"),
        name="up_conv_sample",
    )


def _hgrn_gates(q_in, z, lb):
    q = q_in * _sigmoid(q_in)
    log_sig = jnp.minimum(z, 0.0) - jnp.log(1.0 + jnp.exp(-jnp.abs(z)))
    a1 = jnp.log(lb)
    a2 = jnp.log1p(-lb) + log_sig
    log_f = jnp.maximum(a1, a2) + jnp.log(1.0 + jnp.exp(-jnp.abs(a1 - a2)))
    k = (1.0 - lb) * _sigmoid(-z)
    return q, log_f, k


def _hgrn_out(o, gate, gn):
    ms = jnp.mean(o * o, axis=-1, keepdims=True)
    return o * lax.rsqrt(ms + EPS) * gn * (gate * _sigmoid(gate))


def _cumsum_rows(x, tril):
    hi = x.astype(BF16)
    r1 = x - hi.astype(F32)
    mid = r1.astype(BF16)
    lo = (r1 - mid.astype(F32)).astype(BF16)
    return _dot(tril, hi) + _dot(tril, mid) + _dot(tril, lo)


def _block_row(x, blk, r):
    C = x.shape[0]
    x3 = x.reshape(C // blk, blk, LANES)
    return jnp.broadcast_to(x3[:, r:r + 1, :], (C // blk, blk, LANES)).reshape(C, LANES)


def _hgrn_pair_codes():
    t = np.arange(HGRN_CHUNK)[:, None]
    s = np.arange(HGRN_CHUNK)[None, :]
    level = np.floor(np.log2(np.maximum(t ^ s, 1))).astype(np.int32)
    return jnp.asarray(np.where(s > t, -1, np.where(s == t, 0, 1 + level)), jnp.int32)


def _boundary_row(b, m):
    if 2 * m >= 8:
        return _block_row(b, 2 * m, m - 1)
    r8 = lax.broadcasted_iota(jnp.int32, b.shape, 0) & 7
    if m == 2:
        return jnp.where(r8 < 4, _block_row(b, 8, 1), _block_row(b, 8, 5))
    return jnp.where((r8 & 1) == 0, b, pltpu.roll(b, 1, 0))


def _hgrn_attention(q, k, b, code):
    C = HGRN_CHUNK
    rowl = lax.broadcasted_iota(jnp.int32, (C, LANES), 0)
    att = jnp.where(code == 0, jnp.sum(q * k, axis=-1, keepdims=True), 0.0)
    m, level = 1, 1
    while m < C:
        d = b - _boundary_row(b, m)
        isq = (rowl & m) != 0
        x = (jnp.where(isq, q, k) * jnp.exp2(jnp.abs(d) * (-LOG2_E))).astype(BF16)
        att = jnp.where(code == level, _dot_nt(x, x), att)
        m *= 2
        level += 1
    return att


def _hgrn_apply(q, k, v, b, att, S):
    C = HGRN_CHUNK
    vb = v.astype(BF16)
    o = _dot((q * jnp.exp(b)).astype(BF16), S.astype(BF16)) + _dot(att.astype(BF16), vb)
    bl = b[C - 1:C, :]
    kk = k * jnp.exp(bl - b)
    ecol = jnp.transpose(jnp.broadcast_to(jnp.exp(bl), (LANES, LANES)))
    return o, ecol * S + _dot(jnp.transpose(kk).astype(BF16), vb)


def _hgrn_prompt_kernel(q_ref, f_ref, i_ref, g_ref, lb_ref, gn_ref, code_ref, y_ref, so_ref, s_ref, *, rows):
    @pl.when(pl.program_id(1) == 0)
    def _():
        s_ref[...] = jnp.zeros_like(s_ref)

    C = HGRN_CHUNK
    code = code_ref[...]
    tril = jnp.where(code >= 0, 1.0, 0.0).astype(BF16)
    q, log_f, k = _hgrn_gates(q_ref[...], f_ref[...], lb_ref[...])
    chunks = [slice(c * C, (c + 1) * C) for c in range(rows // C)]
    bs = [_cumsum_rows(log_f[sl], tril) for sl in chunks]
    atts = [_hgrn_attention(q[sl], k[sl], b, code) for sl, b in zip(chunks, bs)]
    S = s_ref[...]
    outs = []
    for sl, b, att in zip(chunks, bs, atts):
        o, S = _hgrn_apply(q[sl], k[sl], i_ref[sl, :], b, att, S)
        outs.append(o)
    s_ref[...] = S
    y_ref[...] = _hgrn_out(jnp.concatenate(outs, axis=0), g_ref[...], gn_ref[...]).astype(BF16)

    @pl.when(pl.program_id(1) == pl.num_programs(1) - 1)
    def _():
        so_ref[...] = s_ref[...]


def hgrn_prompt(proj, lb, gn, rows):
    T = proj.shape[0]

    def col(off):
        base = off // LANES
        return pl.BlockSpec((rows, LANES), lambda h, c: (c, base + h))

    vec = pl.BlockSpec((1, LANES), lambda h, c: (0, h))
    return pl.pallas_call(
        functools.partial(_hgrn_prompt_kernel, rows=rows),
        grid=(A_HEADS, T // rows),
        in_specs=[col(COL_HQ), col(COL_HF), col(COL_HI), col(COL_HG), vec, vec,
                  pl.BlockSpec((HGRN_CHUNK, HGRN_CHUNK), lambda h, c: (0, 0))],
        out_specs=[
            pl.BlockSpec((rows, LANES), lambda h, c: (c, h)),
            pl.BlockSpec((None, A_DK, LANES), lambda h, c: (h, 0, 0)),
        ],
        out_shape=[
            jax.ShapeDtypeStruct((T, MIX_W), BF16),
            jax.ShapeDtypeStruct((A_HEADS, A_DK, LANES), F32),
        ],
        scratch_shapes=[pltpu.VMEM((A_DK, LANES), F32)],
        compiler_params=_params("parallel", "arbitrary"),
        name="hgrn_prompt",
    )(proj, proj, proj, proj, lb, gn, _hgrn_pair_codes())


def _hgrn_sample_kernel(q_ref, f_ref, i_ref, g_ref, lb_ref, gn_ref, s_ref, y_ref, so_ref, *, steps, bb, own):
    _zero_other_slabs(so_ref, own)
    lb = lb_ref[...]
    gn = gn_ref[...]
    qs, ks, vs, bs = [], [], [], []
    b = None
    for t in range(steps):
        q, log_f, k = _hgrn_gates(q_ref[t], f_ref[t], lb)
        b = log_f if b is None else b + log_f
        qs.append(q)
        ks.append(k)
        vs.append(i_ref[t])
        bs.append(b)
    intra = []
    for t in range(steps):
        acc = None
        for s in range(t + 1):
            w = jnp.sum(qs[t] * ks[s] * jnp.exp(bs[t] - bs[s]), axis=-1, keepdims=True)
            acc = w * vs[s] if acc is None else acc + w * vs[s]
        intra.append(acc)
    R = steps * bb
    q_stack = jnp.concatenate([qs[t] * jnp.exp(bs[t]) for t in range(steps)], axis=0).astype(BF16)
    k_stack = jnp.concatenate([ks[t] * jnp.exp(bs[-1] - bs[t]) for t in range(steps)], axis=0)
    v_stack = jnp.concatenate(vs, axis=0).astype(BF16)
    k_t = jnp.transpose(k_stack)
    f_pad = jnp.concatenate([jnp.exp(bs[-1])] + [jnp.zeros((R - bb, LANES), F32)], axis=0)
    f_t = jnp.transpose(f_pad)
    rowi = lax.broadcasted_iota(jnp.int32, (R, LANES), 0) % bb
    lanei = lax.broadcasted_iota(jnp.int32, (LANES, R), 1)

    def body(bi, o_acc):
        s_b = s_ref[bi]
        o_acc = jnp.where(rowi == bi, _dot(q_stack, s_b.astype(BF16)), o_acc)
        f_col = jnp.sum(jnp.where(lanei == bi, f_t, 0.0), axis=-1, keepdims=True)
        k_b = jnp.where(lanei % bb == bi, k_t, 0.0).astype(BF16)
        so_ref[own, bi] = f_col * s_b + _dot(k_b, v_stack)
        return o_acc

    o_inter = lax.fori_loop(0, bb, body, jnp.zeros((R, LANES), F32), unroll=HGRN_BATCH_UNROLL)
    for t in range(steps):
        o = o_inter[t * bb:(t + 1) * bb] + intra[t]
        y_ref[t] = _hgrn_out(o, g_ref[t], gn).astype(BF16)


def hgrn_sample(proj3, lb, gn, state, layer, bb, state_out):
    steps, B, _ = proj3.shape
    assert steps * bb == LANES

    def col(off):
        base = off // LANES
        return pl.BlockSpec((steps, bb, LANES), lambda g, h: (0, g, base + h))

    vec = pl.BlockSpec((1, LANES), lambda g, h: (0, h))
    slab = pl.BlockSpec((None, bb, None, A_DK, LANES), lambda g, h: (layer, g, h, 0, 0))
    return _layer_slab_call(
        lambda own: functools.partial(_hgrn_sample_kernel, steps=steps, bb=bb, own=own),
        [col(COL_HQ), col(COL_HF), col(COL_HI), col(COL_HG), vec, vec, slab],
        [proj3, proj3, proj3, proj3, lb, gn, state], state_out, layer,
        (bb, None, A_DK, LANES), lambda g, h: (g, h, 0, 0),
        [pl.BlockSpec((steps, bb, LANES), lambda g, h: (0, g, h))],
        grid=(B // bb, A_HEADS),
        out_shape=[
            jax.ShapeDtypeStruct((steps, B, MIX_W), BF16),
            jax.ShapeDtypeStruct(state.shape, F32),
        ],
        compiler_params=_params("parallel", "arbitrary"),
        name="hgrn_sample",
    )


POOL_HIST = 32


def _pool_project(d, w_ref, sc_ref, g):
    sl = slice(g * POOL_GC, (g + 1) * POOL_GC)
    return _dot(d.astype(BF16), w_ref[g].astype(BF16)) * sc_ref[:, sl]


def _pool_prompt_kernel(u_ref, prev_ref, w_ref, sc_ref, y_ref, ext_ref, sum_ref, *, tm):
    i = pl.program_id(0)
    H = POOL_HIST
    ext_ref[0:H, :] = jnp.where(i == 0, 0.0, prev_ref[...])
    ext_ref[H:, :] = u_ref[...]
    sum_ref[0:H // 2, :] = jnp.zeros((H // 2, MIX_W), F32)
    pos = i * tm + lax.broadcasted_iota(jnp.int32, (tm, 1), 0)
    src, w = ext_ref, 1
    for g, win in enumerate(POOL_WINDOWS):
        lanes = slice(g * POOL_GC, MIX_W)
        while w < win:
            n = H // 2 + tm
            sum_ref[pl.ds(H // 2, n), lanes] = src[pl.ds(H // 2, n), lanes] + src[pl.ds(H // 2 - w, n), lanes]
            src, w = sum_ref, 2 * w
        sl = slice(g * POOL_GC, (g + 1) * POOL_GC)
        cnt = jnp.minimum(pos + 1, win).astype(F32)
        d = src[pl.ds(H, tm), sl] / cnt - u_ref[:, sl]
        y_ref[:, sl] = _pool_project(d, w_ref, sc_ref, g).astype(BF16)


def pool_prompt(proj, pool_w, pool_scale, layer, tm):
    T = proj.shape[0]
    cb = COL_POOL // MIX_W
    return pl.pallas_call(
        functools.partial(_pool_prompt_kernel, tm=tm),
        grid=(T // tm,),
        in_specs=[
            pl.BlockSpec((tm, MIX_W), lambda i: (i, cb)),
            pl.BlockSpec((POOL_HIST, MIX_W), lambda i: (jnp.maximum(i * (tm // POOL_HIST) - 1, 0), cb)),
            pl.BlockSpec((None, len(POOL_WINDOWS), POOL_GC, POOL_GC), lambda i: (layer, 0, 0, 0)),
            pl.BlockSpec((None, 1, MIX_W), lambda i: (layer, 0, 0)),
        ],
        out_specs=pl.BlockSpec((tm, MIX_W), lambda i: (i, 0)),
        out_shape=jax.ShapeDtypeStruct((T, MIX_W), BF16),
        scratch_shapes=[pltpu.VMEM((POOL_HIST + tm, MIX_W), F32), pltpu.VMEM((POOL_HIST + tm, MIX_W), F32)],
        compiler_params=_params("arbitrary"),
        name="pool_prompt",
    )(proj, proj, pool_w, pool_scale)


def _pool_sample_kernel(u_ref, c_ref, w_ref, sc_ref, y_ref, *, steps):
    for t in range(steps):
        for g, win in enumerate(POOL_WINDOWS):
            sl = slice(g * POOL_GC, (g + 1) * POOL_GC)
            acc = u_ref[t, :, sl]
            for j in range(1, win):
                if j <= t:
                    acc = acc + u_ref[t - j, :, sl]
                else:
                    acc = acc + c_ref[POOL_BUF + t - j, :, sl]
            d = acc / float(win) - u_ref[t, :, sl]
            y_ref[t, :, sl] = _pool_project(d, w_ref, sc_ref, g).astype(BF16)


def pool_sample(proj3, cache_pool, pool_w, pool_scale, layer, bb):
    steps, B, _ = proj3.shape
    cb = COL_POOL // MIX_W
    return pl.pallas_call(
        functools.partial(_pool_sample_kernel, steps=steps),
        grid=(B // bb,),
        in_specs=[
            pl.BlockSpec((steps, bb, MIX_W), lambda g: (0, g, cb)),
            pl.BlockSpec((None, POOL_BUF, bb, MIX_W), lambda g: (layer, 0, g, 0)),
            pl.BlockSpec((None, len(POOL_WINDOWS), POOL_GC, POOL_GC), lambda g: (layer, 0, 0, 0)),
            pl.BlockSpec((None, 1, MIX_W), lambda g: (layer, 0, 0)),
        ],
        out_specs=pl.BlockSpec((steps, bb, MIX_W), lambda g: (0, g, 0)),
        out_shape=jax.ShapeDtypeStruct((steps, B, MIX_W), BF16),
        compiler_params=_params("arbitrary"),
        name="pool_sample",
    )(proj3, cache_pool, pool_w, pool_scale)


def _rope_tables(positions):
    half = ROT_DIM // 2
    inv = np.power(ROPE_THETA, -np.arange(0, ROT_DIM, 2, dtype=np.float64) / ROT_DIM)
    ang = np.asarray(positions, np.float64)[:, None] * inv[None, :]
    cos, sin = np.cos(ang), np.sin(ang)
    n = len(positions)
    ct = np.ones((n, LANES))
    sn = np.zeros((n, LANES))
    sp = np.zeros((n, LANES))
    for base in (0, SWA_HEAD_DIM):
        ct[:, base:base + half] = cos
        ct[:, base + half:base + ROT_DIM] = cos
        sn[:, base:base + half] = -sin
        sp[:, base + half:base + ROT_DIM] = sin
    return tuple(jnp.asarray(t, F32) for t in (ct, sn, sp))


def _head_norm_rope(x, g, ct, sn, sp):
    lane = lax.broadcasted_iota(jnp.int32, x.shape, 1)
    lo = lane < SWA_HEAD_DIM
    x2 = x * x
    ms_lo = jnp.sum(jnp.where(lo, x2, 0.0), axis=-1, keepdims=True) / SWA_HEAD_DIM
    ms_hi = jnp.sum(jnp.where(lo, 0.0, x2), axis=-1, keepdims=True) / SWA_HEAD_DIM
    xn = x * jnp.where(lo, lax.rsqrt(ms_lo + EPS), lax.rsqrt(ms_hi + EPS)) * g
    half = ROT_DIM // 2
    return xn * ct + pltpu.roll(xn, LANES - half, 1) * sn + pltpu.roll(xn, half, 1) * sp


def _kprep_kernel(k_ref, g_ref, ct_ref, sn_ref, sp_ref, o_ref):
    ct, sn, sp = ct_ref[...], sn_ref[...], sp_ref[...]
    for j in range(2):
        sl = slice(j * LANES, (j + 1) * LANES)
        o_ref[:, sl] = _head_norm_rope(k_ref[:, sl], g_ref[...], ct, sn, sp)


def swa_kprep(proj, g2, tables, tm):
    T = proj.shape[0]
    kw = SWA_KV_HEADS * SWA_HEAD_DIM
    tab = pl.BlockSpec((tm, LANES), lambda i: (i, 0))
    return pl.pallas_call(
        _kprep_kernel,
        grid=(T // tm,),
        in_specs=[pl.BlockSpec((tm, kw), lambda i: (i, COL_SK // kw)),
                  pl.BlockSpec((1, LANES), lambda i: (0, 0)), tab, tab, tab],
        out_specs=pl.BlockSpec((tm, kw), lambda i: (i, 0)),
        out_shape=jax.ShapeDtypeStruct((T, kw), F32),
        compiler_params=_params("arbitrary"),
        name="swa_kprep",
    )(proj, g2, *tables)


def _dup_head(x, parity):
    lane = lax.broadcasted_iota(jnp.int32, x.shape, 1)
    return jnp.where(lane // SWA_HEAD_DIM == parity, x, pltpu.roll(x, SWA_HEAD_DIM, 1))


def _stack_heads(q):
    lane = lax.broadcasted_iota(jnp.int32, q.shape, 1)
    lo = lane < SWA_HEAD_DIM
    return jnp.concatenate([jnp.where(lo, q, 0.0), jnp.where(lo, 0.0, q)], axis=0)


def _unstack_heads(o2):
    R = o2.shape[0] // 2
    lane = lax.broadcasted_iota(jnp.int32, (R, LANES), 1)
    return jnp.where(lane < SWA_HEAD_DIM, o2[:R], o2[R:])


def _swa_prompt_kernel(sink_ref, q_ref, kc_ref, kp_ref, vc_ref, vp_ref, g_ref, ct_ref, sn_ref, sp_ref, y_ref,
                       *, layer, nb):
    first = pl.program_id(0) == 0
    W = WINDOW
    G = SWA_Q_HEADS // SWA_KV_HEADS
    r4 = lax.broadcasted_iota(jnp.int32, (G * W, 1), 0)
    hh = r4 // W
    ci = lax.broadcasted_iota(jnp.int32, (1, W), 1)
    cur = ci <= r4 % W
    scale = SWA_HEAD_DIM ** -0.5

    def head_blocks(cur_ref, prev_ref, kvh):
        ksl = slice((kvh // 2) * LANES, (kvh // 2 + 1) * LANES)
        x = _dup_head(jnp.concatenate([prev_ref[:, ksl], cur_ref[:, ksl]], axis=0), kvh % 2).astype(BF16)
        return [x[j * W:(j + 1) * W] for j in range(nb + 1)]

    scores, sinks = [], []
    for kvh in range(SWA_KV_HEADS):
        kb = head_blocks(kc_ref, kp_ref, kvh)
        sink = sink_ref[layer, G * kvh + G - 1]
        for i in range(G - 2, -1, -1):
            sink = jnp.where(hh == i, sink_ref[layer, G * kvh + i], sink)
        for blk in range(nb):
            rows = slice(blk * W, (blk + 1) * W)
            qs = []
            for jj in range(G // 2):
                qsl = slice((2 * kvh + jj) * LANES, (2 * kvh + jj + 1) * LANES)
                qn = _head_norm_rope(q_ref[rows, qsl], g_ref[...], ct_ref[rows, :], sn_ref[rows, :], sp_ref[rows, :])
                qs.append(_stack_heads(qn * scale))
            q = jnp.concatenate(qs, axis=0).astype(BF16)
            s_prev = _dot_nt(q, kb[blk])
            if blk == 0:
                s_prev = jnp.where(first, NEG_BIG, s_prev)
            scores.append(jnp.where(cur, _dot_nt(q, kb[blk + 1]), s_prev))
            sinks.append(sink)
    probs = []
    for s, sink in zip(scores, sinks):
        m = jnp.maximum(jnp.max(s, axis=-1, keepdims=True), sink)
        e = jnp.exp(s - m)
        probs.append(e * (1.0 / (jnp.sum(e, axis=-1, keepdims=True) + jnp.exp(sink - m))))
    for kvh in range(SWA_KV_HEADS):
        vb = head_blocks(vc_ref, vp_ref, kvh)
        for blk in range(nb):
            p = probs[kvh * nb + blk]
            o = (_dot(jnp.where(cur, p, 0.0).astype(BF16), vb[blk + 1])
                 + _dot(jnp.where(cur, 0.0, p).astype(BF16), vb[blk]))
            for jj in range(G // 2):
                qsl = slice((2 * kvh + jj) * LANES, (2 * kvh + jj + 1) * LANES)
                y_ref[blk * W:(blk + 1) * W, qsl] = _unstack_heads(o[2 * jj * W:(2 * jj + 2) * W]).astype(BF16)


def swa_prompt(proj, khat, sinks, g2, tables, layer, nb):
    T = proj.shape[0]
    W = WINDOW
    tq = nb * W
    kw = SWA_KV_HEADS * SWA_HEAD_DIM
    tab = pl.BlockSpec((tq, LANES), lambda i: (i, 0))
    prev = lambda i: jnp.maximum(i * nb - 1, 0)
    return pl.pallas_call(
        functools.partial(_swa_prompt_kernel, layer=layer, nb=nb),
        grid=(T // tq,),
        in_specs=[
            pl.BlockSpec(memory_space=pltpu.SMEM),
            pl.BlockSpec((tq, MIX_W), lambda i: (i, COL_SQ // MIX_W)),
            pl.BlockSpec((tq, kw), lambda i: (i, 0)),
            pl.BlockSpec((W, kw), lambda i: (prev(i), 0)),
            pl.BlockSpec((tq, kw), lambda i: (i, COL_SV // kw)),
            pl.BlockSpec((W, kw), lambda i: (prev(i), COL_SV // kw)),
            pl.BlockSpec((1, LANES), lambda i: (0, 0)), tab, tab, tab,
        ],
        out_specs=pl.BlockSpec((tq, MIX_W), lambda i: (i, 0)),
        out_shape=jax.ShapeDtypeStruct((T, MIX_W), BF16),
        compiler_params=_params("arbitrary"),
        name="swa_prompt",
    )(sinks, proj, khat, khat, proj, proj, g2, *tables)


def _swa_sample_kernel(sink_ref, q_ref, kn_ref, vn_ref, kc_ref, vc_ref, g_ref, ct_ref, sn_ref, sp_ref, y_ref,
                       *, layer, steps, bb):
    kvh = pl.program_id(1)
    parity = kvh % 2
    W = WINDOW
    R = steps * bb
    G = SWA_Q_HEADS // SWA_KV_HEADS
    ct, sn, sp = ct_ref[...], sn_ref[...], sp_ref[...]
    scale = SWA_HEAD_DIM ** -0.5
    r4 = lax.broadcasted_iota(jnp.int32, (G * R, 1), 0)
    hh = r4 // R
    tq = (r4 % R) // bb
    bq = r4 % bb
    c_new = lax.broadcasted_iota(jnp.int32, (1, R), 1)
    valid_new = (c_new % bb == bq) & (c_new // bb <= tq)
    c_old = lax.broadcasted_iota(jnp.int32, (1, W), 1)
    valid_old = c_old > tq
    kn = _dup_head(jnp.concatenate([kn_ref[t] for t in range(steps)], axis=0), parity).astype(BF16)
    vn = _dup_head(jnp.concatenate([vn_ref[t] for t in range(steps)], axis=0), parity).astype(BF16)
    qs = []
    for jj in range(G // 2):
        qsl = slice(jj * LANES, (jj + 1) * LANES)
        q = jnp.concatenate([q_ref[t, :, qsl] for t in range(steps)], axis=0)
        qs.append(_stack_heads(_head_norm_rope(q, g_ref[...], ct, sn, sp) * scale))
    q4 = jnp.concatenate(qs, axis=0)
    s_new = jnp.where(valid_new, _dot_nt(q4.astype(BF16), kn), NEG_BIG)
    s_old = None
    for b in range(0, bb, 2):
        lhs = jnp.concatenate([jnp.where(bq == b + i, q4, 0.0).astype(BF16) for i in range(2)], axis=1)
        k_t = [kc_ref[b + i].astype(BF16) for i in range(2)]
        d = _dot(lhs, jnp.concatenate([k_t[0], k_t[0], k_t[1], k_t[1]], axis=0))
        s_old = d if s_old is None else s_old + d
    s_old = jnp.where(valid_old, s_old, NEG_BIG)
    sink = sink_ref[layer, G * kvh + G - 1]
    for i in range(G - 2, -1, -1):
        sink = jnp.where(hh == i, sink_ref[layer, G * kvh + i], sink)
    m = jnp.maximum(jnp.maximum(jnp.max(s_new, axis=-1, keepdims=True),
                                jnp.max(s_old, axis=-1, keepdims=True)), sink)
    e_new = jnp.exp(s_new - m)
    e_old = jnp.exp(s_old - m)
    den = jnp.sum(e_new, axis=-1, keepdims=True) + jnp.sum(e_old, axis=-1, keepdims=True) + jnp.exp(sink - m)
    p_old = e_old / den
    o = _dot((e_new / den).astype(BF16), vn)
    for b in range(0, bb, 2):
        lhs = jnp.concatenate([jnp.where(bq == b + i, p_old, 0.0).astype(BF16) for i in range(2)], axis=1)
        v_t = [vc_ref[b + i].astype(BF16) for i in range(2)]
        rhs = jnp.concatenate([jnp.concatenate([v_t[i], v_t[i]], axis=0) for i in range(2)], axis=1)
        o = o + _dot_nt(lhs, rhs)
    for jj in range(G // 2):
        o_j = _unstack_heads(o[2 * jj * R:(2 * jj + 2) * R])
        for t in range(steps):
            y_ref[t, :, jj * LANES:(jj + 1) * LANES] = o_j[t * bb:(t + 1) * bb]


def swa_sample(proj3, khat3, cache_kt, cache_vt, sinks, g2, tables, layer, bb):
    steps, B, _ = proj3.shape
    R = steps * bb
    qw = MIX_W // SWA_KV_HEADS
    tab = pl.BlockSpec((R, LANES), lambda g, h: (0, 0))
    cache = pl.BlockSpec((None, bb, None, SWA_HEAD_DIM, WINDOW), lambda g, h: (layer, g, h, 0, 0))
    return pl.pallas_call(
        functools.partial(_swa_sample_kernel, layer=layer, steps=steps, bb=bb),
        grid=(B // bb, SWA_KV_HEADS),
        in_specs=[
            pl.BlockSpec(memory_space=pltpu.SMEM),
            pl.BlockSpec((steps, bb, qw), lambda g, h: (0, g, COL_SQ // qw + h)),
            pl.BlockSpec((steps, bb, LANES), lambda g, h: (0, g, h // 2)),
            pl.BlockSpec((steps, bb, LANES), lambda g, h: (0, g, COL_SV // LANES + h // 2)),
            cache, cache,
            pl.BlockSpec((1, LANES), lambda g, h: (0, 0)), tab, tab, tab,
        ],
        out_specs=pl.BlockSpec((steps, bb, qw), lambda g, h: (0, g, h)),
        out_shape=jax.ShapeDtypeStruct((steps, B, MIX_W), F32),
        compiler_params=_params("parallel", "arbitrary"),
        name="swa_sample",
    )(sinks, proj3, khat3, proj3, cache_kt, cache_vt, g2, *tables)


def _mem_kv_kernel(x_ref, g_ref, w_ref, kg_ref, o_ref, xn_ref):
    j = pl.program_id(0)

    @pl.when(j == 0)
    def _():
        x = x_ref[...]
        ms = jnp.mean(x * x, axis=-1, keepdims=True)
        xn_ref[...] = (x * lax.rsqrt(ms + EPS) * g_ref[...]).astype(BF16)

    y = _dot(xn_ref[...], w_ref[...].astype(BF16))

    @pl.when(j < MEM_HEADS)
    def _():
        ms = jnp.mean(y * y, axis=-1, keepdims=True)
        o_ref[...] = y * lax.rsqrt(ms + EPS) * kg_ref[...]

    @pl.when(j >= MEM_HEADS)
    def _():
        o_ref[...] = y


def mem_kv(mem, mem_norm_g, w_mem_kv, mem_knorm_g, layer):
    M, K = mem.shape
    hd = MEM_HEAD_DIM
    return pl.pallas_call(
        _mem_kv_kernel,
        grid=(2 * MEM_HEADS,),
        in_specs=[
            pl.BlockSpec((M, K), lambda j: (0, 0)),
            pl.BlockSpec((None, 1, K), lambda j: (layer, 0, 0)),
            pl.BlockSpec((None, K, hd), lambda j: (layer, 0, j)),
            pl.BlockSpec((None, 1, hd), lambda j: (layer, 0, 0)),
        ],
        out_specs=pl.BlockSpec((M, hd), lambda j: (0, j)),
        out_shape=jax.ShapeDtypeStruct((M, 2 * MIX_W), F32),
        scratch_shapes=[pltpu.VMEM((M, K), BF16)],
        compiler_params=_params("arbitrary"),
        name="mem_kv",
    )(mem, mem_norm_g, w_mem_kv, mem_knorm_g)


def _mem_qnorm(q, g):
    ms = jnp.mean(q * q, axis=-1, keepdims=True)
    return q * lax.rsqrt(ms + EPS) * g * (MEM_HEAD_DIM ** -0.5)


def _softmax_rows(s):
    m = jnp.max(s, axis=-1, keepdims=True)
    e = jnp.exp(s - m)
    return e / jnp.sum(e, axis=-1, keepdims=True)


def _mem_prompt_kernel(q0_ref, q1_ref, q2_ref, q3_ref, kv_ref, g_ref, y_ref):
    hd = MEM_HEAD_DIM
    for h, q_ref in enumerate((q0_ref, q1_ref, q2_ref, q3_ref)):
        q = _mem_qnorm(q_ref[...], g_ref[...]).astype(BF16)
        p = _softmax_rows(_dot_nt(q, kv_ref[:, h * hd:(h + 1) * hd].astype(BF16)))
        v = kv_ref[:, MIX_W + h * hd:MIX_W + (h + 1) * hd].astype(BF16)
        y_ref[:, h * hd:(h + 1) * hd] = _dot(p.astype(BF16), v).astype(BF16)


def mem_attn_prompt(proj, kv, mem_qnorm_g, layer, tq):
    T = proj.shape[0]
    hd = MEM_HEAD_DIM

    def q_spec(h):
        cb = COL_MQ // hd + h
        return pl.BlockSpec((tq, hd), lambda i: (i, cb))

    return pl.pallas_call(
        _mem_prompt_kernel,
        grid=(T // tq,),
        in_specs=[q_spec(h) for h in range(MEM_HEADS)] + [
            pl.BlockSpec((N_MEM, 2 * MIX_W), lambda i: (0, 0)),
            pl.BlockSpec((None, 1, hd), lambda i: (layer, 0, 0)),
        ],
        out_specs=pl.BlockSpec((tq, MIX_W), lambda i: (i, 0)),
        out_shape=jax.ShapeDtypeStruct((T, MIX_W), BF16),
        compiler_params=_params("arbitrary"),
        name="mem_prompt",
    )(proj, proj, proj, proj, kv, mem_qnorm_g)


def _mem_rows_view(c):
    L_, B_, M, H, hd = c.shape
    c = c.reshape(L_, B_, M, H, hd // LANES, LANES)
    return jnp.transpose(c, (0, 1, 2, 4, 3, 5)).reshape(L_, B_, M * H * (hd // LANES), LANES)


def _mem_head(c_ref, b, h):
    nt = MEM_HEAD_DIM // LANES
    parts = [c_ref[b, pl.ds(lt * MEM_HEADS + h, N_MEM, stride=nt * MEM_HEADS), :] for lt in range(nt)]
    return jnp.concatenate(parts, axis=1).astype(BF16)


def _mem_sample_kernel(q0_ref, q1_ref, q2_ref, q3_ref, k_ref, v_ref, g_ref, y_ref, *, steps, bb):
    R = steps * bb
    hd = MEM_HEAD_DIM
    bq = lax.broadcasted_iota(jnp.int32, (R, 1), 0) % bb
    for h, q_ref in enumerate((q0_ref, q1_ref, q2_ref, q3_ref)):
        q = _mem_qnorm(jnp.concatenate([q_ref[t] for t in range(steps)], axis=0), g_ref[...])
        s = None
        for b in range(bb):
            d = _dot_nt(jnp.where(bq == b, q, 0.0).astype(BF16), _mem_head(k_ref, b, h))
            s = d if s is None else s + d
        p = _softmax_rows(s)
        o = None
        for b in range(bb):
            d = _dot(jnp.where(bq == b, p, 0.0).astype(BF16), _mem_head(v_ref, b, h))
            o = d if o is None else o + d
        for t in range(steps):
            y_ref[t, :, h * hd:(h + 1) * hd] = o[t * bb:(t + 1) * bb]


def mem_attn_sample(proj3, cache_k, cache_v, mem_qnorm_g, layer, bb):
    steps, B, _ = proj3.shape
    hd = MEM_HEAD_DIM
    cache = pl.BlockSpec((None, bb) + cache_k.shape[2:], lambda g: (layer, g, 0, 0))

    def q_spec(h):
        cb = COL_MQ // hd + h
        return pl.BlockSpec((steps, bb, hd), lambda g: (0, g, cb))

    return pl.pallas_call(
        functools.partial(_mem_sample_kernel, steps=steps, bb=bb),
        grid=(B // bb,),
        in_specs=[q_spec(h) for h in range(MEM_HEADS)] + [
            cache, cache,
            pl.BlockSpec((None, 1, hd), lambda g: (layer, 0, 0)),
        ],
        out_specs=pl.BlockSpec((steps, bb, MIX_W), lambda g: (0, g, 0)),
        out_shape=jax.ShapeDtypeStruct((steps, B, MIX_W), F32),
        compiler_params=_params("arbitrary"),
        name="mem_sample",
    )(proj3, proj3, proj3, proj3, cache_k, cache_v, mem_qnorm_g)


def _row_tile(T, cap):
    t = cap
    while T % t:
        t //= 2
    return t


TM_STREAM = 2048
TM_DOWN = 1024
TM_LOCAL = 512
TM_MERGE = 512
TM_RESIDENT = 256
TN_STREAM = 512
TN_DOWN = 256
TN_SAMPLE_IN = 1536
UP_PIECE_ROWS = 1024
HGRN_ROWS = 2048
HGRN_BATCH_UNROLL = 16
SWA_BLOCKS = 8
POOL_BATCH = 64
ATTN_BATCH = 8


def _token_tail(x, xn, ys, layer, w_in, w_branch, w_o, norm2_g):
    T = x.shape[0]
    tm = _row_tile(T, TM_MERGE)
    merged = merge_branches(xn, ys, w_in, w_branch, layer, tm, TN_STREAM if T // tm > 1 else TN_DOWN)
    return matmul_res_norm(merged, w_o, layer, x, norm2_g, _row_tile(T, TM_RESIDENT))


def kernel(x_prompt, x_sample, mem_prompt, state_hgrn, cache_pool, cache_swa_k, cache_swa_v, state_conv, cache_mem_k, cache_mem_v, norm1_g, w_in, hgrn_lb, hgrn_norm_g, pool_w, pool_scale, swa_qnorm_g, swa_knorm_g, swa_sinks, mem_norm_g, w_mem_kv, mem_qnorm_g, mem_knorm_g, w_branch, w_o, norm2_g, w_up, conv_w, conv_b, w_down):
    depth = w_in.shape[0]
    bp, L, _ = x_prompt.shape
    B, steps, _ = x_sample.shape
    assert bp == 1
    kw = SWA_KV_HEADS * SWA_HEAD_DIM
    Ts = steps * B

    lb_all = jnp.cumsum(jax.nn.softmax(hgrn_lb.astype(F32), axis=0), axis=0)
    lb_all = lb_all - lb_all[:1]

    tab_p = _rope_tables(np.arange(L))
    tab_s = _rope_tables(np.repeat(PAST_LEN + np.arange(steps), B))
    tab_sb = _rope_tables(np.repeat(PAST_LEN + np.arange(steps), ATTN_BATCH))

    xp = x_prompt.reshape(L, D_MODEL)
    xs = jnp.transpose(x_sample, (1, 0, 2)).reshape(Ts, D_MODEL)
    mem = mem_prompt.reshape(N_MEM, D_MODEL)
    ckt_all = jnp.transpose(cache_swa_k, (0, 1, 3, 4, 2))
    cvt_all = jnp.transpose(cache_swa_v, (0, 1, 3, 4, 2))
    mk_rows = _mem_rows_view(cache_mem_k)
    mv_rows = _mem_rows_view(cache_mem_v)
    cpool_v = jnp.transpose(cache_pool, (0, 2, 1, 3))
    row3 = lambda a: a.reshape(depth, 1, a.shape[-1])
    norm1_g, norm2_g, pool_scale, conv_b = row3(norm1_g), row3(norm2_g), row3(pool_scale), row3(conv_b)
    mem_norm_g, mem_qnorm_g, mem_knorm_g = row3(mem_norm_g), row3(mem_qnorm_g), row3(mem_knorm_g)
    tm_p = _row_tile(L, TM_STREAM)
    tl_p = _row_tile(L, TM_LOCAL)

    outs = {k: [] for k in ("sp", "pp", "ps", "kp", "ks", "vp", "vs", "cp", "mk", "mv")}
    hgrn_states = None
    conv_states = None
    for l in range(depth):
        lb = lb_all[l].reshape(1, MIX_W)
        gn = hgrn_norm_g[l].reshape(1, MIX_W)
        gq2 = jnp.tile(swa_qnorm_g[l], 2).reshape(1, LANES)
        gk2 = jnp.tile(swa_knorm_g[l], 2).reshape(1, LANES)

        kv = mem_kv(mem, mem_norm_g, w_mem_kv, mem_knorm_g, l)

        xn = prenorm(xp, norm1_g, l, tl_p)
        proj = matmul_cols(xn, w_in, l, COL_GATE, tm_p, TN_STREAM)
        ya, s_p = hgrn_prompt(proj, lb, gn, _row_tile(L, HGRN_ROWS))
        yb = pool_prompt(proj, pool_w, pool_scale, l, tl_p)
        khat = swa_kprep(proj, gk2, tab_p, tl_p)
        yc = swa_prompt(proj, khat, swa_sinks, gq2, tab_p, l, _row_tile(L, SWA_BLOCKS * WINDOW) // WINDOW)
        ym = mem_attn_prompt(proj, kv, mem_qnorm_g, l, tl_p)
        h, hn = _token_tail(xp, xn, (ya, yb, yc, ym), l, w_in, w_branch, w_o, norm2_g)
        gact, a_tail = up_conv_prompt(hn, w_up, conv_w, conv_b, l, tm_p, TN_STREAM)
        xp = matmul_res(gact, w_down, l, h, _row_tile(L, TM_DOWN), TN_DOWN)

        outs["sp"].append(s_p[None])
        outs["pp"].append(proj[None, L - POOL_BUF:, COL_POOL:COL_POOL + MIX_W])
        outs["kp"].append(khat[None, L - WINDOW:].reshape(1, WINDOW, SWA_KV_HEADS, SWA_HEAD_DIM))
        outs["vp"].append(proj[None, L - WINDOW:, COL_SV:COL_SV + kw].reshape(1, WINDOW, SWA_KV_HEADS, SWA_HEAD_DIM))
        outs["cp"].append(a_tail[-1:, CONV_HIST - 2:])
        outs["mk"].append(kv[None, :, :MIX_W].reshape(1, N_MEM, MEM_HEADS, MEM_HEAD_DIM))
        outs["mv"].append(kv[None, :, MIX_W:].reshape(1, N_MEM, MEM_HEADS, MEM_HEAD_DIM))

        xn = prenorm(xs, norm1_g, l, Ts)
        proj_s = matmul_cols(xn, w_in, l, COL_GATE, Ts, TN_SAMPLE_IN)
        proj3 = proj_s.reshape(steps, B, COL_GATE)
        ya, hgrn_states = hgrn_sample(proj3, lb, gn, state_hgrn, l, LANES // steps, hgrn_states)
        yb = pool_sample(proj3, cpool_v, pool_w, pool_scale, l, POOL_BATCH)
        khat_s = swa_kprep(proj_s, gk2, tab_s, Ts)
        khat3 = khat_s.reshape(steps, B, kw)
        yc = swa_sample(proj3, khat3, ckt_all, cvt_all, swa_sinks, gq2, tab_sb, l, ATTN_BATCH)
        ym = mem_attn_sample(proj3, mk_rows, mv_rows, mem_qnorm_g, l, ATTN_BATCH)
        ys = tuple(y.reshape(Ts, MIX_W).astype(BF16) for y in (ya, yb, yc, ym))
        h, hn = _token_tail(xs, xn, ys, l, w_in, w_branch, w_o, norm2_g)
        gact, conv_states = up_conv_sample(hn, w_up, state_conv, conv_w, conv_b, l, steps, TN_STREAM, conv_states)
        xs = matmul_res(gact, w_down, l, h, Ts, TN_DOWN)

        outs["ps"].append(proj3[:, :, COL_POOL:COL_POOL + MIX_W])
        to_window_minor = lambda a: jnp.transpose(a.reshape(steps, B, SWA_KV_HEADS, SWA_HEAD_DIM), (1, 2, 3, 0))
        outs["ks"].append(to_window_minor(khat3))
        outs["vs"].append(to_window_minor(proj3[:, :, COL_SV:COL_SV + kw]))

    stk = lambda k: jnp.stack(outs[k], axis=0)
    pool_s = jnp.transpose(jnp.concatenate([cpool_v[:, steps:], stk("ps")], axis=1), (0, 2, 1, 3))
    def slide_window(old, new):
        lead = [(0, 0, 0)] * (old.ndim - 1)
        shifted = lax.pad(old, jnp.zeros((), old.dtype), lead + [(-steps, steps, 0)])
        tail = lax.pad(new, jnp.zeros((), old.dtype), lead + [(WINDOW - steps, 0, 0)])
        pos = lax.broadcasted_iota(jnp.int32, old.shape, old.ndim - 1)
        return jnp.transpose(jnp.where(pos < WINDOW - steps, shifted, tail), (0, 1, 4, 2, 3))

    swa_k_s = slide_window(ckt_all, stk("ks"))
    swa_v_s = slide_window(cvt_all, stk("vs"))
    y_prompt = xp.reshape(1, L, D_MODEL)
    y_sample = jnp.transpose(xs.reshape(steps, B, D_MODEL), (1, 0, 2))
    return (y_prompt, y_sample,
            stk("sp"), hgrn_states, stk("pp"), pool_s, stk("kp"), swa_k_s, stk("vp"), swa_v_s,
            stk("cp"), conv_states, jnp.concatenate(outs["mk"], axis=0)[:, None], jnp.concatenate(outs["mv"], axis=0)[:, None])
```

```python
import functools

import numpy as np
import jax
import jax.numpy as jnp
from jax import lax
from jax.experimental import pallas as pl
from jax.experimental.pallas import tpu as pltpu

F32 = jnp.float32
BF16 = jnp.bfloat16

D_MODEL = 2048
MIX_W = D_MODEL // 2
N_BRANCH = 4
A_DK = 128
A_HEADS = MIX_W // A_DK
POOL_WINDOWS = (2, 4, 8, 16)
POOL_GC = MIX_W // len(POOL_WINDOWS)
POOL_BUF = max(POOL_WINDOWS) - 1
SWA_HEAD_DIM = 64
SWA_Q_HEADS = MIX_W // SWA_HEAD_DIM
SWA_KV_HEADS = SWA_Q_HEADS // 4
WINDOW = 128
ROT_DIM = SWA_HEAD_DIM // 4
ROPE_THETA = 500000.0
N_MEM = 256
MEM_HEADS = 4
MEM_HEAD_DIM = MIX_W // MEM_HEADS
D_FF = 11 * D_MODEL // 4
EPS = 1e-6
PAST_LEN = 8192

COL_HQ, COL_HF, COL_HI, COL_HG = 0, MIX_W, 2 * MIX_W, 3 * MIX_W
COL_POOL = 4 * MIX_W
COL_SQ = 5 * MIX_W
COL_SK = 6 * MIX_W
COL_SV = COL_SK + SWA_KV_HEADS * SWA_HEAD_DIM
COL_MQ = COL_SV + SWA_KV_HEADS * SWA_HEAD_DIM
COL_GATE = COL_MQ + MIX_W
IN_COLS = COL_GATE + N_BRANCH * D_MODEL

LANES = 128
V7X_VMEM_BYTES = 64 * 1024 * 1024
VMEM_LIMIT = V7X_VMEM_BYTES * 7 // 8
HGRN_CHUNK = 128
NEG_BIG = -1e30
LOG2_E = 1.4426950408889634


def _params(*sem):
    return pltpu.CompilerParams(dimension_semantics=sem, vmem_limit_bytes=VMEM_LIMIT)


def _sigmoid(x):
    return 0.5 * jnp.tanh(0.5 * x) + 0.5


def _dot(a, b):
    return jnp.dot(a, b, preferred_element_type=F32)


def _dot_nt(a, b):
    return lax.dot_general(a, b, (((1,), (1,)), ((), ())), preferred_element_type=F32)


def _skip_ref(kernel_fn, idx):
    def wrapped(*refs):
        return kernel_fn(*refs[:idx], *refs[idx + 1:])
    return wrapped


def _layer_slab_call(make_kernel, in_specs, args, slab_out, layer, slab_block, slab_index, out_specs, **kw):
    n_layers = kw["out_shape"][-1].shape[0]
    if slab_out is None:
        spec = pl.BlockSpec((n_layers,) + slab_block, lambda *g: (0,) + slab_index(*g))
        return pl.pallas_call(make_kernel(layer), in_specs=in_specs, out_specs=list(out_specs) + [spec], **kw)(*args)
    spec = pl.BlockSpec((1,) + slab_block, lambda *g: (layer,) + slab_index(*g))
    idx = len(args)
    return pl.pallas_call(
        _skip_ref(make_kernel(0), idx),
        in_specs=list(in_specs) + [pl.BlockSpec(memory_space=pl.ANY)],
        out_specs=list(out_specs) + [spec],
        input_output_aliases={idx: len(kw["out_shape"]) - 1},
        **kw)(*args, slab_out)


def _zero_other_slabs(so_ref, own):
    for l in range(so_ref.shape[0]):
        if l != own:
            so_ref[l] = jnp.zeros(so_ref.shape[1:], so_ref.dtype)


def _rms_rows(x, g):
    ms = jnp.mean(x * x, axis=-1, keepdims=True)
    return x * lax.rsqrt(ms + EPS) * g


def _prenorm_kernel(x_ref, g_ref, o_ref):
    o_ref[...] = _rms_rows(x_ref[...], g_ref[...]).astype(BF16)


def prenorm(x, g, layer, tm):
    T, K = x.shape
    return pl.pallas_call(
        _prenorm_kernel,
        grid=(T // tm,),
        in_specs=[pl.BlockSpec((tm, K), lambda i: (i, 0)),
                  pl.BlockSpec((None, 1, K), lambda i: (layer, 0, 0))],
        out_specs=pl.BlockSpec((tm, K), lambda i: (i, 0)),
        out_shape=jax.ShapeDtypeStruct((T, K), BF16),
        compiler_params=_params("arbitrary"),
        name="prenorm",
    )(x, g)


def _matmul_kernel(a_ref, w_ref, o_ref):
    o_ref[...] = _dot(a_ref[...], w_ref[...].astype(BF16))


def matmul_cols(a, w, layer, n_cols, tm, tn):
    T, K = a.shape
    return pl.pallas_call(
        _matmul_kernel,
        grid=(T // tm, n_cols // tn),
        in_specs=[
            pl.BlockSpec((tm, K), lambda i, j: (i, 0)),
            pl.BlockSpec((None, K, tn), lambda i, j: (layer, 0, j)),
        ],
        out_specs=pl.BlockSpec((tm, tn), lambda i, j: (i, j)),
        out_shape=jax.ShapeDtypeStruct((T, n_cols), F32),
        compiler_params=_params("parallel", "arbitrary"),
        name="matmul_cols",
    )(a, w)


def _matmul_res_kernel(a_ref, w_ref, r_ref, o_ref):
    o_ref[...] = r_ref[...] + _dot(a_ref[...], w_ref[...].astype(BF16))


def matmul_res(a, w, layer, res, tm, tn):
    T, K = a.shape
    N = w.shape[2]
    return pl.pallas_call(
        _matmul_res_kernel,
        grid=(T // tm, N // tn),
        in_specs=[
            pl.BlockSpec((tm, K), lambda i, j: (i, 0)),
            pl.BlockSpec((None, K, tn), lambda i, j: (layer, 0, j)),
            pl.BlockSpec((tm, tn), lambda i, j: (i, j)),
        ],
        out_specs=pl.BlockSpec((tm, tn), lambda i, j: (i, j)),
        out_shape=jax.ShapeDtypeStruct((T, N), F32),
        compiler_params=_params("parallel", "arbitrary"),
        name="matmul_res",
    )(a, w, res)


def _res_norm_kernel(a_ref, w_ref, r_ref, g_ref, h_ref, hn_ref, w_s):
    @pl.when(pl.program_id(0) == 0)
    def _():
        w_s[...] = w_ref[...].astype(BF16)

    h = r_ref[...] + _dot(a_ref[...], w_s[...])
    h_ref[...] = h
    hn_ref[...] = _rms_rows(h, g_ref[...]).astype(BF16)


def matmul_res_norm(a, w, layer, res, g, tm):
    T, K = a.shape
    N = w.shape[2]
    return pl.pallas_call(
        _res_norm_kernel,
        grid=(T // tm,),
        in_specs=[
            pl.BlockSpec((tm, K), lambda i: (i, 0)),
            pl.BlockSpec((None, K, N), lambda i: (layer, 0, 0), pipeline_mode=pl.Buffered(1)),
            pl.BlockSpec((tm, N), lambda i: (i, 0)),
            pl.BlockSpec((None, 1, N), lambda i: (layer, 0, 0)),
        ],
        out_specs=[pl.BlockSpec((tm, N), lambda i: (i, 0)), pl.BlockSpec((tm, N), lambda i: (i, 0))],
        out_shape=[jax.ShapeDtypeStruct((T, N), F32), jax.ShapeDtypeStruct((T, N), BF16)],
        scratch_shapes=[pltpu.VMEM((K, N), BF16)],
        compiler_params=_params("arbitrary"),
        name="matmul_res_norm",
    )(a, w, res, g)


def _merge_kernel(xn_ref, ya_ref, yb_ref, yc_ref, ym_ref, wg0_ref, wg1_ref, wg2_ref, wg3_ref, wb_ref, o_ref,
                  wg_s, wb_s):
    @pl.when(pl.program_id(1) == 0)
    def _():
        for n, wg_ref in enumerate((wg0_ref, wg1_ref, wg2_ref, wg3_ref)):
            wg_s[n] = wg_ref[...].astype(BF16)
            wb_s[n] = wb_ref[n].astype(BF16)

    xn = xn_ref[...]
    acc = None
    for n, y_ref in enumerate((ya_ref, yb_ref, yc_ref, ym_ref)):
        t = _sigmoid(_dot(xn, wg_s[n])) * _dot(y_ref[...], wb_s[n])
        acc = t if acc is None else acc + t
    o_ref[...] = acc.astype(BF16)


def merge_branches(xn, ys, w_in, w_branch, layer, tm, tn):
    T = xn.shape[0]
    once = pl.Buffered(1) if T // tm > 1 else None
    y_spec = pl.BlockSpec((tm, MIX_W), lambda j, i: (i, 0))

    def gate_spec(n):
        off = (COL_GATE + n * D_MODEL) // tn
        return pl.BlockSpec((None, D_MODEL, tn), lambda j, i: (layer, 0, off + j), pipeline_mode=once)

    return pl.pallas_call(
        _merge_kernel,
        grid=(D_MODEL // tn, T // tm),
        in_specs=[pl.BlockSpec((tm, D_MODEL), lambda j, i: (i, 0))] + [y_spec] * 4
        + [gate_spec(n) for n in range(N_BRANCH)]
        + [pl.BlockSpec((None, N_BRANCH, MIX_W, tn), lambda j, i: (layer, 0, 0, j), pipeline_mode=once)],
        out_specs=pl.BlockSpec((tm, tn), lambda j, i: (i, j)),
        out_shape=jax.ShapeDtypeStruct((T, D_MODEL), BF16),
        scratch_shapes=[pltpu.VMEM((N_BRANCH, D_MODEL, tn), BF16), pltpu.VMEM((N_BRANCH, MIX_W, tn), BF16)],
        compiler_params=_params("arbitrary", "arbitrary"),
        name="merge_branches",
    )(xn, *ys, w_in, w_in, w_in, w_in, w_branch)


CONV_HIST = 8


def _gelu(x):
    return 0.5 * x * (1.0 + lax.erf(x * (2.0 ** -0.5)))


def _up_conv_prompt_kernel(xn_ref, wa_ref, wv_ref, cw_ref, cb_ref, g_ref, tail_ref, carry_ref, *, tm, rc):
    i, j = pl.program_id(0), pl.program_id(1)
    wa = wa_ref[...].astype(BF16)
    wv = wv_ref[...].astype(BF16)
    prev = jnp.where(i == 0, 0.0, carry_ref[j])
    row = lax.broadcasted_iota(jnp.int32, (rc, wa.shape[1]), 0)
    for c in range(tm // rc):
        sl = pl.ds(c * rc, rc)
        xn = xn_ref[sl, :]
        a = _dot(xn, wa)
        v = _dot(xn, wv)
        a1 = jnp.where(row == 0, prev[CONV_HIST - 1:CONV_HIST], pltpu.roll(a, 1, 0))
        a2 = jnp.where(row == 0, prev[CONV_HIST - 2:CONV_HIST - 1],
                       jnp.where(row == 1, prev[CONV_HIST - 1:CONV_HIST], pltpu.roll(a, 2, 0)))
        cc = cb_ref[...] + cw_ref[0:1, :] * a2 + cw_ref[1:2, :] * a1 + cw_ref[2:3, :] * a
        g_ref[sl, :] = (_gelu(cc) * v).astype(BF16)
        prev = a[rc - CONV_HIST:, :]
    carry_ref[j] = prev
    tail_ref[...] = prev


def up_conv_prompt(xn, w_up, conv_w, conv_b, layer, tm, tn):
    T, K = xn.shape
    nc = D_FF // tn
    return pl.pallas_call(
        functools.partial(_up_conv_prompt_kernel, tm=tm, rc=min(tm, UP_PIECE_ROWS)),
        grid=(T // tm, nc),
        in_specs=[
            pl.BlockSpec((tm, K), lambda i, j: (i, 0)),
            pl.BlockSpec((None, K, tn), lambda i, j: (layer, 0, j)),
            pl.BlockSpec((None, K, tn), lambda i, j: (layer, 0, nc + j)),
            pl.BlockSpec((None, 3, tn), lambda i, j: (layer, 0, j)),
            pl.BlockSpec((None, 1, tn), lambda i, j: (layer, 0, j)),
        ],
        out_specs=[
            pl.BlockSpec((tm, tn), lambda i, j: (i, j)),
            pl.BlockSpec((None, CONV_HIST, tn), lambda i, j: (i, 0, j)),
        ],
        out_shape=[
            jax.ShapeDtypeStruct((T, D_FF), BF16),
            jax.ShapeDtypeStruct((T // tm, CONV_HIST, D_FF), F32),
        ],
        scratch_shapes=[pltpu.VMEM((nc, CONV_HIST, tn), F32)],
        compiler_params=_params("arbitrary", "arbitrary"),
        name="up_conv_prompt",
    )(xn, w_up, w_up, conv_w, conv_b)


def _up_conv_sample_kernel(xn_ref, wa_ref, wv_ref, st_ref, cw_ref, cb_ref, g_ref, so_ref, *, steps, B, own):
    _zero_other_slabs(so_ref, own)
    xn = xn_ref[...]
    a = _dot(xn, wa_ref[...].astype(BF16))
    v = _dot(xn, wv_ref[...].astype(BF16))
    hist = [st_ref[:, 0, :], st_ref[:, 1, :]] + [a[t * B:(t + 1) * B] for t in range(steps)]
    for t in range(steps):
        c = cb_ref[...] + cw_ref[0:1, :] * hist[t] + cw_ref[1:2, :] * hist[t + 1] + cw_ref[2:3, :] * hist[t + 2]
        g_ref[t * B:(t + 1) * B, :] = (_gelu(c) * v[t * B:(t + 1) * B]).astype(BF16)
    so_ref[own, :, 0, :] = hist[steps]
    so_ref[own, :, 1, :] = hist[steps + 1]


def up_conv_sample(xn, w_up, state_conv, conv_w, conv_b, layer, steps, tn, state_out):
    T, K = xn.shape
    B = T // steps
    nc = D_FF // tn
    in_specs = [
        pl.BlockSpec((T, K), lambda j: (0, 0)),
        pl.BlockSpec((None, K, tn), lambda j: (layer, 0, j)),
        pl.BlockSpec((None, K, tn), lambda j: (layer, 0, nc + j)),
        pl.BlockSpec((None, B, 2, tn), lambda j: (layer, 0, 0, j)),
        pl.BlockSpec((None, 3, tn), lambda j: (layer, 0, j)),
        pl.BlockSpec((None, 1, tn), lambda j: (layer, 0, j)),
    ]
    return _layer_slab_call(
        lambda own: functools.partial(_up_conv_sample_kernel, steps=steps, B=B, own=own),
        in_specs, [xn, w_up, w_up, state_conv, conv_w, conv_b], state_out, layer,
        (B, 2, tn), lambda j: (0, 0, j),
        [pl.BlockSpec((T, tn), lambda j: (0, j))],
        grid=(nc,),
        out_shape=[
            jax.ShapeDtypeStruct((T, D_FF), BF16),
            jax.ShapeDtypeStruct(state_conv.shape, F32),
        ],
        compiler_params=_params("arbitrary"),
        name="up_conv_sample",
    )


def _hgrn_gates(q_in, z, lb):
    q = q_in * _sigmoid(q_in)
    log_sig = jnp.minimum(z, 0.0) - jnp.log(1.0 + jnp.exp(-jnp.abs(z)))
    a1 = jnp.log(lb)
    a2 = jnp.log1p(-lb) + log_sig
    log_f = jnp.maximum(a1, a2) + jnp.log(1.0 + jnp.exp(-jnp.abs(a1 - a2)))
    k = (1.0 - lb) * _sigmoid(-z)
    return q, log_f, k


def _hgrn_out(o, gate, gn):
    ms = jnp.mean(o * o, axis=-1, keepdims=True)
    return o * lax.rsqrt(ms + EPS) * gn * (gate * _sigmoid(gate))


def _cumsum_rows(x, tril):
    hi = x.astype(BF16)
    r1 = x - hi.astype(F32)
    mid = r1.astype(BF16)
    lo = (r1 - mid.astype(F32)).astype(BF16)
    return _dot(tril, hi) + _dot(tril, mid) + _dot(tril, lo)


def _block_row(x, blk, r):
    C = x.shape[0]
    x3 = x.reshape(C // blk, blk, LANES)
    return jnp.broadcast_to(x3[:, r:r + 1, :], (C // blk, blk, LANES)).reshape(C, LANES)


def _hgrn_pair_codes():
    t = np.arange(HGRN_CHUNK)[:, None]
    s = np.arange(HGRN_CHUNK)[None, :]
    level = np.floor(np.log2(np.maximum(t ^ s, 1))).astype(np.int32)
    return jnp.asarray(np.where(s > t, -1, np.where(s == t, 0, 1 + level)), jnp.int32)


def _boundary_row(b, m):
    if 2 * m >= 8:
        return _block_row(b, 2 * m, m - 1)
    r8 = lax.broadcasted_iota(jnp.int32, b.shape, 0) & 7
    if m == 2:
        return jnp.where(r8 < 4, _block_row(b, 8, 1), _block_row(b, 8, 5))
    return jnp.where((r8 & 1) == 0, b, pltpu.roll(b, 1, 0))


def _hgrn_attention(q, k, b, code):
    C = HGRN_CHUNK
    rowl = lax.broadcasted_iota(jnp.int32, (C, LANES), 0)
    att = jnp.where(code == 0, jnp.sum(q * k, axis=-1, keepdims=True), 0.0)
    m, level = 1, 1
    while m < C:
        d = b - _boundary_row(b, m)
        isq = (rowl & m) != 0
        x = (jnp.where(isq, q, k) * jnp.exp2(jnp.abs(d) * (-LOG2_E))).astype(BF16)
        att = jnp.where(code == level, _dot_nt(x, x), att)
        m *= 2
        level += 1
    return att


def _hgrn_apply(q, k, v, b, att, S):
    C = HGRN_CHUNK
    vb = v.astype(BF16)
    o = _dot((q * jnp.exp(b)).astype(BF16), S.astype(BF16)) + _dot(att.astype(BF16), vb)
    bl = b[C - 1:C, :]
    kk = k * jnp.exp(bl - b)
    ecol = jnp.transpose(jnp.broadcast_to(jnp.exp(bl), (LANES, LANES)))
    return o, ecol * S + _dot(jnp.transpose(kk).astype(BF16), vb)


def _hgrn_prompt_kernel(q_ref, f_ref, i_ref, g_ref, lb_ref, gn_ref, code_ref, y_ref, so_ref, s_ref, *, rows):
    @pl.when(pl.program_id(1) == 0)
    def _():
        s_ref[...] = jnp.zeros_like(s_ref)

    C = HGRN_CHUNK
    code = code_ref[...]
    tril = jnp.where(code >= 0, 1.0, 0.0).astype(BF16)
    q, log_f, k = _hgrn_gates(q_ref[...], f_ref[...], lb_ref[...])
    chunks = [slice(c * C, (c + 1) * C) for c in range(rows // C)]
    bs = [_cumsum_rows(log_f[sl], tril) for sl in chunks]
    atts = [_hgrn_attention(q[sl], k[sl], b, code) for sl, b in zip(chunks, bs)]
    S = s_ref[...]
    outs = []
    for sl, b, att in zip(chunks, bs, atts):
        o, S = _hgrn_apply(q[sl], k[sl], i_ref[sl, :], b, att, S)
        outs.append(o)
    s_ref[...] = S
    y_ref[...] = _hgrn_out(jnp.concatenate(outs, axis=0), g_ref[...], gn_ref[...]).astype(BF16)

    @pl.when(pl.program_id(1) == pl.num_programs(1) - 1)
    def _():
        so_ref[...] = s_ref[...]


def hgrn_prompt(proj, lb, gn, rows):
    T = proj.shape[0]

    def col(off):
        base = off // LANES
        return pl.BlockSpec((rows, LANES), lambda h, c: (c, base + h))

    vec = pl.BlockSpec((1, LANES), lambda h, c: (0, h))
    return pl.pallas_call(
        functools.partial(_hgrn_prompt_kernel, rows=rows),
        grid=(A_HEADS, T // rows),
        in_specs=[col(COL_HQ), col(COL_HF), col(COL_HI), col(COL_HG), vec, vec,
                  pl.BlockSpec((HGRN_CHUNK, HGRN_CHUNK), lambda h, c: (0, 0))],
        out_specs=[
            pl.BlockSpec((rows, LANES), lambda h, c: (c, h)),
            pl.BlockSpec((None, A_DK, LANES), lambda h, c: (h, 0, 0)),
        ],
        out_shape=[
            jax.ShapeDtypeStruct((T, MIX_W), BF16),
            jax.ShapeDtypeStruct((A_HEADS, A_DK, LANES), F32),
        ],
        scratch_shapes=[pltpu.VMEM((A_DK, LANES), F32)],
        compiler_params=_params("parallel", "arbitrary"),
        name="hgrn_prompt",
    )(proj, proj, proj, proj, lb, gn, _hgrn_pair_codes())


def _hgrn_sample_kernel(q_ref, f_ref, i_ref, g_ref, lb_ref, gn_ref, s_ref, y_ref, so_ref, *, steps, bb, own):
    _zero_other_slabs(so_ref, own)
    lb = lb_ref[...]
    gn = gn_ref[...]
    qs, ks, vs, bs = [], [], [], []
    b = None
    for t in range(steps):
        q, log_f, k = _hgrn_gates(q_ref[t], f_ref[t], lb)
        b = log_f if b is None else b + log_f
        qs.append(q)
        ks.append(k)
        vs.append(i_ref[t])
        bs.append(b)
    intra = []
    for t in range(steps):
        acc = None
        for s in range(t + 1):
            w = jnp.sum(qs[t] * ks[s] * jnp.exp(bs[t] - bs[s]), axis=-1, keepdims=True)
            acc = w * vs[s] if acc is None else acc + w * vs[s]
        intra.append(acc)
    R = steps * bb
    q_stack = jnp.concatenate([qs[t] * jnp.exp(bs[t]) for t in range(steps)], axis=0).astype(BF16)
    k_stack = jnp.concatenate([ks[t] * jnp.exp(bs[-1] - bs[t]) for t in range(steps)], axis=0)
    v_stack = jnp.concatenate(vs, axis=0).astype(BF16)
    k_t = jnp.transpose(k_stack)
    f_pad = jnp.concatenate([jnp.exp(bs[-1])] + [jnp.zeros((R - bb, LANES), F32)], axis=0)
    f_t = jnp.transpose(f_pad)
    rowi = lax.broadcasted_iota(jnp.int32, (R, LANES), 0) % bb
    lanei = lax.broadcasted_iota(jnp.int32, (LANES, R), 1)

    def body(bi, o_acc):
        s_b = s_ref[bi]
        o_acc = jnp.where(rowi == bi, _dot(q_stack, s_b.astype(BF16)), o_acc)
        f_col = jnp.sum(jnp.where(lanei == bi, f_t, 0.0), axis=-1, keepdims=True)
        k_b = jnp.where(lanei % bb == bi, k_t, 0.0).astype(BF16)
        so_ref[own, bi] = f_col * s_b + _dot(k_b, v_stack)
        return o_acc

    o_inter = lax.fori_loop(0, bb, body, jnp.zeros((R, LANES), F32), unroll=HGRN_BATCH_UNROLL)
    for t in range(steps):
        o = o_inter[t * bb:(t + 1) * bb] + intra[t]
        y_ref[t] = _hgrn_out(o, g_ref[t], gn).astype(BF16)


def hgrn_sample(proj3, lb, gn, state, layer, bb, state_out):
    steps, B, _ = proj3.shape
    assert steps * bb == LANES

    def col(off):
        base = off // LANES
        return pl.BlockSpec((steps, bb, LANES), lambda g, h: (0, g, base + h))

    vec = pl.BlockSpec((1, LANES), lambda g, h: (0, h))
    slab = pl.BlockSpec((None, bb, None, A_DK, LANES), lambda g, h: (layer, g, h, 0, 0))
    return _layer_slab_call(
        lambda own: functools.partial(_hgrn_sample_kernel, steps=steps, bb=bb, own=own),
        [col(COL_HQ), col(COL_HF), col(COL_HI), col(COL_HG), vec, vec, slab],
        [proj3, proj3, proj3, proj3, lb, gn, state], state_out, layer,
        (bb, None, A_DK, LANES), lambda g, h: (g, h, 0, 0),
        [pl.BlockSpec((steps, bb, LANES), lambda g, h: (0, g, h))],
        grid=(B // bb, A_HEADS),
        out_shape=[
            jax.ShapeDtypeStruct((steps, B, MIX_W), BF16),
            jax.ShapeDtypeStruct(state.shape, F32),
        ],
        compiler_params=_params("parallel", "arbitrary"),
        name="hgrn_sample",
    )


POOL_HIST = 32


def _pool_project(d, w_ref, sc_ref, g):
    sl = slice(g * POOL_GC, (g + 1) * POOL_GC)
    return _dot(d.astype(BF16), w_ref[g].astype(BF16)) * sc_ref[:, sl]


def _pool_prompt_kernel(u_ref, prev_ref, w_ref, sc_ref, y_ref, ext_ref, sum_ref, *, tm):
    i = pl.program_id(0)
    H = POOL_HIST
    ext_ref[0:H, :] = jnp.where(i == 0, 0.0, prev_ref[...])
    ext_ref[H:, :] = u_ref[...]
    sum_ref[0:H // 2, :] = jnp.zeros((H // 2, MIX_W), F32)
    pos = i * tm + lax.broadcasted_iota(jnp.int32, (tm, 1), 0)
    src, w = ext_ref, 1
    for g, win in enumerate(POOL_WINDOWS):
        lanes = slice(g * POOL_GC, MIX_W)
        while w < win:
            n = H // 2 + tm
            sum_ref[pl.ds(H // 2, n), lanes] = src[pl.ds(H // 2, n), lanes] + src[pl.ds(H // 2 - w, n), lanes]
            src, w = sum_ref, 2 * w
        sl = slice(g * POOL_GC, (g + 1) * POOL_GC)
        cnt = jnp.minimum(pos + 1, win).astype(F32)
        d = src[pl.ds(H, tm), sl] / cnt - u_ref[:, sl]
        y_ref[:, sl] = _pool_project(d, w_ref, sc_ref, g).astype(BF16)


def _pool_sample_kernel(u_ref, c_ref, w_ref, sc_ref, y_ref, *, steps):
    for t in range(steps):
        for g, win in enumerate(POOL_WINDOWS):
            sl = slice(g * POOL_GC, (g + 1) * POOL_GC)
            acc = u_ref[t, :, sl]
            for j in range(1, win):
                if j <= t:
                    acc = acc + u_ref[t - j, :, sl]
                else:
                    acc = acc + c_ref[POOL_BUF + t - j, :, sl]
            d = acc / float(win) - u_ref[t, :, sl]
            y_ref[t, :, sl] = _pool_project(d, w_ref, sc_ref, g).astype(BF16)


def pool_sample(proj3, cache_pool, pool_w, pool_scale, layer, bb):
    steps, B, _ = proj3.shape
    cb = COL_POOL // MIX_W
    return pl.pallas_call(
        functools.partial(_pool_sample_kernel, steps=steps),
        grid=(B // bb,),
        in_specs=[
            pl.BlockSpec((steps, bb, MIX_W), lambda g: (0, g, cb)),
            pl.BlockSpec((None, POOL_BUF, bb, MIX_W), lambda g: (layer, 0, g, 0)),
            pl.BlockSpec((None, len(POOL_WINDOWS), POOL_GC, POOL_GC), lambda g: (layer, 0, 0, 0)),
            pl.BlockSpec((None, 1, MIX_W), lambda g: (layer, 0, 0)),
        ],
        out_specs=pl.BlockSpec((steps, bb, MIX_W), lambda g: (0, g, 0)),
        out_shape=jax.ShapeDtypeStruct((steps, B, MIX_W), BF16),
        compiler_params=_params("arbitrary"),
        name="pool_sample",
    )(proj3, cache_pool, pool_w, pool_scale)


def _rope_tables(positions):
    half = ROT_DIM // 2
    inv = np.power(ROPE_THETA, -np.arange(0, ROT_DIM, 2, dtype=np.float64) / ROT_DIM)
    ang = np.asarray(positions, np.float64)[:, None] * inv[None, :]
    cos, sin = np.cos(ang), np.sin(ang)
    n = len(positions)
    ct = np.ones((n, LANES))
    sn = np.zeros((n, LANES))
    sp = np.zeros((n, LANES))
    for base in (0, SWA_HEAD_DIM):
        ct[:, base:base + half] = cos
        ct[:, base + half:base + ROT_DIM] = cos
        sn[:, base:base + half] = -sin
        sp[:, base + half:base + ROT_DIM] = sin
    return tuple(jnp.asarray(t, F32) for t in (ct, sn, sp))


def _head_norm_rope(x, g, ct, sn, sp):
    lane = lax.broadcasted_iota(jnp.int32, x.shape, 1)
    lo = lane < SWA_HEAD_DIM
    x2 = x * x
    ms_lo = jnp.sum(jnp.where(lo, x2, 0.0), axis=-1, keepdims=True) / SWA_HEAD_DIM
    ms_hi = jnp.sum(jnp.where(lo, 0.0, x2), axis=-1, keepdims=True) / SWA_HEAD_DIM
    xn = x * jnp.where(lo, lax.rsqrt(ms_lo + EPS), lax.rsqrt(ms_hi + EPS)) * g
    half = ROT_DIM // 2
    return xn * ct + pltpu.roll(xn, LANES - half, 1) * sn + pltpu.roll(xn, half, 1) * sp


def _kprep_kernel(k_ref, g_ref, ct_ref, sn_ref, sp_ref, o_ref):
    ct, sn, sp = ct_ref[...], sn_ref[...], sp_ref[...]
    for j in range(2):
        sl = slice(j * LANES, (j + 1) * LANES)
        o_ref[:, sl] = _head_norm_rope(k_ref[:, sl], g_ref[...], ct, sn, sp)


def swa_kprep(proj, g2, tables, tm):
    T = proj.shape[0]
    kw = SWA_KV_HEADS * SWA_HEAD_DIM
    tab = pl.BlockSpec((tm, LANES), lambda i: (i, 0))
    return pl.pallas_call(
        _kprep_kernel,
        grid=(T // tm,),
        in_specs=[pl.BlockSpec((tm, kw), lambda i: (i, COL_SK // kw)),
                  pl.BlockSpec((1, LANES), lambda i: (0, 0)), tab, tab, tab],
        out_specs=pl.BlockSpec((tm, kw), lambda i: (i, 0)),
        out_shape=jax.ShapeDtypeStruct((T, kw), F32),
        compiler_params=_params("arbitrary"),
        name="swa_kprep",
    )(proj, g2, *tables)


def _dup_head(x, parity):
    lane = lax.broadcasted_iota(jnp.int32, x.shape, 1)
    return jnp.where(lane // SWA_HEAD_DIM == parity, x, pltpu.roll(x, SWA_HEAD_DIM, 1))


def _stack_heads(q):
    lane = lax.broadcasted_iota(jnp.int32, q.shape, 1)
    lo = lane < SWA_HEAD_DIM
    return jnp.concatenate([jnp.where(lo, q, 0.0), jnp.where(lo, 0.0, q)], axis=0)


def _unstack_heads(o2):
    R = o2.shape[0] // 2
    lane = lax.broadcasted_iota(jnp.int32, (R, LANES), 1)
    return jnp.where(lane < SWA_HEAD_DIM, o2[:R], o2[R:])


def _swa_prompt_kernel(sink_ref, q_ref, kc_ref, kp_ref, vc_ref, vp_ref, g_ref, ct_ref, sn_ref, sp_ref, y_ref,
                       *, layer, nb):
    first = pl.program_id(0) == 0
    W = WINDOW
    G = SWA_Q_HEADS // SWA_KV_HEADS
    r4 = lax.broadcasted_iota(jnp.int32, (G * W, 1), 0)
    hh = r4 // W
    ci = lax.broadcasted_iota(jnp.int32, (1, W), 1)
    cur = ci <= r4 % W
    scale = SWA_HEAD_DIM ** -0.5

    def head_blocks(cur_ref, prev_ref, kvh):
        ksl = slice((kvh // 2) * LANES, (kvh // 2 + 1) * LANES)
        x = _dup_head(jnp.concatenate([prev_ref[:, ksl], cur_ref[:, ksl]], axis=0), kvh % 2).astype(BF16)
        return [x[j * W:(j + 1) * W] for j in range(nb + 1)]

    scores, sinks = [], []
    for kvh in range(SWA_KV_HEADS):
        kb = head_blocks(kc_ref, kp_ref, kvh)
        sink = sink_ref[layer, G * kvh + G - 1]
        for i in range(G - 2, -1, -1):
            sink = jnp.where(hh == i, sink_ref[layer, G * kvh + i], sink)
        for blk in range(nb):
            rows = slice(blk * W, (blk + 1) * W)
            qs = []
            for jj in range(G // 2):
                qsl = slice((2 * kvh + jj) * LANES, (2 * kvh + jj + 1) * LANES)
                qn = _head_norm_rope(q_ref[rows, qsl], g_ref[...], ct_ref[rows, :], sn_ref[rows, :], sp_ref[rows, :])
                qs.append(_stack_heads(qn * scale))
            q = jnp.concatenate(qs, axis=0).astype(BF16)
            s_prev = _dot_nt(q, kb[blk])
            if blk == 0:
                s_prev = jnp.where(first, NEG_BIG, s_prev)
            scores.append(jnp.where(cur, _dot_nt(q, kb[blk + 1]), s_prev))
            sinks.append(sink)
    probs = []
    for s, sink in zip(scores, sinks):
        m = jnp.maximum(jnp.max(s, axis=-1, keepdims=True), sink)
        e = jnp.exp(s - m)
        probs.append(e * (1.0 / (jnp.sum(e, axis=-1, keepdims=True) + jnp.exp(sink - m))))
    for kvh in range(SWA_KV_HEADS):
        vb = head_blocks(vc_ref, vp_ref, kvh)
        for blk in range(nb):
            p = probs[kvh * nb + blk]
            o = (_dot(jnp.where(cur, p, 0.0).astype(BF16), vb[blk + 1])
                 + _dot(jnp.where(cur, 0.0, p).astype(BF16), vb[blk]))
            for jj in range(G // 2):
                qsl = slice((2 * kvh + jj) * LANES, (2 * kvh + jj + 1) * LANES)
                y_ref[blk * W:(blk + 1) * W, qsl] = _unstack_heads(o[2 * jj * W:(2 * jj + 2) * W]).astype(BF16)


def swa_prompt(proj, khat, sinks, g2, tables, layer, nb):
    T = proj.shape[0]
    W = WINDOW
    tq = nb * W
    kw = SWA_KV_HEADS * SWA_HEAD_DIM
    tab = pl.BlockSpec((tq, LANES), lambda i: (i, 0))
    prev = lambda i: jnp.maximum(i * nb - 1, 0)
    return pl.pallas_call(
        functools.partial(_swa_prompt_kernel, layer=layer, nb=nb),
        grid=(T // tq,),
        in_specs=[
            pl.BlockSpec(memory_space=pltpu.SMEM),
            pl.BlockSpec((tq, MIX_W), lambda i: (i, COL_SQ // MIX_W)),
            pl.BlockSpec((tq, kw), lambda i: (i, 0)),
            pl.BlockSpec((W, kw), lambda i: (prev(i), 0)),
            pl.BlockSpec((tq, kw), lambda i: (i, COL_SV // kw)),
            pl.BlockSpec((W, kw), lambda i: (prev(i), COL_SV // kw)),
            pl.BlockSpec((1, LANES), lambda i: (0, 0)), tab, tab, tab,
        ],
        out_specs=pl.BlockSpec((tq, MIX_W), lambda i: (i, 0)),
        out_shape=jax.ShapeDtypeStruct((T, MIX_W), BF16),
        compiler_params=_params("arbitrary"),
        name="swa_prompt",
    )(sinks, proj, khat, khat, proj, proj, g2, *tables)


def _swa_sample_kernel(sink_ref, q_ref, kn_ref, vn_ref, kc_ref, vc_ref, g_ref, ct_ref, sn_ref, sp_ref, y_ref,
                       *, layer, steps, bb):
    kvh = pl.program_id(1)
    parity = kvh % 2
    W = WINDOW
    R = steps * bb
    G = SWA_Q_HEADS // SWA_KV_HEADS
    ct, sn, sp = ct_ref[...], sn_ref[...], sp_ref[...]
    scale = SWA_HEAD_DIM ** -0.5
    r4 = lax.broadcasted_iota(jnp.int32, (G * R, 1), 0)
    hh = r4 // R
    tq = (r4 % R) // bb
    bq = r4 % bb
    c_new = lax.broadcasted_iota(jnp.int32, (1, R), 1)
    valid_new = (c_new % bb == bq) & (c_new // bb <= tq)
    c_old = lax.broadcasted_iota(jnp.int32, (1, W), 1)
    valid_old = c_old > tq
    kn = _dup_head(jnp.concatenate([kn_ref[t] for t in range(steps)], axis=0), parity).astype(BF16)
    vn = _dup_head(jnp.concatenate([vn_ref[t] for t in range(steps)], axis=0), parity).astype(BF16)
    qs = []
    for jj in range(G // 2):
        qsl = slice(jj * LANES, (jj + 1) * LANES)
        q = jnp.concatenate([q_ref[t, :, qsl] for t in range(steps)], axis=0)
        qs.append(_stack_heads(_head_norm_rope(q, g_ref[...], ct, sn, sp) * scale))
    q4 = jnp.concatenate(qs, axis=0)
    s_new = jnp.where(valid_new, _dot_nt(q4.astype(BF16), kn), NEG_BIG)
    s_old = None
    for b in range(0, bb, 2):
        lhs = jnp.concatenate([jnp.where(bq == b + i, q4, 0.0).astype(BF16) for i in range(2)], axis=1)
        k_t = [kc_ref[b + i].astype(BF16) for i in range(2)]
        d = _dot(lhs, jnp.concatenate([k_t[0], k_t[0], k_t[1], k_t[1]], axis=0))
        s_old = d if s_old is None else s_old + d
    s_old = jnp.where(valid_old, s_old, NEG_BIG)
    sink = sink_ref[layer, G * kvh + G - 1]
    for i in range(G - 2, -1, -1):
        sink = jnp.where(hh == i, sink_ref[layer, G * kvh + i], sink)
    m = jnp.maximum(jnp.maximum(jnp.max(s_new, axis=-1, keepdims=True),
                                jnp.max(s_old, axis=-1, keepdims=True)), sink)
    e_new = jnp.exp(s_new - m)
    e_old = jnp.exp(s_old - m)
    den = jnp.sum(e_new, axis=-1, keepdims=True) + jnp.sum(e_old, axis=-1, keepdims=True) + jnp.exp(sink - m)
    p_old = e_old / den
    o = _dot((e_new / den).astype(BF16), vn)
    for b in range(0, bb, 2):
        lhs = jnp.concatenate([jnp.where(bq == b + i, p_old, 0.0).astype(BF16) for i in range(2)], axis=1)
        v_t = [vc_ref[b + i].astype(BF16) for i in range(2)]
        rhs = jnp.concatenate([jnp.concatenate([v_t[i], v_t[i]], axis=0) for i in range(2)], axis=1)
        o = o + _dot_nt(lhs, rhs)
    for jj in range(G // 2):
        o_j = _unstack_heads(o[2 * jj * R:(2 * jj + 2) * R])
        for t in range(steps):
            y_ref[t, :, jj * LANES:(jj + 1) * LANES] = o_j[t * bb:(t + 1) * bb]


def swa_sample(proj3, khat3, cache_kt, cache_vt, sinks, g2, tables, layer, bb):
    steps, B, _ = proj3.shape
    R = steps * bb
    qw = MIX_W // SWA_KV_HEADS
    tab = pl.BlockSpec((R, LANES), lambda g, h: (0, 0))
    cache = pl.BlockSpec((None, bb, None, SWA_HEAD_DIM, WINDOW), lambda g, h: (layer, g, h, 0, 0))
    return pl.pallas_call(
        functools.partial(_swa_sample_kernel, layer=layer, steps=steps, bb=bb),
        grid=(B // bb, SWA_KV_HEADS),
        in_specs=[
            pl.BlockSpec(memory_space=pltpu.SMEM),
            pl.BlockSpec((steps, bb, qw), lambda g, h: (0, g, COL_SQ // qw + h)),
            pl.BlockSpec((steps, bb, LANES), lambda g, h: (0, g, h // 2)),
            pl.BlockSpec((steps, bb, LANES), lambda g, h: (0, g, COL_SV // LANES + h // 2)),
            cache, cache,
            pl.BlockSpec((1, LANES), lambda g, h: (0, 0)), tab, tab, tab,
        ],
        out_specs=pl.BlockSpec((steps, bb, qw), lambda g, h: (0, g, h)),
        out_shape=jax.ShapeDtypeStruct((steps, B, MIX_W), F32),
        compiler_params=_params("parallel", "arbitrary"),
        name="swa_sample",
    )(sinks, proj3, khat3, proj3, cache_kt, cache_vt, g2, *tables)


def _mem_kv_kernel(x_ref, g_ref, w_ref, kg_ref, o_ref, xn_ref):
    j = pl.program_id(0)

    @pl.when(j == 0)
    def _():
        x = x_ref[...]
        ms = jnp.mean(x * x, axis=-1, keepdims=True)
        xn_ref[...] = (x * lax.rsqrt(ms + EPS) * g_ref[...]).astype(BF16)

    y = _dot(xn_ref[...], w_ref[...].astype(BF16))

    @pl.when(j < MEM_HEADS)
    def _():
        ms = jnp.mean(y * y, axis=-1, keepdims=True)
        o_ref[...] = y * lax.rsqrt(ms + EPS) * kg_ref[...]

    @pl.when(j >= MEM_HEADS)
    def _():
        o_ref[...] = y


def mem_kv(mem, mem_norm_g, w_mem_kv, mem_knorm_g, layer):
    M, K = mem.shape
    hd = MEM_HEAD_DIM
    return pl.pallas_call(
        _mem_kv_kernel,
        grid=(2 * MEM_HEADS,),
        in_specs=[
            pl.BlockSpec((M, K), lambda j: (0, 0)),
            pl.BlockSpec((None, 1, K), lambda j: (layer, 0, 0)),
            pl.BlockSpec((None, K, hd), lambda j: (layer, 0, j)),
            pl.BlockSpec((None, 1, hd), lambda j: (layer, 0, 0)),
        ],
        out_specs=pl.BlockSpec((M, hd), lambda j: (0, j)),
        out_shape=jax.ShapeDtypeStruct((M, 2 * MIX_W), F32),
        scratch_shapes=[pltpu.VMEM((M, K), BF16)],
        compiler_params=_params("arbitrary"),
        name="mem_kv",
    )(mem, mem_norm_g, w_mem_kv, mem_knorm_g)


def _mem_qnorm(q, g):
    ms = jnp.mean(q * q, axis=-1, keepdims=True)
    return q * lax.rsqrt(ms + EPS) * g * (MEM_HEAD_DIM ** -0.5)


def _softmax_rows(s):
    m = jnp.max(s, axis=-1, keepdims=True)
    e = jnp.exp(s - m)
    return e / jnp.sum(e, axis=-1, keepdims=True)


def _mem_prompt_kernel(q0_ref, q1_ref, q2_ref, q3_ref, kv_ref, g_ref, y_ref):
    hd = MEM_HEAD_DIM
    for h, q_ref in enumerate((q0_ref, q1_ref, q2_ref, q3_ref)):
        q = _mem_qnorm(q_ref[...], g_ref[...]).astype(BF16)
        p = _softmax_rows(_dot_nt(q, kv_ref[:, h * hd:(h + 1) * hd].astype(BF16)))
        v = kv_ref[:, MIX_W + h * hd:MIX_W + (h + 1) * hd].astype(BF16)
        y_ref[:, h * hd:(h + 1) * hd] = _dot(p.astype(BF16), v).astype(BF16)


def _light_mixers_kernel(k_ref, gk_ref, ct_ref, sn_ref, sp_ref,
                         u_ref, prev_ref, pw_ref, sc_ref,
                         q0_ref, q1_ref, q2_ref, q3_ref, kv_ref, gq_ref,
                         khat_ref, yb_ref, ym_ref, ext_ref, sum_ref, *, tm):
    _kprep_kernel(k_ref, gk_ref, ct_ref, sn_ref, sp_ref, khat_ref)
    _pool_prompt_kernel(u_ref, prev_ref, pw_ref, sc_ref, yb_ref, ext_ref, sum_ref, tm=tm)
    _mem_prompt_kernel(q0_ref, q1_ref, q2_ref, q3_ref, kv_ref, gq_ref, ym_ref)


def light_mixers_prompt(proj, gk2, tables, pool_w, pool_scale, kv, mem_qnorm_g, layer, tm):
    T = proj.shape[0]
    kw = SWA_KV_HEADS * SWA_HEAD_DIM
    hd = MEM_HEAD_DIM
    cb = COL_POOL // MIX_W
    tab = pl.BlockSpec((tm, LANES), lambda i: (i, 0))

    def q_spec(h):
        qb = COL_MQ // hd + h
        return pl.BlockSpec((tm, hd), lambda i: (i, qb))

    return pl.pallas_call(
        functools.partial(_light_mixers_kernel, tm=tm),
        grid=(T // tm,),
        in_specs=[
            pl.BlockSpec((tm, kw), lambda i: (i, COL_SK // kw)),
            pl.BlockSpec((1, LANES), lambda i: (0, 0)), tab, tab, tab,
            pl.BlockSpec((tm, MIX_W), lambda i: (i, cb)),
            pl.BlockSpec((POOL_HIST, MIX_W), lambda i: (jnp.maximum(i * (tm // POOL_HIST) - 1, 0), cb)),
            pl.BlockSpec((None, len(POOL_WINDOWS), POOL_GC, POOL_GC), lambda i: (layer, 0, 0, 0)),
            pl.BlockSpec((None, 1, MIX_W), lambda i: (layer, 0, 0)),
        ] + [q_spec(h) for h in range(MEM_HEADS)] + [
            pl.BlockSpec((N_MEM, 2 * MIX_W), lambda i: (0, 0)),
            pl.BlockSpec((None, 1, hd), lambda i: (layer, 0, 0)),
        ],
        out_specs=[
            pl.BlockSpec((tm, kw), lambda i: (i, 0)),
            pl.BlockSpec((tm, MIX_W), lambda i: (i, 0)),
            pl.BlockSpec((tm, MIX_W), lambda i: (i, 0)),
        ],
        out_shape=[
            jax.ShapeDtypeStruct((T, kw), F32),
            jax.ShapeDtypeStruct((T, MIX_W), BF16),
            jax.ShapeDtypeStruct((T, MIX_W), BF16),
        ],
        scratch_shapes=[pltpu.VMEM((POOL_HIST + tm, MIX_W), F32), pltpu.VMEM((POOL_HIST + tm, MIX_W), F32)],
        compiler_params=_params("arbitrary"),
        name="light_mixers_prompt",
    )(proj, gk2, *tables, proj, proj, pool_w, pool_scale, proj, proj, proj, proj, kv, mem_qnorm_g)


def _mem_rows_view(c):
    L_, B_, M, H, hd = c.shape
    c = c.reshape(L_, B_, M, H, hd // LANES, LANES)
    return jnp.transpose(c, (0, 1, 2, 4, 3, 5)).reshape(L_, B_, M * H * (hd // LANES), LANES)


def _mem_head(c_ref, b, h):
    nt = MEM_HEAD_DIM // LANES
    parts = [c_ref[b, pl.ds(lt * MEM_HEADS + h, N_MEM, stride=nt * MEM_HEADS), :] for lt in range(nt)]
    return jnp.concatenate(parts, axis=1).astype(BF16)


def _mem_sample_kernel(q0_ref, q1_ref, q2_ref, q3_ref, k_ref, v_ref, g_ref, y_ref, *, steps, bb):
    R = steps * bb
    hd = MEM_HEAD_DIM
    bq = lax.broadcasted_iota(jnp.int32, (R, 1), 0) % bb
    for h, q_ref in enumerate((q0_ref, q1_ref, q2_ref, q3_ref)):
        q = _mem_qnorm(jnp.concatenate([q_ref[t] for t in range(steps)], axis=0), g_ref[...])
        s = None
        for b in range(bb):
            d = _dot_nt(jnp.where(bq == b, q, 0.0).astype(BF16), _mem_head(k_ref, b, h))
            s = d if s is None else s + d
        p = _softmax_rows(s)
        o = None
        for b in range(bb):
            d = _dot(jnp.where(bq == b, p, 0.0).astype(BF16), _mem_head(v_ref, b, h))
            o = d if o is None else o + d
        for t in range(steps):
            y_ref[t, :, h * hd:(h + 1) * hd] = o[t * bb:(t + 1) * bb]


def mem_attn_sample(proj3, cache_k, cache_v, mem_qnorm_g, layer, bb):
    steps, B, _ = proj3.shape
    hd = MEM_HEAD_DIM
    cache = pl.BlockSpec((None, bb) + cache_k.shape[2:], lambda g: (layer, g, 0, 0))

    def q_spec(h):
        cb = COL_MQ // hd + h
        return pl.BlockSpec((steps, bb, hd), lambda g: (0, g, cb))

    return pl.pallas_call(
        functools.partial(_mem_sample_kernel, steps=steps, bb=bb),
        grid=(B // bb,),
        in_specs=[q_spec(h) for h in range(MEM_HEADS)] + [
            cache, cache,
            pl.BlockSpec((None, 1, hd), lambda g: (layer, 0, 0)),
        ],
        out_specs=pl.BlockSpec((steps, bb, MIX_W), lambda g: (0, g, 0)),
        out_shape=jax.ShapeDtypeStruct((steps, B, MIX_W), F32),
        compiler_params=_params("arbitrary"),
        name="mem_sample",
    )(proj3, proj3, proj3, proj3, cache_k, cache_v, mem_qnorm_g)


def _row_tile(T, cap):
    t = cap
    while T % t:
        t //= 2
    return t


TM_STREAM = 2048
TM_DOWN = 1024
TM_LOCAL = 512
TM_MERGE = 512
TM_RESIDENT = 256
TN_STREAM = 512
TN_DOWN = 256
TN_SAMPLE_IN = 1536
UP_PIECE_ROWS = 1024
HGRN_ROWS = 2048
HGRN_BATCH_UNROLL = 16
SWA_BLOCKS = 8
POOL_BATCH = 64
ATTN_BATCH = 8


def _token_tail(x, xn, ys, layer, w_in, w_branch, w_o, norm2_g):
    T = x.shape[0]
    tm = _row_tile(T, TM_MERGE)
    merged = merge_branches(xn, ys, w_in, w_branch, layer, tm, TN_STREAM if T // tm > 1 else TN_DOWN)
    return matmul_res_norm(merged, w_o, layer, x, norm2_g, _row_tile(T, TM_RESIDENT))


def kernel(x_prompt, x_sample, mem_prompt, state_hgrn, cache_pool, cache_swa_k, cache_swa_v, state_conv, cache_mem_k, cache_mem_v, norm1_g, w_in, hgrn_lb, hgrn_norm_g, pool_w, pool_scale, swa_qnorm_g, swa_knorm_g, swa_sinks, mem_norm_g, w_mem_kv, mem_qnorm_g, mem_knorm_g, w_branch, w_o, norm2_g, w_up, conv_w, conv_b, w_down):
    depth = w_in.shape[0]
    bp, L, _ = x_prompt.shape
    B, steps, _ = x_sample.shape
    assert bp == 1
    kw = SWA_KV_HEADS * SWA_HEAD_DIM
    Ts = steps * B

    lb_all = jnp.cumsum(jax.nn.softmax(hgrn_lb.astype(F32), axis=0), axis=0)
    lb_all = lb_all - lb_all[:1]

    tab_p = _rope_tables(np.arange(L))
    tab_s = _rope_tables(np.repeat(PAST_LEN + np.arange(steps), B))
    tab_sb = _rope_tables(np.repeat(PAST_LEN + np.arange(steps), ATTN_BATCH))

    xp = x_prompt.reshape(L, D_MODEL)
    xs = jnp.transpose(x_sample, (1, 0, 2)).reshape(Ts, D_MODEL)
    mem = mem_prompt.reshape(N_MEM, D_MODEL)
    ckt_all = jnp.transpose(cache_swa_k, (0, 1, 3, 4, 2))
    cvt_all = jnp.transpose(cache_swa_v, (0, 1, 3, 4, 2))
    mk_rows = _mem_rows_view(cache_mem_k)
    mv_rows = _mem_rows_view(cache_mem_v)
    cpool_v = jnp.transpose(cache_pool, (0, 2, 1, 3))
    row3 = lambda a: a.reshape(depth, 1, a.shape[-1])
    norm1_g, norm2_g, pool_scale, conv_b = row3(norm1_g), row3(norm2_g), row3(pool_scale), row3(conv_b)
    mem_norm_g, mem_qnorm_g, mem_knorm_g = row3(mem_norm_g), row3(mem_qnorm_g), row3(mem_knorm_g)
    tm_p = _row_tile(L, TM_STREAM)
    tl_p = _row_tile(L, TM_LOCAL)

    outs = {k: [] for k in ("sp", "pp", "ps", "kp", "ks", "vp", "vs", "cp", "mk", "mv")}
    hgrn_states = None
    conv_states = None
    for l in range(depth):
        lb = lb_all[l].reshape(1, MIX_W)
        gn = hgrn_norm_g[l].reshape(1, MIX_W)
        gq2 = jnp.tile(swa_qnorm_g[l], 2).reshape(1, LANES)
        gk2 = jnp.tile(swa_knorm_g[l], 2).reshape(1, LANES)

        kv = mem_kv(mem, mem_norm_g, w_mem_kv, mem_knorm_g, l)

        xn = prenorm(xp, norm1_g, l, tl_p)
        proj = matmul_cols(xn, w_in, l, COL_GATE, tm_p, TN_STREAM)
        ya, s_p = hgrn_prompt(proj, lb, gn, _row_tile(L, HGRN_ROWS))
        khat, yb, ym = light_mixers_prompt(proj, gk2, tab_p, pool_w, pool_scale, kv, mem_qnorm_g, l, tl_p)
        yc = swa_prompt(proj, khat, swa_sinks, gq2, tab_p, l, _row_tile(L, SWA_BLOCKS * WINDOW) // WINDOW)
        h, hn = _token_tail(xp, xn, (ya, yb, yc, ym), l, w_in, w_branch, w_o, norm2_g)
        gact, a_tail = up_conv_prompt(hn, w_up, conv_w, conv_b, l, tm_p, TN_STREAM)
        xp = matmul_res(gact, w_down, l, h, _row_tile(L, TM_DOWN), TN_DOWN)

        outs["sp"].append(s_p[None])
        outs["pp"].append(proj[None, L - POOL_BUF:, COL_POOL:COL_POOL + MIX_W])
        outs["kp"].append(khat[None, L - WINDOW:].reshape(1, WINDOW, SWA_KV_HEADS, SWA_HEAD_DIM))
        outs["vp"].append(proj[None, L - WINDOW:, COL_SV:COL_SV + kw].reshape(1, WINDOW, SWA_KV_HEADS, SWA_HEAD_DIM))
        outs["cp"].append(a_tail[-1:, CONV_HIST - 2:])
        outs["mk"].append(kv[None, :, :MIX_W].reshape(1, N_MEM, MEM_HEADS, MEM_HEAD_DIM))
        outs["mv"].append(kv[None, :, MIX_W:].reshape(1, N_MEM, MEM_HEADS, MEM_HEAD_DIM))

        xn = prenorm(xs, norm1_g, l, Ts)
        proj_s = matmul_cols(xn, w_in, l, COL_GATE, Ts, TN_SAMPLE_IN)
        proj3 = proj_s.reshape(steps, B, COL_GATE)
        ya, hgrn_states = hgrn_sample(proj3, lb, gn, state_hgrn, l, LANES // steps, hgrn_states)
        yb = pool_sample(proj3, cpool_v, pool_w, pool_scale, l, POOL_BATCH)
        khat_s = swa_kprep(proj_s, gk2, tab_s, Ts)
        khat3 = khat_s.reshape(steps, B, kw)
        yc = swa_sample(proj3, khat3, ckt_all, cvt_all, swa_sinks, gq2, tab_sb, l, ATTN_BATCH)
        ym = mem_attn_sample(proj3, mk_rows, mv_rows, mem_qnorm_g, l, ATTN_BATCH)
        ys = tuple(y.reshape(Ts, MIX_W).astype(BF16) for y in (ya, yb, yc, ym))
        h, hn = _token_tail(xs, xn, ys, l, w_in, w_branch, w_o, norm2_g)
        gact, conv_states = up_conv_sample(hn, w_up, state_conv, conv_w, conv_b, l, steps, TN_STREAM, conv_states)
        xs = matmul_res(gact, w_down, l, h, Ts, TN_DOWN)

        outs["ps"].append(proj3[:, :, COL_POOL:COL_POOL + MIX_W])
        to_window_minor = lambda a: jnp.transpose(a.reshape(steps, B, SWA_KV_HEADS, SWA_HEAD_DIM), (1, 2, 3, 0))
        outs["ks"].append(to_window_minor(khat3))
        outs["vs"].append(to_window_minor(proj3[:, :, COL_SV:COL_SV + kw]))

    stk = lambda k: jnp.stack(outs[k], axis=0)
    pool_s = jnp.transpose(jnp.concatenate([cpool_v[:, steps:], stk("ps")], axis=1), (0, 2, 1, 3))
    def slide_window(old, new):
        lead = [(0, 0, 0)] * (old.ndim - 1)
        shifted = lax.pad(old, jnp.zeros((), old.dtype), lead + [(-steps, steps, 0)])
        tail = lax.pad(new, jnp.zeros((), old.dtype), lead + [(WINDOW - steps, 0, 0)])
        pos = lax.broadcasted_iota(jnp.int32, old.shape, old.ndim - 1)
        return jnp.transpose(jnp.where(pos < WINDOW - steps, shifted, tail), (0, 1, 4, 2, 3))

    swa_k_s = slide_window(ckt_all, stk("ks"))
    swa_v_s = slide_window(cvt_all, stk("vs"))
    y_prompt = xp.reshape(1, L, D_MODEL)
    y_sample = jnp.transpose(xs.reshape(steps, B, D_MODEL), (1, 0, 2))
    return (y_prompt, y_sample,
            stk("sp"), hgrn_states, stk("pp"), pool_s, stk("kp"), swa_k_s, stk("vp"), swa_v_s,
            stk("cp"), conv_states, jnp.concatenate(outs["mk"], axis=0)[:, None], jnp.concatenate(outs["mv"], axis=0)[:, None])
```

```python
import functools

import numpy as np
import jax
import jax.numpy as jnp
from jax import lax
from jax.experimental import pallas as pl
from jax.experimental.pallas import tpu as pltpu

F32 = jnp.float32
BF16 = jnp.bfloat16

D_MODEL = 2048
MIX_W = D_MODEL // 2
N_BRANCH = 4
A_DK = 128
A_HEADS = MIX_W // A_DK
POOL_WINDOWS = (2, 4, 8, 16)
POOL_GC = MIX_W // len(POOL_WINDOWS)
POOL_BUF = max(POOL_WINDOWS) - 1
SWA_HEAD_DIM = 64
SWA_Q_HEADS = MIX_W // SWA_HEAD_DIM
SWA_KV_HEADS = SWA_Q_HEADS // 4
WINDOW = 128
ROT_DIM = SWA_HEAD_DIM // 4
ROPE_THETA = 500000.0
N_MEM = 256
MEM_HEADS = 4
MEM_HEAD_DIM = MIX_W // MEM_HEADS
D_FF = 11 * D_MODEL // 4
EPS = 1e-6
PAST_LEN = 8192

COL_HQ, COL_HF, COL_HI, COL_HG = 0, MIX_W, 2 * MIX_W, 3 * MIX_W
COL_POOL = 4 * MIX_W
COL_SQ = 5 * MIX_W
COL_SK = 6 * MIX_W
COL_SV = COL_SK + SWA_KV_HEADS * SWA_HEAD_DIM
COL_MQ = COL_SV + SWA_KV_HEADS * SWA_HEAD_DIM
COL_GATE = COL_MQ + MIX_W
IN_COLS = COL_GATE + N_BRANCH * D_MODEL

LANES = 128
V7X_VMEM_BYTES = 64 * 1024 * 1024
VMEM_LIMIT = V7X_VMEM_BYTES * 7 // 8
HGRN_CHUNK = 128
NEG_BIG = -1e30
LOG2_E = 1.4426950408889634


def _params(*sem):
    return pltpu.CompilerParams(dimension_semantics=sem, vmem_limit_bytes=VMEM_LIMIT)


def _sigmoid(x):
    return 0.5 * jnp.tanh(0.5 * x) + 0.5


def _dot(a, b):
    return jnp.dot(a, b, preferred_element_type=F32)


def _dot_nt(a, b):
    return lax.dot_general(a, b, (((1,), (1,)), ((), ())), preferred_element_type=F32)


def _skip_ref(kernel_fn, idx):
    def wrapped(*refs):
        return kernel_fn(*refs[:idx], *refs[idx + 1:])
    return wrapped


def _layer_slab_call(make_kernel, in_specs, args, slab_out, layer, slab_block, slab_index, out_specs, **kw):
    n_layers = kw["out_shape"][-1].shape[0]
    if slab_out is None:
        spec = pl.BlockSpec((n_layers,) + slab_block, lambda *g: (0,) + slab_index(*g))
        return pl.pallas_call(make_kernel(layer), in_specs=in_specs, out_specs=list(out_specs) + [spec], **kw)(*args)
    spec = pl.BlockSpec((1,) + slab_block, lambda *g: (layer,) + slab_index(*g))
    idx = len(args)
    return pl.pallas_call(
        _skip_ref(make_kernel(0), idx),
        in_specs=list(in_specs) + [pl.BlockSpec(memory_space=pl.ANY)],
        out_specs=list(out_specs) + [spec],
        input_output_aliases={idx: len(kw["out_shape"]) - 1},
        **kw)(*args, slab_out)


def _zero_other_slabs(so_ref, own):
    for l in range(so_ref.shape[0]):
        if l != own:
            so_ref[l] = jnp.zeros(so_ref.shape[1:], so_ref.dtype)


def _rms_rows(x, g):
    ms = jnp.mean(x * x, axis=-1, keepdims=True)
    return x * lax.rsqrt(ms + EPS) * g


def _prenorm_kernel(x_ref, g_ref, o_ref):
    o_ref[...] = _rms_rows(x_ref[...], g_ref[...]).astype(BF16)


def prenorm(x, g, layer, tm):
    T, K = x.shape
    return pl.pallas_call(
        _prenorm_kernel,
        grid=(T // tm,),
        in_specs=[pl.BlockSpec((tm, K), lambda i: (i, 0)),
                  pl.BlockSpec((None, 1, K), lambda i: (layer, 0, 0))],
        out_specs=pl.BlockSpec((tm, K), lambda i: (i, 0)),
        out_shape=jax.ShapeDtypeStruct((T, K), BF16),
        compiler_params=_params("arbitrary"),
        name="prenorm",
    )(x, g)


def _matmul_kernel(a_ref, w_ref, o_ref):
    o_ref[...] = _dot(a_ref[...], w_ref[...].astype(BF16))


def matmul_cols(a, w, layer, n_cols, tm, tn):
    T, K = a.shape
    return pl.pallas_call(
        _matmul_kernel,
        grid=(T // tm, n_cols // tn),
        in_specs=[
            pl.BlockSpec((tm, K), lambda i, j: (i, 0)),
            pl.BlockSpec((None, K, tn), lambda i, j: (layer, 0, j)),
        ],
        out_specs=pl.BlockSpec((tm, tn), lambda i, j: (i, j)),
        out_shape=jax.ShapeDtypeStruct((T, n_cols), F32),
        compiler_params=_params("parallel", "arbitrary"),
        name="matmul_cols",
    )(a, w)


def _matmul_res_kernel(a_ref, w_ref, r_ref, o_ref):
    o_ref[...] = r_ref[...] + _dot(a_ref[...], w_ref[...].astype(BF16))


def matmul_res(a, w, layer, res, tm, tn):
    T, K = a.shape
    N = w.shape[2]
    return pl.pallas_call(
        _matmul_res_kernel,
        grid=(T // tm, N // tn),
        in_specs=[
            pl.BlockSpec((tm, K), lambda i, j: (i, 0)),
            pl.BlockSpec((None, K, tn), lambda i, j: (layer, 0, j)),
            pl.BlockSpec((tm, tn), lambda i, j: (i, j)),
        ],
        out_specs=pl.BlockSpec((tm, tn), lambda i, j: (i, j)),
        out_shape=jax.ShapeDtypeStruct((T, N), F32),
        compiler_params=_params("parallel", "arbitrary"),
        name="matmul_res",
    )(a, w, res)


def _res_norm_kernel(a_ref, w_ref, r_ref, g_ref, h_ref, hn_ref, w_s):
    @pl.when(pl.program_id(0) == 0)
    def _():
        w_s[...] = w_ref[...].astype(BF16)

    h = r_ref[...] + _dot(a_ref[...], w_s[...])
    h_ref[...] = h
    hn_ref[...] = _rms_rows(h, g_ref[...]).astype(BF16)


def matmul_res_norm(a, w, layer, res, g, tm):
    T, K = a.shape
    N = w.shape[2]
    return pl.pallas_call(
        _res_norm_kernel,
        grid=(T // tm,),
        in_specs=[
            pl.BlockSpec((tm, K), lambda i: (i, 0)),
            pl.BlockSpec((None, K, N), lambda i: (layer, 0, 0), pipeline_mode=pl.Buffered(1)),
            pl.BlockSpec((tm, N), lambda i: (i, 0)),
            pl.BlockSpec((None, 1, N), lambda i: (layer, 0, 0)),
        ],
        out_specs=[pl.BlockSpec((tm, N), lambda i: (i, 0)), pl.BlockSpec((tm, N), lambda i: (i, 0))],
        out_shape=[jax.ShapeDtypeStruct((T, N), F32), jax.ShapeDtypeStruct((T, N), BF16)],
        scratch_shapes=[pltpu.VMEM((K, N), BF16)],
        compiler_params=_params("arbitrary"),
        name="matmul_res_norm",
    )(a, w, res, g)


def _merge_kernel(xn_ref, ya_ref, yb_ref, yc_ref, ym_ref, wg0_ref, wg1_ref, wg2_ref, wg3_ref, wb_ref, o_ref,
                  wg_s, wb_s):
    @pl.when(pl.program_id(1) == 0)
    def _():
        for n, wg_ref in enumerate((wg0_ref, wg1_ref, wg2_ref, wg3_ref)):
            wg_s[n] = wg_ref[...].astype(BF16)
            wb_s[n] = wb_ref[n].astype(BF16)

    xn = xn_ref[...]
    acc = None
    for n, y_ref in enumerate((ya_ref, yb_ref, yc_ref, ym_ref)):
        t = _sigmoid(_dot(xn, wg_s[n])) * _dot(y_ref[...], wb_s[n])
        acc = t if acc is None else acc + t
    o_ref[...] = acc.astype(BF16)


def merge_branches(xn, ys, w_in, w_branch, layer, tm, tn):
    T = xn.shape[0]
    once = pl.Buffered(1) if T // tm > 1 else None
    y_spec = pl.BlockSpec((tm, MIX_W), lambda j, i: (i, 0))

    def gate_spec(n):
        off = (COL_GATE + n * D_MODEL) // tn
        return pl.BlockSpec((None, D_MODEL, tn), lambda j, i: (layer, 0, off + j), pipeline_mode=once)

    return pl.pallas_call(
        _merge_kernel,
        grid=(D_MODEL // tn, T // tm),
        in_specs=[pl.BlockSpec((tm, D_MODEL), lambda j, i: (i, 0))] + [y_spec] * 4
        + [gate_spec(n) for n in range(N_BRANCH)]
        + [pl.BlockSpec((None, N_BRANCH, MIX_W, tn), lambda j, i: (layer, 0, 0, j), pipeline_mode=once)],
        out_specs=pl.BlockSpec((tm, tn), lambda j, i: (i, j)),
        out_shape=jax.ShapeDtypeStruct((T, D_MODEL), BF16),
        scratch_shapes=[pltpu.VMEM((N_BRANCH, D_MODEL, tn), BF16), pltpu.VMEM((N_BRANCH, MIX_W, tn), BF16)],
        compiler_params=_params("arbitrary", "arbitrary"),
        name="merge_branches",
    )(xn, *ys, w_in, w_in, w_in, w_in, w_branch)


CONV_HIST = 8


def _gelu(x):
    return 0.5 * x * (1.0 + lax.erf(x * (2.0 ** -0.5)))


def _up_conv_prompt_kernel(xn_ref, wa_ref, wv_ref, cw_ref, cb_ref, g_ref, tail_ref, carry_ref, *, tm, rc):
    i, j = pl.program_id(0), pl.program_id(1)
    wa = wa_ref[...].astype(BF16)
    wv = wv_ref[...].astype(BF16)
    prev = jnp.where(i == 0, 0.0, carry_ref[j])
    row = lax.broadcasted_iota(jnp.int32, (rc, wa.shape[1]), 0)
    for c in range(tm // rc):
        sl = pl.ds(c * rc, rc)
        xn = xn_ref[sl, :]
        a = _dot(xn, wa)
        v = _dot(xn, wv)
        a1 = jnp.where(row == 0, prev[CONV_HIST - 1:CONV_HIST], pltpu.roll(a, 1, 0))
        a2 = jnp.where(row == 0, prev[CONV_HIST - 2:CONV_HIST - 1],
                       jnp.where(row == 1, prev[CONV_HIST - 1:CONV_HIST], pltpu.roll(a, 2, 0)))
        cc = cb_ref[...] + cw_ref[0:1, :] * a2 + cw_ref[1:2, :] * a1 + cw_ref[2:3, :] * a
        g_ref[sl, :] = (_gelu(cc) * v).astype(BF16)
        prev = a[rc - CONV_HIST:, :]
    carry_ref[j] = prev
    tail_ref[...] = prev


def up_conv_prompt(xn, w_up, conv_w, conv_b, layer, tm, tn):
    T, K = xn.shape
    nc = D_FF // tn
    return pl.pallas_call(
        functools.partial(_up_conv_prompt_kernel, tm=tm, rc=min(tm, UP_PIECE_ROWS)),
        grid=(T // tm, nc),
        in_specs=[
            pl.BlockSpec((tm, K), lambda i, j: (i, 0)),
            pl.BlockSpec((None, K, tn), lambda i, j: (layer, 0, j)),
            pl.BlockSpec((None, K, tn), lambda i, j: (layer, 0, nc + j)),
            pl.BlockSpec((None, 3, tn), lambda i, j: (layer, 0, j)),
            pl.BlockSpec((None, 1, tn), lambda i, j: (layer, 0, j)),
        ],
        out_specs=[
            pl.BlockSpec((tm, tn), lambda i, j: (i, j)),
            pl.BlockSpec((None, CONV_HIST, tn), lambda i, j: (i, 0, j)),
        ],
        out_shape=[
            jax.ShapeDtypeStruct((T, D_FF), BF16),
            jax.ShapeDtypeStruct((T // tm, CONV_HIST, D_FF), F32),
        ],
        scratch_shapes=[pltpu.VMEM((nc, CONV_HIST, tn), F32)],
        compiler_params=_params("arbitrary", "arbitrary"),
        name="up_conv_prompt",
    )(xn, w_up, w_up, conv_w, conv_b)


def _up_conv_sample_kernel(xn_ref, wa_ref, wv_ref, st_ref, cw_ref, cb_ref, g_ref, so_ref, *, steps, B, own):
    _zero_other_slabs(so_ref, own)
    xn = xn_ref[...]
    a = _dot(xn, wa_ref[...].astype(BF16))
    v = _dot(xn, wv_ref[...].astype(BF16))
    hist = [st_ref[:, 0, :], st_ref[:, 1, :]] + [a[t * B:(t + 1) * B] for t in range(steps)]
    for t in range(steps):
        c = cb_ref[...] + cw_ref[0:1, :] * hist[t] + cw_ref[1:2, :] * hist[t + 1] + cw_ref[2:3, :] * hist[t + 2]
        g_ref[t * B:(t + 1) * B, :] = (_gelu(c) * v[t * B:(t + 1) * B]).astype(BF16)
    so_ref[own, :, 0, :] = hist[steps]
    so_ref[own, :, 1, :] = hist[steps + 1]


def up_conv_sample(xn, w_up, state_conv, conv_w, conv_b, layer, steps, tn, state_out):
    T, K = xn.shape
    B = T // steps
    nc = D_FF // tn
    in_specs = [
        pl.BlockSpec((T, K), lambda j: (0, 0)),
        pl.BlockSpec((None, K, tn), lambda j: (layer, 0, j)),
        pl.BlockSpec((None, K, tn), lambda j: (layer, 0, nc + j)),
        pl.BlockSpec((None, B, 2, tn), lambda j: (layer, 0, 0, j)),
        pl.BlockSpec((None, 3, tn), lambda j: (layer, 0, j)),
        pl.BlockSpec((None, 1, tn), lambda j: (layer, 0, j)),
    ]
    return _layer_slab_call(
        lambda own: functools.partial(_up_conv_sample_kernel, steps=steps, B=B, own=own),
        in_specs, [xn, w_up, w_up, state_conv, conv_w, conv_b], state_out, layer,
        (B, 2, tn), lambda j: (0, 0, j),
        [pl.BlockSpec((T, tn), lambda j: (0, j))],
        grid=(nc,),
        out_shape=[
            jax.ShapeDtypeStruct((T, D_FF), BF16),
            jax.ShapeDtypeStruct(state_conv.shape, F32),
        ],
        compiler_params=_params("arbitrary"),
        name="up_conv_sample",
    )


def _hgrn_gates(q_in, z, lb):
    q = q_in * _sigmoid(q_in)
    log_sig = jnp.minimum(z, 0.0) - jnp.log(1.0 + jnp.exp(-jnp.abs(z)))
    a1 = jnp.log(lb)
    a2 = jnp.log1p(-lb) + log_sig
    log_f = jnp.maximum(a1, a2) + jnp.log(1.0 + jnp.exp(-jnp.abs(a1 - a2)))
    k = (1.0 - lb) * _sigmoid(-z)
    return q, log_f, k


def _hgrn_out(o, gate, gn):
    ms = jnp.mean(o * o, axis=-1, keepdims=True)
    return o * lax.rsqrt(ms + EPS) * gn * (gate * _sigmoid(gate))


def _cumsum_rows(x, tril):
    hi = x.astype(BF16)
    r1 = x - hi.astype(F32)
    mid = r1.astype(BF16)
    lo = (r1 - mid.astype(F32)).astype(BF16)
    return _dot(tril, hi) + _dot(tril, mid) + _dot(tril, lo)


def _block_row(x, blk, r):
    C = x.shape[0]
    x3 = x.reshape(C // blk, blk, LANES)
    return jnp.broadcast_to(x3[:, r:r + 1, :], (C // blk, blk, LANES)).reshape(C, LANES)


def _hgrn_pair_codes():
    t = np.arange(HGRN_CHUNK)[:, None]
    s = np.arange(HGRN_CHUNK)[None, :]
    level = np.floor(np.log2(np.maximum(t ^ s, 1))).astype(np.int32)
    return jnp.asarray(np.where(s > t, -1, np.where(s == t, 0, 1 + level)), jnp.int32)


def _boundary_row(b, m):
    if 2 * m >= 8:
        return _block_row(b, 2 * m, m - 1)
    r8 = lax.broadcasted_iota(jnp.int32, b.shape, 0) & 7
    if m == 2:
        return jnp.where(r8 < 4, _block_row(b, 8, 1), _block_row(b, 8, 5))
    return jnp.where((r8 & 1) == 0, b, pltpu.roll(b, 1, 0))


def _hgrn_attention(q, k, b, code):
    C = HGRN_CHUNK
    rowl = lax.broadcasted_iota(jnp.int32, (C, LANES), 0)
    att = jnp.where(code == 0, jnp.sum(q * k, axis=-1, keepdims=True), 0.0)
    m, level = 1, 1
    while m < C:
        d = b - _boundary_row(b, m)
        isq = (rowl & m) != 0
        x = (jnp.where(isq, q, k) * jnp.exp2(jnp.abs(d) * (-LOG2_E))).astype(BF16)
        att = jnp.where(code == level, _dot_nt(x, x), att)
        m *= 2
        level += 1
    return att


def _hgrn_apply(q, k, v, b, att, S):
    C = HGRN_CHUNK
    vb = v.astype(BF16)
    o = _dot((q * jnp.exp(b)).astype(BF16), S.astype(BF16)) + _dot(att.astype(BF16), vb)
    bl = b[C - 1:C, :]
    kk = k * jnp.exp(bl - b)
    ecol = jnp.transpose(jnp.broadcast_to(jnp.exp(bl), (LANES, LANES)))
    return o, ecol * S + _dot(jnp.transpose(kk).astype(BF16), vb)


def _hgrn_prompt_kernel(q_ref, f_ref, i_ref, g_ref, lb_ref, gn_ref, code_ref, y_ref, so_ref, s_ref, *, rows):
    @pl.when(pl.program_id(1) == 0)
    def _():
        s_ref[...] = jnp.zeros_like(s_ref)

    C = HGRN_CHUNK
    code = code_ref[...]
    tril = jnp.where(code >= 0, 1.0, 0.0).astype(BF16)
    q, log_f, k = _hgrn_gates(q_ref[...], f_ref[...], lb_ref[...])
    chunks = [slice(c * C, (c + 1) * C) for c in range(rows // C)]
    bs = [_cumsum_rows(log_f[sl], tril) for sl in chunks]
    atts = [_hgrn_attention(q[sl], k[sl], b, code) for sl, b in zip(chunks, bs)]
    S = s_ref[...]
    outs = []
    for sl, b, att in zip(chunks, bs, atts):
        o, S = _hgrn_apply(q[sl], k[sl], i_ref[sl, :], b, att, S)
        outs.append(o)
    s_ref[...] = S
    y_ref[...] = _hgrn_out(jnp.concatenate(outs, axis=0), g_ref[...], gn_ref[...]).astype(BF16)

    @pl.when(pl.program_id(1) == pl.num_programs(1) - 1)
    def _():
        so_ref[...] = s_ref[...]


def hgrn_prompt(proj, lb, gn, rows):
    T = proj.shape[0]

    def col(off):
        base = off // LANES
        return pl.BlockSpec((rows, LANES), lambda h, c: (c, base + h))

    vec = pl.BlockSpec((1, LANES), lambda h, c: (0, h))
    return pl.pallas_call(
        functools.partial(_hgrn_prompt_kernel, rows=rows),
        grid=(A_HEADS, T // rows),
        in_specs=[col(COL_HQ), col(COL_HF), col(COL_HI), col(COL_HG), vec, vec,
                  pl.BlockSpec((HGRN_CHUNK, HGRN_CHUNK), lambda h, c: (0, 0))],
        out_specs=[
            pl.BlockSpec((rows, LANES), lambda h, c: (c, h)),
            pl.BlockSpec((None, A_DK, LANES), lambda h, c: (h, 0, 0)),
        ],
        out_shape=[
            jax.ShapeDtypeStruct((T, MIX_W), BF16),
            jax.ShapeDtypeStruct((A_HEADS, A_DK, LANES), F32),
        ],
        scratch_shapes=[pltpu.VMEM((A_DK, LANES), F32)],
        compiler_params=_params("parallel", "arbitrary"),
        name="hgrn_prompt",
    )(proj, proj, proj, proj, lb, gn, _hgrn_pair_codes())


def _hgrn_sample_kernel(q_ref, f_ref, i_ref, g_ref, lb_ref, gn_ref, s_ref, y_ref, so_ref, *, steps, bb, own):
    _zero_other_slabs(so_ref, own)
    lb = lb_ref[...]
    gn = gn_ref[...]
    qs, ks, vs, bs = [], [], [], []
    b = None
    for t in range(steps):
        q, log_f, k = _hgrn_gates(q_ref[t], f_ref[t], lb)
        b = log_f if b is None else b + log_f
        qs.append(q)
        ks.append(k)
        vs.append(i_ref[t])
        bs.append(b)
    intra = []
    for t in range(steps):
        acc = None
        for s in range(t + 1):
            w = jnp.sum(qs[t] * ks[s] * jnp.exp(bs[t] - bs[s]), axis=-1, keepdims=True)
            acc = w * vs[s] if acc is None else acc + w * vs[s]
        intra.append(acc)
    R = steps * bb
    q_stack = jnp.concatenate([qs[t] * jnp.exp(bs[t]) for t in range(steps)], axis=0).astype(BF16)
    k_stack = jnp.concatenate([ks[t] * jnp.exp(bs[-1] - bs[t]) for t in range(steps)], axis=0)
    v_stack = jnp.concatenate(vs, axis=0).astype(BF16)
    k_t = jnp.transpose(k_stack)
    f_pad = jnp.concatenate([jnp.exp(bs[-1])] + [jnp.zeros((R - bb, LANES), F32)], axis=0)
    f_t = jnp.transpose(f_pad)
    rowi = lax.broadcasted_iota(jnp.int32, (R, LANES), 0) % bb
    lanei = lax.broadcasted_iota(jnp.int32, (LANES, R), 1)

    def body(bi, o_acc):
        s_b = s_ref[bi]
        o_acc = jnp.where(rowi == bi, _dot(q_stack, s_b.astype(BF16)), o_acc)
        f_col = jnp.sum(jnp.where(lanei == bi, f_t, 0.0), axis=-1, keepdims=True)
        k_b = jnp.where(lanei % bb == bi, k_t, 0.0).astype(BF16)
        so_ref[own, bi] = f_col * s_b + _dot(k_b, v_stack)
        return o_acc

    o_inter = lax.fori_loop(0, bb, body, jnp.zeros((R, LANES), F32), unroll=HGRN_BATCH_UNROLL)
    for t in range(steps):
        o = o_inter[t * bb:(t + 1) * bb] + intra[t]
        y_ref[t] = _hgrn_out(o, g_ref[t], gn).astype(BF16)


def hgrn_sample(proj3, lb, gn, state, layer, bb, state_out):
    steps, B, _ = proj3.shape
    assert steps * bb == LANES

    def col(off):
        base = off // LANES
        return pl.BlockSpec((steps, bb, LANES), lambda g, h: (0, g, base + h))

    vec = pl.BlockSpec((1, LANES), lambda g, h: (0, h))
    slab = pl.BlockSpec((None, bb, None, A_DK, LANES), lambda g, h: (layer, g, h, 0, 0))
    return _layer_slab_call(
        lambda own: functools.partial(_hgrn_sample_kernel, steps=steps, bb=bb, own=own),
        [col(COL_HQ), col(COL_HF), col(COL_HI), col(COL_HG), vec, vec, slab],
        [proj3, proj3, proj3, proj3, lb, gn, state], state_out, layer,
        (bb, None, A_DK, LANES), lambda g, h: (g, h, 0, 0),
        [pl.BlockSpec((steps, bb, LANES), lambda g, h: (0, g, h))],
        grid=(B // bb, A_HEADS),
        out_shape=[
            jax.ShapeDtypeStruct((steps, B, MIX_W), BF16),
            jax.ShapeDtypeStruct(state.shape, F32),
        ],
        compiler_params=_params("parallel", "arbitrary"),
        name="hgrn_sample",
    )


POOL_HIST = 32


def _pool_project(d, w_ref, sc_ref, g):
    sl = slice(g * POOL_GC, (g + 1) * POOL_GC)
    return _dot(d.astype(BF16), w_ref[g].astype(BF16)) * sc_ref[:, sl]


def _pool_prompt_kernel(u_ref, prev_ref, w_ref, sc_ref, y_ref, ext_ref, sum_ref, *, tm):
    i = pl.program_id(0)
    H = POOL_HIST
    ext_ref[0:H, :] = jnp.where(i == 0, 0.0, prev_ref[...])
    ext_ref[H:, :] = u_ref[...]
    sum_ref[0:H // 2, :] = jnp.zeros((H // 2, MIX_W), F32)
    pos = i * tm + lax.broadcasted_iota(jnp.int32, (tm, 1), 0)
    src, w = ext_ref, 1
    for g, win in enumerate(POOL_WINDOWS):
        lanes = slice(g * POOL_GC, MIX_W)
        while w < win:
            n = H // 2 + tm
            sum_ref[pl.ds(H // 2, n), lanes] = src[pl.ds(H // 2, n), lanes] + src[pl.ds(H // 2 - w, n), lanes]
            src, w = sum_ref, 2 * w
        sl = slice(g * POOL_GC, (g + 1) * POOL_GC)
        cnt = jnp.minimum(pos + 1, win).astype(F32)
        d = src[pl.ds(H, tm), sl] / cnt - u_ref[:, sl]
        y_ref[:, sl] = _pool_project(d, w_ref, sc_ref, g).astype(BF16)


def _pool_sample_kernel(u_ref, c_ref, w_ref, sc_ref, y_ref, *, steps):
    for t in range(steps):
        for g, win in enumerate(POOL_WINDOWS):
            sl = slice(g * POOL_GC, (g + 1) * POOL_GC)
            acc = u_ref[t, :, sl]
            for j in range(1, win):
                if j <= t:
                    acc = acc + u_ref[t - j, :, sl]
                else:
                    acc = acc + c_ref[POOL_BUF + t - j, :, sl]
            d = acc / float(win) - u_ref[t, :, sl]
            y_ref[t, :, sl] = _pool_project(d, w_ref, sc_ref, g).astype(BF16)


def pool_sample(proj3, cache_pool, pool_w, pool_scale, layer, bb):
    steps, B, _ = proj3.shape
    cb = COL_POOL // MIX_W
    return pl.pallas_call(
        functools.partial(_pool_sample_kernel, steps=steps),
        grid=(B // bb,),
        in_specs=[
            pl.BlockSpec((steps, bb, MIX_W), lambda g: (0, g, cb)),
            pl.BlockSpec((None, POOL_BUF, bb, MIX_W), lambda g: (layer, 0, g, 0)),
            pl.BlockSpec((None, len(POOL_WINDOWS), POOL_GC, POOL_GC), lambda g: (layer, 0, 0, 0)),
            pl.BlockSpec((None, 1, MIX_W), lambda g: (layer, 0, 0)),
        ],
        out_specs=pl.BlockSpec((steps, bb, MIX_W), lambda g: (0, g, 0)),
        out_shape=jax.ShapeDtypeStruct((steps, B, MIX_W), BF16),
        compiler_params=_params("arbitrary"),
        name="pool_sample",
    )(proj3, cache_pool, pool_w, pool_scale)


def _rope_tables(positions):
    half = ROT_DIM // 2
    inv = np.power(ROPE_THETA, -np.arange(0, ROT_DIM, 2, dtype=np.float64) / ROT_DIM)
    ang = np.asarray(positions, np.float64)[:, None] * inv[None, :]
    cos, sin = np.cos(ang), np.sin(ang)
    n = len(positions)
    ct = np.ones((n, LANES))
    sn = np.zeros((n, LANES))
    sp = np.zeros((n, LANES))
    for base in (0, SWA_HEAD_DIM):
        ct[:, base:base + half] = cos
        ct[:, base + half:base + ROT_DIM] = cos
        sn[:, base:base + half] = -sin
        sp[:, base + half:base + ROT_DIM] = sin
    return tuple(jnp.asarray(t, F32) for t in (ct, sn, sp))


def _head_norm_rope(x, g, ct, sn, sp):
    lane = lax.broadcasted_iota(jnp.int32, x.shape, 1)
    lo = lane < SWA_HEAD_DIM
    x2 = x * x
    ms_lo = jnp.sum(jnp.where(lo, x2, 0.0), axis=-1, keepdims=True) / SWA_HEAD_DIM
    ms_hi = jnp.sum(jnp.where(lo, 0.0, x2), axis=-1, keepdims=True) / SWA_HEAD_DIM
    xn = x * jnp.where(lo, lax.rsqrt(ms_lo + EPS), lax.rsqrt(ms_hi + EPS)) * g
    half = ROT_DIM // 2
    return xn * ct + pltpu.roll(xn, LANES - half, 1) * sn + pltpu.roll(xn, half, 1) * sp


def _kprep_kernel(k_ref, g_ref, ct_ref, sn_ref, sp_ref, o_ref):
    ct, sn, sp = ct_ref[...], sn_ref[...], sp_ref[...]
    for j in range(2):
        sl = slice(j * LANES, (j + 1) * LANES)
        o_ref[:, sl] = _head_norm_rope(k_ref[:, sl], g_ref[...], ct, sn, sp)


def swa_kprep(proj, g2, tables, tm):
    T = proj.shape[0]
    kw = SWA_KV_HEADS * SWA_HEAD_DIM
    tab = pl.BlockSpec((tm, LANES), lambda i: (i, 0))
    return pl.pallas_call(
        _kprep_kernel,
        grid=(T // tm,),
        in_specs=[pl.BlockSpec((tm, kw), lambda i: (i, COL_SK // kw)),
                  pl.BlockSpec((1, LANES), lambda i: (0, 0)), tab, tab, tab],
        out_specs=pl.BlockSpec((tm, kw), lambda i: (i, 0)),
        out_shape=jax.ShapeDtypeStruct((T, kw), F32),
        compiler_params=_params("arbitrary"),
        name="swa_kprep",
    )(proj, g2, *tables)


def _dup_head(x, parity):
    lane = lax.broadcasted_iota(jnp.int32, x.shape, 1)
    return jnp.where(lane // SWA_HEAD_DIM == parity, x, pltpu.roll(x, SWA_HEAD_DIM, 1))


def _stack_heads(q):
    lane = lax.broadcasted_iota(jnp.int32, q.shape, 1)
    lo = lane < SWA_HEAD_DIM
    return jnp.concatenate([jnp.where(lo, q, 0.0), jnp.where(lo, 0.0, q)], axis=0)


def _unstack_heads(o2):
    R = o2.shape[0] // 2
    lane = lax.broadcasted_iota(jnp.int32, (R, LANES), 1)
    return jnp.where(lane < SWA_HEAD_DIM, o2[:R], o2[R:])


def _swa_prompt_kernel(sink_ref, q_ref, kc_ref, kp_ref, vc_ref, vp_ref, g_ref, ct_ref, sn_ref, sp_ref, y_ref,
                       *, layer, nb):
    first = pl.program_id(0) == 0
    W = WINDOW
    G = SWA_Q_HEADS // SWA_KV_HEADS
    r4 = lax.broadcasted_iota(jnp.int32, (G * W, 1), 0)
    hh = r4 // W
    ci = lax.broadcasted_iota(jnp.int32, (1, W), 1)
    cur = ci <= r4 % W
    scale = SWA_HEAD_DIM ** -0.5

    def head_blocks(cur_ref, prev_ref, kvh):
        ksl = slice((kvh // 2) * LANES, (kvh // 2 + 1) * LANES)
        x = _dup_head(jnp.concatenate([prev_ref[:, ksl], cur_ref[:, ksl]], axis=0), kvh % 2).astype(BF16)
        return [x[j * W:(j + 1) * W] for j in range(nb + 1)]

    scores, sinks = [], []
    for kvh in range(SWA_KV_HEADS):
        kb = head_blocks(kc_ref, kp_ref, kvh)
        sink = sink_ref[layer, G * kvh + G - 1]
        for i in range(G - 2, -1, -1):
            sink = jnp.where(hh == i, sink_ref[layer, G * kvh + i], sink)
        for blk in range(nb):
            rows = slice(blk * W, (blk + 1) * W)
            qs = []
            for jj in range(G // 2):
                qsl = slice((2 * kvh + jj) * LANES, (2 * kvh + jj + 1) * LANES)
                qn = _head_norm_rope(q_ref[rows, qsl], g_ref[...], ct_ref[rows, :], sn_ref[rows, :], sp_ref[rows, :])
                qs.append(_stack_heads(qn * scale))
            q = jnp.concatenate(qs, axis=0).astype(BF16)
            s_prev = _dot_nt(q, kb[blk])
            if blk == 0:
                s_prev = jnp.where(first, NEG_BIG, s_prev)
            scores.append(jnp.where(cur, _dot_nt(q, kb[blk + 1]), s_prev))
            sinks.append(sink)
    probs = []
    for s, sink in zip(scores, sinks):
        m = jnp.maximum(jnp.max(s, axis=-1, keepdims=True), sink)
        e = jnp.exp(s - m)
        probs.append(e * (1.0 / (jnp.sum(e, axis=-1, keepdims=True) + jnp.exp(sink - m))))
    for kvh in range(SWA_KV_HEADS):
        vb = head_blocks(vc_ref, vp_ref, kvh)
        for blk in range(nb):
            p = probs[kvh * nb + blk]
            o = (_dot(jnp.where(cur, p, 0.0).astype(BF16), vb[blk + 1])
                 + _dot(jnp.where(cur, 0.0, p).astype(BF16), vb[blk]))
            for jj in range(G // 2):
                qsl = slice((2 * kvh + jj) * LANES, (2 * kvh + jj + 1) * LANES)
                y_ref[blk * W:(blk + 1) * W, qsl] = _unstack_heads(o[2 * jj * W:(2 * jj + 2) * W]).astype(BF16)


def swa_prompt(proj, khat, sinks, g2, tables, layer, nb):
    T = proj.shape[0]
    W = WINDOW
    tq = nb * W
    kw = SWA_KV_HEADS * SWA_HEAD_DIM
    tab = pl.BlockSpec((tq, LANES), lambda i: (i, 0))
    prev = lambda i: jnp.maximum(i * nb - 1, 0)
    return pl.pallas_call(
        functools.partial(_swa_prompt_kernel, layer=layer, nb=nb),
        grid=(T // tq,),
        in_specs=[
            pl.BlockSpec(memory_space=pltpu.SMEM),
            pl.BlockSpec((tq, MIX_W), lambda i: (i, COL_SQ // MIX_W)),
            pl.BlockSpec((tq, kw), lambda i: (i, 0)),
            pl.BlockSpec((W, kw), lambda i: (prev(i), 0)),
            pl.BlockSpec((tq, kw), lambda i: (i, COL_SV // kw)),
            pl.BlockSpec((W, kw), lambda i: (prev(i), COL_SV // kw)),
            pl.BlockSpec((1, LANES), lambda i: (0, 0)), tab, tab, tab,
        ],
        out_specs=pl.BlockSpec((tq, MIX_W), lambda i: (i, 0)),
        out_shape=jax.ShapeDtypeStruct((T, MIX_W), BF16),
        compiler_params=_params("arbitrary"),
        name="swa_prompt",
    )(sinks, proj, khat, khat, proj, proj, g2, *tables)


def _swa_sample_kernel(sink_ref, q_ref, kn_ref, vn_ref, kc_ref, vc_ref, g_ref, ct_ref, sn_ref, sp_ref, y_ref,
                       *, layer, steps, bb):
    kvh = pl.program_id(1)
    parity = kvh % 2
    W = WINDOW
    R = steps * bb
    G = SWA_Q_HEADS // SWA_KV_HEADS
    ct, sn, sp = ct_ref[...], sn_ref[...], sp_ref[...]
    scale = SWA_HEAD_DIM ** -0.5
    r4 = lax.broadcasted_iota(jnp.int32, (G * R, 1), 0)
    hh = r4 // R
    tq = (r4 % R) // bb
    bq = r4 % bb
    c_new = lax.broadcasted_iota(jnp.int32, (1, R), 1)
    valid_new = (c_new % bb == bq) & (c_new // bb <= tq)
    c_old = lax.broadcasted_iota(jnp.int32, (1, W), 1)
    valid_old = c_old > tq
    kn = _dup_head(jnp.concatenate([kn_ref[t] for t in range(steps)], axis=0), parity).astype(BF16)
    vn = _dup_head(jnp.concatenate([vn_ref[t] for t in range(steps)], axis=0), parity).astype(BF16)
    qs = []
    for jj in range(G // 2):
        qsl = slice(jj * LANES, (jj + 1) * LANES)
        q = jnp.concatenate([q_ref[t, :, qsl] for t in range(steps)], axis=0)
        qs.append(_stack_heads(_head_norm_rope(q, g_ref[...], ct, sn, sp) * scale))
    q4 = jnp.concatenate(qs, axis=0)
    s_new = jnp.where(valid_new, _dot_nt(q4.astype(BF16), kn), NEG_BIG)
    s_old = None
    for b in range(0, bb, 2):
        lhs = jnp.concatenate([jnp.where(bq == b + i, q4, 0.0).astype(BF16) for i in range(2)], axis=1)
        k_t = [kc_ref[b + i].astype(BF16) for i in range(2)]
        d = _dot(lhs, jnp.concatenate([k_t[0], k_t[0], k_t[1], k_t[1]], axis=0))
        s_old = d if s_old is None else s_old + d
    s_old = jnp.where(valid_old, s_old, NEG_BIG)
    sink = sink_ref[layer, G * kvh + G - 1]
    for i in range(G - 2, -1, -1):
        sink = jnp.where(hh == i, sink_ref[layer, G * kvh + i], sink)
    m = jnp.maximum(jnp.maximum(jnp.max(s_new, axis=-1, keepdims=True),
                                jnp.max(s_old, axis=-1, keepdims=True)), sink)
    e_new = jnp.exp(s_new - m)
    e_old = jnp.exp(s_old - m)
    den = jnp.sum(e_new, axis=-1, keepdims=True) + jnp.sum(e_old, axis=-1, keepdims=True) + jnp.exp(sink - m)
    p_old = e_old / den
    o = _dot((e_new / den).astype(BF16), vn)
    for b in range(0, bb, 2):
        lhs = jnp.concatenate([jnp.where(bq == b + i, p_old, 0.0).astype(BF16) for i in range(2)], axis=1)
        v_t = [vc_ref[b + i].astype(BF16) for i in range(2)]
        rhs = jnp.concatenate([jnp.concatenate([v_t[i], v_t[i]], axis=0) for i in range(2)], axis=1)
        o = o + _dot_nt(lhs, rhs)
    for jj in range(G // 2):
        o_j = _unstack_heads(o[2 * jj * R:(2 * jj + 2) * R])
        for t in range(steps):
            y_ref[t, :, jj * LANES:(jj + 1) * LANES] = o_j[t * bb:(t + 1) * bb]


def swa_sample(proj3, khat3, cache_kt, cache_vt, sinks, g2, tables, layer, bb):
    steps, B, _ = proj3.shape
    R = steps * bb
    qw = MIX_W // SWA_KV_HEADS
    tab = pl.BlockSpec((R, LANES), lambda g, h: (0, 0))
    cache = pl.BlockSpec((None, bb, None, SWA_HEAD_DIM, WINDOW), lambda g, h: (layer, g, h, 0, 0))
    return pl.pallas_call(
        functools.partial(_swa_sample_kernel, layer=layer, steps=steps, bb=bb),
        grid=(B // bb, SWA_KV_HEADS),
        in_specs=[
            pl.BlockSpec(memory_space=pltpu.SMEM),
            pl.BlockSpec((steps, bb, qw), lambda g, h: (0, g, COL_SQ // qw + h)),
            pl.BlockSpec((steps, bb, LANES), lambda g, h: (0, g, h // 2)),
            pl.BlockSpec((steps, bb, LANES), lambda g, h: (0, g, COL_SV // LANES + h // 2)),
            cache, cache,
            pl.BlockSpec((1, LANES), lambda g, h: (0, 0)), tab, tab, tab,
        ],
        out_specs=pl.BlockSpec((steps, bb, qw), lambda g, h: (0, g, h)),
        out_shape=jax.ShapeDtypeStruct((steps, B, MIX_W), F32),
        compiler_params=_params("parallel", "arbitrary"),
        name="swa_sample",
    )(sinks, proj3, khat3, proj3, cache_kt, cache_vt, g2, *tables)


def _mem_kv_kernel(x_ref, g_ref, w_ref, kg_ref, o_ref, xn_ref):
    j = pl.program_id(0)

    @pl.when(j == 0)
    def _():
        x = x_ref[...]
        ms = jnp.mean(x * x, axis=-1, keepdims=True)
        xn_ref[...] = (x * lax.rsqrt(ms + EPS) * g_ref[...]).astype(BF16)

    y = _dot(xn_ref[...], w_ref[...].astype(BF16))

    @pl.when(j < MEM_HEADS)
    def _():
        ms = jnp.mean(y * y, axis=-1, keepdims=True)
        o_ref[...] = y * lax.rsqrt(ms + EPS) * kg_ref[...]

    @pl.when(j >= MEM_HEADS)
    def _():
        o_ref[...] = y


def mem_kv(mem, mem_norm_g, w_mem_kv, mem_knorm_g, layer):
    M, K = mem.shape
    hd = MEM_HEAD_DIM
    return pl.pallas_call(
        _mem_kv_kernel,
        grid=(2 * MEM_HEADS,),
        in_specs=[
            pl.BlockSpec((M, K), lambda j: (0, 0)),
            pl.BlockSpec((None, 1, K), lambda j: (layer, 0, 0)),
            pl.BlockSpec((None, K, hd), lambda j: (layer, 0, j)),
            pl.BlockSpec((None, 1, hd), lambda j: (layer, 0, 0)),
        ],
        out_specs=pl.BlockSpec((M, hd), lambda j: (0, j)),
        out_shape=jax.ShapeDtypeStruct((M, 2 * MIX_W), F32),
        scratch_shapes=[pltpu.VMEM((M, K), BF16)],
        compiler_params=_params("arbitrary"),
        name="mem_kv",
    )(mem, mem_norm_g, w_mem_kv, mem_knorm_g)


def _mem_qnorm(q, g):
    ms = jnp.mean(q * q, axis=-1, keepdims=True)
    return q * lax.rsqrt(ms + EPS) * g * (MEM_HEAD_DIM ** -0.5)


def _softmax_rows(s):
    m = jnp.max(s, axis=-1, keepdims=True)
    e = jnp.exp(s - m)
    return e / jnp.sum(e, axis=-1, keepdims=True)


def _mem_prompt_kernel(q0_ref, q1_ref, q2_ref, q3_ref, kv_ref, g_ref, y_ref):
    hd = MEM_HEAD_DIM
    for h, q_ref in enumerate((q0_ref, q1_ref, q2_ref, q3_ref)):
        q = _mem_qnorm(q_ref[...], g_ref[...]).astype(BF16)
        p = _softmax_rows(_dot_nt(q, kv_ref[:, h * hd:(h + 1) * hd].astype(BF16)))
        v = kv_ref[:, MIX_W + h * hd:MIX_W + (h + 1) * hd].astype(BF16)
        y_ref[:, h * hd:(h + 1) * hd] = _dot(p.astype(BF16), v).astype(BF16)


def _light_mixers_kernel(k_ref, gk_ref, ct_ref, sn_ref, sp_ref,
                         u_ref, prev_ref, pw_ref, sc_ref,
                         q0_ref, q1_ref, q2_ref, q3_ref, kv_ref, gq_ref,
                         khat_ref, yb_ref, ym_ref, ext_ref, sum_ref, *, tm):
    _kprep_kernel(k_ref, gk_ref, ct_ref, sn_ref, sp_ref, khat_ref)
    _pool_prompt_kernel(u_ref, prev_ref, pw_ref, sc_ref, yb_ref, ext_ref, sum_ref, tm=tm)
    _mem_prompt_kernel(q0_ref, q1_ref, q2_ref, q3_ref, kv_ref, gq_ref, ym_ref)


def light_mixers_prompt(proj, gk2, tables, pool_w, pool_scale, kv, mem_qnorm_g, layer, tm):
    T = proj.shape[0]
    kw = SWA_KV_HEADS * SWA_HEAD_DIM
    hd = MEM_HEAD_DIM
    cb = COL_POOL // MIX_W
    tab = pl.BlockSpec((tm, LANES), lambda i: (i, 0))

    def q_spec(h):
        qb = COL_MQ // hd + h
        return pl.BlockSpec((tm, hd), lambda i: (i, qb))

    return pl.pallas_call(
        functools.partial(_light_mixers_kernel, tm=tm),
        grid=(T // tm,),
        in_specs=[
            pl.BlockSpec((tm, kw), lambda i: (i, COL_SK // kw)),
            pl.BlockSpec((1, LANES), lambda i: (0, 0)), tab, tab, tab,
            pl.BlockSpec((tm, MIX_W), lambda i: (i, cb)),
            pl.BlockSpec((POOL_HIST, MIX_W), lambda i: (jnp.maximum(i * (tm // POOL_HIST) - 1, 0), cb)),
            pl.BlockSpec((None, len(POOL_WINDOWS), POOL_GC, POOL_GC), lambda i: (layer, 0, 0, 0)),
            pl.BlockSpec((None, 1, MIX_W), lambda i: (layer, 0, 0)),
        ] + [q_spec(h) for h in range(MEM_HEADS)] + [
            pl.BlockSpec((N_MEM, 2 * MIX_W), lambda i: (0, 0)),
            pl.BlockSpec((None, 1, hd), lambda i: (layer, 0, 0)),
        ],
        out_specs=[
            pl.BlockSpec((tm, kw), lambda i: (i, 0)),
            pl.BlockSpec((tm, MIX_W), lambda i: (i, 0)),
            pl.BlockSpec((tm, MIX_W), lambda i: (i, 0)),
        ],
        out_shape=[
            jax.ShapeDtypeStruct((T, kw), F32),
            jax.ShapeDtypeStruct((T, MIX_W), BF16),
            jax.ShapeDtypeStruct((T, MIX_W), BF16),
        ],
        scratch_shapes=[pltpu.VMEM((POOL_HIST + tm, MIX_W), F32), pltpu.VMEM((POOL_HIST + tm, MIX_W), F32)],
        compiler_params=_params("arbitrary"),
        name="light_mixers_prompt",
    )(proj, gk2, *tables, proj, proj, pool_w, pool_scale, proj, proj, proj, proj, kv, mem_qnorm_g)


def _mem_rows_view(c):
    L_, B_, M, H, hd = c.shape
    c = c.reshape(L_, B_, M, H, hd // LANES, LANES)
    return jnp.transpose(c, (0, 1, 2, 4, 3, 5)).reshape(L_, B_, M * H * (hd // LANES), LANES)


def _mem_head(c_ref, b, h):
    nt = MEM_HEAD_DIM // LANES
    parts = [c_ref[b, pl.ds(lt * MEM_HEADS + h, N_MEM, stride=nt * MEM_HEADS), :] for lt in range(nt)]
    return jnp.concatenate(parts, axis=1).astype(BF16)


def _mem_sample_kernel(q0_ref, q1_ref, q2_ref, q3_ref, k_ref, v_ref, g_ref, y_ref, *, steps, bb):
    R = steps * bb
    hd = MEM_HEAD_DIM
    bq = lax.broadcasted_iota(jnp.int32, (R, 1), 0) % bb
    for h, q_ref in enumerate((q0_ref, q1_ref, q2_ref, q3_ref)):
        q = _mem_qnorm(jnp.concatenate([q_ref[t] for t in range(steps)], axis=0), g_ref[...])
        s = None
        for b in range(bb):
            d = _dot_nt(jnp.where(bq == b, q, 0.0).astype(BF16), _mem_head(k_ref, b, h))
            s = d if s is None else s + d
        p = _softmax_rows(s)
        o = None
        for b in range(bb):
            d = _dot(jnp.where(bq == b, p, 0.0).astype(BF16), _mem_head(v_ref, b, h))
            o = d if o is None else o + d
        for t in range(steps):
            y_ref[t, :, h * hd:(h + 1) * hd] = o[t * bb:(t + 1) * bb]


def mem_attn_sample(proj3, cache_k, cache_v, mem_qnorm_g, layer, bb):
    steps, B, _ = proj3.shape
    hd = MEM_HEAD_DIM
    cache = pl.BlockSpec((None, bb) + cache_k.shape[2:], lambda g: (layer, g, 0, 0))

    def q_spec(h):
        cb = COL_MQ // hd + h
        return pl.BlockSpec((steps, bb, hd), lambda g: (0, g, cb))

    return pl.pallas_call(
        functools.partial(_mem_sample_kernel, steps=steps, bb=bb),
        grid=(B // bb,),
        in_specs=[q_spec(h) for h in range(MEM_HEADS)] + [
            cache, cache,
            pl.BlockSpec((None, 1, hd), lambda g: (layer, 0, 0)),
        ],
        out_specs=pl.BlockSpec((steps, bb, MIX_W), lambda g: (0, g, 0)),
        out_shape=jax.ShapeDtypeStruct((steps, B, MIX_W), F32),
        compiler_params=_params("arbitrary"),
        name="mem_sample",
    )(proj3, proj3, proj3, proj3, cache_k, cache_v, mem_qnorm_g)


def _row_tile(T, cap):
    t = cap
    while T % t:
        t //= 2
    return t


TM_STREAM = 2048
TM_DOWN = 1024
TM_LOCAL = 512
TM_MERGE = 512
TM_RESIDENT = 256
TN_STREAM = 512
TN_DOWN = 256
TN_SAMPLE_IN = 1536
UP_PIECE_ROWS = 1024
HGRN_ROWS = 2048
HGRN_BATCH_UNROLL = 16
SWA_BLOCKS = 8
POOL_BATCH = 64
ATTN_BATCH = 8


def _token_tail(x, xn, ys, layer, w_in, w_branch, w_o, norm2_g):
    T = x.shape[0]
    tm = _row_tile(T, TM_MERGE)
    merged = merge_branches(xn, ys, w_in, w_branch, layer, tm, TN_STREAM if T // tm > 1 else TN_DOWN)
    return matmul_res_norm(merged, w_o, layer, x, norm2_g, _row_tile(T, TM_RESIDENT))


def kernel(x_prompt, x_sample, mem_prompt, state_hgrn, cache_pool, cache_swa_k, cache_swa_v, state_conv, cache_mem_k, cache_mem_v, norm1_g, w_in, hgrn_lb, hgrn_norm_g, pool_w, pool_scale, swa_qnorm_g, swa_knorm_g, swa_sinks, mem_norm_g, w_mem_kv, mem_qnorm_g, mem_knorm_g, w_branch, w_o, norm2_g, w_up, conv_w, conv_b, w_down):
    depth = w_in.shape[0]
    bp, L, _ = x_prompt.shape
    B, steps, _ = x_sample.shape
    assert bp == 1
    kw = SWA_KV_HEADS * SWA_HEAD_DIM
    Ts = steps * B

    lb_all = jnp.cumsum(jax.nn.softmax(hgrn_lb.astype(F32), axis=0), axis=0)
    lb_all = lb_all - lb_all[:1]

    tab_p = _rope_tables(np.arange(L))
    tab_s = _rope_tables(np.repeat(PAST_LEN + np.arange(steps), B))
    tab_sb = _rope_tables(np.repeat(PAST_LEN + np.arange(steps), ATTN_BATCH))

    xp = x_prompt.reshape(L, D_MODEL)
    xs = jnp.transpose(x_sample, (1, 0, 2)).reshape(Ts, D_MODEL)
    mem = mem_prompt.reshape(N_MEM, D_MODEL)
    ckt_all = jnp.transpose(cache_swa_k, (0, 1, 3, 4, 2))
    cvt_all = jnp.transpose(cache_swa_v, (0, 1, 3, 4, 2))
    mk_rows = _mem_rows_view(cache_mem_k)
    mv_rows = _mem_rows_view(cache_mem_v)
    cpool_v = jnp.transpose(cache_pool, (0, 2, 1, 3))
    row3 = lambda a: a.reshape(depth, 1, a.shape[-1])
    norm1_g, norm2_g, pool_scale, conv_b = row3(norm1_g), row3(norm2_g), row3(pool_scale), row3(conv_b)
    mem_norm_g, mem_qnorm_g, mem_knorm_g = row3(mem_norm_g), row3(mem_qnorm_g), row3(mem_knorm_g)
    tm_p = _row_tile(L, TM_STREAM)
    tl_p = _row_tile(L, TM_LOCAL)

    outs = {k: [] for k in ("sp", "pp", "ps", "kp", "ks", "vp", "vs", "cp", "mk", "mv")}
    hgrn_states = None
    conv_states = None
    for l in range(depth):
        lb = lb_all[l].reshape(1, MIX_W)
        gn = hgrn_norm_g[l].reshape(1, MIX_W)
        gq2 = jnp.tile(swa_qnorm_g[l], 2).reshape(1, LANES)
        gk2 = jnp.tile(swa_knorm_g[l], 2).reshape(1, LANES)

        kv = mem_kv(mem, mem_norm_g, w_mem_kv, mem_knorm_g, l)

        xn = prenorm(xp, norm1_g, l, tl_p)
        proj = matmul_cols(xn, w_in, l, COL_GATE, tm_p, TN_STREAM)
        ya, s_p = hgrn_prompt(proj, lb, gn, _row_tile(L, HGRN_ROWS))
        khat, yb, ym = light_mixers_prompt(proj, gk2, tab_p, pool_w, pool_scale, kv, mem_qnorm_g, l,
                                           _row_tile(L, TM_DOWN))
        yc = swa_prompt(proj, khat, swa_sinks, gq2, tab_p, l, _row_tile(L, SWA_BLOCKS * WINDOW) // WINDOW)
        h, hn = _token_tail(xp, xn, (ya, yb, yc, ym), l, w_in, w_branch, w_o, norm2_g)
        gact, a_tail = up_conv_prompt(hn, w_up, conv_w, conv_b, l, tm_p, TN_STREAM)
        xp = matmul_res(gact, w_down, l, h, _row_tile(L, TM_DOWN), TN_DOWN)

        outs["sp"].append(s_p[None])
        outs["pp"].append(proj[None, L - POOL_BUF:, COL_POOL:COL_POOL + MIX_W])
        outs["kp"].append(khat[None, L - WINDOW:].reshape(1, WINDOW, SWA_KV_HEADS, SWA_HEAD_DIM))
        outs["vp"].append(proj[None, L - WINDOW:, COL_SV:COL_SV + kw].reshape(1, WINDOW, SWA_KV_HEADS, SWA_HEAD_DIM))
        outs["cp"].append(a_tail[-1:, CONV_HIST - 2:])
        outs["mk"].append(kv[None, :, :MIX_W].reshape(1, N_MEM, MEM_HEADS, MEM_HEAD_DIM))
        outs["mv"].append(kv[None, :, MIX_W:].reshape(1, N_MEM, MEM_HEADS, MEM_HEAD_DIM))

        xn = prenorm(xs, norm1_g, l, Ts)
        proj_s = matmul_cols(xn, w_in, l, COL_GATE, Ts, TN_SAMPLE_IN)
        proj3 = proj_s.reshape(steps, B, COL_GATE)
        ya, hgrn_states = hgrn_sample(proj3, lb, gn, state_hgrn, l, LANES // steps, hgrn_states)
        yb = pool_sample(proj3, cpool_v, pool_w, pool_scale, l, POOL_BATCH)
        khat_s = swa_kprep(proj_s, gk2, tab_s, Ts)
        khat3 = khat_s.reshape(steps, B, kw)
        yc = swa_sample(proj3, khat3, ckt_all, cvt_all, swa_sinks, gq2, tab_sb, l, ATTN_BATCH)
        ym = mem_attn_sample(proj3, mk_rows, mv_rows, mem_qnorm_g, l, ATTN_BATCH)
        ys = tuple(y.reshape(Ts, MIX_W).astype(BF16) for y in (ya, yb, yc, ym))
        h, hn = _token_tail(xs, xn, ys, l, w_in, w_branch, w_o, norm2_g)
        gact, conv_states = up_conv_sample(hn, w_up, state_conv, conv_w, conv_b, l, steps, TN_STREAM, conv_states)
        xs = matmul_res(gact, w_down, l, h, Ts, TN_STREAM)

        outs["ps"].append(proj3[:, :, COL_POOL:COL_POOL + MIX_W])
        to_window_minor = lambda a: jnp.transpose(a.reshape(steps, B, SWA_KV_HEADS, SWA_HEAD_DIM), (1, 2, 3, 0))
        outs["ks"].append(to_window_minor(khat3))
        outs["vs"].append(to_window_minor(proj3[:, :, COL_SV:COL_SV + kw]))

    stk = lambda k: jnp.stack(outs[k], axis=0)
    pool_s = jnp.transpose(jnp.concatenate([cpool_v[:, steps:], stk("ps")], axis=1), (0, 2, 1, 3))
    def slide_window(old, new):
        lead = [(0, 0, 0)] * (old.ndim - 1)
        shifted = lax.pad(old, jnp.zeros((), old.dtype), lead + [(-steps, steps, 0)])
        tail = lax.pad(new, jnp.zeros((), old.dtype), lead + [(WINDOW - steps, 0, 0)])
        pos = lax.broadcasted_iota(jnp.int32, old.shape, old.ndim - 1)
        return jnp.transpose(jnp.where(pos < WINDOW - steps, shifted, tail), (0, 1, 4, 2, 3))

    swa_k_s = slide_window(ckt_all, stk("ks"))
    swa_v_s = slide_window(cvt_all, stk("vs"))
    y_prompt = xp.reshape(1, L, D_MODEL)
    y_sample = jnp.transpose(xs.reshape(steps, B, D_MODEL), (1, 0, 2))
    return (y_prompt, y_sample,
            stk("sp"), hgrn_states, stk("pp"), pool_s, stk("kp"), swa_k_s, stk("vp"), swa_v_s,
            stk("cp"), conv_states, jnp.concatenate(outs["mk"], axis=0)[:, None], jnp.concatenate(outs["mv"], axis=0)[:, None])
```

```python
import functools

import numpy as np
import jax
import jax.numpy as jnp
from jax import lax
from jax.experimental import pallas as pl
from jax.experimental.pallas import tpu as pltpu

F32 = jnp.float32
BF16 = jnp.bfloat16

D_MODEL = 2048
MIX_W = D_MODEL // 2
N_BRANCH = 4
A_DK = 128
A_HEADS = MIX_W // A_DK
POOL_WINDOWS = (2, 4, 8, 16)
POOL_GC = MIX_W // len(POOL_WINDOWS)
POOL_BUF = max(POOL_WINDOWS) - 1
SWA_HEAD_DIM = 64
SWA_Q_HEADS = MIX_W // SWA_HEAD_DIM
SWA_KV_HEADS = SWA_Q_HEADS // 4
WINDOW = 128
ROT_DIM = SWA_HEAD_DIM // 4
ROPE_THETA = 500000.0
N_MEM = 256
MEM_HEADS = 4
MEM_HEAD_DIM = MIX_W // MEM_HEADS
D_FF = 11 * D_MODEL // 4
EPS = 1e-6
PAST_LEN = 8192

COL_HQ, COL_HF, COL_HI, COL_HG = 0, MIX_W, 2 * MIX_W, 3 * MIX_W
COL_POOL = 4 * MIX_W
COL_SQ = 5 * MIX_W
COL_SK = 6 * MIX_W
COL_SV = COL_SK + SWA_KV_HEADS * SWA_HEAD_DIM
COL_MQ = COL_SV + SWA_KV_HEADS * SWA_HEAD_DIM
COL_GATE = COL_MQ + MIX_W
IN_COLS = COL_GATE + N_BRANCH * D_MODEL

LANES = 128
V7X_VMEM_BYTES = 64 * 1024 * 1024
VMEM_LIMIT = V7X_VMEM_BYTES * 7 // 8
HGRN_CHUNK = 128
NEG_BIG = -1e30
LOG2_E = 1.4426950408889634


def _params(*sem):
    return pltpu.CompilerParams(dimension_semantics=sem, vmem_limit_bytes=VMEM_LIMIT)


def _sigmoid(x):
    return 0.5 * jnp.tanh(0.5 * x) + 0.5


def _dot(a, b):
    return jnp.dot(a, b, preferred_element_type=F32)


def _dot_nt(a, b):
    return lax.dot_general(a, b, (((1,), (1,)), ((), ())), preferred_element_type=F32)


def _skip_ref(kernel_fn, idx):
    def wrapped(*refs):
        return kernel_fn(*refs[:idx], *refs[idx + 1:])
    return wrapped


def _layer_slab_call(make_kernel, in_specs, args, slab_out, layer, slab_block, slab_index, out_specs, **kw):
    n_layers = kw["out_shape"][-1].shape[0]
    if slab_out is None:
        spec = pl.BlockSpec((n_layers,) + slab_block, lambda *g: (0,) + slab_index(*g))
        return pl.pallas_call(make_kernel(layer), in_specs=in_specs, out_specs=list(out_specs) + [spec], **kw)(*args)
    spec = pl.BlockSpec((1,) + slab_block, lambda *g: (layer,) + slab_index(*g))
    idx = len(args)
    return pl.pallas_call(
        _skip_ref(make_kernel(0), idx),
        in_specs=list(in_specs) + [pl.BlockSpec(memory_space=pl.ANY)],
        out_specs=list(out_specs) + [spec],
        input_output_aliases={idx: len(kw["out_shape"]) - 1},
        **kw)(*args, slab_out)


def _zero_other_slabs(so_ref, own):
    for l in range(so_ref.shape[0]):
        if l != own:
            so_ref[l] = jnp.zeros(so_ref.shape[1:], so_ref.dtype)


def _rms_rows(x, g):
    ms = jnp.mean(x * x, axis=-1, keepdims=True)
    return x * lax.rsqrt(ms + EPS) * g


def _prenorm_kernel(x_ref, g_ref, o_ref):
    o_ref[...] = _rms_rows(x_ref[...], g_ref[...]).astype(BF16)


def prenorm(x, g, layer, tm):
    T, K = x.shape
    return pl.pallas_call(
        _prenorm_kernel,
        grid=(T // tm,),
        in_specs=[pl.BlockSpec((tm, K), lambda i: (i, 0)),
                  pl.BlockSpec((None, 1, K), lambda i: (layer, 0, 0))],
        out_specs=pl.BlockSpec((tm, K), lambda i: (i, 0)),
        out_shape=jax.ShapeDtypeStruct((T, K), BF16),
        compiler_params=_params("arbitrary"),
        name="prenorm",
    )(x, g)


def _matmul_kernel(a_ref, w_ref, o_ref):
    o_ref[...] = _dot(a_ref[...], w_ref[...].astype(BF16))


def matmul_cols(a, w, layer, n_cols, tm, tn):
    T, K = a.shape
    return pl.pallas_call(
        _matmul_kernel,
        grid=(T // tm, n_cols // tn),
        in_specs=[
            pl.BlockSpec((tm, K), lambda i, j: (i, 0)),
            pl.BlockSpec((None, K, tn), lambda i, j: (layer, 0, j)),
        ],
        out_specs=pl.BlockSpec((tm, tn), lambda i, j: (i, j)),
        out_shape=jax.ShapeDtypeStruct((T, n_cols), F32),
        compiler_params=_params("parallel", "arbitrary"),
        name="matmul_cols",
    )(a, w)


def _matmul_res_kernel(a_ref, w_ref, r_ref, o_ref):
    o_ref[...] = r_ref[...] + _dot(a_ref[...], w_ref[...].astype(BF16))


def matmul_res(a, w, layer, res, tm, tn):
    T, K = a.shape
    N = w.shape[2]
    return pl.pallas_call(
        _matmul_res_kernel,
        grid=(T // tm, N // tn),
        in_specs=[
            pl.BlockSpec((tm, K), lambda i, j: (i, 0)),
            pl.BlockSpec((None, K, tn), lambda i, j: (layer, 0, j)),
            pl.BlockSpec((tm, tn), lambda i, j: (i, j)),
        ],
        out_specs=pl.BlockSpec((tm, tn), lambda i, j: (i, j)),
        out_shape=jax.ShapeDtypeStruct((T, N), F32),
        compiler_params=_params("parallel", "arbitrary"),
        name="matmul_res",
    )(a, w, res)


def _res_norm_kernel(a_ref, w_ref, r_ref, g_ref, h_ref, hn_ref, w_s):
    @pl.when(pl.program_id(0) == 0)
    def _():
        w_s[...] = w_ref[...].astype(BF16)

    h = r_ref[...] + _dot(a_ref[...], w_s[...])
    h_ref[...] = h
    hn_ref[...] = _rms_rows(h, g_ref[...]).astype(BF16)


def matmul_res_norm(a, w, layer, res, g, tm):
    T, K = a.shape
    N = w.shape[2]
    return pl.pallas_call(
        _res_norm_kernel,
        grid=(T // tm,),
        in_specs=[
            pl.BlockSpec((tm, K), lambda i: (i, 0)),
            pl.BlockSpec((None, K, N), lambda i: (layer, 0, 0), pipeline_mode=pl.Buffered(1)),
            pl.BlockSpec((tm, N), lambda i: (i, 0)),
            pl.BlockSpec((None, 1, N), lambda i: (layer, 0, 0)),
        ],
        out_specs=[pl.BlockSpec((tm, N), lambda i: (i, 0)), pl.BlockSpec((tm, N), lambda i: (i, 0))],
        out_shape=[jax.ShapeDtypeStruct((T, N), F32), jax.ShapeDtypeStruct((T, N), BF16)],
        scratch_shapes=[pltpu.VMEM((K, N), BF16)],
        compiler_params=_params("arbitrary"),
        name="matmul_res_norm",
    )(a, w, res, g)


def _merge_kernel(xn_ref, ya_ref, yb_ref, yc_ref, ym_ref, wg0_ref, wg1_ref, wg2_ref, wg3_ref, wb_ref, o_ref,
                  wg_s, wb_s):
    @pl.when(pl.program_id(1) == 0)
    def _():
        for n, wg_ref in enumerate((wg0_ref, wg1_ref, wg2_ref, wg3_ref)):
            wg_s[n] = wg_ref[...].astype(BF16)
            wb_s[n] = wb_ref[n].astype(BF16)

    xn = xn_ref[...]
    acc = None
    for n, y_ref in enumerate((ya_ref, yb_ref, yc_ref, ym_ref)):
        t = _sigmoid(_dot(xn, wg_s[n])) * _dot(y_ref[...], wb_s[n])
        acc = t if acc is None else acc + t
    o_ref[...] = acc.astype(BF16)


def merge_branches(xn, ys, w_in, w_branch, layer, tm, tn):
    T = xn.shape[0]
    once = pl.Buffered(1) if T // tm > 1 else None
    y_spec = pl.BlockSpec((tm, MIX_W), lambda j, i: (i, 0))

    def gate_spec(n):
        off = (COL_GATE + n * D_MODEL) // tn
        return pl.BlockSpec((None, D_MODEL, tn), lambda j, i: (layer, 0, off + j), pipeline_mode=once)

    return pl.pallas_call(
        _merge_kernel,
        grid=(D_MODEL // tn, T // tm),
        in_specs=[pl.BlockSpec((tm, D_MODEL), lambda j, i: (i, 0))] + [y_spec] * 4
        + [gate_spec(n) for n in range(N_BRANCH)]
        + [pl.BlockSpec((None, N_BRANCH, MIX_W, tn), lambda j, i: (layer, 0, 0, j), pipeline_mode=once)],
        out_specs=pl.BlockSpec((tm, tn), lambda j, i: (i, j)),
        out_shape=jax.ShapeDtypeStruct((T, D_MODEL), BF16),
        scratch_shapes=[pltpu.VMEM((N_BRANCH, D_MODEL, tn), BF16), pltpu.VMEM((N_BRANCH, MIX_W, tn), BF16)],
        compiler_params=_params("arbitrary", "arbitrary"),
        name="merge_branches",
    )(xn, *ys, w_in, w_in, w_in, w_in, w_branch)


CONV_HIST = 8


def _gelu(x):
    return 0.5 * x * (1.0 + lax.erf(x * (2.0 ** -0.5)))


def _up_conv_prompt_kernel(xn_ref, wa_ref, wv_ref, cw_ref, cb_ref, g_ref, tail_ref, carry_ref, *, tm, rc):
    i, j = pl.program_id(0), pl.program_id(1)
    wa = wa_ref[...].astype(BF16)
    wv = wv_ref[...].astype(BF16)
    prev = jnp.where(i == 0, 0.0, carry_ref[j])
    row = lax.broadcasted_iota(jnp.int32, (rc, wa.shape[1]), 0)
    for c in range(tm // rc):
        sl = pl.ds(c * rc, rc)
        xn = xn_ref[sl, :]
        a = _dot(xn, wa)
        v = _dot(xn, wv)
        a1 = jnp.where(row == 0, prev[CONV_HIST - 1:CONV_HIST], pltpu.roll(a, 1, 0))
        a2 = jnp.where(row == 0, prev[CONV_HIST - 2:CONV_HIST - 1],
                       jnp.where(row == 1, prev[CONV_HIST - 1:CONV_HIST], pltpu.roll(a, 2, 0)))
        cc = cb_ref[...] + cw_ref[0:1, :] * a2 + cw_ref[1:2, :] * a1 + cw_ref[2:3, :] * a
        g_ref[sl, :] = (_gelu(cc) * v).astype(BF16)
        prev = a[rc - CONV_HIST:, :]
    carry_ref[j] = prev
    tail_ref[...] = prev


def up_conv_prompt(xn, w_up, conv_w, conv_b, layer, tm, tn):
    T, K = xn.shape
    nc = D_FF // tn
    return pl.pallas_call(
        functools.partial(_up_conv_prompt_kernel, tm=tm, rc=min(tm, UP_PIECE_ROWS)),
        grid=(T // tm, nc),
        in_specs=[
            pl.BlockSpec((tm, K), lambda i, j: (i, 0)),
            pl.BlockSpec((None, K, tn), lambda i, j: (layer, 0, j)),
            pl.BlockSpec((None, K, tn), lambda i, j: (layer, 0, nc + j)),
            pl.BlockSpec((None, 3, tn), lambda i, j: (layer, 0, j)),
            pl.BlockSpec((None, 1, tn), lambda i, j: (layer, 0, j)),
        ],
        out_specs=[
            pl.BlockSpec((tm, tn), lambda i, j: (i, j)),
            pl.BlockSpec((None, CONV_HIST, tn), lambda i, j: (i, 0, j)),
        ],
        out_shape=[
            jax.ShapeDtypeStruct((T, D_FF), BF16),
            jax.ShapeDtypeStruct((T // tm, CONV_HIST, D_FF), F32),
        ],
        scratch_shapes=[pltpu.VMEM((nc, CONV_HIST, tn), F32)],
        compiler_params=_params("arbitrary", "arbitrary"),
        name="up_conv_prompt",
    )(xn, w_up, w_up, conv_w, conv_b)


def _up_conv_sample_kernel(xn_ref, wa_ref, wv_ref, st_ref, cw_ref, cb_ref, g_ref, so_ref, *, steps, B, own):
    _zero_other_slabs(so_ref, own)
    xn = xn_ref[...]
    a = _dot(xn, wa_ref[...].astype(BF16))
    v = _dot(xn, wv_ref[...].astype(BF16))
    hist = [st_ref[:, 0, :], st_ref[:, 1, :]] + [a[t * B:(t + 1) * B] for t in range(steps)]
    for t in range(steps):
        c = cb_ref[...] + cw_ref[0:1, :] * hist[t] + cw_ref[1:2, :] * hist[t + 1] + cw_ref[2:3, :] * hist[t + 2]
        g_ref[t * B:(t + 1) * B, :] = (_gelu(c) * v[t * B:(t + 1) * B]).astype(BF16)
    so_ref[own, :, 0, :] = hist[steps]
    so_ref[own, :, 1, :] = hist[steps + 1]


def up_conv_sample(xn, w_up, state_conv, conv_w, conv_b, layer, steps, tn, state_out):
    T, K = xn.shape
    B = T // steps
    nc = D_FF // tn
    in_specs = [
        pl.BlockSpec((T, K), lambda j: (0, 0)),
        pl.BlockSpec((None, K, tn), lambda j: (layer, 0, j)),
        pl.BlockSpec((None, K, tn), lambda j: (layer, 0, nc + j)),
        pl.BlockSpec((None, B, 2, tn), lambda j: (layer, 0, 0, j)),
        pl.BlockSpec((None, 3, tn), lambda j: (layer, 0, j)),
        pl.BlockSpec((None, 1, tn), lambda j: (layer, 0, j)),
    ]
    return _layer_slab_call(
        lambda own: functools.partial(_up_conv_sample_kernel, steps=steps, B=B, own=own),
        in_specs, [xn, w_up, w_up, state_conv, conv_w, conv_b], state_out, layer,
        (B, 2, tn), lambda j: (0, 0, j),
        [pl.BlockSpec((T, tn), lambda j: (0, j))],
        grid=(nc,),
        out_shape=[
            jax.ShapeDtypeStruct((T, D_FF), BF16),
            jax.ShapeDtypeStruct(state_conv.shape, F32),
        ],
        compiler_params=_params("arbitrary"),
        name="up_conv_sample",
    )


def _hgrn_gates(q_in, z, lb):
    q = q_in * _sigmoid(q_in)
    log_sig = jnp.minimum(z, 0.0) - jnp.log(1.0 + jnp.exp(-jnp.abs(z)))
    a1 = jnp.log(lb)
    a2 = jnp.log1p(-lb) + log_sig
    log_f = jnp.maximum(a1, a2) + jnp.log(1.0 + jnp.exp(-jnp.abs(a1 - a2)))
    k = (1.0 - lb) * _sigmoid(-z)
    return q, log_f, k


def _hgrn_out(o, gate, gn):
    ms = jnp.mean(o * o, axis=-1, keepdims=True)
    return o * lax.rsqrt(ms + EPS) * gn * (gate * _sigmoid(gate))


def _cumsum_rows(x, tril):
    hi = x.astype(BF16)
    r1 = x - hi.astype(F32)
    mid = r1.astype(BF16)
    lo = (r1 - mid.astype(F32)).astype(BF16)
    return _dot(tril, hi) + _dot(tril, mid) + _dot(tril, lo)


def _block_row(x, blk, r):
    C = x.shape[0]
    x3 = x.reshape(C // blk, blk, LANES)
    return jnp.broadcast_to(x3[:, r:r + 1, :], (C // blk, blk, LANES)).reshape(C, LANES)


def _hgrn_pair_codes():
    t = np.arange(HGRN_CHUNK)[:, None]
    s = np.arange(HGRN_CHUNK)[None, :]
    level = np.floor(np.log2(np.maximum(t ^ s, 1))).astype(np.int32)
    return jnp.asarray(np.where(s > t, -1, np.where(s == t, 0, 1 + level)), jnp.int32)


def _boundary_row(b, m):
    if 2 * m >= 8:
        return _block_row(b, 2 * m, m - 1)
    r8 = lax.broadcasted_iota(jnp.int32, b.shape, 0) & 7
    if m == 2:
        return jnp.where(r8 < 4, _block_row(b, 8, 1), _block_row(b, 8, 5))
    return jnp.where((r8 & 1) == 0, b, pltpu.roll(b, 1, 0))


def _hgrn_attention(q, k, b, code):
    C = HGRN_CHUNK
    rowl = lax.broadcasted_iota(jnp.int32, (C, LANES), 0)
    att = jnp.where(code == 0, jnp.sum(q * k, axis=-1, keepdims=True), 0.0)
    m, level = 1, 1
    while m < C:
        d = b - _boundary_row(b, m)
        isq = (rowl & m) != 0
        x = (jnp.where(isq, q, k) * jnp.exp2(jnp.abs(d) * (-LOG2_E))).astype(BF16)
        att = jnp.where(code == level, _dot_nt(x, x), att)
        m *= 2
        level += 1
    return att


def _hgrn_apply(q, k, v, b, att, S):
    C = HGRN_CHUNK
    vb = v.astype(BF16)
    o = _dot((q * jnp.exp(b)).astype(BF16), S.astype(BF16)) + _dot(att.astype(BF16), vb)
    bl = b[C - 1:C, :]
    kk = k * jnp.exp(bl - b)
    ecol = jnp.transpose(jnp.broadcast_to(jnp.exp(bl), (LANES, LANES)))
    return o, ecol * S + _dot(jnp.transpose(kk).astype(BF16), vb)


def _hgrn_prompt_kernel(q_ref, f_ref, i_ref, g_ref, lb_ref, gn_ref, code_ref, y_ref, so_ref, s_ref, *, rows):
    @pl.when(pl.program_id(1) == 0)
    def _():
        s_ref[...] = jnp.zeros_like(s_ref)

    C = HGRN_CHUNK
    code = code_ref[...]
    tril = jnp.where(code >= 0, 1.0, 0.0).astype(BF16)
    q, log_f, k = _hgrn_gates(q_ref[...], f_ref[...], lb_ref[...])
    chunks = [slice(c * C, (c + 1) * C) for c in range(rows // C)]
    bs = [_cumsum_rows(log_f[sl], tril) for sl in chunks]
    atts = [_hgrn_attention(q[sl], k[sl], b, code) for sl, b in zip(chunks, bs)]
    S = s_ref[...]
    outs = []
    for sl, b, att in zip(chunks, bs, atts):
        o, S = _hgrn_apply(q[sl], k[sl], i_ref[sl, :], b, att, S)
        outs.append(o)
    s_ref[...] = S
    y_ref[...] = _hgrn_out(jnp.concatenate(outs, axis=0), g_ref[...], gn_ref[...]).astype(BF16)

    @pl.when(pl.program_id(1) == pl.num_programs(1) - 1)
    def _():
        so_ref[...] = s_ref[...]


def hgrn_prompt(proj, lb, gn, rows):
    T = proj.shape[0]

    def col(off):
        base = off // LANES
        return pl.BlockSpec((rows, LANES), lambda h, c: (c, base + h))

    vec = pl.BlockSpec((1, LANES), lambda h, c: (0, h))
    return pl.pallas_call(
        functools.partial(_hgrn_prompt_kernel, rows=rows),
        grid=(A_HEADS, T // rows),
        in_specs=[col(COL_HQ), col(COL_HF), col(COL_HI), col(COL_HG), vec, vec,
                  pl.BlockSpec((HGRN_CHUNK, HGRN_CHUNK), lambda h, c: (0, 0))],
        out_specs=[
            pl.BlockSpec((rows, LANES), lambda h, c: (c, h)),
            pl.BlockSpec((None, A_DK, LANES), lambda h, c: (h, 0, 0)),
        ],
        out_shape=[
            jax.ShapeDtypeStruct((T, MIX_W), BF16),
            jax.ShapeDtypeStruct((A_HEADS, A_DK, LANES), F32),
        ],
        scratch_shapes=[pltpu.VMEM((A_DK, LANES), F32)],
        compiler_params=_params("parallel", "arbitrary"),
        name="hgrn_prompt",
    )(proj, proj, proj, proj, lb, gn, _hgrn_pair_codes())


def _hgrn_sample_kernel(q_ref, f_ref, i_ref, g_ref, lb_ref, gn_ref, s_ref, y_ref, so_ref, *, steps, bb, own):
    _zero_other_slabs(so_ref, own)
    lb = lb_ref[...]
    gn = gn_ref[...]
    qs, ks, vs, bs = [], [], [], []
    b = None
    for t in range(steps):
        q, log_f, k = _hgrn_gates(q_ref[t], f_ref[t], lb)
        b = log_f if b is None else b + log_f
        qs.append(q)
        ks.append(k)
        vs.append(i_ref[t])
        bs.append(b)
    intra = []
    for t in range(steps):
        acc = None
        for s in range(t + 1):
            w = jnp.sum(qs[t] * ks[s] * jnp.exp(bs[t] - bs[s]), axis=-1, keepdims=True)
            acc = w * vs[s] if acc is None else acc + w * vs[s]
        intra.append(acc)
    R = steps * bb
    q_stack = jnp.concatenate([qs[t] * jnp.exp(bs[t]) for t in range(steps)], axis=0).astype(BF16)
    k_stack = jnp.concatenate([ks[t] * jnp.exp(bs[-1] - bs[t]) for t in range(steps)], axis=0)
    v_stack = jnp.concatenate(vs, axis=0).astype(BF16)
    k_t = jnp.transpose(k_stack)
    f_pad = jnp.concatenate([jnp.exp(bs[-1])] + [jnp.zeros((R - bb, LANES), F32)], axis=0)
    f_t = jnp.transpose(f_pad)
    rowi = lax.broadcasted_iota(jnp.int32, (R, LANES), 0) % bb
    lanei = lax.broadcasted_iota(jnp.int32, (LANES, R), 1)

    def body(bi, o_acc):
        s_b = s_ref[bi]
        o_acc = jnp.where(rowi == bi, _dot(q_stack, s_b.astype(BF16)), o_acc)
        f_col = jnp.sum(jnp.where(lanei == bi, f_t, 0.0), axis=-1, keepdims=True)
        k_b = jnp.where(lanei % bb == bi, k_t, 0.0).astype(BF16)
        so_ref[own, bi] = f_col * s_b + _dot(k_b, v_stack)
        return o_acc

    o_inter = lax.fori_loop(0, bb, body, jnp.zeros((R, LANES), F32), unroll=HGRN_BATCH_UNROLL)
    for t in range(steps):
        o = o_inter[t * bb:(t + 1) * bb] + intra[t]
        y_ref[t] = _hgrn_out(o, g_ref[t], gn).astype(BF16)


def hgrn_sample(proj3, lb, gn, state, layer, bb, state_out):
    steps, B, _ = proj3.shape
    assert steps * bb == LANES

    def col(off):
        base = off // LANES
        return pl.BlockSpec((steps, bb, LANES), lambda g, h: (0, g, base + h))

    vec = pl.BlockSpec((1, LANES), lambda g, h: (0, h))
    slab = pl.BlockSpec((None, bb, None, A_DK, LANES), lambda g, h: (layer, g, h, 0, 0))
    return _layer_slab_call(
        lambda own: functools.partial(_hgrn_sample_kernel, steps=steps, bb=bb, own=own),
        [col(COL_HQ), col(COL_HF), col(COL_HI), col(COL_HG), vec, vec, slab],
        [proj3, proj3, proj3, proj3, lb, gn, state], state_out, layer,
        (bb, None, A_DK, LANES), lambda g, h: (g, h, 0, 0),
        [pl.BlockSpec((steps, bb, LANES), lambda g, h: (0, g, h))],
        grid=(B // bb, A_HEADS),
        out_shape=[
            jax.ShapeDtypeStruct((steps, B, MIX_W), BF16),
            jax.ShapeDtypeStruct(state.shape, F32),
        ],
        compiler_params=_params("parallel", "arbitrary"),
        name="hgrn_sample",
    )


POOL_HIST = 32


def _pool_project(d, w_ref, sc_ref, g):
    sl = slice(g * POOL_GC, (g + 1) * POOL_GC)
    return _dot(d.astype(BF16), w_ref[g].astype(BF16)) * sc_ref[:, sl]


def _pool_prompt_kernel(u_ref, prev_ref, w_ref, sc_ref, y_ref, ext_ref, sum_ref, *, tm):
    i = pl.program_id(0)
    H = POOL_HIST
    ext_ref[0:H, :] = jnp.where(i == 0, 0.0, prev_ref[...])
    ext_ref[H:, :] = u_ref[...]
    sum_ref[0:H // 2, :] = jnp.zeros((H // 2, MIX_W), F32)
    pos = i * tm + lax.broadcasted_iota(jnp.int32, (tm, 1), 0)
    src, w = ext_ref, 1
    for g, win in enumerate(POOL_WINDOWS):
        lanes = slice(g * POOL_GC, MIX_W)
        while w < win:
            n = H // 2 + tm
            sum_ref[pl.ds(H // 2, n), lanes] = src[pl.ds(H // 2, n), lanes] + src[pl.ds(H // 2 - w, n), lanes]
            src, w = sum_ref, 2 * w
        sl = slice(g * POOL_GC, (g + 1) * POOL_GC)
        cnt = jnp.minimum(pos + 1, win).astype(F32)
        d = src[pl.ds(H, tm), sl] / cnt - u_ref[:, sl]
        y_ref[:, sl] = _pool_project(d, w_ref, sc_ref, g).astype(BF16)


def _pool_sample_kernel(u_ref, c_ref, w_ref, sc_ref, y_ref, *, steps):
    for t in range(steps):
        for g, win in enumerate(POOL_WINDOWS):
            sl = slice(g * POOL_GC, (g + 1) * POOL_GC)
            acc = u_ref[t, :, sl]
            for j in range(1, win):
                if j <= t:
                    acc = acc + u_ref[t - j, :, sl]
                else:
                    acc = acc + c_ref[POOL_BUF + t - j, :, sl]
            d = acc / float(win) - u_ref[t, :, sl]
            y_ref[t, :, sl] = _pool_project(d, w_ref, sc_ref, g).astype(BF16)


def pool_sample(proj3, cache_pool, pool_w, pool_scale, layer, bb):
    steps, B, _ = proj3.shape
    cb = COL_POOL // MIX_W
    return pl.pallas_call(
        functools.partial(_pool_sample_kernel, steps=steps),
        grid=(B // bb,),
        in_specs=[
            pl.BlockSpec((steps, bb, MIX_W), lambda g: (0, g, cb)),
            pl.BlockSpec((None, POOL_BUF, bb, MIX_W), lambda g: (layer, 0, g, 0)),
            pl.BlockSpec((None, len(POOL_WINDOWS), POOL_GC, POOL_GC), lambda g: (layer, 0, 0, 0)),
            pl.BlockSpec((None, 1, MIX_W), lambda g: (layer, 0, 0)),
        ],
        out_specs=pl.BlockSpec((steps, bb, MIX_W), lambda g: (0, g, 0)),
        out_shape=jax.ShapeDtypeStruct((steps, B, MIX_W), BF16),
        compiler_params=_params("arbitrary"),
        name="pool_sample",
    )(proj3, cache_pool, pool_w, pool_scale)


def _rope_tables(positions):
    half = ROT_DIM // 2
    inv = np.power(ROPE_THETA, -np.arange(0, ROT_DIM, 2, dtype=np.float64) / ROT_DIM)
    ang = np.asarray(positions, np.float64)[:, None] * inv[None, :]
    cos, sin = np.cos(ang), np.sin(ang)
    n = len(positions)
    ct = np.ones((n, LANES))
    sn = np.zeros((n, LANES))
    sp = np.zeros((n, LANES))
    for base in (0, SWA_HEAD_DIM):
        ct[:, base:base + half] = cos
        ct[:, base + half:base + ROT_DIM] = cos
        sn[:, base:base + half] = -sin
        sp[:, base + half:base + ROT_DIM] = sin
    return tuple(jnp.asarray(t, F32) for t in (ct, sn, sp))


def _head_norm_rope(x, g, ct, sn, sp):
    lane = lax.broadcasted_iota(jnp.int32, x.shape, 1)
    lo = lane < SWA_HEAD_DIM
    x2 = x * x
    ms_lo = jnp.sum(jnp.where(lo, x2, 0.0), axis=-1, keepdims=True) / SWA_HEAD_DIM
    ms_hi = jnp.sum(jnp.where(lo, 0.0, x2), axis=-1, keepdims=True) / SWA_HEAD_DIM
    xn = x * jnp.where(lo, lax.rsqrt(ms_lo + EPS), lax.rsqrt(ms_hi + EPS)) * g
    half = ROT_DIM // 2
    return xn * ct + pltpu.roll(xn, LANES - half, 1) * sn + pltpu.roll(xn, half, 1) * sp


def _kprep_kernel(k_ref, g_ref, ct_ref, sn_ref, sp_ref, o_ref):
    ct, sn, sp = ct_ref[...], sn_ref[...], sp_ref[...]
    for j in range(2):
        sl = slice(j * LANES, (j + 1) * LANES)
        o_ref[:, sl] = _head_norm_rope(k_ref[:, sl], g_ref[...], ct, sn, sp)


def swa_kprep(proj, g2, tables, tm):
    T = proj.shape[0]
    kw = SWA_KV_HEADS * SWA_HEAD_DIM
    tab = pl.BlockSpec((tm, LANES), lambda i: (i, 0))
    return pl.pallas_call(
        _kprep_kernel,
        grid=(T // tm,),
        in_specs=[pl.BlockSpec((tm, kw), lambda i: (i, COL_SK // kw)),
                  pl.BlockSpec((1, LANES), lambda i: (0, 0)), tab, tab, tab],
        out_specs=pl.BlockSpec((tm, kw), lambda i: (i, 0)),
        out_shape=jax.ShapeDtypeStruct((T, kw), F32),
        compiler_params=_params("arbitrary"),
        name="swa_kprep",
    )(proj, g2, *tables)


def _dup_head(x, parity):
    lane = lax.broadcasted_iota(jnp.int32, x.shape, 1)
    return jnp.where(lane // SWA_HEAD_DIM == parity, x, pltpu.roll(x, SWA_HEAD_DIM, 1))


def _stack_heads(q):
    lane = lax.broadcasted_iota(jnp.int32, q.shape, 1)
    lo = lane < SWA_HEAD_DIM
    return jnp.concatenate([jnp.where(lo, q, 0.0), jnp.where(lo, 0.0, q)], axis=0)


def _unstack_heads(o2):
    R = o2.shape[0] // 2
    lane = lax.broadcasted_iota(jnp.int32, (R, LANES), 1)
    return jnp.where(lane < SWA_HEAD_DIM, o2[:R], o2[R:])


def _swa_prompt_kernel(sink_ref, q_ref, kc_ref, kp_ref, vc_ref, vp_ref, g_ref, ct_ref, sn_ref, sp_ref, y_ref,
                       *, layer, nb):
    first = pl.program_id(0) == 0
    W = WINDOW
    G = SWA_Q_HEADS // SWA_KV_HEADS
    r4 = lax.broadcasted_iota(jnp.int32, (G * W, 1), 0)
    hh = r4 // W
    ci = lax.broadcasted_iota(jnp.int32, (1, W), 1)
    cur = ci <= r4 % W
    scale = SWA_HEAD_DIM ** -0.5

    def head_blocks(cur_ref, prev_ref, kvh):
        ksl = slice((kvh // 2) * LANES, (kvh // 2 + 1) * LANES)
        x = _dup_head(jnp.concatenate([prev_ref[:, ksl], cur_ref[:, ksl]], axis=0), kvh % 2).astype(BF16)
        return [x[j * W:(j + 1) * W] for j in range(nb + 1)]

    scores, sinks = [], []
    for kvh in range(SWA_KV_HEADS):
        kb = head_blocks(kc_ref, kp_ref, kvh)
        sink = sink_ref[layer, G * kvh + G - 1]
        for i in range(G - 2, -1, -1):
            sink = jnp.where(hh == i, sink_ref[layer, G * kvh + i], sink)
        for blk in range(nb):
            rows = slice(blk * W, (blk + 1) * W)
            qs = []
            for jj in range(G // 2):
                qsl = slice((2 * kvh + jj) * LANES, (2 * kvh + jj + 1) * LANES)
                qn = _head_norm_rope(q_ref[rows, qsl], g_ref[...], ct_ref[rows, :], sn_ref[rows, :], sp_ref[rows, :])
                qs.append(_stack_heads(qn * scale))
            q = jnp.concatenate(qs, axis=0).astype(BF16)
            s_prev = _dot_nt(q, kb[blk])
            if blk == 0:
                s_prev = jnp.where(first, NEG_BIG, s_prev)
            scores.append(jnp.where(cur, _dot_nt(q, kb[blk + 1]), s_prev))
            sinks.append(sink)
    probs = []
    for s, sink in zip(scores, sinks):
        m = jnp.maximum(jnp.max(s, axis=-1, keepdims=True), sink)
        e = jnp.exp(s - m)
        probs.append(e * (1.0 / (jnp.sum(e, axis=-1, keepdims=True) + jnp.exp(sink - m))))
    for kvh in range(SWA_KV_HEADS):
        vb = head_blocks(vc_ref, vp_ref, kvh)
        for blk in range(nb):
            p = probs[kvh * nb + blk]
            o = (_dot(jnp.where(cur, p, 0.0).astype(BF16), vb[blk + 1])
                 + _dot(jnp.where(cur, 0.0, p).astype(BF16), vb[blk]))
            for jj in range(G // 2):
                qsl = slice((2 * kvh + jj) * LANES, (2 * kvh + jj + 1) * LANES)
                y_ref[blk * W:(blk + 1) * W, qsl] = _unstack_heads(o[2 * jj * W:(2 * jj + 2) * W]).astype(BF16)


def swa_prompt(proj, khat, sinks, g2, tables, layer, nb):
    T = proj.shape[0]
    W = WINDOW
    tq = nb * W
    kw = SWA_KV_HEADS * SWA_HEAD_DIM
    tab = pl.BlockSpec((tq, LANES), lambda i: (i, 0))
    prev = lambda i: jnp.maximum(i * nb - 1, 0)
    return pl.pallas_call(
        functools.partial(_swa_prompt_kernel, layer=layer, nb=nb),
        grid=(T // tq,),
        in_specs=[
            pl.BlockSpec(memory_space=pltpu.SMEM),
            pl.BlockSpec((tq, MIX_W), lambda i: (i, COL_SQ // MIX_W)),
            pl.BlockSpec((tq, kw), lambda i: (i, 0)),
            pl.BlockSpec((W, kw), lambda i: (prev(i), 0)),
            pl.BlockSpec((tq, kw), lambda i: (i, COL_SV // kw)),
            pl.BlockSpec((W, kw), lambda i: (prev(i), COL_SV // kw)),
            pl.BlockSpec((1, LANES), lambda i: (0, 0)), tab, tab, tab,
        ],
        out_specs=pl.BlockSpec((tq, MIX_W), lambda i: (i, 0)),
        out_shape=jax.ShapeDtypeStruct((T, MIX_W), BF16),
        compiler_params=_params("arbitrary"),
        name="swa_prompt",
    )(sinks, proj, khat, khat, proj, proj, g2, *tables)


def _swa_sample_kernel(sink_ref, q_ref, kn_ref, vn_ref, kc_ref, vc_ref, g_ref, ct_ref, sn_ref, sp_ref, y_ref,
                       *, layer, steps, bb):
    kvh = pl.program_id(1)
    parity = kvh % 2
    W = WINDOW
    R = steps * bb
    G = SWA_Q_HEADS // SWA_KV_HEADS
    ct, sn, sp = ct_ref[...], sn_ref[...], sp_ref[...]
    scale = SWA_HEAD_DIM ** -0.5
    r4 = lax.broadcasted_iota(jnp.int32, (G * R, 1), 0)
    hh = r4 // R
    tq = (r4 % R) // bb
    bq = r4 % bb
    c_new = lax.broadcasted_iota(jnp.int32, (1, R), 1)
    valid_new = (c_new % bb == bq) & (c_new // bb <= tq)
    c_old = lax.broadcasted_iota(jnp.int32, (1, W), 1)
    valid_old = c_old > tq
    kn = _dup_head(jnp.concatenate([kn_ref[t] for t in range(steps)], axis=0), parity).astype(BF16)
    vn = _dup_head(jnp.concatenate([vn_ref[t] for t in range(steps)], axis=0), parity).astype(BF16)
    qs = []
    for jj in range(G // 2):
        qsl = slice(jj * LANES, (jj + 1) * LANES)
        q = jnp.concatenate([q_ref[t, :, qsl] for t in range(steps)], axis=0)
        qs.append(_stack_heads(_head_norm_rope(q, g_ref[...], ct, sn, sp) * scale))
    q4 = jnp.concatenate(qs, axis=0)
    s_new = jnp.where(valid_new, _dot_nt(q4.astype(BF16), kn), NEG_BIG)
    s_old = None
    for b in range(0, bb, 2):
        lhs = jnp.concatenate([jnp.where(bq == b + i, q4, 0.0).astype(BF16) for i in range(2)], axis=1)
        k_t = [kc_ref[b + i].astype(BF16) for i in range(2)]
        d = _dot(lhs, jnp.concatenate([k_t[0], k_t[0], k_t[1], k_t[1]], axis=0))
        s_old = d if s_old is None else s_old + d
    s_old = jnp.where(valid_old, s_old, NEG_BIG)
    sink = sink_ref[layer, G * kvh + G - 1]
    for i in range(G - 2, -1, -1):
        sink = jnp.where(hh == i, sink_ref[layer, G * kvh + i], sink)
    m = jnp.maximum(jnp.maximum(jnp.max(s_new, axis=-1, keepdims=True),
                                jnp.max(s_old, axis=-1, keepdims=True)), sink)
    e_new = jnp.exp(s_new - m)
    e_old = jnp.exp(s_old - m)
    den = jnp.sum(e_new, axis=-1, keepdims=True) + jnp.sum(e_old, axis=-1, keepdims=True) + jnp.exp(sink - m)
    p_old = e_old / den
    o = _dot((e_new / den).astype(BF16), vn)
    for b in range(0, bb, 2):
        lhs = jnp.concatenate([jnp.where(bq == b + i, p_old, 0.0).astype(BF16) for i in range(2)], axis=1)
        v_t = [vc_ref[b + i].astype(BF16) for i in range(2)]
        rhs = jnp.concatenate([jnp.concatenate([v_t[i], v_t[i]], axis=0) for i in range(2)], axis=1)
        o = o + _dot_nt(lhs, rhs)
    for jj in range(G // 2):
        o_j = _unstack_heads(o[2 * jj * R:(2 * jj + 2) * R])
        for t in range(steps):
            y_ref[t, :, jj * LANES:(jj + 1) * LANES] = o_j[t * bb:(t + 1) * bb]


def swa_sample(proj3, khat3, cache_kt, cache_vt, sinks, g2, tables, layer, bb):
    steps, B, _ = proj3.shape
    R = steps * bb
    qw = MIX_W // SWA_KV_HEADS
    tab = pl.BlockSpec((R, LANES), lambda g, h: (0, 0))
    cache = pl.BlockSpec((None, bb, None, SWA_HEAD_DIM, WINDOW), lambda g, h: (layer, g, h, 0, 0))
    return pl.pallas_call(
        functools.partial(_swa_sample_kernel, layer=layer, steps=steps, bb=bb),
        grid=(B // bb, SWA_KV_HEADS),
        in_specs=[
            pl.BlockSpec(memory_space=pltpu.SMEM),
            pl.BlockSpec((steps, bb, qw), lambda g, h: (0, g, COL_SQ // qw + h)),
            pl.BlockSpec((steps, bb, LANES), lambda g, h: (0, g, h // 2)),
            pl.BlockSpec((steps, bb, LANES), lambda g, h: (0, g, COL_SV // LANES + h // 2)),
            cache, cache,
            pl.BlockSpec((1, LANES), lambda g, h: (0, 0)), tab, tab, tab,
        ],
        out_specs=pl.BlockSpec((steps, bb, qw), lambda g, h: (0, g, h)),
        out_shape=jax.ShapeDtypeStruct((steps, B, MIX_W), F32),
        compiler_params=_params("parallel", "arbitrary"),
        name="swa_sample",
    )(sinks, proj3, khat3, proj3, cache_kt, cache_vt, g2, *tables)


def _mem_kv_kernel(x_ref, g_ref, w_ref, kg_ref, o_ref, xn_ref):
    j = pl.program_id(0)

    @pl.when(j == 0)
    def _():
        x = x_ref[...]
        ms = jnp.mean(x * x, axis=-1, keepdims=True)
        xn_ref[...] = (x * lax.rsqrt(ms + EPS) * g_ref[...]).astype(BF16)

    y = _dot(xn_ref[...], w_ref[...].astype(BF16))

    @pl.when(j < MEM_HEADS)
    def _():
        ms = jnp.mean(y * y, axis=-1, keepdims=True)
        o_ref[...] = y * lax.rsqrt(ms + EPS) * kg_ref[...]

    @pl.when(j >= MEM_HEADS)
    def _():
        o_ref[...] = y


def mem_kv(mem, mem_norm_g, w_mem_kv, mem_knorm_g, layer):
    M, K = mem.shape
    hd = MEM_HEAD_DIM
    return pl.pallas_call(
        _mem_kv_kernel,
        grid=(2 * MEM_HEADS,),
        in_specs=[
            pl.BlockSpec((M, K), lambda j: (0, 0)),
            pl.BlockSpec((None, 1, K), lambda j: (layer, 0, 0)),
            pl.BlockSpec((None, K, hd), lambda j: (layer, 0, j)),
            pl.BlockSpec((None, 1, hd), lambda j: (layer, 0, 0)),
        ],
        out_specs=pl.BlockSpec((M, hd), lambda j: (0, j)),
        out_shape=jax.ShapeDtypeStruct((M, 2 * MIX_W), F32),
        scratch_shapes=[pltpu.VMEM((M, K), BF16)],
        compiler_params=_params("arbitrary"),
        name="mem_kv",
    )(mem, mem_norm_g, w_mem_kv, mem_knorm_g)


def _mem_qnorm(q, g):
    ms = jnp.mean(q * q, axis=-1, keepdims=True)
    return q * lax.rsqrt(ms + EPS) * g * (MEM_HEAD_DIM ** -0.5)


def _softmax_rows(s):
    m = jnp.max(s, axis=-1, keepdims=True)
    e = jnp.exp(s - m)
    return e / jnp.sum(e, axis=-1, keepdims=True)


def _mem_prompt_kernel(q0_ref, q1_ref, q2_ref, q3_ref, kv_ref, g_ref, y_ref):
    hd = MEM_HEAD_DIM
    for h, q_ref in enumerate((q0_ref, q1_ref, q2_ref, q3_ref)):
        q = _mem_qnorm(q_ref[...], g_ref[...]).astype(BF16)
        p = _softmax_rows(_dot_nt(q, kv_ref[:, h * hd:(h + 1) * hd].astype(BF16)))
        v = kv_ref[:, MIX_W + h * hd:MIX_W + (h + 1) * hd].astype(BF16)
        y_ref[:, h * hd:(h + 1) * hd] = _dot(p.astype(BF16), v).astype(BF16)


def _light_mixers_kernel(k_ref, gk_ref, ct_ref, sn_ref, sp_ref,
                         u_ref, prev_ref, pw_ref, sc_ref,
                         q0_ref, q1_ref, q2_ref, q3_ref, kv_ref, gq_ref,
                         khat_ref, yb_ref, ym_ref, ext_ref, sum_ref, *, tm):
    _kprep_kernel(k_ref, gk_ref, ct_ref, sn_ref, sp_ref, khat_ref)
    _pool_prompt_kernel(u_ref, prev_ref, pw_ref, sc_ref, yb_ref, ext_ref, sum_ref, tm=tm)
    _mem_prompt_kernel(q0_ref, q1_ref, q2_ref, q3_ref, kv_ref, gq_ref, ym_ref)


def light_mixers_prompt(proj, gk2, tables, pool_w, pool_scale, kv, mem_qnorm_g, layer, tm):
    T = proj.shape[0]
    kw = SWA_KV_HEADS * SWA_HEAD_DIM
    hd = MEM_HEAD_DIM
    cb = COL_POOL // MIX_W
    tab = pl.BlockSpec((tm, LANES), lambda i: (i, 0))

    def q_spec(h):
        qb = COL_MQ // hd + h
        return pl.BlockSpec((tm, hd), lambda i: (i, qb))

    return pl.pallas_call(
        functools.partial(_light_mixers_kernel, tm=tm),
        grid=(T // tm,),
        in_specs=[
            pl.BlockSpec((tm, kw), lambda i: (i, COL_SK // kw)),
            pl.BlockSpec((1, LANES), lambda i: (0, 0)), tab, tab, tab,
            pl.BlockSpec((tm, MIX_W), lambda i: (i, cb)),
            pl.BlockSpec((POOL_HIST, MIX_W), lambda i: (jnp.maximum(i * (tm // POOL_HIST) - 1, 0), cb)),
            pl.BlockSpec((None, len(POOL_WINDOWS), POOL_GC, POOL_GC), lambda i: (layer, 0, 0, 0)),
            pl.BlockSpec((None, 1, MIX_W), lambda i: (layer, 0, 0)),
        ] + [q_spec(h) for h in range(MEM_HEADS)] + [
            pl.BlockSpec((N_MEM, 2 * MIX_W), lambda i: (0, 0)),
            pl.BlockSpec((None, 1, hd), lambda i: (layer, 0, 0)),
        ],
        out_specs=[
            pl.BlockSpec((tm, kw), lambda i: (i, 0)),
            pl.BlockSpec((tm, MIX_W), lambda i: (i, 0)),
            pl.BlockSpec((tm, MIX_W), lambda i: (i, 0)),
        ],
        out_shape=[
            jax.ShapeDtypeStruct((T, kw), F32),
            jax.ShapeDtypeStruct((T, MIX_W), BF16),
            jax.ShapeDtypeStruct((T, MIX_W), BF16),
        ],
        scratch_shapes=[pltpu.VMEM((POOL_HIST + tm, MIX_W), F32), pltpu.VMEM((POOL_HIST + tm, MIX_W), F32)],
        compiler_params=_params("arbitrary"),
        name="light_mixers_prompt",
    )(proj, gk2, *tables, proj, proj, pool_w, pool_scale, proj, proj, proj, proj, kv, mem_qnorm_g)


def _mem_rows_view(c):
    L_, B_, M, H, hd = c.shape
    c = c.reshape(L_, B_, M, H, hd // LANES, LANES)
    return jnp.transpose(c, (0, 1, 2, 4, 3, 5)).reshape(L_, B_, M * H * (hd // LANES), LANES)


def _mem_head(c_ref, b, h):
    nt = MEM_HEAD_DIM // LANES
    parts = [c_ref[b, pl.ds(lt * MEM_HEADS + h, N_MEM, stride=nt * MEM_HEADS), :] for lt in range(nt)]
    return jnp.concatenate(parts, axis=1).astype(BF16)


def _mem_sample_kernel(q0_ref, q1_ref, q2_ref, q3_ref, k_ref, v_ref, g_ref, y_ref, *, steps, bb):
    R = steps * bb
    hd = MEM_HEAD_DIM
    bq = lax.broadcasted_iota(jnp.int32, (R, 1), 0) % bb
    for h, q_ref in enumerate((q0_ref, q1_ref, q2_ref, q3_ref)):
        q = _mem_qnorm(jnp.concatenate([q_ref[t] for t in range(steps)], axis=0), g_ref[...])
        s = None
        for b in range(bb):
            d = _dot_nt(jnp.where(bq == b, q, 0.0).astype(BF16), _mem_head(k_ref, b, h))
            s = d if s is None else s + d
        p = _softmax_rows(s)
        o = None
        for b in range(bb):
            d = _dot(jnp.where(bq == b, p, 0.0).astype(BF16), _mem_head(v_ref, b, h))
            o = d if o is None else o + d
        for t in range(steps):
            y_ref[t, :, h * hd:(h + 1) * hd] = o[t * bb:(t + 1) * bb]


def mem_attn_sample(proj3, cache_k, cache_v, mem_qnorm_g, layer, bb):
    steps, B, _ = proj3.shape
    hd = MEM_HEAD_DIM
    cache = pl.BlockSpec((None, bb) + cache_k.shape[2:], lambda g: (layer, g, 0, 0))

    def q_spec(h):
        cb = COL_MQ // hd + h
        return pl.BlockSpec((steps, bb, hd), lambda g: (0, g, cb))

    return pl.pallas_call(
        functools.partial(_mem_sample_kernel, steps=steps, bb=bb),
        grid=(B // bb,),
        in_specs=[q_spec(h) for h in range(MEM_HEADS)] + [
            cache, cache,
            pl.BlockSpec((None, 1, hd), lambda g: (layer, 0, 0)),
        ],
        out_specs=pl.BlockSpec((steps, bb, MIX_W), lambda g: (0, g, 0)),
        out_shape=jax.ShapeDtypeStruct((steps, B, MIX_W), F32),
        compiler_params=_params("arbitrary"),
        name="mem_sample",
    )(proj3, proj3, proj3, proj3, cache_k, cache_v, mem_qnorm_g)


def _row_tile(T, cap):
    t = cap
    while T % t:
        t //= 2
    return t


TM_STREAM = 2048
TM_DOWN = 1024
TM_LOCAL = 512
TM_MERGE = 512
TM_RESIDENT = 256
TN_STREAM = 512
TN_DOWN = 256
TN_SAMPLE_IN = 1536
UP_PIECE_ROWS = 1024
HGRN_ROWS = 4096
HGRN_BATCH_UNROLL = 16
SWA_BLOCKS = 8
POOL_BATCH = 64
ATTN_BATCH = 8


def _token_tail(x, xn, ys, layer, w_in, w_branch, w_o, norm2_g):
    T = x.shape[0]
    tm = _row_tile(T, TM_MERGE)
    merged = merge_branches(xn, ys, w_in, w_branch, layer, tm, TN_STREAM if T // tm > 1 else TN_DOWN)
    return matmul_res_norm(merged, w_o, layer, x, norm2_g, _row_tile(T, TM_RESIDENT))


def kernel(x_prompt, x_sample, mem_prompt, state_hgrn, cache_pool, cache_swa_k, cache_swa_v, state_conv, cache_mem_k, cache_mem_v, norm1_g, w_in, hgrn_lb, hgrn_norm_g, pool_w, pool_scale, swa_qnorm_g, swa_knorm_g, swa_sinks, mem_norm_g, w_mem_kv, mem_qnorm_g, mem_knorm_g, w_branch, w_o, norm2_g, w_up, conv_w, conv_b, w_down):
    depth = w_in.shape[0]
    bp, L, _ = x_prompt.shape
    B, steps, _ = x_sample.shape
    assert bp == 1
    kw = SWA_KV_HEADS * SWA_HEAD_DIM
    Ts = steps * B

    lb_all = jnp.cumsum(jax.nn.softmax(hgrn_lb.astype(F32), axis=0), axis=0)
    lb_all = lb_all - lb_all[:1]

    tab_p = _rope_tables(np.arange(L))
    tab_s = _rope_tables(np.repeat(PAST_LEN + np.arange(steps), B))
    tab_sb = _rope_tables(np.repeat(PAST_LEN + np.arange(steps), ATTN_BATCH))

    xp = x_prompt.reshape(L, D_MODEL)
    xs = jnp.transpose(x_sample, (1, 0, 2)).reshape(Ts, D_MODEL)
    mem = mem_prompt.reshape(N_MEM, D_MODEL)
    ckt_all = jnp.transpose(cache_swa_k, (0, 1, 3, 4, 2))
    cvt_all = jnp.transpose(cache_swa_v, (0, 1, 3, 4, 2))
    mk_rows = _mem_rows_view(cache_mem_k)
    mv_rows = _mem_rows_view(cache_mem_v)
    cpool_v = jnp.transpose(cache_pool, (0, 2, 1, 3))
    row3 = lambda a: a.reshape(depth, 1, a.shape[-1])
    norm1_g, norm2_g, pool_scale, conv_b = row3(norm1_g), row3(norm2_g), row3(pool_scale), row3(conv_b)
    mem_norm_g, mem_qnorm_g, mem_knorm_g = row3(mem_norm_g), row3(mem_qnorm_g), row3(mem_knorm_g)
    tm_p = _row_tile(L, TM_STREAM)
    tl_p = _row_tile(L, TM_LOCAL)

    outs = {k: [] for k in ("sp", "pp", "ps", "kp", "ks", "vp", "vs", "cp", "mk", "mv")}
    hgrn_states = None
    conv_states = None
    for l in range(depth):
        lb = lb_all[l].reshape(1, MIX_W)
        gn = hgrn_norm_g[l].reshape(1, MIX_W)
        gq2 = jnp.tile(swa_qnorm_g[l], 2).reshape(1, LANES)
        gk2 = jnp.tile(swa_knorm_g[l], 2).reshape(1, LANES)

        kv = mem_kv(mem, mem_norm_g, w_mem_kv, mem_knorm_g, l)

        xn = prenorm(xp, norm1_g, l, tl_p)
        proj = matmul_cols(xn, w_in, l, COL_GATE, tm_p, TN_STREAM)
        ya, s_p = hgrn_prompt(proj, lb, gn, _row_tile(L, HGRN_ROWS))
        khat, yb, ym = light_mixers_prompt(proj, gk2, tab_p, pool_w, pool_scale, kv, mem_qnorm_g, l,
                                           _row_tile(L, TM_DOWN))
        yc = swa_prompt(proj, khat, swa_sinks, gq2, tab_p, l, _row_tile(L, SWA_BLOCKS * WINDOW) // WINDOW)
        h, hn = _token_tail(xp, xn, (ya, yb, yc, ym), l, w_in, w_branch, w_o, norm2_g)
        gact, a_tail = up_conv_prompt(hn, w_up, conv_w, conv_b, l, tm_p, TN_STREAM)
        xp = matmul_res(gact, w_down, l, h, _row_tile(L, TM_DOWN), TN_DOWN)

        outs["sp"].append(s_p[None])
        outs["pp"].append(proj[None, L - POOL_BUF:, COL_POOL:COL_POOL + MIX_W])
        outs["kp"].append(khat[None, L - WINDOW:].reshape(1, WINDOW, SWA_KV_HEADS, SWA_HEAD_DIM))
        outs["vp"].append(proj[None, L - WINDOW:, COL_SV:COL_SV + kw].reshape(1, WINDOW, SWA_KV_HEADS, SWA_HEAD_DIM))
        outs["cp"].append(a_tail[-1:, CONV_HIST - 2:])
        outs["mk"].append(kv[None, :, :MIX_W].reshape(1, N_MEM, MEM_HEADS, MEM_HEAD_DIM))
        outs["mv"].append(kv[None, :, MIX_W:].reshape(1, N_MEM, MEM_HEADS, MEM_HEAD_DIM))

        xn = prenorm(xs, norm1_g, l, Ts)
        proj_s = matmul_cols(xn, w_in, l, COL_GATE, Ts, TN_SAMPLE_IN)
        proj3 = proj_s.reshape(steps, B, COL_GATE)
        ya, hgrn_states = hgrn_sample(proj3, lb, gn, state_hgrn, l, LANES // steps, hgrn_states)
        yb = pool_sample(proj3, cpool_v, pool_w, pool_scale, l, POOL_BATCH)
        khat_s = swa_kprep(proj_s, gk2, tab_s, Ts)
        khat3 = khat_s.reshape(steps, B, kw)
        yc = swa_sample(proj3, khat3, ckt_all, cvt_all, swa_sinks, gq2, tab_sb, l, ATTN_BATCH)
        ym = mem_attn_sample(proj3, mk_rows, mv_rows, mem_qnorm_g, l, ATTN_BATCH)
        ys = tuple(y.reshape(Ts, MIX_W).astype(BF16) for y in (ya, yb, yc, ym))
        h, hn = _token_tail(xs, xn, ys, l, w_in, w_branch, w_o, norm2_g)
        gact, conv_states = up_conv_sample(hn, w_up, state_conv, conv_w, conv_b, l, steps, TN_STREAM, conv_states)
        xs = matmul_res(gact, w_down, l, h, Ts, TN_DOWN)

        outs["ps"].append(proj3[:, :, COL_POOL:COL_POOL + MIX_W])
        to_window_minor = lambda a: jnp.transpose(a.reshape(steps, B, SWA_KV_HEADS, SWA_HEAD_DIM), (1, 2, 3, 0))
        outs["ks"].append(to_window_minor(khat3))
        outs["vs"].append(to_window_minor(proj3[:, :, COL_SV:COL_SV + kw]))

    stk = lambda k: jnp.stack(outs[k], axis=0)
    pool_s = jnp.transpose(jnp.concatenate([cpool_v[:, steps:], stk("ps")], axis=1), (0, 2, 1, 3))
    def slide_window(old, new):
        lead = [(0, 0, 0)] * (old.ndim - 1)
        shifted = lax.pad(old, jnp.zeros((), old.dtype), lead + [(-steps, steps, 0)])
        tail = lax.pad(new, jnp.zeros((), old.dtype), lead + [(WINDOW - steps, 0, 0)])
        pos = lax.broadcasted_iota(jnp.int32, old.shape, old.ndim - 1)
        return jnp.transpose(jnp.where(pos < WINDOW - steps, shifted, tail), (0, 1, 4, 2, 3))

    swa_k_s = slide_window(ckt_all, stk("ks"))
    swa_v_s = slide_window(cvt_all, stk("vs"))
    y_prompt = xp.reshape(1, L, D_MODEL)
    y_sample = jnp.transpose(xs.reshape(steps, B, D_MODEL), (1, 0, 2))
    return (y_prompt, y_sample,
            stk("sp"), hgrn_states, stk("pp"), pool_s, stk("kp"), swa_k_s, stk("vp"), swa_v_s,
            stk("cp"), conv_states, jnp.concatenate(outs["mk"], axis=0)[:, None], jnp.concatenate(outs["mv"], axis=0)[:, None])
```
